```python
import jax, jax.numpy as jnp
from jax import lax
import numpy as np

D_MODEL = 1024
BATCH = 32
SEQ = 2048
DEPTH = 4

MEM_LEN = 256
HEAD_DIM = D_MODEL // 16
A_HEADS = 6
B_HEADS = 4
C_HEADS = 6
D_A = A_HEADS * HEAD_DIM
D_B = B_HEADS * HEAD_DIM
D_C = C_HEADS * HEAD_DIM
CONV_A_WIDTH = 31
CONV_C_WIDTH = 3
CHUNK = 128
IN_WIDTH = 2 * D_A + 2 * D_B + 3 * D_C
SPLITS = (D_A, 2 * D_A, 2 * D_A + D_B, 2 * D_A + 2 * D_B,
          2 * D_A + 2 * D_B + D_C, 2 * D_A + 2 * D_B + 2 * D_C)
X_HEADS = 4
X_HEAD_DIM = D_MODEL // X_HEADS
D_FF = 4 * D_MODEL
LN_EPS = 1e-5
DEEPNORM_ALPHA = (2.0 * DEPTH) ** 0.25
DEEPNORM_BETA = (8.0 * DEPTH) ** -0.25

kernel_name = 'hybrid_conv_gmlp_shortconv_deepnorm_trunk'


def _layer_norm(x, g, b):
    xf = x.astype(jnp.float32)
    mu = jnp.mean(xf, axis=-1, keepdims=True)
    var = jnp.mean(jnp.square(xf - mu), axis=-1, keepdims=True)
    y = (xf - mu) * lax.rsqrt(var + LN_EPS)
    return (y * g.astype(jnp.float32) + b.astype(jnp.float32)).astype(x.dtype)


def _causal_depthwise_conv(x, w):
    k, c = w.shape
    return lax.conv_general_dilated(
        x, w[:, None, :].astype(x.dtype), window_strides=(1,), padding=[(k - 1, 0)],
        dimension_numbers=('NWC', 'WIO', 'NWC'), feature_group_count=c)


def _chunked_spatial_gate(u, v, w_s, b_s):
    bsz, s, _ = v.shape
    vh = v.reshape(bsz, s // CHUNK, CHUNK, B_HEADS, HEAD_DIM)
    causal = jnp.tril(jnp.ones((CHUNK, CHUNK), dtype=bool))
    w = jnp.where(causal[None], w_s, jnp.zeros_like(w_s)).astype(v.dtype)
    mixed = jnp.einsum('hts,bcshd->bcthd', w, vh) + b_s.T[None, None, :, :, None].astype(v.dtype)
    return u * mixed.reshape(bsz, s, D_B)


def _hybrid_mixer(h, w_in, conv_a_w, conv_a_b, ln_a_g, ln_a_b, ln_v_g, ln_v_b, w_s, b_s, conv_c_w, w_out):
    proj = h @ w_in
    a_val, a_gate, b_u, b_v, c_b, c_c, c_x = jnp.split(proj, SPLITS, axis=-1)
    a = _causal_depthwise_conv(a_val * jax.nn.sigmoid(a_gate), conv_a_w) + conv_a_b
    a = jax.nn.swish(_layer_norm(a, ln_a_g, ln_a_b))
    u = jax.nn.gelu(b_u, approximate=False)
    v = _layer_norm(jax.nn.gelu(b_v, approximate=False), ln_v_g, ln_v_b)
    bo = _chunked_spatial_gate(u, v, w_s, b_s)
    co = c_b * _causal_depthwise_conv(c_c * c_x, conv_c_w)
    return jnp.concatenate([a, bo, co], axis=-1) @ w_out


def _memory_cross_attention(h, mem, w_q, w_kv, w_o):
    bsz, s, _ = h.shape
    m = mem.shape[1]
    q = (h @ w_q).reshape(bsz, s, X_HEADS, X_HEAD_DIM)
    k, v = jnp.split(mem @ w_kv, 2, axis=-1)
    k = k.reshape(bsz, m, X_HEADS, X_HEAD_DIM)
    v = v.reshape(bsz, m, X_HEADS, X_HEAD_DIM)
    scores = jnp.einsum('bshd,bmhd->bhsm', q.astype(jnp.float32), k.astype(jnp.float32)) * (X_HEAD_DIM ** -0.5)
    p = jax.nn.softmax(scores, axis=-1).astype(h.dtype)
    o = jnp.einsum('bhsm,bmhd->bshd', p, v).reshape(bsz, s, D_MODEL)
    return o @ w_o


def _sq_relu_mlp(h, w_ff1, w_ff2):
    return jnp.square(jax.nn.relu(h @ w_ff1)) @ w_ff2


def _fwd_setup_inputs(seed: int = 0) -> dict:
    key = jax.random.key(seed)
    ks = jax.random.split(key, 26)

    def nrm(k, shape, scale):
        return jax.random.normal(k, shape, dtype=jnp.float32) * scale

    def gain(k, n):
        return 1.0 + nrm(k, (DEPTH, n), 0.02)

    w_kv = jnp.concatenate([nrm(ks[16], (DEPTH, D_MODEL, D_MODEL), D_MODEL ** -0.5),
                            nrm(ks[17], (DEPTH, D_MODEL, D_MODEL), D_MODEL ** -0.5 * DEEPNORM_BETA)], axis=-1)
    return {
        'x': nrm(ks[0], (BATCH, SEQ, D_MODEL), 1.0),
        'mem': nrm(ks[1], (BATCH, MEM_LEN, D_MODEL), 1.0),
        'w_in': nrm(ks[2], (DEPTH, D_MODEL, IN_WIDTH), D_MODEL ** -0.5),
        'conv_a_w': nrm(ks[3], (DEPTH, CONV_A_WIDTH, D_A), CONV_A_WIDTH ** -0.5),
        'conv_a_b': nrm(ks[4], (DEPTH, D_A), 0.02),
        'ln_a_g': gain(ks[5], D_A),
        'ln_a_b': nrm(ks[6], (DEPTH, D_A), 0.02),
        'ln_v_g': gain(ks[7], D_B),
        'ln_v_b': nrm(ks[8], (DEPTH, D_B), 0.02),
        'w_s': nrm(ks[9], (DEPTH, B_HEADS, CHUNK, CHUNK), CHUNK ** -0.5),
        'b_s': 1.0 + nrm(ks[10], (DEPTH, B_HEADS, CHUNK), 0.02),
        'conv_c_w': nrm(ks[11], (DEPTH, CONV_C_WIDTH, D_C), CONV_C_WIDTH ** -0.5),
        'w_out': nrm(ks[12], (DEPTH, D_MODEL, D_MODEL), D_MODEL ** -0.5 * DEEPNORM_BETA),
        'ln1_g': gain(ks[13], D_MODEL),
        'ln1_b': nrm(ks[14], (DEPTH, D_MODEL), 0.02),
        'w_q': nrm(ks[15], (DEPTH, D_MODEL, D_MODEL), D_MODEL ** -0.5),
        'w_kv': w_kv,
        'w_o': nrm(ks[18], (DEPTH, D_MODEL, D_MODEL), D_MODEL ** -0.5 * DEEPNORM_BETA),
        'ln2_g': gain(ks[19], D_MODEL),
        'ln2_b': nrm(ks[20], (DEPTH, D_MODEL), 0.02),
        'w_ff1': nrm(ks[21], (DEPTH, D_MODEL, D_FF), D_MODEL ** -0.5),
        'w_ff2': nrm(ks[22], (DEPTH, D_FF, D_MODEL), D_FF ** -0.5 * DEEPNORM_BETA),
        'ln3_g': gain(ks[23], D_MODEL),
        'ln3_b': nrm(ks[24], (DEPTH, D_MODEL), 0.02),
    }


def _fwd_reference(x, mem, w_in, conv_a_w, conv_a_b, ln_a_g, ln_a_b, ln_v_g, ln_v_b, w_s, b_s, conv_c_w, w_out,
              ln1_g, ln1_b, w_q, w_kv, w_o, ln2_g, ln2_b, w_ff1, w_ff2, ln3_g, ln3_b):
    for l in range(DEPTH):
        mix = _hybrid_mixer(x, w_in[l], conv_a_w[l], conv_a_b[l], ln_a_g[l], ln_a_b[l],
                            ln_v_g[l], ln_v_b[l], w_s[l], b_s[l], conv_c_w[l], w_out[l])
        x = _layer_norm(DEEPNORM_ALPHA * x + mix, ln1_g[l], ln1_b[l])
        att = _memory_cross_attention(x, mem, w_q[l], w_kv[l], w_o[l])
        x = _layer_norm(DEEPNORM_ALPHA * x + att, ln2_g[l], ln2_b[l])
        ff = _sq_relu_mlp(x, w_ff1[l], w_ff2[l])
        x = _layer_norm(DEEPNORM_ALPHA * x + ff, ln3_g[l], ln3_b[l])
    return x


import jax as _jax
import jax.numpy as _jnp

TWIN_FORMAT = 'train_step'
FWD_PARAMS = ['x', 'mem', 'w_in', 'conv_a_w', 'conv_a_b', 'ln_a_g', 'ln_a_b', 'ln_v_g', 'ln_v_b', 'w_s', 'b_s', 'conv_c_w', 'w_out', 'ln1_g', 'ln1_b', 'w_q', 'w_kv', 'w_o', 'ln2_g', 'ln2_b', 'w_ff1', 'w_ff2', 'ln3_g', 'ln3_b']
TWIN_WEIGHTS = ['w_in', 'conv_a_w', 'conv_a_b', 'ln_a_g', 'ln_a_b', 'ln_v_g', 'ln_v_b', 'w_s', 'b_s', 'conv_c_w', 'w_out', 'ln1_g', 'ln1_b', 'w_q', 'w_kv', 'w_o', 'ln2_g', 'ln2_b', 'w_ff1', 'w_ff2', 'ln3_g', 'ln3_b']
TWIN_DIFF_INPUT = 'x'
TWIN_INPUTS = ['x', 'mem', 'w_in', 'conv_a_w', 'conv_a_b', 'ln_a_g', 'ln_a_b', 'ln_v_g', 'ln_v_b', 'w_s', 'b_s', 'conv_c_w', 'w_out', 'ln1_g', 'ln1_b', 'w_q', 'w_kv', 'w_o', 'ln2_g', 'ln2_b', 'w_ff1', 'w_ff2', 'ln3_g', 'ln3_b', 'loss_target', 'm_w_in', 'm_conv_a_w', 'm_conv_a_b', 'm_ln_a_g', 'm_ln_a_b', 'm_ln_v_g', 'm_ln_v_b', 'm_w_s', 'm_b_s', 'm_conv_c_w', 'm_w_out', 'm_ln1_g', 'm_ln1_b', 'm_w_q', 'm_w_kv', 'm_w_o', 'm_ln2_g', 'm_ln2_b', 'm_w_ff1', 'm_w_ff2', 'm_ln3_g', 'm_ln3_b', 'v_w_in', 'v_conv_a_w', 'v_conv_a_b', 'v_ln_a_g', 'v_ln_a_b', 'v_ln_v_g', 'v_ln_v_b', 'v_w_s', 'v_b_s', 'v_conv_c_w', 'v_w_out', 'v_ln1_g', 'v_ln1_b', 'v_w_q', 'v_w_kv', 'v_w_o', 'v_ln2_g', 'v_ln2_b', 'v_w_ff1', 'v_w_ff2', 'v_ln3_g', 'v_ln3_b']
TWIN_OUTPUTS = ['loss', 'grad_x', 'grad_w_in', 'grad_conv_a_w', 'grad_conv_a_b', 'grad_ln_a_g', 'grad_ln_a_b', 'grad_ln_v_g', 'grad_ln_v_b', 'grad_w_s', 'grad_b_s', 'grad_conv_c_w', 'grad_w_out', 'grad_ln1_g', 'grad_ln1_b', 'grad_w_q', 'grad_w_kv', 'grad_w_o', 'grad_ln2_g', 'grad_ln2_b', 'grad_w_ff1', 'grad_w_ff2', 'grad_ln3_g', 'grad_ln3_b', 'delta_w_in', 'delta_conv_a_w', 'delta_conv_a_b', 'delta_ln_a_g', 'delta_ln_a_b', 'delta_ln_v_g', 'delta_ln_v_b', 'delta_w_s', 'delta_b_s', 'delta_conv_c_w', 'delta_w_out', 'delta_ln1_g', 'delta_ln1_b', 'delta_w_q', 'delta_w_kv', 'delta_w_o', 'delta_ln2_g', 'delta_ln2_b', 'delta_w_ff1', 'delta_w_ff2', 'delta_ln3_g', 'delta_ln3_b', 'new_m_w_in', 'new_m_conv_a_w', 'new_m_conv_a_b', 'new_m_ln_a_g', 'new_m_ln_a_b', 'new_m_ln_v_g', 'new_m_ln_v_b', 'new_m_w_s', 'new_m_b_s', 'new_m_conv_c_w', 'new_m_w_out', 'new_m_ln1_g', 'new_m_ln1_b', 'new_m_w_q', 'new_m_w_kv', 'new_m_w_o', 'new_m_ln2_g', 'new_m_ln2_b', 'new_m_w_ff1', 'new_m_w_ff2', 'new_m_ln3_g', 'new_m_ln3_b', 'new_v_w_in', 'new_v_conv_a_w', 'new_v_conv_a_b', 'new_v_ln_a_g', 'new_v_ln_a_b', 'new_v_ln_v_g', 'new_v_ln_v_b', 'new_v_w_s', 'new_v_b_s', 'new_v_conv_c_w', 'new_v_w_out', 'new_v_ln1_g', 'new_v_ln1_b', 'new_v_w_q', 'new_v_w_kv', 'new_v_w_o', 'new_v_ln2_g', 'new_v_ln2_b', 'new_v_w_ff1', 'new_v_w_ff2', 'new_v_ln3_g', 'new_v_ln3_b']
TWIN_LEAF_KINDS = {'loss': 'loss', 'grad_x': 'grad_x', 'grad_w_in': 'grad_w', 'grad_conv_a_w': 'grad_w', 'grad_conv_a_b': 'grad_w', 'grad_ln_a_g': 'grad_w', 'grad_ln_a_b': 'grad_w', 'grad_ln_v_g': 'grad_w', 'grad_ln_v_b': 'grad_w', 'grad_w_s': 'grad_w', 'grad_b_s': 'grad_w', 'grad_conv_c_w': 'grad_w', 'grad_w_out': 'grad_w', 'grad_ln1_g': 'grad_w', 'grad_ln1_b': 'grad_w', 'grad_w_q': 'grad_w', 'grad_w_kv': 'grad_w', 'grad_w_o': 'grad_w', 'grad_ln2_g': 'grad_w', 'grad_ln2_b': 'grad_w', 'grad_w_ff1': 'grad_w', 'grad_w_ff2': 'grad_w', 'grad_ln3_g': 'grad_w', 'grad_ln3_b': 'grad_w', 'delta_w_in': 'delta_w', 'delta_conv_a_w': 'delta_w', 'delta_conv_a_b': 'delta_w', 'delta_ln_a_g': 'delta_w', 'delta_ln_a_b': 'delta_w', 'delta_ln_v_g': 'delta_w', 'delta_ln_v_b': 'delta_w', 'delta_w_s': 'delta_w', 'delta_b_s': 'delta_w', 'delta_conv_c_w': 'delta_w', 'delta_w_out': 'delta_w', 'delta_ln1_g': 'delta_w', 'delta_ln1_b': 'delta_w', 'delta_w_q': 'delta_w', 'delta_w_kv': 'delta_w', 'delta_w_o': 'delta_w', 'delta_ln2_g': 'delta_w', 'delta_ln2_b': 'delta_w', 'delta_w_ff1': 'delta_w', 'delta_w_ff2': 'delta_w', 'delta_ln3_g': 'delta_w', 'delta_ln3_b': 'delta_w', 'new_m_w_in': 'new_m', 'new_m_conv_a_w': 'new_m', 'new_m_conv_a_b': 'new_m', 'new_m_ln_a_g': 'new_m', 'new_m_ln_a_b': 'new_m', 'new_m_ln_v_g': 'new_m', 'new_m_ln_v_b': 'new_m', 'new_m_w_s': 'new_m', 'new_m_b_s': 'new_m', 'new_m_conv_c_w': 'new_m', 'new_m_w_out': 'new_m', 'new_m_ln1_g': 'new_m', 'new_m_ln1_b': 'new_m', 'new_m_w_q': 'new_m', 'new_m_w_kv': 'new_m', 'new_m_w_o': 'new_m', 'new_m_ln2_g': 'new_m', 'new_m_ln2_b': 'new_m', 'new_m_w_ff1': 'new_m', 'new_m_w_ff2': 'new_m', 'new_m_ln3_g': 'new_m', 'new_m_ln3_b': 'new_m', 'new_v_w_in': 'new_v', 'new_v_conv_a_w': 'new_v', 'new_v_conv_a_b': 'new_v', 'new_v_ln_a_g': 'new_v', 'new_v_ln_a_b': 'new_v', 'new_v_ln_v_g': 'new_v', 'new_v_ln_v_b': 'new_v', 'new_v_w_s': 'new_v', 'new_v_b_s': 'new_v', 'new_v_conv_c_w': 'new_v', 'new_v_w_out': 'new_v', 'new_v_ln1_g': 'new_v', 'new_v_ln1_b': 'new_v', 'new_v_w_q': 'new_v', 'new_v_w_kv': 'new_v', 'new_v_w_o': 'new_v', 'new_v_ln2_g': 'new_v', 'new_v_ln2_b': 'new_v', 'new_v_w_ff1': 'new_v', 'new_v_w_ff2': 'new_v', 'new_v_ln3_g': 'new_v', 'new_v_ln3_b': 'new_v'}


def _forward(args):
    return _fwd_reference(*[args[k] for k in FWD_PARAMS])


def _output_shape():
    out = _jax.eval_shape(lambda: _forward(_fwd_setup_inputs(0)))
    return out.shape, out.dtype

N_MICROBATCH = 1
ADAM_LR = 0.001
ADAM_B1 = 0.9
ADAM_B2 = 0.999
ADAM_EPS = 1e-08
ADAM_WD = 0.01
ADAM_STEP = 10
PER_EXAMPLE_BATCH_AXIS = {'x': 0, 'mem': 0, 'loss_target': 0}
SHARED_INPUTS = []
_WEIGHT_DTYPES = {'w_in': _jnp.float32, 'conv_a_w': _jnp.float32, 'conv_a_b': _jnp.float32, 'ln_a_g': _jnp.float32, 'ln_a_b': _jnp.float32, 'ln_v_g': _jnp.float32, 'ln_v_b': _jnp.float32, 'w_s': _jnp.float32, 'b_s': _jnp.float32, 'conv_c_w': _jnp.float32, 'w_out': _jnp.float32, 'ln1_g': _jnp.float32, 'ln1_b': _jnp.float32, 'w_q': _jnp.float32, 'w_kv': _jnp.float32, 'w_o': _jnp.float32, 'ln2_g': _jnp.float32, 'ln2_b': _jnp.float32, 'w_ff1': _jnp.float32, 'w_ff2': _jnp.float32, 'ln3_g': _jnp.float32, 'ln3_b': _jnp.float32}
MOMENT_SCALE = {'w_in': 5.299011e-02, 'conv_a_w': 4.329773e-02, 'conv_a_b': 1.515124e-01, 'ln_a_g': 6.824961e-02, 'ln_a_b': 9.752437e-02, 'ln_v_g': 3.275619e-02, 'ln_v_b': 3.242291e-02, 'w_s': 2.152847e-02, 'b_s': 3.082140e-02, 'conv_c_w': 6.926378e-02, 'w_out': 1.444554e-01, 'ln1_g': 1.419316e+00, 'ln1_b': 8.265096e-01, 'w_q': 2.993452e-03, 'w_kv': 6.316443e-03, 'w_o': 8.356786e-03, 'ln2_g': 1.422305e+00, 'ln2_b': 8.281100e-01, 'w_ff1': 4.605027e-02, 'w_ff2': 2.602705e-01, 'ln3_g': 3.224529e+01, 'ln3_b': 7.825602e+00}


def _to_microbatches(a, axis):
    t = _jnp.moveaxis(a, axis, 0)
    t = t.reshape((N_MICROBATCH, t.shape[0] // N_MICROBATCH) + t.shape[1:])
    return _jnp.moveaxis(t, 1, axis + 1)


def setup_inputs(seed: int = 0) -> dict:
    inp = _fwd_setup_inputs(seed)
    key = _jax.random.fold_in(_jax.random.key(seed), 7919)
    shape, _ = _output_shape()
    out = dict(inp)
    out["loss_target"] = _jax.random.normal(_jax.random.fold_in(key, 0), shape, _jnp.float32)
    for i, name in enumerate(TWIN_WEIGHTS):
        w = inp[name].astype(_jnp.float32)
        if MOMENT_SCALE is None:
            s = _jnp.sqrt(_jnp.mean(_jnp.square(w)) + 1e-30)
        else:
            s = MOMENT_SCALE[name]
        km, kv = _jax.random.split(_jax.random.fold_in(key, i + 1))
        out[name] = w
        out["m_" + name] = s * _jax.random.normal(km, w.shape, _jnp.float32)
        out["v_" + name] = (s * s) * _jax.random.uniform(kv, w.shape, _jnp.float32, 0.5, 1.5)
    if N_MICROBATCH > 1:
        for name, axis in PER_EXAMPLE_BATCH_AXIS.items():
            out[name] = _to_microbatches(out[name], axis)
    return {'x': out['x'], 'mem': out['mem'], 'w_in': out['w_in'], 'conv_a_w': out['conv_a_w'], 'conv_a_b': out['conv_a_b'], 'ln_a_g': out['ln_a_g'], 'ln_a_b': out['ln_a_b'], 'ln_v_g': out['ln_v_g'], 'ln_v_b': out['ln_v_b'], 'w_s': out['w_s'], 'b_s': out['b_s'], 'conv_c_w': out['conv_c_w'], 'w_out': out['w_out'], 'ln1_g': out['ln1_g'], 'ln1_b': out['ln1_b'], 'w_q': out['w_q'], 'w_kv': out['w_kv'], 'w_o': out['w_o'], 'ln2_g': out['ln2_g'], 'ln2_b': out['ln2_b'], 'w_ff1': out['w_ff1'], 'w_ff2': out['w_ff2'], 'ln3_g': out['ln3_g'], 'ln3_b': out['ln3_b'], 'loss_target': out['loss_target'], 'm_w_in': out['m_w_in'], 'm_conv_a_w': out['m_conv_a_w'], 'm_conv_a_b': out['m_conv_a_b'], 'm_ln_a_g': out['m_ln_a_g'], 'm_ln_a_b': out['m_ln_a_b'], 'm_ln_v_g': out['m_ln_v_g'], 'm_ln_v_b': out['m_ln_v_b'], 'm_w_s': out['m_w_s'], 'm_b_s': out['m_b_s'], 'm_conv_c_w': out['m_conv_c_w'], 'm_w_out': out['m_w_out'], 'm_ln1_g': out['m_ln1_g'], 'm_ln1_b': out['m_ln1_b'], 'm_w_q': out['m_w_q'], 'm_w_kv': out['m_w_kv'], 'm_w_o': out['m_w_o'], 'm_ln2_g': out['m_ln2_g'], 'm_ln2_b': out['m_ln2_b'], 'm_w_ff1': out['m_w_ff1'], 'm_w_ff2': out['m_w_ff2'], 'm_ln3_g': out['m_ln3_g'], 'm_ln3_b': out['m_ln3_b'], 'v_w_in': out['v_w_in'], 'v_conv_a_w': out['v_conv_a_w'], 'v_conv_a_b': out['v_conv_a_b'], 'v_ln_a_g': out['v_ln_a_g'], 'v_ln_a_b': out['v_ln_a_b'], 'v_ln_v_g': out['v_ln_v_g'], 'v_ln_v_b': out['v_ln_v_b'], 'v_w_s': out['v_w_s'], 'v_b_s': out['v_b_s'], 'v_conv_c_w': out['v_conv_c_w'], 'v_w_out': out['v_w_out'], 'v_ln1_g': out['v_ln1_g'], 'v_ln1_b': out['v_ln1_b'], 'v_w_q': out['v_w_q'], 'v_w_kv': out['v_w_kv'], 'v_w_o': out['v_w_o'], 'v_ln2_g': out['v_ln2_g'], 'v_ln2_b': out['v_ln2_b'], 'v_w_ff1': out['v_w_ff1'], 'v_w_ff2': out['v_w_ff2'], 'v_ln3_g': out['v_ln3_g'], 'v_ln3_b': out['v_ln3_b']}


def _loss(weights, diff, rest, loss_target):
    with _jax.named_scope("forward"):
        args = {**rest, TWIN_DIFF_INPUT: diff, **{k: w.astype(_WEIGHT_DTYPES[k]) for k, w in weights.items()}}
        y = _forward(args)
    with _jax.named_scope("loss_head"):
        err = _jnp.square(y.astype(_jnp.float32) - loss_target)
        return 0.5 * _jnp.sum(_jnp.mean(err, axis=-1)) if err.ndim else 0.5 * err


def _adamw(w, g, m, v):
    m = ADAM_B1 * m + (1.0 - ADAM_B1) * g
    v = ADAM_B2 * v + (1.0 - ADAM_B2) * _jnp.square(g)
    m_hat = m / (1.0 - ADAM_B1 ** ADAM_STEP)
    v_hat = v / (1.0 - ADAM_B2 ** ADAM_STEP)
    delta = -ADAM_LR * (m_hat / (_jnp.sqrt(v_hat) + ADAM_EPS) + ADAM_WD * w)
    return delta, m, v


def reference(x, mem, w_in, conv_a_w, conv_a_b, ln_a_g, ln_a_b, ln_v_g, ln_v_b, w_s, b_s, conv_c_w, w_out, ln1_g, ln1_b, w_q, w_kv, w_o, ln2_g, ln2_b, w_ff1, w_ff2, ln3_g, ln3_b, loss_target, m_w_in, m_conv_a_w, m_conv_a_b, m_ln_a_g, m_ln_a_b, m_ln_v_g, m_ln_v_b, m_w_s, m_b_s, m_conv_c_w, m_w_out, m_ln1_g, m_ln1_b, m_w_q, m_w_kv, m_w_o, m_ln2_g, m_ln2_b, m_w_ff1, m_w_ff2, m_ln3_g, m_ln3_b, v_w_in, v_conv_a_w, v_conv_a_b, v_ln_a_g, v_ln_a_b, v_ln_v_g, v_ln_v_b, v_w_s, v_b_s, v_conv_c_w, v_w_out, v_ln1_g, v_ln1_b, v_w_q, v_w_kv, v_w_o, v_ln2_g, v_ln2_b, v_w_ff1, v_w_ff2, v_ln3_g, v_ln3_b):
    given = dict(x=x, mem=mem, w_in=w_in, conv_a_w=conv_a_w, conv_a_b=conv_a_b, ln_a_g=ln_a_g, ln_a_b=ln_a_b, ln_v_g=ln_v_g, ln_v_b=ln_v_b, w_s=w_s, b_s=b_s, conv_c_w=conv_c_w, w_out=w_out, ln1_g=ln1_g, ln1_b=ln1_b, w_q=w_q, w_kv=w_kv, w_o=w_o, ln2_g=ln2_g, ln2_b=ln2_b, w_ff1=w_ff1, w_ff2=w_ff2, ln3_g=ln3_g, ln3_b=ln3_b, loss_target=loss_target, m_w_in=m_w_in, m_conv_a_w=m_conv_a_w, m_conv_a_b=m_conv_a_b, m_ln_a_g=m_ln_a_g, m_ln_a_b=m_ln_a_b, m_ln_v_g=m_ln_v_g, m_ln_v_b=m_ln_v_b, m_w_s=m_w_s, m_b_s=m_b_s, m_conv_c_w=m_conv_c_w, m_w_out=m_w_out, m_ln1_g=m_ln1_g, m_ln1_b=m_ln1_b, m_w_q=m_w_q, m_w_kv=m_w_kv, m_w_o=m_w_o, m_ln2_g=m_ln2_g, m_ln2_b=m_ln2_b, m_w_ff1=m_w_ff1, m_w_ff2=m_w_ff2, m_ln3_g=m_ln3_g, m_ln3_b=m_ln3_b, v_w_in=v_w_in, v_conv_a_w=v_conv_a_w, v_conv_a_b=v_conv_a_b, v_ln_a_g=v_ln_a_g, v_ln_a_b=v_ln_a_b, v_ln_v_g=v_ln_v_g, v_ln_v_b=v_ln_v_b, v_w_s=v_w_s, v_b_s=v_b_s, v_conv_c_w=v_conv_c_w, v_w_out=v_w_out, v_ln1_g=v_ln1_g, v_ln1_b=v_ln1_b, v_w_q=v_w_q, v_w_kv=v_w_kv, v_w_o=v_w_o, v_ln2_g=v_ln2_g, v_ln2_b=v_ln2_b, v_w_ff1=v_w_ff1, v_w_ff2=v_w_ff2, v_ln3_g=v_ln3_g, v_ln3_b=v_ln3_b)
    weights = {n: given[n] for n in TWIN_WEIGHTS}
    shared = {n: given[n] for n in SHARED_INPUTS}
    per_example = {n: given[n] for n in ['x', 'mem']}
    grad_fn = _jax.value_and_grad(_loss, argnums=(0, 1))

    def one_microbatch(ex, loss_target):
        ex = dict(ex)
        diff = ex.pop(TWIN_DIFF_INPUT)
        return grad_fn(weights, diff, {**shared, **ex}, loss_target)

    if N_MICROBATCH == 1:
        loss, (grad_w, grad_x) = one_microbatch(per_example, given["loss_target"])
    else:
        def body(carry, xs):
            loss_sum, grad_sum = carry
            l_k, (gw_k, gx_k) = one_microbatch(xs[0], xs[1])
            with _jax.named_scope("update"):
                return (loss_sum + l_k, _jax.tree.map(_jnp.add, grad_sum, gw_k)), gx_k

        init = (_jnp.zeros((), _jnp.float32), _jax.tree.map(_jnp.zeros_like, weights))
        (loss, grad_w), grad_x = _jax.lax.scan(body, init, (per_example, given["loss_target"]))
    with _jax.named_scope("update"):
        delta_w, new_m, new_v = {}, {}, {}
        for n in TWIN_WEIGHTS:
            delta_w[n], new_m[n], new_v[n] = _adamw(weights[n], grad_w[n], given["m_" + n], given["v_" + n])
    return (loss, grad_x, *[grad_w[n] for n in TWIN_WEIGHTS], *[delta_w[n] for n in TWIN_WEIGHTS],
            *[new_m[n] for n in TWIN_WEIGHTS], *[new_v[n] for n in TWIN_WEIGHTS])
```

```python
import functools
import math

import jax
import jax.numpy as jnp
from jax import lax
from jax.experimental import pallas as pl
from jax.experimental.pallas import tpu as pltpu

F32 = jnp.float32
BF16 = jnp.bfloat16
SDS = jax.ShapeDtypeStruct

HEADS = 16
A_HEADS, B_HEADS, C_HEADS = 6, 4, 6
X_HEADS = 4
CHUNK = 128
LN_EPS = 1e-5
ADAM_LR, ADAM_B1, ADAM_B2, ADAM_EPS, ADAM_WD, ADAM_STEP = 0.001, 0.9, 0.999, 1e-08, 0.01, 10

N_CHIPS = 4
V7X_VMEM_LIMIT = 56 << 20
SUBLANES = 8
PACK_W = 1024
PACK_ROWS = 32
ROW_TILE = 512
WIDE_ROW_TILE = 256
CONV_ROWS = 256
ATT_ROWS = 512
TN_TILE = 1024
MESH = pl.DeviceIdType.MESH
ANY = pl.BlockSpec(memory_space=pl.ANY)


def _tile(n, t):
    for d in range(min(n, t), 0, -1):
        if n % d == 0 and d % (2 * SUBLANES) == 0:
            return d
    return n


def _round_up(n, m):
    return -(-n // m) * m


def _params(*sem):
    return pltpu.CompilerParams(dimension_semantics=sem or None, vmem_limit_bytes=V7X_VMEM_LIMIT)


def _dot(a, b):
    return jnp.dot(a, b, preferred_element_type=F32)


def _dot_nt(a, b):
    return lax.dot_general(a, b, (((1,), (1,)), ((), ())), preferred_element_type=F32)


def _dot_tn(a, b):
    return lax.dot_general(a, b, (((0,), (0,)), ((), ())), preferred_element_type=F32)


def _sigmoid(x):
    return 1.0 / (1.0 + jnp.exp(-x))


def _gelu(x):
    return 0.5 * x * (1.0 + lax.erf(x * (2.0 ** -0.5)))


def _gelu_grad(x):
    return 0.5 * (1.0 + lax.erf(x * (2.0 ** -0.5))) + x * jnp.exp(-0.5 * x * x) * ((2.0 * math.pi) ** -0.5)


def _ln_stats(z):
    mu = jnp.mean(z, axis=-1, keepdims=True)
    zc = z - mu
    rstd = lax.rsqrt(jnp.mean(zc * zc, axis=-1, keepdims=True) + LN_EPS)
    return zc * rstd, rstd


def _ln_bwd(dy, y, rstd, g):
    dyh = dy * g
    return rstd * (dyh - jnp.mean(dyh, axis=-1, keepdims=True) - y * jnp.mean(dyh * y, axis=-1, keepdims=True))


def _colsum(x):
    return jnp.sum(x, axis=0, keepdims=True)


def _rowwise(body, rows, consts, outs, accs=(), *, tm, name):
    t = rows[0].shape[0]
    steps = t // tm

    def kern(*refs):
        body(pl.program_id(0), steps, *refs)

    def whole(a):
        return pl.BlockSpec(a.shape, lambda i, nd=len(a.shape): (0,) * nd)

    in_specs = [pl.BlockSpec((tm, r.shape[1]), lambda i: (i, 0)) for r in rows] + [whole(c) for c in consts]
    out_shape = [SDS((t, n), dt) for n, dt in outs] + [SDS(s, dt) for s, dt in accs]
    out_specs = [pl.BlockSpec((tm, n), lambda i: (i, 0)) for n, _ in outs] + [whole(SDS(s, dt)) for s, dt in accs]
    return pl.pallas_call(kern, grid=(steps,), in_specs=in_specs, out_specs=out_specs, out_shape=out_shape,
                          compiler_params=_params("arbitrary"), name=name)(*rows, *consts)


def _in_proj(xb, w_in, widths, name):
    offs = [sum(widths[:k]) for k in range(len(widths))]

    def body(i, steps, x_ref, w_ref, *o_refs):
        x = x_ref[...]
        for o_ref, off, n in zip(o_refs, offs, widths):
            o_ref[...] = _dot(x, w_ref[:, off:off + n])

    return _rowwise(body, [xb], [w_in], [(n, F32) for n in widths], tm=_tile(xb.shape[0], ROW_TILE), name=name)


def _matmul(ab, w, name):
    def body(i, steps, a_ref, w_ref, o_ref):
        o_ref[...] = _dot(a_ref[...], w_ref[...]).astype(BF16)

    return _rowwise(body, [ab], [w], [(w.shape[1], BF16)], tm=_tile(ab.shape[0], ROW_TILE), name=name)[0]


def _ff1(xb, w, name):
    def body(i, steps, a_ref, w_ref, h_ref, r_ref):
        h = _dot(a_ref[...], w_ref[...])
        h_ref[...] = h.astype(BF16)
        r = jnp.maximum(h, 0.0)
        r_ref[...] = (r * r).astype(BF16)

    n = w.shape[1]
    return _rowwise(body, [xb], [w], [(n, BF16), (n, BF16)], tm=_tile(xb.shape[0], WIDE_ROW_TILE), name=name)


def _proj_ln(a_list, w, x, g, b, alpha, tm, name):
    widths = [a.shape[1] for a in a_list]
    offs = [sum(widths[:k]) for k in range(len(widths))]
    na = len(a_list)

    def body(i, steps, *refs):
        a_refs, (x_ref, w_ref, g_ref, b_ref, z_ref, xn_ref, xb_ref) = refs[:na], refs[na:]
        acc = alpha * x_ref[...]
        for a_ref, off, n in zip(a_refs, offs, widths):
            acc = acc + _dot(a_ref[...], w_ref[off:off + n, :])
        z_ref[...] = acc
        y, _ = _ln_stats(acc)
        xn = y * g_ref[...] + b_ref[...]
        xn_ref[...] = xn
        xb_ref[...] = xn.astype(BF16)

    d = w.shape[1]
    return _rowwise(body, [*a_list, x], [w, g, b], [(d, F32), (d, F32), (d, BF16)], tm=tm, name=name)


def _loss_head(y, target, name):
    d = y.shape[1]

    def body(i, steps, y_ref, t_ref, dy_ref, l_ref):
        @pl.when(i == 0)
        def _():
            l_ref[...] = jnp.zeros_like(l_ref)

        err = y_ref[...] - t_ref[...]
        dy_ref[...] = err * (1.0 / d)
        l_ref[...] += jnp.sum(err * err) * (0.5 / d)

    return _rowwise(body, [y, target], [], [(d, F32)], [((SUBLANES, 128), F32)], tm=_tile(y.shape[0], ROW_TILE), name=name)


def _ln_bwd_rows(dy, z, g, name):
    d = z.shape[1]

    def body(i, steps, dy_ref, z_ref, g_ref, dz_ref, dzb_ref, dg_ref, db_ref):
        @pl.when(i == 0)
        def _():
            dg_ref[...] = jnp.zeros_like(dg_ref)
            db_ref[...] = jnp.zeros_like(db_ref)

        dy_ = dy_ref[...]
        y, rstd = _ln_stats(z_ref[...])
        dz = _ln_bwd(dy_, y, rstd, g_ref[...])
        dz_ref[...] = dz
        dzb_ref[...] = dz.astype(BF16)
        dg_ref[...] += _colsum(dy_ * y)
        db_ref[...] += _colsum(dy_)

    return _rowwise(body, [dy, z], [g], [(d, F32), (d, BF16)], [((1, d), F32), ((1, d), F32)],
                    tm=_tile(z.shape[0], ROW_TILE), name=name)


def _bwd_ff2(dzb, h, w_ff2, name):
    def body(i, steps, dz_ref, h_ref, w_ref, dh_ref):
        dr = _dot_nt(dz_ref[...], w_ref[...])
        dh_ref[...] = (dr * (2.0 * jnp.maximum(h_ref[...].astype(F32), 0.0))).astype(BF16)

    return _rowwise(body, [dzb, h], [w_ff2], [(h.shape[1], BF16)], tm=_tile(h.shape[0], WIDE_ROW_TILE), name=name)[0]


def _bwd_nt(g_list, w, col_offs, res, alpha, tm, name):
    ng = len(g_list)
    widths = [g.shape[1] for g in g_list]

    def body(i, steps, *refs):
        g_refs = refs[:ng]
        if res is None:
            w_ref, o_ref = refs[ng:]
            acc = None
        else:
            r_ref, w_ref, o_ref = refs[ng:]
            acc = alpha * r_ref[...]
        for g_ref, off, n in zip(g_refs, col_offs, widths):
            part = _dot_nt(g_ref[...], w_ref[:, off:off + n])
            acc = part if acc is None else acc + part
        o_ref[...] = acc.astype(o_ref.dtype)

    rows = list(g_list) + ([] if res is None else [res])
    return rows, w, body, tm, name


def _bwd_data(g_list, w, col_offs, res, alpha, out_dtype, tm, name):
    rows, w, body, tm, name = _bwd_nt(g_list, w, col_offs, res, alpha, tm, name)
    return _rowwise(body, rows, [w], [(w.shape[0], out_dtype)], tm=tm, name=name)[0]


def _bwd_out_proj(dzb, w_out, widths, name):
    offs = [sum(widths[:k]) for k in range(len(widths))]

    def body(i, steps, dz_ref, w_ref, *o_refs):
        dz = dz_ref[...]
        for o_ref, off, n in zip(o_refs, offs, widths):
            o_ref[...] = _dot_nt(dz, w_ref[off:off + n, :])

    return _rowwise(body, [dzb], [w_out], [(n, F32) for n in widths], tm=_tile(dzb.shape[0], ROW_TILE), name=name)


def _adamw(w, g, m, v, name):
    def body(i, steps, w_ref, g_ref, m_ref, v_ref, d_ref, nm_ref, nv_ref):
        g_ = g_ref[...]
        nm = ADAM_B1 * m_ref[...] + (1.0 - ADAM_B1) * g_
        nv = ADAM_B2 * v_ref[...] + (1.0 - ADAM_B2) * (g_ * g_)
        m_hat = nm / (1.0 - ADAM_B1 ** ADAM_STEP)
        v_hat = nv / (1.0 - ADAM_B2 ** ADAM_STEP)
        d_ref[...] = -ADAM_LR * (m_hat / (jnp.sqrt(v_hat) + ADAM_EPS) + ADAM_WD * w_ref[...])
        nm_ref[...] = nm
        nv_ref[...] = nv

    c = w.shape[1]
    return _rowwise(body, [w, g, m, v], [], [(c, F32)] * 3, tm=_tile(w.shape[0], ROW_TILE), name=name)


def _mm_tn(a, g, name):
    t, ka = a.shape
    n = g.shape[1]
    ta, tn, tk = _tile(ka, TN_TILE), _tile(n, TN_TILE), _tile(t, TN_TILE)

    def kern(a_ref, g_ref, o_ref):
        @pl.when(pl.program_id(2) == 0)
        def _():
            o_ref[...] = jnp.zeros_like(o_ref)

        o_ref[...] += _dot_tn(a_ref[...], g_ref[...])

    return pl.pallas_call(
        kern, grid=(ka // ta, n // tn, t // tk),
        in_specs=[pl.BlockSpec((tk, ta), lambda i, j, k: (k, i)), pl.BlockSpec((tk, tn), lambda i, j, k: (k, j))],
        out_specs=pl.BlockSpec((ta, tn), lambda i, j, k: (i, j)), out_shape=SDS((ka, n), F32),
        compiler_params=_params("arbitrary", "arbitrary", "arbitrary"), name=name)(a, g)


def _conv_geometry(seq, taps):
    pad = _round_up(taps - 1, SUBLANES)
    rc = _tile(seq, CONV_ROWS)
    assert rc % pad == 0 and seq % rc == 0
    return pad, rc, seq // rc


def _chunk_spec(rc, n, nch):
    return pl.BlockSpec((rc, n), lambda b, i: (b * nch + i, 0))


def _prev_halo_spec(pad, rc, n, nch):
    per = rc // pad
    return pl.BlockSpec((pad, n), lambda b, i: (jnp.maximum((b * nch + i) * per - 1, 0), 0))


def _next_halo_spec(pad, rc, n, nch, total_rows):
    per = rc // pad
    last = total_rows // pad - 1
    return pl.BlockSpec((pad, n), lambda b, i: (jnp.minimum((b * nch + i + 1) * per, last), 0))


def _whole2(a):
    return pl.BlockSpec(a.shape, lambda b, i, nd=len(a.shape): (0,) * nd)


def _causal_taps(pad_ref, w_ref, taps, pad, rc):
    acc = None
    for k in range(taps):
        term = pad_ref[pad - (taps - 1) + k: pad - (taps - 1) + k + rc, :] * w_ref[k:k + 1, :]
        acc = term if acc is None else acc + term
    return acc


def _mix_a_fwd(a_val, a_gate, conv_w, conv_b, ln_g, ln_b, batch, name):
    t, da = a_val.shape
    taps = conv_w.shape[0]
    pad, rc, nch = _conv_geometry(t // batch, taps)

    def kern(av_ref, ag_ref, pav_ref, pag_ref, w_ref, cb_ref, g_ref, b_ref, a_ref, ac_ref, pad_ref):
        first = pl.program_id(1) == 0
        halo = pav_ref[...] * _sigmoid(pag_ref[...])
        pad_ref[0:pad, :] = jnp.where(first, 0.0, halo)
        pad_ref[pad:pad + rc, :] = av_ref[...] * _sigmoid(ag_ref[...])
        ac = _causal_taps(pad_ref, w_ref, taps, pad, rc) + cb_ref[...]
        ac_ref[...] = ac
        y, _ = _ln_stats(ac)
        aln = y * g_ref[...] + b_ref[...]
        a_ref[...] = (aln * _sigmoid(aln)).astype(BF16)

    chunk, halo = _chunk_spec(rc, da, nch), _prev_halo_spec(pad, rc, da, nch)
    return pl.pallas_call(
        kern, grid=(batch, nch),
        in_specs=[chunk, chunk, halo, halo, _whole2(conv_w), _whole2(conv_b), _whole2(ln_g), _whole2(ln_b)],
        out_specs=[chunk, chunk], out_shape=[SDS((t, da), BF16), SDS((t, da), F32)],
        scratch_shapes=[pltpu.VMEM((pad + rc, da), F32)],
        compiler_params=_params("arbitrary", "arbitrary"), name=name)(a_val, a_gate, a_val, a_gate, conv_w, conv_b, ln_g, ln_b)


def _mix_a_bwd(a_val, a_gate, ac, da_, conv_w, ln_g, ln_b, batch, name):
    t, da = a_val.shape
    taps = conv_w.shape[0]
    pad, rc, nch = _conv_geometry(t // batch, taps)

    def kern(av_ref, ag_ref, pav_ref, pag_ref, ac_ref, da_ref, nac_ref, nda_ref, w_ref, g_ref, b_ref,
             dav_ref, dag_ref, dw_ref, dcb_ref, dg_ref, db_ref, gpad_ref, dpad_ref):
        i = pl.program_id(1)

        @pl.when((pl.program_id(0) == 0) & (i == 0))
        def _():
            for r in (dw_ref, dcb_ref, dg_ref, db_ref):
                r[...] = jnp.zeros_like(r)

        gain, bias = g_ref[...], b_ref[...]

        def d_conv_out(ac_, dout):
            y, rstd = _ln_stats(ac_)
            aln = y * gain + bias
            sig = _sigmoid(aln)
            daln = dout * (sig * (1.0 + aln * (1.0 - sig)))
            return _ln_bwd(daln, y, rstd, gain), daln, y

        dac, daln, y = d_conv_out(ac_ref[...], da_ref[...])
        dg_ref[...] += _colsum(daln * y)
        db_ref[...] += _colsum(daln)
        dcb_ref[...] += _colsum(dac)

        av = av_ref[...]
        sig = _sigmoid(ag_ref[...])
        gpad_ref[0:pad, :] = jnp.where(i == 0, 0.0, pav_ref[...] * _sigmoid(pag_ref[...]))
        gpad_ref[pad:pad + rc, :] = av * sig
        dac_next, _, _ = d_conv_out(nac_ref[...], nda_ref[...])
        dpad_ref[0:rc, :] = dac
        dpad_ref[rc:rc + pad, :] = jnp.where(i == nch - 1, 0.0, dac_next)

        dgl = None
        for k in range(taps):
            off = pad - (taps - 1) + k
            dw_ref[k:k + 1, :] += _colsum(dac * gpad_ref[off:off + rc, :])
            term = dpad_ref[taps - 1 - k: taps - 1 - k + rc, :] * w_ref[k:k + 1, :]
            dgl = term if dgl is None else dgl + term
        dav_ref[...] = (dgl * sig).astype(BF16)
        dag_ref[...] = (dgl * av * sig * (1.0 - sig)).astype(BF16)

    chunk, prev = _chunk_spec(rc, da, nch), _prev_halo_spec(pad, rc, da, nch)
    nxt = _next_halo_spec(pad, rc, da, nch, t)
    vec = SDS((1, da), F32)
    return pl.pallas_call(
        kern, grid=(batch, nch),
        in_specs=[chunk, chunk, prev, prev, chunk, chunk, nxt, nxt, _whole2(conv_w), _whole2(ln_g), _whole2(ln_b)],
        out_specs=[chunk, chunk, _whole2(conv_w), _whole2(vec), _whole2(vec), _whole2(vec)],
        out_shape=[SDS((t, da), BF16), SDS((t, da), BF16), SDS(conv_w.shape, F32), vec, vec, vec],
        scratch_shapes=[pltpu.VMEM((pad + rc, da), F32), pltpu.VMEM((rc + pad, da), F32)],
        compiler_params=_params("arbitrary", "arbitrary"), name=name)(
            a_val, a_gate, a_val, a_gate, ac, da_, ac, da_, conv_w, ln_g, ln_b)


def _mix_c_fwd(c_b, c_c, c_x, conv_w, batch, name):
    t, dc = c_b.shape
    taps = conv_w.shape[0]
    pad, rc, nch = _conv_geometry(t // batch, taps)

    def kern(cb_ref, cc_ref, cx_ref, pcc_ref, pcx_ref, w_ref, o_ref, pad_ref):
        pad_ref[0:pad, :] = jnp.where(pl.program_id(1) == 0, 0.0, pcc_ref[...] * pcx_ref[...])
        pad_ref[pad:pad + rc, :] = cc_ref[...] * cx_ref[...]
        o_ref[...] = (cb_ref[...] * _causal_taps(pad_ref, w_ref, taps, pad, rc)).astype(BF16)

    chunk, prev = _chunk_spec(rc, dc, nch), _prev_halo_spec(pad, rc, dc, nch)
    return pl.pallas_call(
        kern, grid=(batch, nch), in_specs=[chunk, chunk, chunk, prev, prev, _whole2(conv_w)],
        out_specs=chunk, out_shape=SDS((t, dc), BF16), scratch_shapes=[pltpu.VMEM((pad + rc, dc), F32)],
        compiler_params=_params("arbitrary", "arbitrary"), name=name)(c_b, c_c, c_x, c_c, c_x, conv_w)


def _mix_c_bwd(c_b, c_c, c_x, dco, conv_w, batch, name):
    t, dc = c_b.shape
    taps = conv_w.shape[0]
    pad, rc, nch = _conv_geometry(t // batch, taps)

    def kern(cb_ref, cc_ref, cx_ref, do_ref, pcc_ref, pcx_ref, ncb_ref, ndo_ref, w_ref,
             dcb_ref, dcc_ref, dcx_ref, dw_ref, ppad_ref, dpad_ref):
        i = pl.program_id(1)

        @pl.when((pl.program_id(0) == 0) & (i == 0))
        def _():
            dw_ref[...] = jnp.zeros_like(dw_ref)

        cc, cx, dout = cc_ref[...], cx_ref[...], do_ref[...]
        ppad_ref[0:pad, :] = jnp.where(i == 0, 0.0, pcc_ref[...] * pcx_ref[...])
        ppad_ref[pad:pad + rc, :] = cc * cx
        dcv = dout * cb_ref[...]
        dpad_ref[0:rc, :] = dcv
        dpad_ref[rc:rc + pad, :] = jnp.where(i == nch - 1, 0.0, ndo_ref[...] * ncb_ref[...])
        cv, dp = None, None
        for k in range(taps):
            off = pad - (taps - 1) + k
            shifted = ppad_ref[off:off + rc, :]
            dw_ref[k:k + 1, :] += _colsum(dcv * shifted)
            w_k = w_ref[k:k + 1, :]
            cv = shifted * w_k if cv is None else cv + shifted * w_k
            term = dpad_ref[taps - 1 - k: taps - 1 - k + rc, :] * w_k
            dp = term if dp is None else dp + term
        dcb_ref[...] = (dout * cv).astype(BF16)
        dcc_ref[...] = (dp * cx).astype(BF16)
        dcx_ref[...] = (dp * cc).astype(BF16)

    chunk, prev = _chunk_spec(rc, dc, nch), _prev_halo_spec(pad, rc, dc, nch)
    nxt = _next_halo_spec(pad, rc, dc, nch, t)
    return pl.pallas_call(
        kern, grid=(batch, nch), in_specs=[chunk, chunk, chunk, chunk, prev, prev, nxt, nxt, _whole2(conv_w)],
        out_specs=[chunk, chunk, chunk, _whole2(conv_w)],
        out_shape=[SDS((t, dc), BF16)] * 3 + [SDS(conv_w.shape, F32)],
        scratch_shapes=[pltpu.VMEM((pad + rc, dc), F32), pltpu.VMEM((rc + pad, dc), F32)],
        compiler_params=_params("arbitrary", "arbitrary"), name=name)(c_b, c_c, c_x, dco, c_c, c_x, c_b, dco, conv_w)


def _head_of_lane(db):
    return lax.broadcasted_iota(jnp.int32, (1, db), 1) // (db // B_HEADS)


def _tril(rows_ge_cols=True):
    r = lax.broadcasted_iota(jnp.int32, (CHUNK, CHUNK), 0)
    c = lax.broadcasted_iota(jnp.int32, (CHUNK, CHUNK), 1)
    return (r >= c) if rows_ge_cols else (r <= c)


def _spatial_mix(wm, vb, bias, head):
    mixed = bias
    for h in range(B_HEADS):
        mixed = mixed + jnp.where(head == h, _dot(wm[h], vb), 0.0)
    return mixed


def _mix_b_fwd(b_u, b_v, w_s, bias_full, ln_g, ln_b, name):
    t, db = b_u.shape
    rb = _tile(t, ROW_TILE)

    def kern(bu_ref, bv_ref, ws_ref, bias_ref, g_ref, b_ref, o_ref):
        head = _head_of_lane(db)
        wm = [jnp.where(_tril(), ws_ref[h], 0.0).astype(BF16) for h in range(B_HEADS)]
        for ch in range(rb // CHUNK):
            rows = slice(ch * CHUNK, (ch + 1) * CHUNK)
            y, _ = _ln_stats(_gelu(bv_ref[rows, :]))
            vb = (y * g_ref[...] + b_ref[...]).astype(BF16)
            mixed = _spatial_mix(wm, vb, bias_ref[...], head)
            o_ref[rows, :] = (_gelu(bu_ref[rows, :]) * mixed).astype(BF16)

    def whole(a):
        return pl.BlockSpec(a.shape, lambda i, nd=len(a.shape): (0,) * nd)

    tile = pl.BlockSpec((rb, db), lambda i: (i, 0))
    return pl.pallas_call(
        kern, grid=(t // rb,), in_specs=[tile, tile, whole(w_s), whole(bias_full), whole(ln_g), whole(ln_b)],
        out_specs=tile, out_shape=SDS((t, db), BF16), compiler_params=_params("arbitrary"), name=name)(
            b_u, b_v, w_s, bias_full, ln_g, ln_b)


def _mix_b_bwd(b_u, b_v, dbo, w_s, w_s_t, bias_full, ln_g, ln_b, name):
    t, db = b_u.shape
    rb = _tile(t, ROW_TILE)
    steps = t // rb

    def kern(bu_ref, bv_ref, do_ref, ws_ref, wst_ref, bias_ref, g_ref, b_ref,
             dbu_ref, dbv_ref, dws_ref, dbs_ref, dg_ref, dbeta_ref, dbias_ref):
        i = pl.program_id(0)

        @pl.when(i == 0)
        def _():
            for r in (dws_ref, dg_ref, dbeta_ref, dbias_ref):
                r[...] = jnp.zeros_like(r)

        head = _head_of_lane(db)
        gain = g_ref[...]
        wm = [jnp.where(_tril(), ws_ref[h], 0.0).astype(BF16) for h in range(B_HEADS)]
        wmt = [jnp.where(_tril(False), wst_ref[h], 0.0).astype(BF16) for h in range(B_HEADS)]
        for ch in range(rb // CHUNK):
            rows = slice(ch * CHUNK, (ch + 1) * CHUNK)
            bu, bv, dout = bu_ref[rows, :], bv_ref[rows, :], do_ref[rows, :]
            y, rstd = _ln_stats(_gelu(bv))
            vb = (y * gain + b_ref[...]).astype(BF16)
            mixed = _spatial_mix(wm, vb, bias_ref[...], head)
            du = dout * mixed
            dmixed = dout * _gelu(bu)
            dbias_ref[...] += dmixed
            dmb = dmixed.astype(BF16)
            dv = None
            for h in range(B_HEADS):
                dws_ref[h] += _dot_nt(jnp.where(head == h, dmb, 0.0).astype(BF16), vb)
                part = jnp.where(head == h, _dot(wmt[h], dmb), 0.0)
                dv = part if dv is None else dv + part
            dg_ref[...] += _colsum(dv * y)
            dbeta_ref[...] += _colsum(dv)
            dbv_ref[rows, :] = (_ln_bwd(dv, y, rstd, gain) * _gelu_grad(bv)).astype(BF16)
            dbu_ref[rows, :] = (du * _gelu_grad(bu)).astype(BF16)

        @pl.when(i == steps - 1)
        def _():
            for h in range(B_HEADS):
                dws_ref[h] = jnp.where(_tril(), dws_ref[h], 0.0)
                dbs_ref[h] = jnp.sum(jnp.where(head == h, dbias_ref[...], 0.0), axis=1, keepdims=True)

    def whole(a):
        return pl.BlockSpec(a.shape, lambda i, nd=len(a.shape): (0,) * nd)

    tile = pl.BlockSpec((rb, db), lambda i: (i, 0))
    vec = SDS((1, db), F32)
    dbs = SDS((B_HEADS, CHUNK, 1), F32)
    return pl.pallas_call(
        kern, grid=(steps,),
        in_specs=[tile, tile, tile, whole(w_s), whole(w_s_t), whole(bias_full), whole(ln_g), whole(ln_b)],
        out_specs=[tile, tile, whole(w_s), whole(dbs), whole(vec), whole(vec)],
        out_shape=[SDS((t, db), BF16), SDS((t, db), BF16), SDS(w_s.shape, F32), dbs, vec, vec],
        scratch_shapes=[pltpu.VMEM((CHUNK, db), F32)],
        compiler_params=_params("arbitrary"), name=name)(b_u, b_v, dbo, w_s, w_s_t, bias_full, ln_g, ln_b)


def _softmax_rows(s):
    e = jnp.exp(s - jnp.max(s, axis=-1, keepdims=True))
    return e / jnp.sum(e, axis=-1, keepdims=True)


def _attention_fwd(q, kv, batch, name):
    t, d = q.shape
    seq, mlen, hd = t // batch, kv.shape[0] // batch, d // X_HEADS
    ar = _tile(seq, ATT_ROWS)
    scale = hd ** -0.5

    def kern(q_ref, k_ref, v_ref, o_ref):
        k, v = k_ref[...], v_ref[...]
        for r0 in range(0, seq, ar):
            p = _softmax_rows(_dot_nt(q_ref[r0:r0 + ar, :], k) * scale)
            o_ref[r0:r0 + ar, :] = _dot(p.astype(BF16), v).astype(BF16)

    qs = pl.BlockSpec((seq, hd), lambda b, h: (b, h))
    return pl.pallas_call(
        kern, grid=(batch, X_HEADS),
        in_specs=[qs, pl.BlockSpec((mlen, hd), lambda b, h: (b, h)), pl.BlockSpec((mlen, hd), lambda b, h: (b, X_HEADS + h))],
        out_specs=qs, out_shape=SDS((t, d), BF16), compiler_params=_params("arbitrary", "arbitrary"), name=name)(q, kv, kv)


def _attention_bwd(q, kv, do, batch, name):
    t, d = q.shape
    seq, mlen, hd = t // batch, kv.shape[0] // batch, d // X_HEADS
    ar = _tile(seq, ATT_ROWS)
    scale = hd ** -0.5

    def kern(q_ref, k_ref, v_ref, do_ref, dq_ref, dk_ref, dv_ref):
        k, v = k_ref[...], v_ref[...]
        dk = jnp.zeros((mlen, hd), F32)
        dv = jnp.zeros((mlen, hd), F32)
        for r0 in range(0, seq, ar):
            qr, dor = q_ref[r0:r0 + ar, :], do_ref[r0:r0 + ar, :]
            p = _softmax_rows(_dot_nt(qr, k) * scale)
            dp = _dot_nt(dor, v)
            ds = (p * (dp - jnp.sum(p * dp, axis=-1, keepdims=True)) * scale).astype(BF16)
            dq_ref[r0:r0 + ar, :] = _dot(ds, k).astype(BF16)
            dk = dk + _dot_tn(ds, qr)
            dv = dv + _dot_tn(p.astype(BF16), dor)
        dk_ref[...] = dk.astype(BF16)
        dv_ref[...] = dv.astype(BF16)

    qs = pl.BlockSpec((seq, hd), lambda b, h: (b, h))
    ks = pl.BlockSpec((mlen, hd), lambda b, h: (b, h))
    dkv = SDS((kv.shape[0], d), BF16)
    return pl.pallas_call(
        kern, grid=(batch, X_HEADS),
        in_specs=[qs, ks, pl.BlockSpec((mlen, hd), lambda b, h: (b, X_HEADS + h)), qs],
        out_specs=[qs, ks, ks], out_shape=[SDS((t, d), BF16), dkv, dkv],
        compiler_params=_params("arbitrary", "arbitrary"), name=name)(q, kv, kv, do)


def _place():
    x, y, c = lax.axis_index("x"), lax.axis_index("y"), lax.axis_index("c")
    other_chips = [(1 - x, y), (x, 1 - y), (1 - x, 1 - y)]
    return x, y, c, other_chips


def _comm_call(kern, out_shape, n_pairs, name, *args):
    return pl.pallas_call(
        kern, out_shape=out_shape, in_specs=[ANY] * len(args), out_specs=jax.tree.map(lambda _: ANY, out_shape),
        scratch_shapes=[pltpu.SemaphoreType.DMA((n_pairs,)), pltpu.SemaphoreType.DMA((n_pairs,)), pltpu.SemaphoreType.DMA(())],
        name=name)(*args)


def _all_gather_chips(shard, name):
    r, cols = shard.shape
    rh = r // 2

    def kern(s_ref, o_ref, send_sems, recv_sems, local_sem):
        x, y, c, chips = _place()
        mine_slot = 2 * x + y
        half = pl.ds(c * rh, rh)
        other_half = pl.ds((1 - c) * rh, rh)

        def copy(k, src, dst, to):
            return pltpu.make_async_remote_copy(src_ref=src, dst_ref=dst, send_sem=send_sems.at[k], recv_sem=recv_sems.at[k],
                                                device_id=to, device_id_type=MESH)

        mine = pltpu.make_async_copy(s_ref, o_ref.at[mine_slot], local_sem)
        mine.start()
        first = [copy(j, s_ref.at[half], o_ref.at[mine_slot, half], (px, py, c)) for j, (px, py) in enumerate(chips)]
        for cp in first:
            cp.start()
        passed = []
        for j, (px, py) in enumerate(chips):
            landed = o_ref.at[2 * px + py, half]
            copy(j, landed, landed, (px, py, c)).wait_recv()
            cp = copy(3 + j, landed, landed, (x, y, 1 - c))
            cp.start()
            passed.append(cp)
        for j, (px, py) in enumerate(chips):
            theirs = o_ref.at[2 * px + py, other_half]
            copy(3 + j, theirs, theirs, (x, y, 1 - c)).wait_recv()
        for cp in first + passed:
            cp.wait_send()
        mine.wait()

    return _comm_call(kern, SDS((N_CHIPS, r, cols), shard.dtype), 6, name, shard)


def _swap_halves(full, name):
    n, r, cols = full.shape
    rh = r // 2

    def kern(g_ref, o_ref, send_sems, recv_sems, local_sem):
        x, y, c, _ = _place()
        cp = pltpu.make_async_remote_copy(src_ref=g_ref.at[:, pl.ds((1 - c) * rh, rh), :], dst_ref=o_ref,
                                          send_sem=send_sems.at[0], recv_sem=recv_sems.at[0],
                                          device_id=(x, y, 1 - c), device_id_type=MESH)
        cp.start()
        cp.wait()

    return _comm_call(kern, SDS((n, rh, cols), full.dtype), 1, name, full)


def _send_to_owners(partial, name):
    n, rh, cols = partial.shape

    def kern(s_ref, o_ref, send_sems, recv_sems, local_sem):
        x, y, c, chips = _place()
        copies = [pltpu.make_async_remote_copy(src_ref=s_ref.at[2 * px + py], dst_ref=o_ref.at[j],
                                               send_sem=send_sems.at[j], recv_sem=recv_sems.at[j],
                                               device_id=(px, py, c), device_id_type=MESH)
                  for j, (px, py) in enumerate(chips)]
        for cp in copies:
            cp.start()
        for cp in copies:
            cp.wait()

    return _comm_call(kern, SDS((N_CHIPS - 1, rh, cols), partial.dtype), 3, name, partial)


def _join_halves(mine, name):
    rh, cols = mine.shape

    def kern(h_ref, o_ref, send_sems, recv_sems, local_sem):
        x, y, c, _ = _place()
        rows = o_ref.at[pl.ds(c * rh, rh)]
        keep = pltpu.make_async_copy(h_ref, rows, local_sem)
        keep.start()
        cp = pltpu.make_async_remote_copy(src_ref=h_ref, dst_ref=rows, send_sem=send_sems.at[0], recv_sem=recv_sems.at[0],
                                          device_id=(x, y, 1 - c), device_id_type=MESH)
        cp.start()
        theirs = o_ref.at[pl.ds((1 - c) * rh, rh)]
        pltpu.make_async_remote_copy(src_ref=theirs, dst_ref=theirs, send_sem=send_sems.at[0], recv_sem=recv_sems.at[0],
                                     device_id=(x, y, 1 - c), device_id_type=MESH).wait_recv()
        cp.wait_send()
        keep.wait()

    return _comm_call(kern, SDS((2 * rh, cols), mine.dtype), 1, name, mine)


def _add_sibling(full, theirs, out_dtype, name):
    n, r, cols = full.shape
    rh = r // 2
    tr = _tile(rh, ROW_TILE)
    nb = rh // tr

    def kern(c_ref, a_ref, b_ref, o_ref):
        o_ref[...] = (a_ref[...] + b_ref[...]).astype(out_dtype)

    c = lax.axis_index("c").astype(jnp.int32).reshape(1)
    return pl.pallas_call(
        kern, out_shape=SDS((n, rh, cols), out_dtype),
        grid_spec=pltpu.PrefetchScalarGridSpec(
            num_scalar_prefetch=1, grid=(n, nb),
            in_specs=[pl.BlockSpec((1, tr, cols), lambda j, i, c_ref: (j, c_ref[0] * nb + i, 0)),
                      pl.BlockSpec((1, tr, cols), lambda j, i, c_ref: (j, i, 0))],
            out_specs=pl.BlockSpec((1, tr, cols), lambda j, i, c_ref: (j, i, 0))),
        compiler_params=_params("arbitrary", "arbitrary"), name=name)(c, full, theirs)


def _add_owners(partial, received, name):
    n, rh, cols = partial.shape
    tr = _tile(rh, ROW_TILE)

    def kern(s_ref, a_ref, b_ref, o_ref):
        acc = a_ref[0].astype(F32)
        for j in range(N_CHIPS - 1):
            acc = acc + b_ref[j].astype(F32)
        o_ref[...] = acc

    slot = (2 * lax.axis_index("x") + lax.axis_index("y")).astype(jnp.int32).reshape(1)
    return pl.pallas_call(
        kern, out_shape=SDS((rh, cols), F32),
        grid_spec=pltpu.PrefetchScalarGridSpec(
            num_scalar_prefetch=1, grid=(rh // tr,),
            in_specs=[pl.BlockSpec((1, tr, cols), lambda i, s_ref: (s_ref[0], i, 0)),
                      pl.BlockSpec((N_CHIPS - 1, tr, cols), lambda i, s_ref: (0, i, 0))],
            out_specs=pl.BlockSpec((tr, cols), lambda i, s_ref: (i, 0))),
        compiler_params=_params("arbitrary"), name=name)(slot, partial, received)


def _reduce_scatter(full, wire_dtype, name):
    theirs = _swap_halves(full, name + "_swap")
    chip_sum = _add_sibling(full, theirs, wire_dtype, name + "_add_sibling")
    received = _send_to_owners(chip_sum, name + "_send")
    mine = _add_owners(chip_sum, received, name + "_add_owners")
    return _join_halves(mine, name + "_join")


def _pack(parts, lead):
    lead_shape = parts[0].shape[:lead]
    flat = jnp.concatenate([p.reshape(lead_shape + (-1,)) for p in parts], axis=-1)
    n = flat.shape[-1]
    rows = _round_up(-(-n // PACK_W), PACK_ROWS)
    flat = jnp.pad(flat, [(0, 0)] * lead + [(0, rows * PACK_W - n)])
    return flat.reshape(lead_shape + (rows, PACK_W))


def _unpack(buf, shapes, lead):
    lead_shape = buf.shape[:lead]
    flat = buf.reshape(lead_shape + (-1,))
    out, off = [], 0
    for s in shapes:
        n = math.prod(s)
        out.append(flat[..., off:off + n].reshape(lead_shape + tuple(s)))
        off += n
    return out


def _to_slots(full, axis):
    s = full.shape
    split = full.reshape(s[:axis] + (N_CHIPS, s[axis] // N_CHIPS) + s[axis + 1:])
    return jnp.moveaxis(split, axis, 0)


def _from_slots(slots, axis):
    moved = jnp.moveaxis(slots, 0, axis)
    s = moved.shape
    return moved.reshape(s[:axis] + (s[axis] * s[axis + 1],) + s[axis + 2:])


MATMUL_WEIGHTS = (("w_in", 1), ("w_out", 0), ("w_q", 0), ("w_kv", 1), ("w_o", 0), ("w_ff1", 1), ("w_ff2", 0))
CONV_WEIGHTS = (("conv_a_w", 1), ("conv_c_w", 1))
REPLICATED = ("conv_a_b", "ln_a_g", "ln_a_b", "ln_v_g", "ln_v_b", "w_s", "b_s",
              "ln1_g", "ln1_b", "ln2_g", "ln2_b", "ln3_g", "ln3_b")
WEIGHT_ORDER = ("w_in", "conv_a_w", "conv_a_b", "ln_a_g", "ln_a_b", "ln_v_g", "ln_v_b", "w_s", "b_s", "conv_c_w", "w_out",
                "ln1_g", "ln1_b", "w_q", "w_kv", "w_o", "ln2_g", "ln2_b", "w_ff1", "w_ff2", "ln3_g", "ln3_b")


def _row(v):
    return v.reshape(1, -1)


def kernel(x, mem, w_in, conv_a_w, conv_a_b, ln_a_g, ln_a_b, ln_v_g, ln_v_b, w_s, b_s, conv_c_w, w_out, ln1_g, ln1_b, w_q, w_kv, w_o, ln2_g, ln2_b, w_ff1, w_ff2, ln3_g, ln3_b, loss_target, m_w_in, m_conv_a_w, m_conv_a_b, m_ln_a_g, m_ln_a_b, m_ln_v_g, m_ln_v_b, m_w_s, m_b_s, m_conv_c_w, m_w_out, m_ln1_g, m_ln1_b, m_w_q, m_w_kv, m_w_o, m_ln2_g, m_ln2_b, m_w_ff1, m_w_ff2, m_ln3_g, m_ln3_b, v_w_in, v_conv_a_w, v_conv_a_b, v_ln_a_g, v_ln_a_b, v_ln_v_g, v_ln_v_b, v_w_s, v_b_s, v_conv_c_w, v_w_out, v_ln1_g, v_ln1_b, v_w_q, v_w_kv, v_w_o, v_ln2_g, v_ln2_b, v_w_ff1, v_w_ff2, v_ln3_g, v_ln3_b):
    given = dict(locals())
    weights = {n: given[n] for n in WEIGHT_ORDER}
    moment1 = {n: given["m_" + n] for n in WEIGHT_ORDER}
    moment2 = {n: given["v_" + n] for n in WEIGHT_ORDER}

    depth = w_in.shape[0]
    batch, seq, d = x.shape
    t = batch * seq
    hd = d // HEADS
    d_a, d_b, d_c = A_HEADS * hd, B_HEADS * hd, C_HEADS * hd
    widths = (d_a, d_a, d_b, d_b, d_c, d_c, d_c)
    in_offs = [sum(widths[:k]) for k in range(len(widths))]
    alpha = (2.0 * depth) ** 0.25
    layers = range(depth)
    tm = _tile(t, ROW_TILE)
    tm_wide = _tile(t, WIDE_ROW_TILE)

    mm_shards = [weights[n][l].astype(BF16) for l in layers for n, _ in MATMUL_WEIGHTS]
    mm_all = _all_gather_chips(_pack(mm_shards, 0), "gather_matmul_weights")
    mm_full = _unpack(mm_all, [s.shape for s in mm_shards], 1)
    conv_shards = [weights[n][l] for l in layers for n, _ in CONV_WEIGHTS]
    conv_all = _all_gather_chips(_pack(conv_shards, 0), "gather_conv_weights")
    conv_full = _unpack(conv_all, [s.shape for s in conv_shards], 1)
    full = []
    for l in layers:
        w = {}
        for k, (n, axis) in enumerate(MATMUL_WEIGHTS):
            w[n] = _from_slots(mm_full[l * len(MATMUL_WEIGHTS) + k], axis)
        for k, (n, axis) in enumerate(CONV_WEIGHTS):
            w[n] = _from_slots(conv_full[l * len(CONV_WEIGHTS) + k], axis)
        for n in REPLICATED:
            w[n] = weights[n][l]
        w["w_s_t"] = jnp.swapaxes(w["w_s"], 1, 2)
        w["bias_full"] = jnp.repeat(w["b_s"].T, hd, axis=1)
        full.append(w)

    xf = x.reshape(t, d)
    xb = xf.astype(BF16)
    memb = mem.reshape(-1, d).astype(BF16)
    saved = []
    for l in layers:
        w = full[l]
        s = {"x0b": xb}
        proj = _in_proj(xb, w["w_in"], widths, "in_proj")
        s["proj"] = proj
        a_val, a_gate, b_u, b_v, c_b, c_c, c_x = proj
        a_out, s["ac"] = _mix_a_fwd(a_val, a_gate, w["conv_a_w"], _row(w["conv_a_b"]), _row(w["ln_a_g"]), _row(w["ln_a_b"]),
                                    batch, "mix_a_fwd")
        b_out = _mix_b_fwd(b_u, b_v, w["w_s"], w["bias_full"], _row(w["ln_v_g"]), _row(w["ln_v_b"]), "mix_b_fwd")
        c_out = _mix_c_fwd(c_b, c_c, c_x, w["conv_c_w"], batch, "mix_c_fwd")
        s["cat"] = (a_out, b_out, c_out)
        s["z1"], xf, xb = _proj_ln([a_out, b_out, c_out], w["w_out"], xf, _row(w["ln1_g"]), _row(w["ln1_b"]), alpha, tm, "out_proj_ln")
        s["x1b"] = xb
        s["q"] = _matmul(xb, w["w_q"], "q_proj")
        s["kv"] = _matmul(memb, w["w_kv"], "kv_proj")
        s["o"] = _attention_fwd(s["q"], s["kv"], batch, "attention_fwd")
        s["z2"], xf, xb = _proj_ln([s["o"]], w["w_o"], xf, _row(w["ln2_g"]), _row(w["ln2_b"]), alpha, tm, "o_proj_ln")
        s["x2b"] = xb
        s["h"], s["r"] = _ff1(xb, w["w_ff1"], "ff1")
        s["z3"], xf, xb = _proj_ln([s["r"]], w["w_ff2"], xf, _row(w["ln3_g"]), _row(w["ln3_b"]), alpha, tm_wide, "ff2_ln")
        saved.append(s)

    dx, loss_block = _loss_head(xf, loss_target.reshape(t, d), "loss_head")
    loss = lax.psum(loss_block[0, 0], ("x", "y", "c"))
    grads = [None] * depth
    for l in reversed(layers):
        w, s, g = full[l], saved[l], {}
        dz3, dz3b, g["ln3_g"], g["ln3_b"] = _ln_bwd_rows(dx, s["z3"], _row(w["ln3_g"]), "ln3_bwd")
        g["w_ff2"] = _mm_tn(s["r"], dz3b, "d_w_ff2")
        dh = _bwd_ff2(dz3b, s["h"], w["w_ff2"], "d_ff_hidden")
        g["w_ff1"] = _mm_tn(s["x2b"], dh, "d_w_ff1")
        dx = _bwd_data([dh], w["w_ff1"], [0], dz3, alpha, F32, tm_wide, "d_x2")
        dz2, dz2b, g["ln2_g"], g["ln2_b"] = _ln_bwd_rows(dx, s["z2"], _row(w["ln2_g"]), "ln2_bwd")
        g["w_o"] = _mm_tn(s["o"], dz2b, "d_w_o")
        do = _bwd_data([dz2b], w["w_o"], [0], None, alpha, BF16, tm, "d_att_out")
        dq, dk, dv = _attention_bwd(s["q"], s["kv"], do, batch, "attention_bwd")
        g["w_kv"] = jnp.concatenate([_mm_tn(memb, dk, "d_w_k"), _mm_tn(memb, dv, "d_w_v")], axis=1)
        g["w_q"] = _mm_tn(s["x1b"], dq, "d_w_q")
        dx = _bwd_data([dq], w["w_q"], [0], dz2, alpha, F32, tm, "d_x1")
        dz1, dz1b, g["ln1_g"], g["ln1_b"] = _ln_bwd_rows(dx, s["z1"], _row(w["ln1_g"]), "ln1_bwd")
        g["w_out"] = jnp.concatenate([_mm_tn(part, dz1b, "d_w_out") for part in s["cat"]], axis=0)
        da_, dbo, dco = _bwd_out_proj(dz1b, w["w_out"], (d_a, d_b, d_c), "d_mixer_out")
        a_val, a_gate, b_u, b_v, c_b, c_c, c_x = s["proj"]
        dav, dag, g["conv_a_w"], g["conv_a_b"], g["ln_a_g"], g["ln_a_b"] = _mix_a_bwd(
            a_val, a_gate, s["ac"], da_, w["conv_a_w"], _row(w["ln_a_g"]), _row(w["ln_a_b"]), batch, "mix_a_bwd")
        dbu, dbv, g["w_s"], dbs, g["ln_v_g"], g["ln_v_b"] = _mix_b_bwd(
            b_u, b_v, dbo, w["w_s"], w["w_s_t"], w["bias_full"], _row(w["ln_v_g"]), _row(w["ln_v_b"]), "mix_b_bwd")
        g["b_s"] = dbs.reshape(B_HEADS, CHUNK)
        dcb, dcc, dcx = None, None, None
        dcb, dcc, dcx, g["conv_c_w"] = _mix_c_bwd(c_b, c_c, c_x, dco, w["conv_c_w"], batch, "mix_c_bwd")
        dproj = [dav, dag, dbu, dbv, dcb, dcc, dcx]
        g["w_in"] = jnp.concatenate([_mm_tn(s["x0b"], dp, "d_w_in") for dp in dproj], axis=1)
        dx = _bwd_data(dproj, w["w_in"], in_offs, dz1, alpha, F32, tm, "d_x0")
        grads[l] = g
    grad_x = dx.reshape(batch, seq, d)

    mm_slots = [_to_slots(grads[l][n], axis) for l in layers for n, axis in MATMUL_WEIGHTS]
    mm_grad = _reduce_scatter(_pack(mm_slots, 1), BF16, "reduce_matmul_grads")
    mm_grad = _unpack(mm_grad, [s.shape[1:] for s in mm_slots], 0)

    conv_slots = [_to_slots(grads[l][n], axis) for l in layers for n, axis in CONV_WEIGHTS]
    rep_parts = [jnp.stack([grads[l][n].reshape(weights[n].shape[1:]) for l in layers]) for n in REPLICATED]
    rep_flat = jnp.concatenate([p.reshape(-1) for p in rep_parts])
    n_rep = rep_flat.shape[0]
    per_chip = _round_up(-(-n_rep // N_CHIPS), PACK_W * PACK_ROWS)
    rep_slots = jnp.pad(rep_flat, (0, N_CHIPS * per_chip - n_rep)).reshape(N_CHIPS, per_chip)
    small = _reduce_scatter(_pack(conv_slots + [rep_slots], 1), F32, "reduce_small_grads")
    small = _unpack(small, [s.shape[1:] for s in conv_slots] + [(per_chip,)], 0)
    conv_grad, rep_mine = small[:-1], small[-1]
    rep_all = _all_gather_chips(rep_mine.reshape(-1, PACK_W), "gather_small_grads").reshape(-1)[:n_rep]

    grad = {}
    for k, (n, _) in enumerate(MATMUL_WEIGHTS):
        grad[n] = jnp.stack([mm_grad[l * len(MATMUL_WEIGHTS) + k] for l in layers])
    for k, (n, _) in enumerate(CONV_WEIGHTS):
        grad[n] = jnp.stack([conv_grad[l * len(CONV_WEIGHTS) + k] for l in layers])
    off = 0
    for n in REPLICATED:
        size = math.prod(weights[n].shape)
        grad[n] = rep_all[off:off + size].reshape(weights[n].shape)
        off += size

    delta, new_m, new_v = {}, {}, {}
    for n, _ in MATMUL_WEIGHTS:
        shape = weights[n].shape
        as_rows = lambda a: a.reshape(-1, shape[-1])
        delta[n], new_m[n], new_v[n] = (
            r.reshape(shape) for r in _adamw(as_rows(weights[n]), as_rows(grad[n]), as_rows(moment1[n]), as_rows(moment2[n]), "adamw"))
    small_names = [n for n, _ in CONV_WEIGHTS] + list(REPLICATED)
    small_shapes = [weights[n].shape for n in small_names]
    packed = [_pack([src[n] for n in small_names], 0) for src in (weights, grad, moment1, moment2)]
    for dst, res in zip((delta, new_m, new_v), _adamw(*packed, "adamw_small")):
        for n, a in zip(small_names, _unpack(res, small_shapes, 0)):
            dst[n] = a

    return (loss, grad_x, *[grad[n] for n in WEIGHT_ORDER], *[delta[n] for n in WEIGHT_ORDER],
            *[new_m[n] for n in WEIGHT_ORDER], *[new_v[n] for n in WEIGHT_ORDER])
```

```python
import functools
import math

import jax
import jax.numpy as jnp
from jax import lax
from jax.experimental import pallas as pl
from jax.experimental.pallas import tpu as pltpu

F32 = jnp.float32
BF16 = jnp.bfloat16
SDS = jax.ShapeDtypeStruct

HEADS = 16
A_HEADS, B_HEADS, C_HEADS = 6, 4, 6
X_HEADS = 4
CHUNK = 128
LN_EPS = 1e-5
ADAM_LR, ADAM_B1, ADAM_B2, ADAM_EPS, ADAM_WD, ADAM_STEP = 0.001, 0.9, 0.999, 1e-08, 0.01, 10

N_CHIPS = 4
V7X_VMEM_LIMIT = 56 << 20
SUBLANES = 8
PACK_W = 1024
PACK_ROWS = 32
ROW_TILE = 512
WIDE_ROW_TILE = 256
CONV_ROWS = 256
ATT_ROWS = 512
TN_TILE = 1024
MESH = pl.DeviceIdType.MESH
ANY = pl.BlockSpec(memory_space=pl.ANY)


def _tile(n, t):
    for d in range(min(n, t), 0, -1):
        if n % d == 0 and d % (2 * SUBLANES) == 0:
            return d
    return n


def _round_up(n, m):
    return -(-n // m) * m


def _params(*sem):
    return pltpu.CompilerParams(dimension_semantics=sem or None, vmem_limit_bytes=V7X_VMEM_LIMIT)


def _dot(a, b):
    return jnp.dot(a, b, preferred_element_type=F32)


def _dot_nt(a, b):
    return lax.dot_general(a, b, (((1,), (1,)), ((), ())), preferred_element_type=F32)


def _dot_tn(a, b):
    return lax.dot_general(a, b, (((0,), (0,)), ((), ())), preferred_element_type=F32)


def _sigmoid(x):
    return 1.0 / (1.0 + jnp.exp(-x))


def _gelu(x):
    return 0.5 * x * (1.0 + lax.erf(x * (2.0 ** -0.5)))


def _gelu_grad(x):
    return 0.5 * (1.0 + lax.erf(x * (2.0 ** -0.5))) + x * jnp.exp(-0.5 * x * x) * ((2.0 * math.pi) ** -0.5)


def _ln_stats(z):
    mu = jnp.mean(z, axis=-1, keepdims=True)
    zc = z - mu
    rstd = lax.rsqrt(jnp.mean(zc * zc, axis=-1, keepdims=True) + LN_EPS)
    return zc * rstd, rstd


def _ln_bwd(dy, y, rstd, g):
    dyh = dy * g
    return rstd * (dyh - jnp.mean(dyh, axis=-1, keepdims=True) - y * jnp.mean(dyh * y, axis=-1, keepdims=True))


def _colsum(x):
    return jnp.sum(x, axis=0, keepdims=True)


def _rowwise(body, rows, consts, outs, accs=(), *, tm, name):
    t = rows[0].shape[0]
    steps = t // tm

    def kern(*refs):
        body(pl.program_id(0), steps, *refs)

    def whole(a):
        return pl.BlockSpec(a.shape, lambda i, nd=len(a.shape): (0,) * nd)

    in_specs = [pl.BlockSpec((tm, r.shape[1]), lambda i: (i, 0)) for r in rows] + [whole(c) for c in consts]
    out_shape = [SDS((t, n), dt) for n, dt in outs] + [SDS(s, dt) for s, dt in accs]
    out_specs = [pl.BlockSpec((tm, n), lambda i: (i, 0)) for n, _ in outs] + [whole(SDS(s, dt)) for s, dt in accs]
    return pl.pallas_call(kern, grid=(steps,), in_specs=in_specs, out_specs=out_specs, out_shape=out_shape,
                          compiler_params=_params("arbitrary"), name=name)(*rows, *consts)


def _in_proj(xb, w_in, widths, name):
    offs = [sum(widths[:k]) for k in range(len(widths))]

    def body(i, steps, x_ref, w_ref, *o_refs):
        x = x_ref[...]
        for o_ref, off, n in zip(o_refs, offs, widths):
            o_ref[...] = _dot(x, w_ref[:, off:off + n])

    return _rowwise(body, [xb], [w_in], [(n, F32) for n in widths], tm=_tile(xb.shape[0], ROW_TILE), name=name)


def _matmul(ab, w, name):
    def body(i, steps, a_ref, w_ref, o_ref):
        o_ref[...] = _dot(a_ref[...], w_ref[...]).astype(BF16)

    return _rowwise(body, [ab], [w], [(w.shape[1], BF16)], tm=_tile(ab.shape[0], ROW_TILE), name=name)[0]


def _ff1(xb, w, name):
    def body(i, steps, a_ref, w_ref, h_ref, r_ref):
        h = _dot(a_ref[...], w_ref[...])
        h_ref[...] = h.astype(BF16)
        r = jnp.maximum(h, 0.0)
        r_ref[...] = (r * r).astype(BF16)

    n = w.shape[1]
    return _rowwise(body, [xb], [w], [(n, BF16), (n, BF16)], tm=_tile(xb.shape[0], WIDE_ROW_TILE), name=name)


def _proj_ln(a_list, w, x, g, b, alpha, tm, name):
    widths = [a.shape[1] for a in a_list]
    offs = [sum(widths[:k]) for k in range(len(widths))]
    na = len(a_list)

    def body(i, steps, *refs):
        a_refs, (x_ref, w_ref, g_ref, b_ref, z_ref, xn_ref, xb_ref) = refs[:na], refs[na:]
        acc = alpha * x_ref[...]
        for a_ref, off, n in zip(a_refs, offs, widths):
            acc = acc + _dot(a_ref[...], w_ref[off:off + n, :])
        z_ref[...] = acc
        y, _ = _ln_stats(acc)
        xn = y * g_ref[...] + b_ref[...]
        xn_ref[...] = xn
        xb_ref[...] = xn.astype(BF16)

    d = w.shape[1]
    return _rowwise(body, [*a_list, x], [w, g, b], [(d, F32), (d, F32), (d, BF16)], tm=tm, name=name)


def _loss_head(y, target, name):
    d = y.shape[1]

    def body(i, steps, y_ref, t_ref, dy_ref, l_ref):
        @pl.when(i == 0)
        def _():
            l_ref[...] = jnp.zeros_like(l_ref)

        err = y_ref[...] - t_ref[...]
        dy_ref[...] = err * (1.0 / d)
        l_ref[...] += jnp.sum(err * err) * (0.5 / d)

    return _rowwise(body, [y, target], [], [(d, F32)], [((SUBLANES, 128), F32)], tm=_tile(y.shape[0], ROW_TILE), name=name)


def _ln_bwd_rows(dy, z, g, name):
    d = z.shape[1]

    def body(i, steps, dy_ref, z_ref, g_ref, dz_ref, dzb_ref, dg_ref, db_ref):
        @pl.when(i == 0)
        def _():
            dg_ref[...] = jnp.zeros_like(dg_ref)
            db_ref[...] = jnp.zeros_like(db_ref)

        dy_ = dy_ref[...]
        y, rstd = _ln_stats(z_ref[...])
        dz = _ln_bwd(dy_, y, rstd, g_ref[...])
        dz_ref[...] = dz
        dzb_ref[...] = dz.astype(BF16)
        dg_ref[...] += _colsum(dy_ * y)
        db_ref[...] += _colsum(dy_)

    return _rowwise(body, [dy, z], [g], [(d, F32), (d, BF16)], [((1, d), F32), ((1, d), F32)],
                    tm=_tile(z.shape[0], ROW_TILE), name=name)


def _bwd_ff2(dzb, h, w_ff2, name):
    def body(i, steps, dz_ref, h_ref, w_ref, dh_ref):
        dr = _dot_nt(dz_ref[...], w_ref[...])
        dh_ref[...] = (dr * (2.0 * jnp.maximum(h_ref[...].astype(F32), 0.0))).astype(BF16)

    return _rowwise(body, [dzb, h], [w_ff2], [(h.shape[1], BF16)], tm=_tile(h.shape[0], WIDE_ROW_TILE), name=name)[0]


def _bwd_nt(g_list, w, col_offs, res, alpha, tm, name):
    ng = len(g_list)
    widths = [g.shape[1] for g in g_list]

    def body(i, steps, *refs):
        g_refs = refs[:ng]
        if res is None:
            w_ref, o_ref = refs[ng:]
            acc = None
        else:
            r_ref, w_ref, o_ref = refs[ng:]
            acc = alpha * r_ref[...]
        for g_ref, off, n in zip(g_refs, col_offs, widths):
            part = _dot_nt(g_ref[...], w_ref[:, off:off + n])
            acc = part if acc is None else acc + part
        o_ref[...] = acc.astype(o_ref.dtype)

    rows = list(g_list) + ([] if res is None else [res])
    return rows, w, body, tm, name


def _bwd_data(g_list, w, col_offs, res, alpha, out_dtype, tm, name):
    rows, w, body, tm, name = _bwd_nt(g_list, w, col_offs, res, alpha, tm, name)
    return _rowwise(body, rows, [w], [(w.shape[0], out_dtype)], tm=tm, name=name)[0]


def _bwd_out_proj(dzb, w_out, widths, name):
    offs = [sum(widths[:k]) for k in range(len(widths))]

    def body(i, steps, dz_ref, w_ref, *o_refs):
        dz = dz_ref[...]
        for o_ref, off, n in zip(o_refs, offs, widths):
            o_ref[...] = _dot_nt(dz, w_ref[off:off + n, :])

    return _rowwise(body, [dzb], [w_out], [(n, F32) for n in widths], tm=_tile(dzb.shape[0], ROW_TILE), name=name)


def _adamw(w, g, m, v, name):
    def body(i, steps, w_ref, g_ref, m_ref, v_ref, d_ref, nm_ref, nv_ref):
        g_ = g_ref[...]
        nm = ADAM_B1 * m_ref[...] + (1.0 - ADAM_B1) * g_
        nv = ADAM_B2 * v_ref[...] + (1.0 - ADAM_B2) * (g_ * g_)
        m_hat = nm / (1.0 - ADAM_B1 ** ADAM_STEP)
        v_hat = nv / (1.0 - ADAM_B2 ** ADAM_STEP)
        d_ref[...] = -ADAM_LR * (m_hat / (jnp.sqrt(v_hat) + ADAM_EPS) + ADAM_WD * w_ref[...])
        nm_ref[...] = nm
        nv_ref[...] = nv

    c = w.shape[1]
    return _rowwise(body, [w, g, m, v], [], [(c, F32)] * 3, tm=_tile(w.shape[0], ROW_TILE), name=name)


def _mm_tn(a, g, name, slots=1):
    t, ka = a.shape
    n = g.shape[1]
    ta, tn, tk = _tile(ka, TN_TILE), _tile(n // slots, TN_TILE), _tile(t, TN_TILE)
    per = n // slots // tn

    def kern(a_ref, g_ref, o_ref):
        @pl.when(pl.program_id(2) == 0)
        def _():
            o_ref[...] = jnp.zeros_like(o_ref)

        o_ref[0] += _dot_tn(a_ref[...], g_ref[...])

    return pl.pallas_call(
        kern, grid=(ka // ta, n // tn, t // tk),
        in_specs=[pl.BlockSpec((tk, ta), lambda i, j, k: (k, i)), pl.BlockSpec((tk, tn), lambda i, j, k: (k, j))],
        out_specs=pl.BlockSpec((1, ta, tn), lambda i, j, k: (j // per, i, j % per)), out_shape=SDS((slots, ka, n // slots), F32),
        compiler_params=_params("arbitrary", "arbitrary", "arbitrary"), name=name)(a, g)


def _conv_geometry(seq, taps):
    pad = _round_up(taps - 1, SUBLANES)
    rc = _tile(seq, CONV_ROWS)
    assert rc % pad == 0 and seq % rc == 0
    return pad, rc, seq // rc


def _chunk_spec(rc, n, nch):
    return pl.BlockSpec((rc, n), lambda b, i: (b * nch + i, 0))


def _prev_halo_spec(pad, rc, n, nch):
    per = rc // pad
    return pl.BlockSpec((pad, n), lambda b, i: (jnp.maximum((b * nch + i) * per - 1, 0), 0))


def _next_halo_spec(pad, rc, n, nch, total_rows):
    per = rc // pad
    last = total_rows // pad - 1
    return pl.BlockSpec((pad, n), lambda b, i: (jnp.minimum((b * nch + i + 1) * per, last), 0))


def _whole2(a):
    return pl.BlockSpec(a.shape, lambda b, i, nd=len(a.shape): (0,) * nd)


def _causal_taps(pad_ref, w_ref, taps, pad, rc):
    acc = None
    for k in range(taps):
        term = pad_ref[pad - (taps - 1) + k: pad - (taps - 1) + k + rc, :] * w_ref[k:k + 1, :]
        acc = term if acc is None else acc + term
    return acc


def _mix_a_fwd(a_val, a_gate, conv_w, conv_b, ln_g, ln_b, batch, name):
    t, da = a_val.shape
    taps = conv_w.shape[0]
    pad, rc, nch = _conv_geometry(t // batch, taps)

    def kern(av_ref, ag_ref, pav_ref, pag_ref, w_ref, cb_ref, g_ref, b_ref, a_ref, ac_ref, pad_ref):
        first = pl.program_id(1) == 0
        halo = pav_ref[...] * _sigmoid(pag_ref[...])
        pad_ref[0:pad, :] = jnp.where(first, 0.0, halo)
        pad_ref[pad:pad + rc, :] = av_ref[...] * _sigmoid(ag_ref[...])
        ac = _causal_taps(pad_ref, w_ref, taps, pad, rc) + cb_ref[...]
        ac_ref[...] = ac
        y, _ = _ln_stats(ac)
        aln = y * g_ref[...] + b_ref[...]
        a_ref[...] = (aln * _sigmoid(aln)).astype(BF16)

    chunk, halo = _chunk_spec(rc, da, nch), _prev_halo_spec(pad, rc, da, nch)
    return pl.pallas_call(
        kern, grid=(batch, nch),
        in_specs=[chunk, chunk, halo, halo, _whole2(conv_w), _whole2(conv_b), _whole2(ln_g), _whole2(ln_b)],
        out_specs=[chunk, chunk], out_shape=[SDS((t, da), BF16), SDS((t, da), F32)],
        scratch_shapes=[pltpu.VMEM((pad + rc, da), F32)],
        compiler_params=_params("arbitrary", "arbitrary"), name=name)(a_val, a_gate, a_val, a_gate, conv_w, conv_b, ln_g, ln_b)


def _mix_a_bwd(a_val, a_gate, ac, da_, conv_w, ln_g, ln_b, batch, name):
    t, da = a_val.shape
    taps = conv_w.shape[0]
    pad, rc, nch = _conv_geometry(t // batch, taps)

    def kern(av_ref, ag_ref, pav_ref, pag_ref, ac_ref, da_ref, nac_ref, nda_ref, w_ref, g_ref, b_ref,
             dav_ref, dag_ref, dw_ref, dcb_ref, dg_ref, db_ref, gpad_ref, dpad_ref):
        i = pl.program_id(1)

        @pl.when((pl.program_id(0) == 0) & (i == 0))
        def _():
            for r in (dw_ref, dcb_ref, dg_ref, db_ref):
                r[...] = jnp.zeros_like(r)

        gain, bias = g_ref[...], b_ref[...]

        def d_conv_out(ac_, dout):
            y, rstd = _ln_stats(ac_)
            aln = y * gain + bias
            sig = _sigmoid(aln)
            daln = dout * (sig * (1.0 + aln * (1.0 - sig)))
            return _ln_bwd(daln, y, rstd, gain), daln, y

        dac, daln, y = d_conv_out(ac_ref[...], da_ref[...])
        dg_ref[...] += _colsum(daln * y)
        db_ref[...] += _colsum(daln)
        dcb_ref[...] += _colsum(dac)

        av = av_ref[...]
        sig = _sigmoid(ag_ref[...])
        gpad_ref[0:pad, :] = jnp.where(i == 0, 0.0, pav_ref[...] * _sigmoid(pag_ref[...]))
        gpad_ref[pad:pad + rc, :] = av * sig
        dac_next, _, _ = d_conv_out(nac_ref[...], nda_ref[...])
        dpad_ref[0:rc, :] = dac
        dpad_ref[rc:rc + pad, :] = jnp.where(i == nch - 1, 0.0, dac_next)

        dgl = None
        for k in range(taps):
            off = pad - (taps - 1) + k
            dw_ref[k:k + 1, :] += _colsum(dac * gpad_ref[off:off + rc, :])
            term = dpad_ref[taps - 1 - k: taps - 1 - k + rc, :] * w_ref[k:k + 1, :]
            dgl = term if dgl is None else dgl + term
        dav_ref[...] = (dgl * sig).astype(BF16)
        dag_ref[...] = (dgl * av * sig * (1.0 - sig)).astype(BF16)

    chunk, prev = _chunk_spec(rc, da, nch), _prev_halo_spec(pad, rc, da, nch)
    nxt = _next_halo_spec(pad, rc, da, nch, t)
    vec = SDS((1, da), F32)
    return pl.pallas_call(
        kern, grid=(batch, nch),
        in_specs=[chunk, chunk, prev, prev, chunk, chunk, nxt, nxt, _whole2(conv_w), _whole2(ln_g), _whole2(ln_b)],
        out_specs=[chunk, chunk, _whole2(conv_w), _whole2(vec), _whole2(vec), _whole2(vec)],
        out_shape=[SDS((t, da), BF16), SDS((t, da), BF16), SDS(conv_w.shape, F32), vec, vec, vec],
        scratch_shapes=[pltpu.VMEM((pad + rc, da), F32), pltpu.VMEM((rc + pad, da), F32)],
        compiler_params=_params("arbitrary", "arbitrary"), name=name)(
            a_val, a_gate, a_val, a_gate, ac, da_, ac, da_, conv_w, ln_g, ln_b)


def _mix_c_fwd(c_b, c_c, c_x, conv_w, batch, name):
    t, dc = c_b.shape
    taps = conv_w.shape[0]
    pad, rc, nch = _conv_geometry(t // batch, taps)

    def kern(cb_ref, cc_ref, cx_ref, pcc_ref, pcx_ref, w_ref, o_ref, pad_ref):
        pad_ref[0:pad, :] = jnp.where(pl.program_id(1) == 0, 0.0, pcc_ref[...] * pcx_ref[...])
        pad_ref[pad:pad + rc, :] = cc_ref[...] * cx_ref[...]
        o_ref[...] = (cb_ref[...] * _causal_taps(pad_ref, w_ref, taps, pad, rc)).astype(BF16)

    chunk, prev = _chunk_spec(rc, dc, nch), _prev_halo_spec(pad, rc, dc, nch)
    return pl.pallas_call(
        kern, grid=(batch, nch), in_specs=[chunk, chunk, chunk, prev, prev, _whole2(conv_w)],
        out_specs=chunk, out_shape=SDS((t, dc), BF16), scratch_shapes=[pltpu.VMEM((pad + rc, dc), F32)],
        compiler_params=_params("arbitrary", "arbitrary"), name=name)(c_b, c_c, c_x, c_c, c_x, conv_w)


def _mix_c_bwd(c_b, c_c, c_x, dco, conv_w, batch, name):
    t, dc = c_b.shape
    taps = conv_w.shape[0]
    pad, rc, nch = _conv_geometry(t // batch, taps)

    def kern(cb_ref, cc_ref, cx_ref, do_ref, pcc_ref, pcx_ref, ncb_ref, ndo_ref, w_ref,
             dcb_ref, dcc_ref, dcx_ref, dw_ref, ppad_ref, dpad_ref):
        i = pl.program_id(1)

        @pl.when((pl.program_id(0) == 0) & (i == 0))
        def _():
            dw_ref[...] = jnp.zeros_like(dw_ref)

        cc, cx, dout = cc_ref[...], cx_ref[...], do_ref[...]
        ppad_ref[0:pad, :] = jnp.where(i == 0, 0.0, pcc_ref[...] * pcx_ref[...])
        ppad_ref[pad:pad + rc, :] = cc * cx
        dcv = dout * cb_ref[...]
        dpad_ref[0:rc, :] = dcv
        dpad_ref[rc:rc + pad, :] = jnp.where(i == nch - 1, 0.0, ndo_ref[...] * ncb_ref[...])
        cv, dp = None, None
        for k in range(taps):
            off = pad - (taps - 1) + k
            shifted = ppad_ref[off:off + rc, :]
            dw_ref[k:k + 1, :] += _colsum(dcv * shifted)
            w_k = w_ref[k:k + 1, :]
            cv = shifted * w_k if cv is None else cv + shifted * w_k
            term = dpad_ref[taps - 1 - k: taps - 1 - k + rc, :] * w_k
            dp = term if dp is None else dp + term
        dcb_ref[...] = (dout * cv).astype(BF16)
        dcc_ref[...] = (dp * cx).astype(BF16)
        dcx_ref[...] = (dp * cc).astype(BF16)

    chunk, prev = _chunk_spec(rc, dc, nch), _prev_halo_spec(pad, rc, dc, nch)
    nxt = _next_halo_spec(pad, rc, dc, nch, t)
    return pl.pallas_call(
        kern, grid=(batch, nch), in_specs=[chunk, chunk, chunk, chunk, prev, prev, nxt, nxt, _whole2(conv_w)],
        out_specs=[chunk, chunk, chunk, _whole2(conv_w)],
        out_shape=[SDS((t, dc), BF16)] * 3 + [SDS(conv_w.shape, F32)],
        scratch_shapes=[pltpu.VMEM((pad + rc, dc), F32), pltpu.VMEM((rc + pad, dc), F32)],
        compiler_params=_params("arbitrary", "arbitrary"), name=name)(c_b, c_c, c_x, dco, c_c, c_x, c_b, dco, conv_w)


def _head_of_lane(db):
    return lax.broadcasted_iota(jnp.int32, (1, db), 1) // (db // B_HEADS)


def _tril(rows_ge_cols=True):
    r = lax.broadcasted_iota(jnp.int32, (CHUNK, CHUNK), 0)
    c = lax.broadcasted_iota(jnp.int32, (CHUNK, CHUNK), 1)
    return (r >= c) if rows_ge_cols else (r <= c)


def _spatial_mix(wm, vb, bias, head):
    mixed = bias
    for h in range(B_HEADS):
        mixed = mixed + jnp.where(head == h, _dot(wm[h], vb), 0.0)
    return mixed


def _mix_b_fwd(b_u, b_v, w_s, bias_full, ln_g, ln_b, name):
    t, db = b_u.shape
    rb = _tile(t, ROW_TILE)

    def kern(bu_ref, bv_ref, ws_ref, bias_ref, g_ref, b_ref, o_ref):
        head = _head_of_lane(db)
        wm = [jnp.where(_tril(), ws_ref[h], 0.0).astype(BF16) for h in range(B_HEADS)]
        for ch in range(rb // CHUNK):
            rows = slice(ch * CHUNK, (ch + 1) * CHUNK)
            y, _ = _ln_stats(_gelu(bv_ref[rows, :]))
            vb = (y * g_ref[...] + b_ref[...]).astype(BF16)
            mixed = _spatial_mix(wm, vb, bias_ref[...], head)
            o_ref[rows, :] = (_gelu(bu_ref[rows, :]) * mixed).astype(BF16)

    def whole(a):
        return pl.BlockSpec(a.shape, lambda i, nd=len(a.shape): (0,) * nd)

    tile = pl.BlockSpec((rb, db), lambda i: (i, 0))
    return pl.pallas_call(
        kern, grid=(t // rb,), in_specs=[tile, tile, whole(w_s), whole(bias_full), whole(ln_g), whole(ln_b)],
        out_specs=tile, out_shape=SDS((t, db), BF16), compiler_params=_params("arbitrary"), name=name)(
            b_u, b_v, w_s, bias_full, ln_g, ln_b)


def _mix_b_bwd(b_u, b_v, dbo, w_s, w_s_t, bias_full, ln_g, ln_b, name):
    t, db = b_u.shape
    rb = _tile(t, ROW_TILE)
    steps = t // rb

    def kern(bu_ref, bv_ref, do_ref, ws_ref, wst_ref, bias_ref, g_ref, b_ref,
             dbu_ref, dbv_ref, dws_ref, dbs_ref, dg_ref, dbeta_ref, dbias_ref):
        i = pl.program_id(0)

        @pl.when(i == 0)
        def _():
            for r in (dws_ref, dg_ref, dbeta_ref, dbias_ref):
                r[...] = jnp.zeros_like(r)

        head = _head_of_lane(db)
        gain = g_ref[...]
        wm = [jnp.where(_tril(), ws_ref[h], 0.0).astype(BF16) for h in range(B_HEADS)]
        wmt = [jnp.where(_tril(False), wst_ref[h], 0.0).astype(BF16) for h in range(B_HEADS)]
        for ch in range(rb // CHUNK):
            rows = slice(ch * CHUNK, (ch + 1) * CHUNK)
            bu, bv, dout = bu_ref[rows, :], bv_ref[rows, :], do_ref[rows, :]
            y, rstd = _ln_stats(_gelu(bv))
            vb = (y * gain + b_ref[...]).astype(BF16)
            mixed = _spatial_mix(wm, vb, bias_ref[...], head)
            du = dout * mixed
            dmixed = dout * _gelu(bu)
            dbias_ref[...] += dmixed
            dmb = dmixed.astype(BF16)
            dv = None
            for h in range(B_HEADS):
                dws_ref[h] += _dot_nt(jnp.where(head == h, dmb, 0.0).astype(BF16), vb)
                part = jnp.where(head == h, _dot(wmt[h], dmb), 0.0)
                dv = part if dv is None else dv + part
            dg_ref[...] += _colsum(dv * y)
            dbeta_ref[...] += _colsum(dv)
            dbv_ref[rows, :] = (_ln_bwd(dv, y, rstd, gain) * _gelu_grad(bv)).astype(BF16)
            dbu_ref[rows, :] = (du * _gelu_grad(bu)).astype(BF16)

        @pl.when(i == steps - 1)
        def _():
            for h in range(B_HEADS):
                dws_ref[h] = jnp.where(_tril(), dws_ref[h], 0.0)
                dbs_ref[h] = jnp.sum(jnp.where(head == h, dbias_ref[...], 0.0), axis=1, keepdims=True)

    def whole(a):
        return pl.BlockSpec(a.shape, lambda i, nd=len(a.shape): (0,) * nd)

    tile = pl.BlockSpec((rb, db), lambda i: (i, 0))
    vec = SDS((1, db), F32)
    dbs = SDS((B_HEADS, CHUNK, 1), F32)
    return pl.pallas_call(
        kern, grid=(steps,),
        in_specs=[tile, tile, tile, whole(w_s), whole(w_s_t), whole(bias_full), whole(ln_g), whole(ln_b)],
        out_specs=[tile, tile, whole(w_s), whole(dbs), whole(vec), whole(vec)],
        out_shape=[SDS((t, db), BF16), SDS((t, db), BF16), SDS(w_s.shape, F32), dbs, vec, vec],
        scratch_shapes=[pltpu.VMEM((CHUNK, db), F32)],
        compiler_params=_params("arbitrary"), name=name)(b_u, b_v, dbo, w_s, w_s_t, bias_full, ln_g, ln_b)


def _softmax_rows(s):
    e = jnp.exp(s - jnp.max(s, axis=-1, keepdims=True))
    return e / jnp.sum(e, axis=-1, keepdims=True)


def _attention_fwd(q, kv, batch, name):
    t, d = q.shape
    seq, mlen, hd = t // batch, kv.shape[0] // batch, d // X_HEADS
    ar = _tile(seq, ATT_ROWS)
    scale = hd ** -0.5

    def kern(q_ref, k_ref, v_ref, o_ref):
        k, v = k_ref[...], v_ref[...]
        for r0 in range(0, seq, ar):
            p = _softmax_rows(_dot_nt(q_ref[r0:r0 + ar, :], k) * scale)
            o_ref[r0:r0 + ar, :] = _dot(p.astype(BF16), v).astype(BF16)

    qs = pl.BlockSpec((seq, hd), lambda b, h: (b, h))
    return pl.pallas_call(
        kern, grid=(batch, X_HEADS),
        in_specs=[qs, pl.BlockSpec((mlen, hd), lambda b, h: (b, h)), pl.BlockSpec((mlen, hd), lambda b, h: (b, X_HEADS + h))],
        out_specs=qs, out_shape=SDS((t, d), BF16), compiler_params=_params("arbitrary", "arbitrary"), name=name)(q, kv, kv)


def _attention_bwd(q, kv, do, batch, name):
    t, d = q.shape
    seq, mlen, hd = t // batch, kv.shape[0] // batch, d // X_HEADS
    ar = _tile(seq, ATT_ROWS)
    scale = hd ** -0.5

    def kern(q_ref, k_ref, v_ref, do_ref, dq_ref, dk_ref, dv_ref):
        k, v = k_ref[...], v_ref[...]
        dk = jnp.zeros((mlen, hd), F32)
        dv = jnp.zeros((mlen, hd), F32)
        for r0 in range(0, seq, ar):
            qr, dor = q_ref[r0:r0 + ar, :], do_ref[r0:r0 + ar, :]
            p = _softmax_rows(_dot_nt(qr, k) * scale)
            dp = _dot_nt(dor, v)
            ds = (p * (dp - jnp.sum(p * dp, axis=-1, keepdims=True)) * scale).astype(BF16)
            dq_ref[r0:r0 + ar, :] = _dot(ds, k).astype(BF16)
            dk = dk + _dot_tn(ds, qr)
            dv = dv + _dot_tn(p.astype(BF16), dor)
        dk_ref[...] = dk.astype(BF16)
        dv_ref[...] = dv.astype(BF16)

    qs = pl.BlockSpec((seq, hd), lambda b, h: (b, h))
    ks = pl.BlockSpec((mlen, hd), lambda b, h: (b, h))
    dkv = SDS((kv.shape[0], d), BF16)
    return pl.pallas_call(
        kern, grid=(batch, X_HEADS),
        in_specs=[qs, ks, pl.BlockSpec((mlen, hd), lambda b, h: (b, X_HEADS + h)), qs],
        out_specs=[qs, ks, ks], out_shape=[SDS((t, d), BF16), dkv, dkv],
        compiler_params=_params("arbitrary", "arbitrary"), name=name)(q, kv, kv, do)


def _place():
    x, y, c = lax.axis_index("x"), lax.axis_index("y"), lax.axis_index("c")
    other_chips = [(1 - x, y), (x, 1 - y), (1 - x, 1 - y)]
    return x, y, c, other_chips


def _comm_call(kern, out_shape, n_pairs, name, *args):
    return pl.pallas_call(
        kern, out_shape=out_shape, in_specs=[ANY] * len(args), out_specs=jax.tree.map(lambda _: ANY, out_shape),
        scratch_shapes=[pltpu.SemaphoreType.DMA((n_pairs,)), pltpu.SemaphoreType.DMA((n_pairs,)), pltpu.SemaphoreType.DMA((n_pairs,))],
        name=name)(*args)


def _all_gather_chips(shard, name):
    r, cols = shard.shape
    rh = r // 2

    def kern(s_ref, o_ref, send_sems, recv_sems, local_sems):
        x, y, c, chips = _place()
        mine_slot = 2 * x + y
        half = pl.ds(c * rh, rh)
        other_half = pl.ds((1 - c) * rh, rh)

        def copy(k, src, dst, to):
            return pltpu.make_async_remote_copy(src_ref=src, dst_ref=dst, send_sem=send_sems.at[k], recv_sem=recv_sems.at[k],
                                                device_id=to, device_id_type=MESH)

        mine = pltpu.make_async_copy(s_ref, o_ref.at[mine_slot], local_sems.at[0])
        mine.start()
        first = [copy(j, s_ref.at[half], o_ref.at[mine_slot, half], (px, py, c)) for j, (px, py) in enumerate(chips)]
        for cp in first:
            cp.start()
        passed = []
        for j, (px, py) in enumerate(chips):
            landed = o_ref.at[2 * px + py, half]
            copy(j, landed, landed, (px, py, c)).wait_recv()
            cp = copy(3 + j, landed, landed, (x, y, 1 - c))
            cp.start()
            passed.append(cp)
        for j, (px, py) in enumerate(chips):
            theirs = o_ref.at[2 * px + py, other_half]
            copy(3 + j, theirs, theirs, (x, y, 1 - c)).wait_recv()
        for cp in first + passed:
            cp.wait_send()
        mine.wait()

    return _comm_call(kern, SDS((N_CHIPS, r, cols), shard.dtype), 6, name, shard)


def _swap_halves(fulls, name):
    def kern(*refs):
        g_refs, o_refs = refs[:len(fulls)], refs[len(fulls):2 * len(fulls)]
        send_sems, recv_sems, _ = refs[2 * len(fulls):]
        x, y, c, _ = _place()
        copies = []
        for k, (g_ref, o_ref) in enumerate(zip(g_refs, o_refs)):
            rh = o_ref.shape[1]
            copies.append(pltpu.make_async_remote_copy(
                src_ref=g_ref.at[:, pl.ds((1 - c) * rh, rh), :], dst_ref=o_ref, send_sem=send_sems.at[k], recv_sem=recv_sems.at[k],
                device_id=(x, y, 1 - c), device_id_type=MESH))
        for cp in copies:
            cp.start()
        for cp in copies:
            cp.wait()

    out_shape = [SDS((f.shape[0], f.shape[1] // 2, f.shape[2]), f.dtype) for f in fulls]
    return _comm_call(kern, out_shape, len(fulls), name, *fulls)


def _join_halves(mines, name):
    def kern(*refs):
        h_refs, o_refs = refs[:len(mines)], refs[len(mines):2 * len(mines)]
        send_sems, recv_sems, local_sems = refs[2 * len(mines):]
        x, y, c, _ = _place()
        started = []
        for k, (h_ref, o_ref) in enumerate(zip(h_refs, o_refs)):
            rh = h_ref.shape[0]
            rows = o_ref.at[pl.ds(c * rh, rh)]
            keep = pltpu.make_async_copy(h_ref, rows, local_sems.at[k])
            cp = pltpu.make_async_remote_copy(src_ref=h_ref, dst_ref=rows, send_sem=send_sems.at[k], recv_sem=recv_sems.at[k],
                                              device_id=(x, y, 1 - c), device_id_type=MESH)
            keep.start()
            cp.start()
            started.append((keep, cp))
        for k, ((keep, cp), o_ref) in enumerate(zip(started, o_refs)):
            rh = o_ref.shape[0] // 2
            theirs = o_ref.at[pl.ds((1 - c) * rh, rh)]
            pltpu.make_async_remote_copy(src_ref=theirs, dst_ref=theirs, send_sem=send_sems.at[k], recv_sem=recv_sems.at[k],
                                         device_id=(x, y, 1 - c), device_id_type=MESH).wait_recv()
            cp.wait_send()
            keep.wait()

    out_shape = [SDS((2 * m.shape[0],) + m.shape[1:], m.dtype) for m in mines]
    return _comm_call(kern, out_shape, len(mines), name, *mines)


HBM = pl.BlockSpec(memory_space=pltpu.HBM)
SEMAPHORES = pl.BlockSpec(memory_space=pltpu.SEMAPHORE)
COPIES_PER_ARRAY = N_CHIPS - 1


def _in_hbm(a):
    return pltpu.with_memory_space_constraint(a, pltpu.HBM)


def _gather_copies(shard_refs, land_refs, send_sems, recv_sems):
    x, y, c, chips = _place()
    copies = []
    for k, (s_ref, land_ref) in enumerate(zip(shard_refs, land_refs)):
        half = pl.ds(c * (s_ref.shape[0] // 2), s_ref.shape[0] // 2)
        for j, (px, py) in enumerate(chips):
            n = COPIES_PER_ARRAY * k + j
            copies.append(pltpu.make_async_remote_copy(
                src_ref=s_ref.at[half], dst_ref=land_ref.at[2 * x + y, half], send_sem=send_sems.at[n], recv_sem=recv_sems.at[n],
                device_id=(px, py, c), device_id_type=MESH))
    return copies


def _owner_copies(sum_refs, land_refs, send_sems, recv_sems):
    x, y, c, chips = _place()
    copies = []
    for k, (s_ref, land_ref) in enumerate(zip(sum_refs, land_refs)):
        for j, (px, py) in enumerate(chips):
            n = COPIES_PER_ARRAY * k + j
            copies.append(pltpu.make_async_remote_copy(
                src_ref=s_ref.at[2 * px + py], dst_ref=land_ref.at[j], send_sem=send_sems.at[n], recv_sem=recv_sems.at[n],
                device_id=(px, py, c), device_id_type=MESH))
    return copies


def _copies_start(build, sources, lands, name):
    ns, nb = len(sources), len(sources) + len(lands)
    n_copies = COPIES_PER_ARRAY * ns

    def kern(*refs):
        for cp in build(refs[:ns], refs[ns:nb], refs[2 * nb], refs[2 * nb + 1]):
            cp.start()
        refs[-1][...] = jnp.zeros_like(refs[-1])

    bufs = [*sources, *lands]
    res = pl.pallas_call(
        kern, name=name,
        out_shape=(*[pltpu.HBM(b.shape, b.dtype) for b in bufs], pltpu.SemaphoreType.DMA((n_copies,)), pltpu.SemaphoreType.DMA((n_copies,)),
                   SDS((SUBLANES, 128), F32)),
        in_specs=[HBM] * nb, out_specs=(*[HBM] * nb, SEMAPHORES, SEMAPHORES, pl.BlockSpec(memory_space=pltpu.VMEM)),
        input_output_aliases={i: i for i in range(nb)},
        compiler_params=pltpu.CompilerParams(has_side_effects=pltpu.SideEffectType.DATAFLOW_SIDE_EFFECTING),
    )(*[_in_hbm(b) for b in bufs])
    return res[nb], res[nb + 1], list(res[:nb]), res[-1]


def _copies_wait(build, send_sems, recv_sems, bufs, ns, after, name):
    nb = len(bufs)

    def kern(*refs):
        for cp in build(refs[:ns], refs[ns:nb], refs[nb], refs[nb + 1]):
            cp.wait_send()
            cp.wait_recv()

    res = pl.pallas_call(
        kern, name=name, out_shape=tuple(pltpu.HBM(b.shape, b.dtype) for b in bufs),
        in_specs=[HBM] * nb + [SEMAPHORES, SEMAPHORES, ANY], out_specs=tuple([HBM] * nb),
        input_output_aliases={i: i for i in range(nb)},
        compiler_params=pltpu.CompilerParams(has_side_effects=pltpu.SideEffectType.DATAFLOW_SIDE_EFFECTING),
    )(*bufs, send_sems, recv_sems, after)
    return list(res[:ns]), list(res[ns:])


def _gather_finish(shards, lands, name):
    ns = len(shards)

    def kern(*refs):
        s_refs, l_refs = refs[:ns], refs[ns:2 * ns]
        send_sems, recv_sems, local_sems = refs[3 * ns:]
        x, y, c, chips = _place()
        placed, passed = [], []
        for k, (s_ref, l_ref) in enumerate(zip(s_refs, l_refs)):
            mine = pltpu.make_async_copy(s_ref, l_ref.at[2 * x + y], local_sems.at[k])
            mine.start()
            placed.append(mine)
            rh = s_ref.shape[0] // 2
            for j, (px, py) in enumerate(chips):
                landed = l_ref.at[2 * px + py, pl.ds(c * rh, rh)]
                cp = pltpu.make_async_remote_copy(
                    src_ref=landed, dst_ref=landed, send_sem=send_sems.at[COPIES_PER_ARRAY * k + j],
                    recv_sem=recv_sems.at[COPIES_PER_ARRAY * k + j], device_id=(x, y, 1 - c), device_id_type=MESH)
                cp.start()
                passed.append(cp)
        for k, l_ref in enumerate(l_refs):
            rh = l_ref.shape[1] // 2
            for j, (px, py) in enumerate(chips):
                theirs = l_ref.at[2 * px + py, pl.ds((1 - c) * rh, rh)]
                pltpu.make_async_remote_copy(
                    src_ref=theirs, dst_ref=theirs, send_sem=send_sems.at[COPIES_PER_ARRAY * k + j],
                    recv_sem=recv_sems.at[COPIES_PER_ARRAY * k + j], device_id=(x, y, 1 - c), device_id_type=MESH).wait_recv()
        for cp in passed:
            cp.wait_send()
        for cp in placed:
            cp.wait()

    return pl.pallas_call(
        kern, out_shape=[SDS(l.shape, l.dtype) for l in lands], in_specs=[ANY] * (2 * ns), out_specs=[ANY] * ns,
        input_output_aliases={ns + k: k for k in range(ns)},
        scratch_shapes=[pltpu.SemaphoreType.DMA((COPIES_PER_ARRAY * ns,)), pltpu.SemaphoreType.DMA((COPIES_PER_ARRAY * ns,)),
                        pltpu.SemaphoreType.DMA((ns,))],
        name=name)(*shards, *lands)


def _add_sibling(full, theirs, out_dtype, name):
    n, r, cols = full.shape
    rh = r // 2
    tr = _tile(rh, ROW_TILE)
    nb = rh // tr

    def kern(c_ref, a_ref, b_ref, o_ref):
        o_ref[...] = (a_ref[...] + b_ref[...]).astype(out_dtype)

    c = lax.axis_index("c").astype(jnp.int32).reshape(1)
    return pl.pallas_call(
        kern, out_shape=SDS((n, rh, cols), out_dtype),
        grid_spec=pltpu.PrefetchScalarGridSpec(
            num_scalar_prefetch=1, grid=(n, nb),
            in_specs=[pl.BlockSpec((1, tr, cols), lambda j, i, c_ref: (j, c_ref[0] * nb + i, 0)),
                      pl.BlockSpec((1, tr, cols), lambda j, i, c_ref: (j, i, 0))],
            out_specs=pl.BlockSpec((1, tr, cols), lambda j, i, c_ref: (j, i, 0))),
        compiler_params=_params("arbitrary", "arbitrary"), name=name)(c, full, theirs)


def _add_owners(partial, received, name):
    n, rh, cols = partial.shape
    tr = _tile(rh, ROW_TILE)

    def kern(s_ref, a_ref, b_ref, o_ref):
        acc = a_ref[0].astype(F32)
        for j in range(N_CHIPS - 1):
            acc = acc + b_ref[j].astype(F32)
        o_ref[...] = acc

    slot = (2 * lax.axis_index("x") + lax.axis_index("y")).astype(jnp.int32).reshape(1)
    return pl.pallas_call(
        kern, out_shape=SDS((rh, cols), F32),
        grid_spec=pltpu.PrefetchScalarGridSpec(
            num_scalar_prefetch=1, grid=(rh // tr,),
            in_specs=[pl.BlockSpec((1, tr, cols), lambda i, s_ref: (s_ref[0], i, 0)),
                      pl.BlockSpec((N_CHIPS - 1, tr, cols), lambda i, s_ref: (0, i, 0))],
            out_specs=pl.BlockSpec((tr, cols), lambda i, s_ref: (i, 0))),
        compiler_params=_params("arbitrary"), name=name)(slot, partial, received)


def _pack(parts, lead):
    lead_shape = parts[0].shape[:lead]
    flat = jnp.concatenate([p.reshape(lead_shape + (-1,)) for p in parts], axis=-1)
    n = flat.shape[-1]
    rows = _round_up(-(-n // PACK_W), PACK_ROWS)
    flat = jnp.pad(flat, [(0, 0)] * lead + [(0, rows * PACK_W - n)])
    return flat.reshape(lead_shape + (rows, PACK_W))


def _unpack(buf, shapes, lead):
    lead_shape = buf.shape[:lead]
    flat = buf.reshape(lead_shape + (-1,))
    out, off = [], 0
    for s in shapes:
        n = math.prod(s)
        out.append(flat[..., off:off + n].reshape(lead_shape + tuple(s)))
        off += n
    return out


def _to_slots(full, axis):
    s = full.shape
    split = full.reshape(s[:axis] + (N_CHIPS, s[axis] // N_CHIPS) + s[axis + 1:])
    return jnp.moveaxis(split, axis, 0)


def _from_slots(slots, axis):
    moved = jnp.moveaxis(slots, 0, axis)
    s = moved.shape
    return moved.reshape(s[:axis] + (s[axis] * s[axis + 1],) + s[axis + 2:])


MATMUL_WEIGHTS = (("w_in", 1), ("w_out", 0), ("w_q", 0), ("w_kv", 1), ("w_o", 0), ("w_ff1", 1), ("w_ff2", 0))
CONV_WEIGHTS = (("conv_a_w", 1), ("conv_c_w", 1))
REPLICATED = ("conv_a_b", "ln_a_g", "ln_a_b", "ln_v_g", "ln_v_b", "w_s", "b_s",
              "ln1_g", "ln1_b", "ln2_g", "ln2_b", "ln3_g", "ln3_b")
WEIGHT_ORDER = ("w_in", "conv_a_w", "conv_a_b", "ln_a_g", "ln_a_b", "ln_v_g", "ln_v_b", "w_s", "b_s", "conv_c_w", "w_out",
                "ln1_g", "ln1_b", "w_q", "w_kv", "w_o", "ln2_g", "ln2_b", "w_ff1", "w_ff2", "ln3_g", "ln3_b")


def _row(v):
    return v.reshape(1, -1)


def kernel(x, mem, w_in, conv_a_w, conv_a_b, ln_a_g, ln_a_b, ln_v_g, ln_v_b, w_s, b_s, conv_c_w, w_out, ln1_g, ln1_b, w_q, w_kv, w_o, ln2_g, ln2_b, w_ff1, w_ff2, ln3_g, ln3_b, loss_target, m_w_in, m_conv_a_w, m_conv_a_b, m_ln_a_g, m_ln_a_b, m_ln_v_g, m_ln_v_b, m_w_s, m_b_s, m_conv_c_w, m_w_out, m_ln1_g, m_ln1_b, m_w_q, m_w_kv, m_w_o, m_ln2_g, m_ln2_b, m_w_ff1, m_w_ff2, m_ln3_g, m_ln3_b, v_w_in, v_conv_a_w, v_conv_a_b, v_ln_a_g, v_ln_a_b, v_ln_v_g, v_ln_v_b, v_w_s, v_b_s, v_conv_c_w, v_w_out, v_ln1_g, v_ln1_b, v_w_q, v_w_kv, v_w_o, v_ln2_g, v_ln2_b, v_w_ff1, v_w_ff2, v_ln3_g, v_ln3_b):
    given = dict(locals())
    weights = {n: given[n] for n in WEIGHT_ORDER}
    moment1 = {n: given["m_" + n] for n in WEIGHT_ORDER}
    moment2 = {n: given["v_" + n] for n in WEIGHT_ORDER}

    depth = w_in.shape[0]
    batch, seq, d = x.shape
    t = batch * seq
    hd = d // HEADS
    d_a, d_b, d_c = A_HEADS * hd, B_HEADS * hd, C_HEADS * hd
    widths = (d_a, d_a, d_b, d_b, d_c, d_c, d_c)
    in_offs = [sum(widths[:k]) for k in range(len(widths))]
    alpha = (2.0 * depth) ** 0.25
    layers = range(depth)
    tm = _tile(t, ROW_TILE)
    tm_wide = _tile(t, WIDE_ROW_TILE)

    conv_shards = [weights[n][l] for l in layers for n, _ in CONV_WEIGHTS]
    gathers, tokens = [], []
    for l in layers:
        shards = [weights[n][l].astype(BF16) for n, _ in MATMUL_WEIGHTS] + ([_pack(conv_shards, 0)] if l == 0 else [])
        lands = [lax.empty((N_CHIPS,) + s.shape, s.dtype) for s in shards]
        send_sems, recv_sems, bufs, token = _copies_start(_gather_copies, shards, lands, f"gather_start_{l}")
        gathers.append((send_sems, recv_sems, bufs, len(shards)))
        tokens.append(token[0, 0])

    def gathered(l, after):
        send_sems, recv_sems, bufs, ns = gathers[l]
        shards, lands = _copies_wait(_gather_copies, send_sems, recv_sems, bufs, ns, after, f"gather_wait_{l}")
        return _gather_finish(shards, lands, "gather_finish")

    def layer_weights(l, slots, conv_full):
        w = {}
        for k, (n, axis) in enumerate(MATMUL_WEIGHTS):
            w[n] = _from_slots(slots[k], axis)
        for k, (n, axis) in enumerate(CONV_WEIGHTS):
            w[n] = _from_slots(conv_full[l * len(CONV_WEIGHTS) + k], axis)
        for n in REPLICATED:
            w[n] = weights[n][l]
        w["w_s_t"] = jnp.swapaxes(w["w_s"], 1, 2)
        w["bias_full"] = jnp.repeat(w["b_s"].T, hd, axis=1)
        return w

    xf = x.reshape(t, d)
    xb = (xf + sum(tokens)).astype(BF16)
    memb = mem.reshape(-1, d).astype(BF16)
    saved, full, conv_full = [], [], None
    for l in layers:
        slots = gathered(l, xb)
        if l == 0:
            conv_full = _unpack(slots[-1], [s.shape for s in conv_shards], 1)
        w = layer_weights(l, slots, conv_full)
        full.append(w)
        s = {"x0b": xb}
        proj = _in_proj(xb, w["w_in"], widths, "in_proj")
        s["proj"] = proj
        a_val, a_gate, b_u, b_v, c_b, c_c, c_x = proj
        a_out, s["ac"] = _mix_a_fwd(a_val, a_gate, w["conv_a_w"], _row(w["conv_a_b"]), _row(w["ln_a_g"]), _row(w["ln_a_b"]),
                                    batch, "mix_a_fwd")
        b_out = _mix_b_fwd(b_u, b_v, w["w_s"], w["bias_full"], _row(w["ln_v_g"]), _row(w["ln_v_b"]), "mix_b_fwd")
        c_out = _mix_c_fwd(c_b, c_c, c_x, w["conv_c_w"], batch, "mix_c_fwd")
        s["cat"] = (a_out, b_out, c_out)
        s["z1"], xf, xb = _proj_ln([a_out, b_out, c_out], w["w_out"], xf, _row(w["ln1_g"]), _row(w["ln1_b"]), alpha, tm, "out_proj_ln")
        s["x1b"] = xb
        s["q"] = _matmul(xb, w["w_q"], "q_proj")
        s["kv"] = _matmul(memb, w["w_kv"], "kv_proj")
        s["o"] = _attention_fwd(s["q"], s["kv"], batch, "attention_fwd")
        s["z2"], xf, xb = _proj_ln([s["o"]], w["w_o"], xf, _row(w["ln2_g"]), _row(w["ln2_b"]), alpha, tm, "o_proj_ln")
        s["x2b"] = xb
        s["h"], s["r"] = _ff1(xb, w["w_ff1"], "ff1")
        s["z3"], xf, xb = _proj_ln([s["r"]], w["w_ff2"], xf, _row(w["ln3_g"]), _row(w["ln3_b"]), alpha, tm_wide, "ff2_ln")
        saved.append(s)

    dx, loss_block = _loss_head(xf, loss_target.reshape(t, d), "loss_head")
    loss = lax.psum(loss_block[0, 0], ("x", "y", "c"))
    grads = [None] * depth
    reductions = [None] * depth
    started = None
    n_mm = len(MATMUL_WEIGHTS)

    def rows_to_slots(full):
        return full.reshape(N_CHIPS, full.shape[1] // N_CHIPS, full.shape[2])

    for l in reversed(layers):
        w, s, g = full[l], saved[l], {}
        gain3 = _row(w["ln3_g"]) if started is None else _row(w["ln3_g"]) + started[0, 0]
        dz3, dz3b, g["ln3_g"], g["ln3_b"] = _ln_bwd_rows(dx, s["z3"], gain3, "ln3_bwd")
        g["w_ff2"] = rows_to_slots(_mm_tn(s["r"], dz3b, "d_w_ff2"))
        dh = _bwd_ff2(dz3b, s["h"], w["w_ff2"], "d_ff_hidden")
        g["w_ff1"] = _mm_tn(s["x2b"], dh, "d_w_ff1", slots=N_CHIPS)
        dx = _bwd_data([dh], w["w_ff1"], [0], dz3, alpha, F32, tm_wide, "d_x2")
        dz2, dz2b, g["ln2_g"], g["ln2_b"] = _ln_bwd_rows(dx, s["z2"], _row(w["ln2_g"]), "ln2_bwd")
        g["w_o"] = rows_to_slots(_mm_tn(s["o"], dz2b, "d_w_o"))
        do = _bwd_data([dz2b], w["w_o"], [0], None, alpha, BF16, tm, "d_att_out")
        dq, dk, dv = _attention_bwd(s["q"], s["kv"], do, batch, "attention_bwd")
        g["w_kv"] = _mm_tn(memb, jnp.concatenate([dk, dv], axis=1), "d_w_kv", slots=N_CHIPS)
        g["w_q"] = rows_to_slots(_mm_tn(s["x1b"], dq, "d_w_q"))
        dx = _bwd_data([dq], w["w_q"], [0], dz2, alpha, F32, tm, "d_x1")
        dz1, dz1b, g["ln1_g"], g["ln1_b"] = _ln_bwd_rows(dx, s["z1"], _row(w["ln1_g"]), "ln1_bwd")
        g["w_out"] = rows_to_slots(jnp.concatenate([_mm_tn(part, dz1b, "d_w_out") for part in s["cat"]], axis=1))
        da_, dbo, dco = _bwd_out_proj(dz1b, w["w_out"], (d_a, d_b, d_c), "d_mixer_out")
        a_val, a_gate, b_u, b_v, c_b, c_c, c_x = s["proj"]
        dav, dag, g["conv_a_w"], g["conv_a_b"], g["ln_a_g"], g["ln_a_b"] = _mix_a_bwd(
            a_val, a_gate, s["ac"], da_, w["conv_a_w"], _row(w["ln_a_g"]), _row(w["ln_a_b"]), batch, "mix_a_bwd")
        dbu, dbv, g["w_s"], dbs, g["ln_v_g"], g["ln_v_b"] = _mix_b_bwd(
            b_u, b_v, dbo, w["w_s"], w["w_s_t"], w["bias_full"], _row(w["ln_v_g"]), _row(w["ln_v_b"]), "mix_b_bwd")
        g["b_s"] = dbs.reshape(B_HEADS, CHUNK)
        dcb, dcc, dcx, g["conv_c_w"] = _mix_c_bwd(c_b, c_c, c_x, dco, w["conv_c_w"], batch, "mix_c_bwd")
        dproj = [dav, dag, dbu, dbv, dcb, dcc, dcx]
        g["w_in"] = _to_slots(jnp.concatenate([_mm_tn(s["x0b"], dp, "d_w_in")[0] for dp in dproj], axis=1), 1)
        dx = _bwd_data(dproj, w["w_in"], in_offs, dz1, alpha, F32, tm, "d_x0")
        grads[l] = g

        slot_grads = [g[n] for n, _ in MATMUL_WEIGHTS]
        wire = [BF16] * n_mm
        if l == 0:
            conv_slots = [_to_slots(grads[k][n], axis) for k in layers for n, axis in CONV_WEIGHTS]
            rep_parts = [jnp.stack([grads[k][n].reshape(weights[n].shape[1:]) for k in layers]) for n in REPLICATED]
            rep_flat = jnp.concatenate([p.reshape(-1) for p in rep_parts])
            n_rep = rep_flat.shape[0]
            per_chip = _round_up(-(-n_rep // N_CHIPS), PACK_W * PACK_ROWS)
            rep_slots = jnp.pad(rep_flat, (0, N_CHIPS * per_chip - n_rep)).reshape(N_CHIPS, per_chip)
            slot_grads.append(_pack(conv_slots + [rep_slots], 1))
            wire.append(F32)
        theirs = _swap_halves(slot_grads, "rs_swap")
        sums = [_add_sibling(f, th, dt, "rs_add_sibling") for f, th, dt in zip(slot_grads, theirs, wire)]
        lands = [lax.empty((N_CHIPS - 1,) + p.shape[1:], p.dtype) for p in sums]
        send_sems, recv_sems, bufs, started = _copies_start(_owner_copies, sums, lands, f"rs_send_start_{l}")
        reductions[l] = (send_sems, recv_sems, bufs, len(sums))
    grad_x = dx.reshape(batch, seq, d)

    reduced, after = [None] * depth, started
    for l in reversed(layers):
        send_sems, recv_sems, bufs, ns = reductions[l]
        sums, lands = _copies_wait(_owner_copies, send_sems, recv_sems, bufs, ns, after, f"rs_send_wait_{l}")
        mine = [_add_owners(p, r, "rs_add_owners") for p, r in zip(sums, lands)]
        reduced[l] = _join_halves(mine, "rs_join")
        after = reduced[l][0]
    small = _unpack(reduced[0][n_mm], [c.shape[1:] for c in conv_slots] + [(per_chip,)], 0)
    conv_grad, rep_mine = small[:-1], small[-1]
    rep_all = _all_gather_chips(rep_mine.reshape(-1, PACK_W), "gather_small_grads").reshape(-1)[:n_rep]

    grad = {}
    for k, (n, _) in enumerate(MATMUL_WEIGHTS):
        grad[n] = jnp.stack([reduced[l][k] for l in layers])
    for k, (n, _) in enumerate(CONV_WEIGHTS):
        grad[n] = jnp.stack([conv_grad[l * len(CONV_WEIGHTS) + k] for l in layers])
    off = 0
    for n in REPLICATED:
        size = math.prod(weights[n].shape)
        grad[n] = rep_all[off:off + size].reshape(weights[n].shape)
        off += size

    delta, new_m, new_v = {}, {}, {}
    for n, _ in MATMUL_WEIGHTS:
        shape = weights[n].shape
        as_rows = lambda a: a.reshape(-1, shape[-1])
        delta[n], new_m[n], new_v[n] = (
            r.reshape(shape) for r in _adamw(as_rows(weights[n]), as_rows(grad[n]), as_rows(moment1[n]), as_rows(moment2[n]), "adamw"))
    small_names = [n for n, _ in CONV_WEIGHTS] + list(REPLICATED)
    small_shapes = [weights[n].shape for n in small_names]
    packed = [_pack([src[n] for n in small_names], 0) for src in (weights, grad, moment1, moment2)]
    for dst, res in zip((delta, new_m, new_v), _adamw(*packed, "adamw_small")):
        for n, a in zip(small_names, _unpack(res, small_shapes, 0)):
            dst[n] = a

    return (loss, grad_x, *[grad[n] for n in WEIGHT_ORDER], *[delta[n] for n in WEIGHT_ORDER],
            *[new_m[n] for n in WEIGHT_ORDER], *[new_v[n] for n in WEIGHT_ORDER])
```

```python
import functools
import math

import jax
import jax.numpy as jnp
from jax import lax
from jax.experimental import pallas as pl
from jax.experimental.pallas import tpu as pltpu

F32 = jnp.float32
BF16 = jnp.bfloat16
SDS = jax.ShapeDtypeStruct

HEADS = 16
A_HEADS, B_HEADS, C_HEADS = 6, 4, 6
X_HEADS = 4
CHUNK = 128
LN_EPS = 1e-5
ADAM_LR, ADAM_B1, ADAM_B2, ADAM_EPS, ADAM_WD, ADAM_STEP = 0.001, 0.9, 0.999, 1e-08, 0.01, 10

N_CHIPS = 4
V7X_VMEM_LIMIT = 56 << 20
SUBLANES = 8
PACK_W = 1024
PACK_ROWS = 32
ROW_TILE = 512
WIDE_ROW_TILE = 256
CONV_ROWS = 256
SUB_ROWS = 64
ATT_ROWS = 512
TN_TILE = 1024
MESH = pl.DeviceIdType.MESH
ANY = pl.BlockSpec(memory_space=pl.ANY)


def _tile(n, t):
    for d in range(min(n, t), 0, -1):
        if n % d == 0 and d % (2 * SUBLANES) == 0:
            return d
    return n


def _round_up(n, m):
    return -(-n // m) * m


def _params(*sem):
    return pltpu.CompilerParams(dimension_semantics=sem or None, vmem_limit_bytes=V7X_VMEM_LIMIT)


def _dot(a, b):
    return jnp.dot(a, b, preferred_element_type=F32)


def _dot_nt(a, b):
    return lax.dot_general(a, b, (((1,), (1,)), ((), ())), preferred_element_type=F32)


def _dot_tn(a, b):
    return lax.dot_general(a, b, (((0,), (0,)), ((), ())), preferred_element_type=F32)


def _sigmoid(x):
    return 1.0 / (1.0 + jnp.exp(-x))


def _gelu(x):
    return 0.5 * x * (1.0 + lax.erf(x * (2.0 ** -0.5)))


def _gelu_grad(x):
    return 0.5 * (1.0 + lax.erf(x * (2.0 ** -0.5))) + x * jnp.exp(-0.5 * x * x) * ((2.0 * math.pi) ** -0.5)


def _ln_stats(z):
    mu = jnp.mean(z, axis=-1, keepdims=True)
    zc = z - mu
    rstd = lax.rsqrt(jnp.mean(zc * zc, axis=-1, keepdims=True) + LN_EPS)
    return zc * rstd, rstd


def _ln_bwd(dy, y, rstd, g):
    dyh = dy * g
    return rstd * (dyh - jnp.mean(dyh, axis=-1, keepdims=True) - y * jnp.mean(dyh * y, axis=-1, keepdims=True))


def _colsum(x):
    return jnp.sum(x, axis=0, keepdims=True)


def _rowwise(body, rows, consts, outs, accs=(), *, tm, name):
    t = rows[0].shape[0]
    steps = t // tm

    def kern(*refs):
        body(pl.program_id(0), steps, *refs)

    def whole(a):
        return pl.BlockSpec(a.shape, lambda i, nd=len(a.shape): (0,) * nd)

    in_specs = [pl.BlockSpec((tm, r.shape[1]), lambda i: (i, 0)) for r in rows] + [whole(c) for c in consts]
    out_shape = [SDS((t, n), dt) for n, dt in outs] + [SDS(s, dt) for s, dt in accs]
    out_specs = [pl.BlockSpec((tm, n), lambda i: (i, 0)) for n, _ in outs] + [whole(SDS(s, dt)) for s, dt in accs]
    return pl.pallas_call(kern, grid=(steps,), in_specs=in_specs, out_specs=out_specs, out_shape=out_shape,
                          compiler_params=_params("arbitrary"), name=name)(*rows, *consts)


def _in_proj(xb, w_in, widths, name):
    offs = [sum(widths[:k]) for k in range(len(widths))]

    def body(i, steps, x_ref, w_ref, *o_refs):
        x = x_ref[...]
        for o_ref, off, n in zip(o_refs, offs, widths):
            o_ref[...] = _dot(x, w_ref[:, off:off + n])

    return _rowwise(body, [xb], [w_in], [(n, F32) for n in widths], tm=_tile(xb.shape[0], ROW_TILE), name=name)


def _matmul(ab, w, name):
    def body(i, steps, a_ref, w_ref, o_ref):
        o_ref[...] = _dot(a_ref[...], w_ref[...]).astype(BF16)

    return _rowwise(body, [ab], [w], [(w.shape[1], BF16)], tm=_tile(ab.shape[0], ROW_TILE), name=name)[0]


def _ff1(xb, w, name):
    def body(i, steps, a_ref, w_ref, h_ref, r_ref):
        h = _dot(a_ref[...], w_ref[...])
        h_ref[...] = h.astype(BF16)
        r = jnp.maximum(h, 0.0)
        r_ref[...] = (r * r).astype(BF16)

    n = w.shape[1]
    return _rowwise(body, [xb], [w], [(n, BF16), (n, BF16)], tm=_tile(xb.shape[0], WIDE_ROW_TILE), name=name)


def _proj_ln(a_list, w, x, g, b, alpha, tm, name):
    widths = [a.shape[1] for a in a_list]
    offs = [sum(widths[:k]) for k in range(len(widths))]
    na = len(a_list)

    def body(i, steps, *refs):
        a_refs, (x_ref, w_ref, g_ref, b_ref, z_ref, xn_ref, xb_ref) = refs[:na], refs[na:]
        acc = alpha * x_ref[...]
        for a_ref, off, n in zip(a_refs, offs, widths):
            acc = acc + _dot(a_ref[...], w_ref[off:off + n, :])
        z_ref[...] = acc
        y, _ = _ln_stats(acc)
        xn = y * g_ref[...] + b_ref[...]
        xn_ref[...] = xn
        xb_ref[...] = xn.astype(BF16)

    d = w.shape[1]
    return _rowwise(body, [*a_list, x], [w, g, b], [(d, F32), (d, F32), (d, BF16)], tm=tm, name=name)


def _loss_head(y, target, name):
    d = y.shape[1]

    def body(i, steps, y_ref, t_ref, dy_ref, l_ref):
        @pl.when(i == 0)
        def _():
            l_ref[...] = jnp.zeros_like(l_ref)

        err = y_ref[...] - t_ref[...]
        dy_ref[...] = err * (1.0 / d)
        l_ref[...] += jnp.sum(err * err) * (0.5 / d)

    return _rowwise(body, [y, target], [], [(d, F32)], [((SUBLANES, 128), F32)], tm=_tile(y.shape[0], ROW_TILE), name=name)


def _ln_bwd_rows(dy, z, g, name):
    d = z.shape[1]

    def body(i, steps, dy_ref, z_ref, g_ref, dz_ref, dzb_ref, dg_ref, db_ref):
        @pl.when(i == 0)
        def _():
            dg_ref[...] = jnp.zeros_like(dg_ref)
            db_ref[...] = jnp.zeros_like(db_ref)

        dy_ = dy_ref[...]
        y, rstd = _ln_stats(z_ref[...])
        dz = _ln_bwd(dy_, y, rstd, g_ref[...])
        dz_ref[...] = dz
        dzb_ref[...] = dz.astype(BF16)
        dg_ref[...] += _colsum(dy_ * y)
        db_ref[...] += _colsum(dy_)

    return _rowwise(body, [dy, z], [g], [(d, F32), (d, BF16)], [((1, d), F32), ((1, d), F32)],
                    tm=_tile(z.shape[0], ROW_TILE), name=name)


def _bwd_ff2(dzb, h, w_ff2, name):
    def body(i, steps, dz_ref, h_ref, w_ref, dh_ref):
        dr = _dot_nt(dz_ref[...], w_ref[...])
        dh_ref[...] = (dr * (2.0 * jnp.maximum(h_ref[...].astype(F32), 0.0))).astype(BF16)

    return _rowwise(body, [dzb, h], [w_ff2], [(h.shape[1], BF16)], tm=_tile(h.shape[0], WIDE_ROW_TILE), name=name)[0]


def _bwd_nt(g_list, w, col_offs, res, alpha, tm, name):
    ng = len(g_list)
    widths = [g.shape[1] for g in g_list]

    def body(i, steps, *refs):
        g_refs = refs[:ng]
        if res is None:
            w_ref, o_ref = refs[ng:]
            acc = None
        else:
            r_ref, w_ref, o_ref = refs[ng:]
            acc = alpha * r_ref[...]
        for g_ref, off, n in zip(g_refs, col_offs, widths):
            part = _dot_nt(g_ref[...], w_ref[:, off:off + n])
            acc = part if acc is None else acc + part
        o_ref[...] = acc.astype(o_ref.dtype)

    rows = list(g_list) + ([] if res is None else [res])
    return rows, w, body, tm, name


def _bwd_data(g_list, w, col_offs, res, alpha, out_dtype, tm, name):
    rows, w, body, tm, name = _bwd_nt(g_list, w, col_offs, res, alpha, tm, name)
    return _rowwise(body, rows, [w], [(w.shape[0], out_dtype)], tm=tm, name=name)[0]


def _bwd_out_proj(dzb, w_out, widths, name):
    offs = [sum(widths[:k]) for k in range(len(widths))]

    def body(i, steps, dz_ref, w_ref, *o_refs):
        dz = dz_ref[...]
        for o_ref, off, n in zip(o_refs, offs, widths):
            o_ref[...] = _dot_nt(dz, w_ref[off:off + n, :])

    return _rowwise(body, [dzb], [w_out], [(n, F32) for n in widths], tm=_tile(dzb.shape[0], ROW_TILE), name=name)


def _adamw(w, g, m, v, name):
    def body(i, steps, w_ref, g_ref, m_ref, v_ref, d_ref, nm_ref, nv_ref):
        g_ = g_ref[...]
        nm = ADAM_B1 * m_ref[...] + (1.0 - ADAM_B1) * g_
        nv = ADAM_B2 * v_ref[...] + (1.0 - ADAM_B2) * (g_ * g_)
        m_hat = nm / (1.0 - ADAM_B1 ** ADAM_STEP)
        v_hat = nv / (1.0 - ADAM_B2 ** ADAM_STEP)
        d_ref[...] = -ADAM_LR * (m_hat / (jnp.sqrt(v_hat) + ADAM_EPS) + ADAM_WD * w_ref[...])
        nm_ref[...] = nm
        nv_ref[...] = nv

    c = w.shape[1]
    return _rowwise(body, [w, g, m, v], [], [(c, F32)] * 3, tm=_tile(w.shape[0], ROW_TILE), name=name)


def _mm_tn(a, g, name, slots=1):
    t, ka = a.shape
    n = g.shape[1]
    ta, tn, tk = _tile(ka, TN_TILE), _tile(n // slots, TN_TILE), _tile(t, TN_TILE)
    per = n // slots // tn

    def kern(a_ref, g_ref, o_ref):
        @pl.when(pl.program_id(2) == 0)
        def _():
            o_ref[...] = jnp.zeros_like(o_ref)

        o_ref[0] += _dot_tn(a_ref[...], g_ref[...])

    return pl.pallas_call(
        kern, grid=(ka // ta, n // tn, t // tk),
        in_specs=[pl.BlockSpec((tk, ta), lambda i, j, k: (k, i)), pl.BlockSpec((tk, tn), lambda i, j, k: (k, j))],
        out_specs=pl.BlockSpec((1, ta, tn), lambda i, j, k: (j // per, i, j % per)), out_shape=SDS((slots, ka, n // slots), F32),
        compiler_params=_params("arbitrary", "arbitrary", "arbitrary"), name=name)(a, g)


def _mm_tn_cols(a, g_list, name):
    t, ka = a.shape
    widths = [g.shape[1] for g in g_list]
    offs = [sum(widths[:k]) for k in range(len(widths))]
    ta, tk = _tile(ka, ROW_TILE), _tile(t, TN_TILE)

    def kern(a_ref, *refs):
        g_refs, o_ref = refs[:-1], refs[-1]

        @pl.when(pl.program_id(1) == 0)
        def _():
            o_ref[...] = jnp.zeros_like(o_ref)

        a_ = a_ref[...]
        for g_ref, off, n in zip(g_refs, offs, widths):
            o_ref[:, off:off + n] += _dot_tn(a_, g_ref[...])

    return pl.pallas_call(
        kern, grid=(ka // ta, t // tk),
        in_specs=[pl.BlockSpec((tk, ta), lambda i, k: (k, i))] + [pl.BlockSpec((tk, n), lambda i, k: (k, 0)) for n in widths],
        out_specs=pl.BlockSpec((ta, sum(widths)), lambda i, k: (i, 0)), out_shape=SDS((ka, sum(widths)), F32),
        compiler_params=_params("arbitrary", "arbitrary"), name=name)(a, *g_list)


def _mm_tn_rows(a_list, g, name):
    t, n = g.shape
    widths = [a.shape[1] for a in a_list]
    offs = [sum(widths[:k]) for k in range(len(widths))]
    tn, tk = _tile(n, TN_TILE), _tile(t, TN_TILE)

    def kern(*refs):
        a_refs, g_ref, o_ref = refs[:-2], refs[-2], refs[-1]

        @pl.when(pl.program_id(1) == 0)
        def _():
            o_ref[...] = jnp.zeros_like(o_ref)

        g_ = g_ref[...]
        for a_ref, off, ka in zip(a_refs, offs, widths):
            o_ref[off:off + ka, :] += _dot_tn(a_ref[...], g_)

    return pl.pallas_call(
        kern, grid=(n // tn, t // tk),
        in_specs=[pl.BlockSpec((tk, ka), lambda j, k: (k, 0)) for ka in widths] + [pl.BlockSpec((tk, tn), lambda j, k: (k, j))],
        out_specs=pl.BlockSpec((sum(widths), tn), lambda j, k: (0, j)), out_shape=SDS((sum(widths), n), F32),
        compiler_params=_params("arbitrary", "arbitrary"), name=name)(*a_list, g)


def _conv_geometry(seq, taps):
    pad = _round_up(taps - 1, SUBLANES)
    rc = _tile(seq, CONV_ROWS)
    assert rc % pad == 0 and seq % rc == 0
    return pad, rc, seq // rc


def _chunk_spec(rc, n, nch):
    return pl.BlockSpec((rc, n), lambda b, i: (b * nch + i, 0))


def _prev_halo_spec(pad, rc, n, nch):
    per = rc // pad
    return pl.BlockSpec((pad, n), lambda b, i: (jnp.maximum((b * nch + i) * per - 1, 0), 0))


def _next_halo_spec(pad, rc, n, nch, total_rows):
    per = rc // pad
    last = total_rows // pad - 1
    return pl.BlockSpec((pad, n), lambda b, i: (jnp.minimum((b * nch + i + 1) * per, last), 0))


def _whole2(a):
    return pl.BlockSpec(a.shape, lambda b, i, nd=len(a.shape): (0,) * nd)


def _sub_rows(rc):
    return SUB_ROWS if rc % SUB_ROWS == 0 else rc


def _build_shifts(sh_ref, src_ref, offsets):
    rows = src_ref.shape[0]
    for r in sorted({o % SUBLANES for o in offsets} - {0}):
        sh_ref[r, 0:rows - SUBLANES, :] = src_ref[r:r + rows - SUBLANES, :]


def _read_shifted(sh_ref, src_ref, o, s0, sub):
    r = o % SUBLANES
    a = o - r + s0
    return src_ref[a:a + sub, :] if r == 0 else sh_ref[r, a:a + sub, :]


def _tap_sum(sh_ref, src_ref, w_ref, offsets, s0, sub):
    acc = None
    for k, o in enumerate(offsets):
        term = _read_shifted(sh_ref, src_ref, o, s0, sub) * w_ref[k:k + 1, :]
        acc = term if acc is None else acc + term
    return acc


def _row_groups(x):
    acc = x[0:SUBLANES, :]
    for g0 in range(SUBLANES, x.shape[0], SUBLANES):
        acc = acc + x[g0:g0 + SUBLANES, :]
    return acc


def _mix_a_fwd(a_val, a_gate, conv_w, conv_b, ln_g, ln_b, batch, name):
    t, da = a_val.shape
    taps = conv_w.shape[0]
    pad, rc, nch = _conv_geometry(t // batch, taps)
    sub = _sub_rows(rc)
    offs = [pad - (taps - 1) + k for k in range(taps)]

    def kern(av_ref, ag_ref, pav_ref, pag_ref, w_ref, cb_ref, g_ref, b_ref, a_ref, ac_ref, pad_ref, sh_ref):
        first = pl.program_id(1) == 0
        pad_ref[0:pad, :] = jnp.where(first, 0.0, pav_ref[...] * _sigmoid(pag_ref[...]))
        for s0 in range(0, rc, sub):
            pad_ref[pad + s0:pad + s0 + sub, :] = av_ref[s0:s0 + sub, :] * _sigmoid(ag_ref[s0:s0 + sub, :])
        _build_shifts(sh_ref, pad_ref, offs)
        for s0 in range(0, rc, sub):
            ac = _tap_sum(sh_ref, pad_ref, w_ref, offs, s0, sub) + cb_ref[...]
            ac_ref[s0:s0 + sub, :] = ac
            y, _ = _ln_stats(ac)
            aln = y * g_ref[...] + b_ref[...]
            a_ref[s0:s0 + sub, :] = (aln * _sigmoid(aln)).astype(BF16)

    chunk, halo = _chunk_spec(rc, da, nch), _prev_halo_spec(pad, rc, da, nch)
    return pl.pallas_call(
        kern, grid=(batch, nch),
        in_specs=[chunk, chunk, halo, halo, _whole2(conv_w), _whole2(conv_b), _whole2(ln_g), _whole2(ln_b)],
        out_specs=[chunk, chunk], out_shape=[SDS((t, da), BF16), SDS((t, da), F32)],
        scratch_shapes=[pltpu.VMEM((pad + rc, da), F32), pltpu.VMEM((SUBLANES, pad + rc, da), F32)],
        compiler_params=_params("arbitrary", "arbitrary"), name=name)(a_val, a_gate, a_val, a_gate, conv_w, conv_b, ln_g, ln_b)


def _mix_a_bwd(a_val, a_gate, ac, da_, conv_w, ln_g, ln_b, batch, name):
    t, da = a_val.shape
    taps = conv_w.shape[0]
    pad, rc, nch = _conv_geometry(t // batch, taps)
    sub = _sub_rows(rc)
    offs_in = [pad - (taps - 1) + k for k in range(taps)]
    offs_out = [taps - 1 - k for k in range(taps)]

    def kern(av_ref, ag_ref, pav_ref, pag_ref, ac_ref, da_ref, nac_ref, nda_ref, w_ref, g_ref, b_ref,
             dav_ref, dag_ref, dw_ref, dcb_ref, dg_ref, db_ref, gpad_ref, dpad_ref, shg_ref, shd_ref, dwacc_ref):
        i = pl.program_id(1)
        start = (pl.program_id(0) == 0) & (i == 0)
        end = (pl.program_id(0) == batch - 1) & (i == nch - 1)

        @pl.when(start)
        def _():
            for r in (dcb_ref, dg_ref, db_ref, dwacc_ref):
                r[...] = jnp.zeros_like(r)

        gain, bias = g_ref[...], b_ref[...]

        def d_conv_out(ac_, dout):
            y, rstd = _ln_stats(ac_)
            aln = y * gain + bias
            sig = _sigmoid(aln)
            daln = dout * (sig * (1.0 + aln * (1.0 - sig)))
            return _ln_bwd(daln, y, rstd, gain), daln, y

        gpad_ref[0:pad, :] = jnp.where(i == 0, 0.0, pav_ref[...] * _sigmoid(pag_ref[...]))
        dac_next, _, _ = d_conv_out(nac_ref[...], nda_ref[...])
        dpad_ref[rc:rc + pad, :] = jnp.where(i == nch - 1, 0.0, dac_next)
        for s0 in range(0, rc, sub):
            rows = slice(s0, s0 + sub)
            dac, daln, y = d_conv_out(ac_ref[rows, :], da_ref[rows, :])
            dg_ref[...] += _colsum(daln * y)
            db_ref[...] += _colsum(daln)
            dcb_ref[...] += _colsum(dac)
            dpad_ref[rows, :] = dac
            gpad_ref[pad + s0:pad + s0 + sub, :] = av_ref[rows, :] * _sigmoid(ag_ref[rows, :])
        _build_shifts(shg_ref, gpad_ref, offs_in)
        _build_shifts(shd_ref, dpad_ref, offs_out)
        for s0 in range(0, rc, sub):
            rows = slice(s0, s0 + sub)
            dac = dpad_ref[rows, :]
            for k, o in enumerate(offs_in):
                dwacc_ref[k] += _row_groups(dac * _read_shifted(shg_ref, gpad_ref, o, s0, sub))
            dgl = _tap_sum(shd_ref, dpad_ref, w_ref, offs_out, s0, sub)
            av = av_ref[rows, :]
            sig = _sigmoid(ag_ref[rows, :])
            dav_ref[rows, :] = (dgl * sig).astype(BF16)
            dag_ref[rows, :] = (dgl * av * sig * (1.0 - sig)).astype(BF16)

        @pl.when(end)
        def _():
            for k in range(taps):
                dw_ref[k:k + 1, :] = _colsum(dwacc_ref[k])

    chunk, prev = _chunk_spec(rc, da, nch), _prev_halo_spec(pad, rc, da, nch)
    nxt = _next_halo_spec(pad, rc, da, nch, t)
    vec = SDS((1, da), F32)
    return pl.pallas_call(
        kern, grid=(batch, nch),
        in_specs=[chunk, chunk, prev, prev, chunk, chunk, nxt, nxt, _whole2(conv_w), _whole2(ln_g), _whole2(ln_b)],
        out_specs=[chunk, chunk, _whole2(conv_w), _whole2(vec), _whole2(vec), _whole2(vec)],
        out_shape=[SDS((t, da), BF16), SDS((t, da), BF16), SDS(conv_w.shape, F32), vec, vec, vec],
        scratch_shapes=[pltpu.VMEM((pad + rc, da), F32), pltpu.VMEM((rc + pad, da), F32),
                        pltpu.VMEM((SUBLANES, pad + rc, da), F32), pltpu.VMEM((SUBLANES, rc + pad, da), F32),
                        pltpu.VMEM((taps, SUBLANES, da), F32)],
        compiler_params=_params("arbitrary", "arbitrary"), name=name)(
            a_val, a_gate, a_val, a_gate, ac, da_, ac, da_, conv_w, ln_g, ln_b)


def _mix_c_fwd(c_b, c_c, c_x, conv_w, batch, name):
    t, dc = c_b.shape
    taps = conv_w.shape[0]
    pad, rc, nch = _conv_geometry(t // batch, taps)
    sub = _sub_rows(rc)
    offs = [pad - (taps - 1) + k for k in range(taps)]

    def kern(cb_ref, cc_ref, cx_ref, pcc_ref, pcx_ref, w_ref, o_ref, pad_ref, sh_ref):
        pad_ref[0:pad, :] = jnp.where(pl.program_id(1) == 0, 0.0, pcc_ref[...] * pcx_ref[...])
        for s0 in range(0, rc, sub):
            pad_ref[pad + s0:pad + s0 + sub, :] = cc_ref[s0:s0 + sub, :] * cx_ref[s0:s0 + sub, :]
        _build_shifts(sh_ref, pad_ref, offs)
        for s0 in range(0, rc, sub):
            o_ref[s0:s0 + sub, :] = (cb_ref[s0:s0 + sub, :] * _tap_sum(sh_ref, pad_ref, w_ref, offs, s0, sub)).astype(BF16)

    chunk, prev = _chunk_spec(rc, dc, nch), _prev_halo_spec(pad, rc, dc, nch)
    return pl.pallas_call(
        kern, grid=(batch, nch), in_specs=[chunk, chunk, chunk, prev, prev, _whole2(conv_w)],
        out_specs=chunk, out_shape=SDS((t, dc), BF16),
        scratch_shapes=[pltpu.VMEM((pad + rc, dc), F32), pltpu.VMEM((SUBLANES, pad + rc, dc), F32)],
        compiler_params=_params("arbitrary", "arbitrary"), name=name)(c_b, c_c, c_x, c_c, c_x, conv_w)


def _mix_c_bwd(c_b, c_c, c_x, dco, conv_w, batch, name):
    t, dc = c_b.shape
    taps = conv_w.shape[0]
    pad, rc, nch = _conv_geometry(t // batch, taps)
    sub = _sub_rows(rc)
    offs_in = [pad - (taps - 1) + k for k in range(taps)]
    offs_out = [taps - 1 - k for k in range(taps)]

    def kern(cb_ref, cc_ref, cx_ref, do_ref, pcc_ref, pcx_ref, ncb_ref, ndo_ref, w_ref,
             dcb_ref, dcc_ref, dcx_ref, dw_ref, ppad_ref, dpad_ref, shp_ref, shd_ref, dwacc_ref):
        i = pl.program_id(1)

        @pl.when((pl.program_id(0) == 0) & (i == 0))
        def _():
            dwacc_ref[...] = jnp.zeros_like(dwacc_ref)

        ppad_ref[0:pad, :] = jnp.where(i == 0, 0.0, pcc_ref[...] * pcx_ref[...])
        dpad_ref[rc:rc + pad, :] = jnp.where(i == nch - 1, 0.0, ndo_ref[...] * ncb_ref[...])
        for s0 in range(0, rc, sub):
            rows = slice(s0, s0 + sub)
            ppad_ref[pad + s0:pad + s0 + sub, :] = cc_ref[rows, :] * cx_ref[rows, :]
            dpad_ref[rows, :] = do_ref[rows, :] * cb_ref[rows, :]
        _build_shifts(shp_ref, ppad_ref, offs_in)
        _build_shifts(shd_ref, dpad_ref, offs_out)
        for s0 in range(0, rc, sub):
            rows = slice(s0, s0 + sub)
            dcv = dpad_ref[rows, :]
            cv = None
            for k, o in enumerate(offs_in):
                shifted = _read_shifted(shp_ref, ppad_ref, o, s0, sub)
                dwacc_ref[k] += _row_groups(dcv * shifted)
                cv = shifted * w_ref[k:k + 1, :] if cv is None else cv + shifted * w_ref[k:k + 1, :]
            dp = _tap_sum(shd_ref, dpad_ref, w_ref, offs_out, s0, sub)
            dcb_ref[rows, :] = (do_ref[rows, :] * cv).astype(BF16)
            dcc_ref[rows, :] = (dp * cx_ref[rows, :]).astype(BF16)
            dcx_ref[rows, :] = (dp * cc_ref[rows, :]).astype(BF16)

        @pl.when((pl.program_id(0) == batch - 1) & (i == nch - 1))
        def _():
            for k in range(taps):
                dw_ref[k:k + 1, :] = _colsum(dwacc_ref[k])

    chunk, prev = _chunk_spec(rc, dc, nch), _prev_halo_spec(pad, rc, dc, nch)
    nxt = _next_halo_spec(pad, rc, dc, nch, t)
    return pl.pallas_call(
        kern, grid=(batch, nch), in_specs=[chunk, chunk, chunk, chunk, prev, prev, nxt, nxt, _whole2(conv_w)],
        out_specs=[chunk, chunk, chunk, _whole2(conv_w)],
        out_shape=[SDS((t, dc), BF16)] * 3 + [SDS(conv_w.shape, F32)],
        scratch_shapes=[pltpu.VMEM((pad + rc, dc), F32), pltpu.VMEM((rc + pad, dc), F32),
                        pltpu.VMEM((SUBLANES, pad + rc, dc), F32), pltpu.VMEM((SUBLANES, rc + pad, dc), F32),
                        pltpu.VMEM((taps, SUBLANES, dc), F32)],
        compiler_params=_params("arbitrary", "arbitrary"), name=name)(c_b, c_c, c_x, dco, c_c, c_x, c_b, dco, conv_w)


def _head_of_lane(db):
    return lax.broadcasted_iota(jnp.int32, (1, db), 1) // (db // B_HEADS)


def _tril(rows_ge_cols=True):
    r = lax.broadcasted_iota(jnp.int32, (CHUNK, CHUNK), 0)
    c = lax.broadcasted_iota(jnp.int32, (CHUNK, CHUNK), 1)
    return (r >= c) if rows_ge_cols else (r <= c)


def _spatial_mix(wm, vb, bias, head):
    mixed = bias
    for h in range(B_HEADS):
        mixed = mixed + jnp.where(head == h, _dot(wm[h], vb), 0.0)
    return mixed


def _mix_b_fwd(b_u, b_v, w_s, bias_full, ln_g, ln_b, name):
    t, db = b_u.shape
    rb = _tile(t, ROW_TILE)

    def kern(bu_ref, bv_ref, ws_ref, bias_ref, g_ref, b_ref, o_ref):
        head = _head_of_lane(db)
        wm = [jnp.where(_tril(), ws_ref[h], 0.0).astype(BF16) for h in range(B_HEADS)]
        for ch in range(rb // CHUNK):
            rows = slice(ch * CHUNK, (ch + 1) * CHUNK)
            y, _ = _ln_stats(_gelu(bv_ref[rows, :]))
            vb = (y * g_ref[...] + b_ref[...]).astype(BF16)
            mixed = _spatial_mix(wm, vb, bias_ref[...], head)
            o_ref[rows, :] = (_gelu(bu_ref[rows, :]) * mixed).astype(BF16)

    def whole(a):
        return pl.BlockSpec(a.shape, lambda i, nd=len(a.shape): (0,) * nd)

    tile = pl.BlockSpec((rb, db), lambda i: (i, 0))
    return pl.pallas_call(
        kern, grid=(t // rb,), in_specs=[tile, tile, whole(w_s), whole(bias_full), whole(ln_g), whole(ln_b)],
        out_specs=tile, out_shape=SDS((t, db), BF16), compiler_params=_params("arbitrary"), name=name)(
            b_u, b_v, w_s, bias_full, ln_g, ln_b)


def _mix_b_bwd(b_u, b_v, dbo, w_s, w_s_t, bias_full, ln_g, ln_b, name):
    t, db = b_u.shape
    rb = _tile(t, ROW_TILE)
    steps = t // rb

    def kern(bu_ref, bv_ref, do_ref, ws_ref, wst_ref, bias_ref, g_ref, b_ref,
             dbu_ref, dbv_ref, dws_ref, dbs_ref, dg_ref, dbeta_ref, dbias_ref):
        i = pl.program_id(0)

        @pl.when(i == 0)
        def _():
            for r in (dws_ref, dg_ref, dbeta_ref, dbias_ref):
                r[...] = jnp.zeros_like(r)

        head = _head_of_lane(db)
        gain = g_ref[...]
        wm = [jnp.where(_tril(), ws_ref[h], 0.0).astype(BF16) for h in range(B_HEADS)]
        wmt = [jnp.where(_tril(False), wst_ref[h], 0.0).astype(BF16) for h in range(B_HEADS)]
        for ch in range(rb // CHUNK):
            rows = slice(ch * CHUNK, (ch + 1) * CHUNK)
            bu, bv, dout = bu_ref[rows, :], bv_ref[rows, :], do_ref[rows, :]
            y, rstd = _ln_stats(_gelu(bv))
            vb = (y * gain + b_ref[...]).astype(BF16)
            mixed = _spatial_mix(wm, vb, bias_ref[...], head)
            du = dout * mixed
            dmixed = dout * _gelu(bu)
            dbias_ref[...] += dmixed
            dmb = dmixed.astype(BF16)
            dv = None
            for h in range(B_HEADS):
                dws_ref[h] += _dot_nt(jnp.where(head == h, dmb, 0.0).astype(BF16), vb)
                part = jnp.where(head == h, _dot(wmt[h], dmb), 0.0)
                dv = part if dv is None else dv + part
            dg_ref[...] += _colsum(dv * y)
            dbeta_ref[...] += _colsum(dv)
            dbv_ref[rows, :] = (_ln_bwd(dv, y, rstd, gain) * _gelu_grad(bv)).astype(BF16)
            dbu_ref[rows, :] = (du * _gelu_grad(bu)).astype(BF16)

        @pl.when(i == steps - 1)
        def _():
            for h in range(B_HEADS):
                dws_ref[h] = jnp.where(_tril(), dws_ref[h], 0.0)
                dbs_ref[h] = jnp.sum(jnp.where(head == h, dbias_ref[...], 0.0), axis=1, keepdims=True)

    def whole(a):
        return pl.BlockSpec(a.shape, lambda i, nd=len(a.shape): (0,) * nd)

    tile = pl.BlockSpec((rb, db), lambda i: (i, 0))
    vec = SDS((1, db), F32)
    dbs = SDS((B_HEADS, CHUNK, 1), F32)
    return pl.pallas_call(
        kern, grid=(steps,),
        in_specs=[tile, tile, tile, whole(w_s), whole(w_s_t), whole(bias_full), whole(ln_g), whole(ln_b)],
        out_specs=[tile, tile, whole(w_s), whole(dbs), whole(vec), whole(vec)],
        out_shape=[SDS((t, db), BF16), SDS((t, db), BF16), SDS(w_s.shape, F32), dbs, vec, vec],
        scratch_shapes=[pltpu.VMEM((CHUNK, db), F32)],
        compiler_params=_params("arbitrary"), name=name)(b_u, b_v, dbo, w_s, w_s_t, bias_full, ln_g, ln_b)


def _softmax_rows(s):
    e = jnp.exp(s - jnp.max(s, axis=-1, keepdims=True))
    return e / jnp.sum(e, axis=-1, keepdims=True)


def _attention_fwd(q, kv, batch, name):
    t, d = q.shape
    seq, mlen, hd = t // batch, kv.shape[0] // batch, d // X_HEADS
    ar = _tile(seq, ATT_ROWS)
    scale = hd ** -0.5

    def kern(q_ref, k_ref, v_ref, o_ref):
        k, v = k_ref[...], v_ref[...]
        for r0 in range(0, seq, ar):
            p = _softmax_rows(_dot_nt(q_ref[r0:r0 + ar, :], k) * scale)
            o_ref[r0:r0 + ar, :] = _dot(p.astype(BF16), v).astype(BF16)

    qs = pl.BlockSpec((seq, hd), lambda b, h: (b, h))
    return pl.pallas_call(
        kern, grid=(batch, X_HEADS),
        in_specs=[qs, pl.BlockSpec((mlen, hd), lambda b, h: (b, h)), pl.BlockSpec((mlen, hd), lambda b, h: (b, X_HEADS + h))],
        out_specs=qs, out_shape=SDS((t, d), BF16), compiler_params=_params("arbitrary", "arbitrary"), name=name)(q, kv, kv)


def _attention_bwd(q, kv, do, batch, name):
    t, d = q.shape
    seq, mlen, hd = t // batch, kv.shape[0] // batch, d // X_HEADS
    ar = _tile(seq, ATT_ROWS)
    scale = hd ** -0.5

    def kern(q_ref, k_ref, v_ref, do_ref, dq_ref, dk_ref, dv_ref):
        k, v = k_ref[...], v_ref[...]
        dk = jnp.zeros((mlen, hd), F32)
        dv = jnp.zeros((mlen, hd), F32)
        for r0 in range(0, seq, ar):
            qr, dor = q_ref[r0:r0 + ar, :], do_ref[r0:r0 + ar, :]
            p = _softmax_rows(_dot_nt(qr, k) * scale)
            dp = _dot_nt(dor, v)
            ds = (p * (dp - jnp.sum(p * dp, axis=-1, keepdims=True)) * scale).astype(BF16)
            dq_ref[r0:r0 + ar, :] = _dot(ds, k).astype(BF16)
            dk = dk + _dot_tn(ds, qr)
            dv = dv + _dot_tn(p.astype(BF16), dor)
        dk_ref[...] = dk.astype(BF16)
        dv_ref[...] = dv.astype(BF16)

    qs = pl.BlockSpec((seq, hd), lambda b, h: (b, h))
    ks = pl.BlockSpec((mlen, hd), lambda b, h: (b, h))
    dkv = SDS((kv.shape[0], d), BF16)
    return pl.pallas_call(
        kern, grid=(batch, X_HEADS),
        in_specs=[qs, ks, pl.BlockSpec((mlen, hd), lambda b, h: (b, X_HEADS + h)), qs],
        out_specs=[qs, ks, ks], out_shape=[SDS((t, d), BF16), dkv, dkv],
        compiler_params=_params("arbitrary", "arbitrary"), name=name)(q, kv, kv, do)


def _place():
    x, y, c = lax.axis_index("x"), lax.axis_index("y"), lax.axis_index("c")
    other_chips = [(1 - x, y), (x, 1 - y), (1 - x, 1 - y)]
    return x, y, c, other_chips


def _comm_call(kern, out_shape, n_pairs, name, *args):
    return pl.pallas_call(
        kern, out_shape=out_shape, in_specs=[ANY] * len(args), out_specs=jax.tree.map(lambda _: ANY, out_shape),
        scratch_shapes=[pltpu.SemaphoreType.DMA((n_pairs,)), pltpu.SemaphoreType.DMA((n_pairs,)), pltpu.SemaphoreType.DMA((n_pairs,))],
        name=name)(*args)


def _all_gather_chips(shard, name):
    r, cols = shard.shape
    rh = r // 2

    def kern(s_ref, o_ref, send_sems, recv_sems, local_sems):
        x, y, c, chips = _place()
        mine_slot = 2 * x + y
        half = pl.ds(c * rh, rh)
        other_half = pl.ds((1 - c) * rh, rh)

        def copy(k, src, dst, to):
            return pltpu.make_async_remote_copy(src_ref=src, dst_ref=dst, send_sem=send_sems.at[k], recv_sem=recv_sems.at[k],
                                                device_id=to, device_id_type=MESH)

        mine = pltpu.make_async_copy(s_ref, o_ref.at[mine_slot], local_sems.at[0])
        mine.start()
        first = [copy(j, s_ref.at[half], o_ref.at[mine_slot, half], (px, py, c)) for j, (px, py) in enumerate(chips)]
        for cp in first:
            cp.start()
        passed = []
        for j, (px, py) in enumerate(chips):
            landed = o_ref.at[2 * px + py, half]
            copy(j, landed, landed, (px, py, c)).wait_recv()
            cp = copy(3 + j, landed, landed, (x, y, 1 - c))
            cp.start()
            passed.append(cp)
        for j, (px, py) in enumerate(chips):
            theirs = o_ref.at[2 * px + py, other_half]
            copy(3 + j, theirs, theirs, (x, y, 1 - c)).wait_recv()
        for cp in first + passed:
            cp.wait_send()
        mine.wait()

    return _comm_call(kern, SDS((N_CHIPS, r, cols), shard.dtype), 6, name, shard)


def _swap_halves(fulls, name):
    def kern(*refs):
        g_refs, o_refs = refs[:len(fulls)], refs[len(fulls):2 * len(fulls)]
        send_sems, recv_sems, _ = refs[2 * len(fulls):]
        x, y, c, _ = _place()
        copies = []
        for k, (g_ref, o_ref) in enumerate(zip(g_refs, o_refs)):
            rh = o_ref.shape[1]
            copies.append(pltpu.make_async_remote_copy(
                src_ref=g_ref.at[:, pl.ds((1 - c) * rh, rh), :], dst_ref=o_ref, send_sem=send_sems.at[k], recv_sem=recv_sems.at[k],
                device_id=(x, y, 1 - c), device_id_type=MESH))
        for cp in copies:
            cp.start()
        for cp in copies:
            cp.wait()

    out_shape = [SDS((f.shape[0], f.shape[1] // 2, f.shape[2]), f.dtype) for f in fulls]
    return _comm_call(kern, out_shape, len(fulls), name, *fulls)


def _join_halves(joined, name):
    n = len(joined)

    def kern(*refs):
        j_refs = refs[:n]
        send_sems, recv_sems = refs[2 * n:]
        x, y, c, _ = _place()
        copies = []
        for k, j_ref in enumerate(j_refs):
            rh = j_ref.shape[0] // 2
            rows = j_ref.at[pl.ds(c * rh, rh)]
            cp = pltpu.make_async_remote_copy(src_ref=rows, dst_ref=rows, send_sem=send_sems.at[k], recv_sem=recv_sems.at[k],
                                              device_id=(x, y, 1 - c), device_id_type=MESH)
            cp.start()
            copies.append(cp)
        for k, (cp, j_ref) in enumerate(zip(copies, j_refs)):
            rh = j_ref.shape[0] // 2
            theirs = j_ref.at[pl.ds((1 - c) * rh, rh)]
            pltpu.make_async_remote_copy(src_ref=theirs, dst_ref=theirs, send_sem=send_sems.at[k], recv_sem=recv_sems.at[k],
                                         device_id=(x, y, 1 - c), device_id_type=MESH).wait_recv()
            cp.wait_send()

    return pl.pallas_call(
        kern, out_shape=[SDS(j.shape, j.dtype) for j in joined], in_specs=[ANY] * n, out_specs=[ANY] * n,
        input_output_aliases={k: k for k in range(n)},
        scratch_shapes=[pltpu.SemaphoreType.DMA((n,)), pltpu.SemaphoreType.DMA((n,))], name=name)(*joined)


HBM = pl.BlockSpec(memory_space=pltpu.HBM)
SEMAPHORES = pl.BlockSpec(memory_space=pltpu.SEMAPHORE)
COPIES_PER_ARRAY = N_CHIPS - 1
GATHER_COPIES = N_CHIPS


def _in_hbm(a):
    return pltpu.with_memory_space_constraint(a, pltpu.HBM)


def _gather_copies(shard_refs, land_refs, send_sems, recv_sems):
    x, y, c, chips = _place()
    copies = []
    for k, (s_ref, land_ref) in enumerate(zip(shard_refs, land_refs)):
        half = pl.ds(c * (s_ref.shape[0] // 2), s_ref.shape[0] // 2)
        for j, (px, py) in enumerate(chips):
            n = GATHER_COPIES * k + j
            copies.append(pltpu.make_async_remote_copy(
                src_ref=s_ref.at[half], dst_ref=land_ref.at[2 * x + y, half], send_sem=send_sems.at[n], recv_sem=recv_sems.at[n],
                device_id=(px, py, c), device_id_type=MESH))
        n = GATHER_COPIES * k + N_CHIPS - 1
        copies.append(pltpu.make_async_remote_copy(
            src_ref=s_ref, dst_ref=land_ref.at[2 * x + y], send_sem=send_sems.at[n], recv_sem=recv_sems.at[n],
            device_id=(x, y, 1 - c), device_id_type=MESH))
    return copies


def _owner_copies(sum_refs, land_refs, send_sems, recv_sems):
    x, y, c, chips = _place()
    copies = []
    for k, (s_ref, land_ref) in enumerate(zip(sum_refs, land_refs)):
        for j, (px, py) in enumerate(chips):
            n = COPIES_PER_ARRAY * k + j
            copies.append(pltpu.make_async_remote_copy(
                src_ref=s_ref.at[2 * px + py], dst_ref=land_ref.at[j], send_sem=send_sems.at[n], recv_sem=recv_sems.at[n],
                device_id=(px, py, c), device_id_type=MESH))
    return copies


def _copies_start(build, per_array, sources, lands, name):
    ns, nb = len(sources), len(sources) + len(lands)
    n_copies = per_array * ns

    def kern(*refs):
        for cp in build(refs[:ns], refs[ns:nb], refs[2 * nb], refs[2 * nb + 1]):
            cp.start()
        refs[-1][...] = jnp.zeros_like(refs[-1])

    bufs = [*sources, *lands]
    res = pl.pallas_call(
        kern, name=name,
        out_shape=(*[pltpu.HBM(b.shape, b.dtype) for b in bufs], pltpu.SemaphoreType.DMA((n_copies,)), pltpu.SemaphoreType.DMA((n_copies,)),
                   SDS((SUBLANES, 128), F32)),
        in_specs=[HBM] * nb, out_specs=(*[HBM] * nb, SEMAPHORES, SEMAPHORES, pl.BlockSpec(memory_space=pltpu.VMEM)),
        input_output_aliases={i: i for i in range(nb)},
        compiler_params=pltpu.CompilerParams(has_side_effects=pltpu.SideEffectType.DATAFLOW_SIDE_EFFECTING),
    )(*[_in_hbm(b) for b in bufs])
    return res[nb], res[nb + 1], list(res[:nb]), res[-1]


def _copies_wait(build, send_sems, recv_sems, bufs, ns, after, name):
    nb = len(bufs)

    def kern(*refs):
        for cp in build(refs[:ns], refs[ns:nb], refs[nb], refs[nb + 1]):
            cp.wait_send()
            cp.wait_recv()

    res = pl.pallas_call(
        kern, name=name, out_shape=tuple(pltpu.HBM(b.shape, b.dtype) for b in bufs),
        in_specs=[HBM] * nb + [SEMAPHORES, SEMAPHORES, ANY], out_specs=tuple([HBM] * nb),
        input_output_aliases={i: i for i in range(nb)},
        compiler_params=pltpu.CompilerParams(has_side_effects=pltpu.SideEffectType.DATAFLOW_SIDE_EFFECTING),
    )(*bufs, send_sems, recv_sems, after)
    return list(res[:ns]), list(res[ns:])


def _gather_finish(lands, name):
    ns = len(lands)

    def kern(*refs):
        l_refs = refs[:ns]
        send_sems, recv_sems = refs[2 * ns:]
        x, y, c, chips = _place()
        passed = []
        for k, l_ref in enumerate(l_refs):
            rh = l_ref.shape[1] // 2
            for j, (px, py) in enumerate(chips):
                landed = l_ref.at[2 * px + py, pl.ds(c * rh, rh)]
                cp = pltpu.make_async_remote_copy(
                    src_ref=landed, dst_ref=landed, send_sem=send_sems.at[COPIES_PER_ARRAY * k + j],
                    recv_sem=recv_sems.at[COPIES_PER_ARRAY * k + j], device_id=(x, y, 1 - c), device_id_type=MESH)
                cp.start()
                passed.append(cp)
        for k, l_ref in enumerate(l_refs):
            rh = l_ref.shape[1] // 2
            for j, (px, py) in enumerate(chips):
                theirs = l_ref.at[2 * px + py, pl.ds((1 - c) * rh, rh)]
                pltpu.make_async_remote_copy(
                    src_ref=theirs, dst_ref=theirs, send_sem=send_sems.at[COPIES_PER_ARRAY * k + j],
                    recv_sem=recv_sems.at[COPIES_PER_ARRAY * k + j], device_id=(x, y, 1 - c), device_id_type=MESH).wait_recv()
        for cp in passed:
            cp.wait_send()

    return pl.pallas_call(
        kern, out_shape=[SDS(l.shape, l.dtype) for l in lands], in_specs=[ANY] * ns, out_specs=[ANY] * ns,
        input_output_aliases={k: k for k in range(ns)},
        scratch_shapes=[pltpu.SemaphoreType.DMA((COPIES_PER_ARRAY * ns,)), pltpu.SemaphoreType.DMA((COPIES_PER_ARRAY * ns,))],
        name=name)(*lands)


def _add_sibling(full, theirs, out_dtype, name):
    n, r, cols = full.shape
    rh = r // 2
    tr = _tile(rh, ROW_TILE)
    nb = rh // tr

    def kern(c_ref, a_ref, b_ref, o_ref):
        o_ref[...] = (a_ref[...] + b_ref[...]).astype(out_dtype)

    c = lax.axis_index("c").astype(jnp.int32).reshape(1)
    return pl.pallas_call(
        kern, out_shape=SDS((n, rh, cols), out_dtype),
        grid_spec=pltpu.PrefetchScalarGridSpec(
            num_scalar_prefetch=1, grid=(n, nb),
            in_specs=[pl.BlockSpec((1, tr, cols), lambda j, i, c_ref: (j, c_ref[0] * nb + i, 0)),
                      pl.BlockSpec((1, tr, cols), lambda j, i, c_ref: (j, i, 0))],
            out_specs=pl.BlockSpec((1, tr, cols), lambda j, i, c_ref: (j, i, 0))),
        compiler_params=_params("arbitrary", "arbitrary"), name=name)(c, full, theirs)


def _add_owners(partial, received, name):
    n, rh, cols = partial.shape
    tr = _tile(rh, ROW_TILE)
    nb = rh // tr

    def kern(s_ref, a_ref, b_ref, o_ref):
        acc = a_ref[0].astype(F32)
        for j in range(N_CHIPS - 1):
            acc = acc + b_ref[j].astype(F32)
        o_ref[...] = acc

    place = jnp.stack([2 * lax.axis_index("x") + lax.axis_index("y"), lax.axis_index("c")]).astype(jnp.int32)
    return pl.pallas_call(
        kern, out_shape=SDS((2 * rh, cols), F32),
        grid_spec=pltpu.PrefetchScalarGridSpec(
            num_scalar_prefetch=1, grid=(nb,),
            in_specs=[pl.BlockSpec((1, tr, cols), lambda i, s_ref: (s_ref[0], i, 0)),
                      pl.BlockSpec((N_CHIPS - 1, tr, cols), lambda i, s_ref: (0, i, 0))],
            out_specs=pl.BlockSpec((tr, cols), lambda i, s_ref: (s_ref[1] * nb + i, 0))),
        compiler_params=_params("arbitrary"), name=name)(place, partial, received)


def _pack(parts, lead):
    lead_shape = parts[0].shape[:lead]
    flat = jnp.concatenate([p.reshape(lead_shape + (-1,)) for p in parts], axis=-1)
    n = flat.shape[-1]
    rows = _round_up(-(-n // PACK_W), PACK_ROWS)
    flat = jnp.pad(flat, [(0, 0)] * lead + [(0, rows * PACK_W - n)])
    return flat.reshape(lead_shape + (rows, PACK_W))


def _unpack(buf, shapes, lead):
    lead_shape = buf.shape[:lead]
    flat = buf.reshape(lead_shape + (-1,))
    out, off = [], 0
    for s in shapes:
        n = math.prod(s)
        out.append(flat[..., off:off + n].reshape(lead_shape + tuple(s)))
        off += n
    return out


def _to_slots(full, axis):
    s = full.shape
    split = full.reshape(s[:axis] + (N_CHIPS, s[axis] // N_CHIPS) + s[axis + 1:])
    return jnp.moveaxis(split, axis, 0)


def _from_slots(slots, axis):
    moved = jnp.moveaxis(slots, 0, axis)
    s = moved.shape
    return moved.reshape(s[:axis] + (s[axis] * s[axis + 1],) + s[axis + 2:])


MATMUL_WEIGHTS = (("w_in", 1), ("w_out", 0), ("w_q", 0), ("w_kv", 1), ("w_o", 0), ("w_ff1", 1), ("w_ff2", 0))
CONV_WEIGHTS = (("conv_a_w", 1), ("conv_c_w", 1))
REPLICATED = ("conv_a_b", "ln_a_g", "ln_a_b", "ln_v_g", "ln_v_b", "w_s", "b_s",
              "ln1_g", "ln1_b", "ln2_g", "ln2_b", "ln3_g", "ln3_b")
WEIGHT_ORDER = ("w_in", "conv_a_w", "conv_a_b", "ln_a_g", "ln_a_b", "ln_v_g", "ln_v_b", "w_s", "b_s", "conv_c_w", "w_out",
                "ln1_g", "ln1_b", "w_q", "w_kv", "w_o", "ln2_g", "ln2_b", "w_ff1", "w_ff2", "ln3_g", "ln3_b")


def _row(v):
    return v.reshape(1, -1)


def kernel(x, mem, w_in, conv_a_w, conv_a_b, ln_a_g, ln_a_b, ln_v_g, ln_v_b, w_s, b_s, conv_c_w, w_out, ln1_g, ln1_b, w_q, w_kv, w_o, ln2_g, ln2_b, w_ff1, w_ff2, ln3_g, ln3_b, loss_target, m_w_in, m_conv_a_w, m_conv_a_b, m_ln_a_g, m_ln_a_b, m_ln_v_g, m_ln_v_b, m_w_s, m_b_s, m_conv_c_w, m_w_out, m_ln1_g, m_ln1_b, m_w_q, m_w_kv, m_w_o, m_ln2_g, m_ln2_b, m_w_ff1, m_w_ff2, m_ln3_g, m_ln3_b, v_w_in, v_conv_a_w, v_conv_a_b, v_ln_a_g, v_ln_a_b, v_ln_v_g, v_ln_v_b, v_w_s, v_b_s, v_conv_c_w, v_w_out, v_ln1_g, v_ln1_b, v_w_q, v_w_kv, v_w_o, v_ln2_g, v_ln2_b, v_w_ff1, v_w_ff2, v_ln3_g, v_ln3_b):
    given = dict(locals())
    weights = {n: given[n] for n in WEIGHT_ORDER}
    moment1 = {n: given["m_" + n] for n in WEIGHT_ORDER}
    moment2 = {n: given["v_" + n] for n in WEIGHT_ORDER}

    depth = w_in.shape[0]
    batch, seq, d = x.shape
    t = batch * seq
    hd = d // HEADS
    d_a, d_b, d_c = A_HEADS * hd, B_HEADS * hd, C_HEADS * hd
    widths = (d_a, d_a, d_b, d_b, d_c, d_c, d_c)
    in_offs = [sum(widths[:k]) for k in range(len(widths))]
    alpha = (2.0 * depth) ** 0.25
    layers = range(depth)
    tm = _tile(t, ROW_TILE)
    tm_wide = _tile(t, WIDE_ROW_TILE)

    conv_shards = [weights[n][l] for l in layers for n, _ in CONV_WEIGHTS]
    gathers, tokens = [], []
    for l in layers:
        shards = [weights[n][l].astype(BF16) for n, _ in MATMUL_WEIGHTS] + ([_pack(conv_shards, 0)] if l == 0 else [])
        lands = [lax.empty((N_CHIPS,) + s.shape, s.dtype) for s in shards]
        send_sems, recv_sems, bufs, token = _copies_start(_gather_copies, GATHER_COPIES, shards, lands, f"gather_start_{l}")
        gathers.append((send_sems, recv_sems, bufs, len(shards)))
        tokens.append(token[0, 0])

    def gathered(l, after):
        send_sems, recv_sems, bufs, ns = gathers[l]
        _, lands = _copies_wait(_gather_copies, send_sems, recv_sems, bufs, ns, after, f"gather_wait_{l}")
        return _gather_finish(lands, "gather_finish")

    def layer_weights(l, slots, conv_full):
        w = {}
        for k, (n, axis) in enumerate(MATMUL_WEIGHTS):
            w[n] = _from_slots(slots[k], axis)
        for k, (n, axis) in enumerate(CONV_WEIGHTS):
            w[n] = _from_slots(conv_full[l * len(CONV_WEIGHTS) + k], axis)
        for n in REPLICATED:
            w[n] = weights[n][l]
        w["w_s_t"] = jnp.swapaxes(w["w_s"], 1, 2)
        w["bias_full"] = jnp.repeat(w["b_s"].T, hd, axis=1)
        return w

    xf = x.reshape(t, d)
    xb = (xf + sum(tokens)).astype(BF16)
    memb = mem.reshape(-1, d).astype(BF16)
    saved, full, conv_full = [], [], None
    for l in layers:
        slots = gathered(l, xb)
        if l == 0:
            conv_full = _unpack(slots[-1], [s.shape for s in conv_shards], 1)
        w = layer_weights(l, slots, conv_full)
        full.append(w)
        s = {"x0b": xb}
        proj = _in_proj(xb, w["w_in"], widths, "in_proj")
        s["proj"] = proj
        a_val, a_gate, b_u, b_v, c_b, c_c, c_x = proj
        a_out, s["ac"] = _mix_a_fwd(a_val, a_gate, w["conv_a_w"], _row(w["conv_a_b"]), _row(w["ln_a_g"]), _row(w["ln_a_b"]),
                                    batch, "mix_a_fwd")
        b_out = _mix_b_fwd(b_u, b_v, w["w_s"], w["bias_full"], _row(w["ln_v_g"]), _row(w["ln_v_b"]), "mix_b_fwd")
        c_out = _mix_c_fwd(c_b, c_c, c_x, w["conv_c_w"], batch, "mix_c_fwd")
        s["cat"] = (a_out, b_out, c_out)
        s["z1"], xf, xb = _proj_ln([a_out, b_out, c_out], w["w_out"], xf, _row(w["ln1_g"]), _row(w["ln1_b"]), alpha, tm, "out_proj_ln")
        s["x1b"] = xb
        s["q"] = _matmul(xb, w["w_q"], "q_proj")
        s["kv"] = _matmul(memb, w["w_kv"], "kv_proj")
        s["o"] = _attention_fwd(s["q"], s["kv"], batch, "attention_fwd")
        s["z2"], xf, xb = _proj_ln([s["o"]], w["w_o"], xf, _row(w["ln2_g"]), _row(w["ln2_b"]), alpha, tm, "o_proj_ln")
        s["x2b"] = xb
        s["h"], s["r"] = _ff1(xb, w["w_ff1"], "ff1")
        s["z3"], xf, xb = _proj_ln([s["r"]], w["w_ff2"], xf, _row(w["ln3_g"]), _row(w["ln3_b"]), alpha, tm_wide, "ff2_ln")
        saved.append(s)

    dx, loss_block = _loss_head(xf, loss_target.reshape(t, d), "loss_head")
    loss = lax.psum(loss_block[0, 0], ("x", "y", "c"))
    grads = [None] * depth
    reductions = [None] * depth
    started = None
    n_mm = len(MATMUL_WEIGHTS)

    def rows_to_slots(full):
        return full.reshape(N_CHIPS, full.shape[1] // N_CHIPS, full.shape[2])

    for l in reversed(layers):
        w, s, g = full[l], saved[l], {}
        gain3 = _row(w["ln3_g"]) if started is None else _row(w["ln3_g"]) + started[0, 0]
        dz3, dz3b, g["ln3_g"], g["ln3_b"] = _ln_bwd_rows(dx, s["z3"], gain3, "ln3_bwd")
        g["w_ff2"] = rows_to_slots(_mm_tn(s["r"], dz3b, "d_w_ff2"))
        dh = _bwd_ff2(dz3b, s["h"], w["w_ff2"], "d_ff_hidden")
        g["w_ff1"] = _mm_tn(s["x2b"], dh, "d_w_ff1", slots=N_CHIPS)
        dx = _bwd_data([dh], w["w_ff1"], [0], dz3, alpha, F32, tm_wide, "d_x2")
        dz2, dz2b, g["ln2_g"], g["ln2_b"] = _ln_bwd_rows(dx, s["z2"], _row(w["ln2_g"]), "ln2_bwd")
        g["w_o"] = rows_to_slots(_mm_tn(s["o"], dz2b, "d_w_o"))
        do = _bwd_data([dz2b], w["w_o"], [0], None, alpha, BF16, tm, "d_att_out")
        dq, dk, dv = _attention_bwd(s["q"], s["kv"], do, batch, "attention_bwd")
        g["w_kv"] = _mm_tn(memb, jnp.concatenate([dk, dv], axis=1), "d_w_kv", slots=N_CHIPS)
        g["w_q"] = rows_to_slots(_mm_tn(s["x1b"], dq, "d_w_q"))
        dx = _bwd_data([dq], w["w_q"], [0], dz2, alpha, F32, tm, "d_x1")
        dz1, dz1b, g["ln1_g"], g["ln1_b"] = _ln_bwd_rows(dx, s["z1"], _row(w["ln1_g"]), "ln1_bwd")
        g["w_out"] = _mm_tn_rows(s["cat"], dz1b, "d_w_out").reshape(N_CHIPS, d // N_CHIPS, d)
        da_, dbo, dco = _bwd_out_proj(dz1b, w["w_out"], (d_a, d_b, d_c), "d_mixer_out")
        a_val, a_gate, b_u, b_v, c_b, c_c, c_x = s["proj"]
        dav, dag, g["conv_a_w"], g["conv_a_b"], g["ln_a_g"], g["ln_a_b"] = _mix_a_bwd(
            a_val, a_gate, s["ac"], da_, w["conv_a_w"], _row(w["ln_a_g"]), _row(w["ln_a_b"]), batch, "mix_a_bwd")
        dbu, dbv, g["w_s"], dbs, g["ln_v_g"], g["ln_v_b"] = _mix_b_bwd(
            b_u, b_v, dbo, w["w_s"], w["w_s_t"], w["bias_full"], _row(w["ln_v_g"]), _row(w["ln_v_b"]), "mix_b_bwd")
        g["b_s"] = dbs.reshape(B_HEADS, CHUNK)
        dcb, dcc, dcx, g["conv_c_w"] = _mix_c_bwd(c_b, c_c, c_x, dco, w["conv_c_w"], batch, "mix_c_bwd")
        dproj = [dav, dag, dbu, dbv, dcb, dcc, dcx]
        g["w_in"] = _to_slots(_mm_tn_cols(s["x0b"], dproj, "d_w_in"), 1)
        dx = _bwd_data(dproj, w["w_in"], in_offs, dz1, alpha, F32, tm, "d_x0")
        grads[l] = g

        slot_grads = [g[n] for n, _ in MATMUL_WEIGHTS]
        wire = [BF16] * n_mm
        if l == 0:
            conv_slots = [_to_slots(grads[k][n], axis) for k in layers for n, axis in CONV_WEIGHTS]
            rep_parts = [jnp.stack([grads[k][n].reshape(weights[n].shape[1:]) for k in layers]) for n in REPLICATED]
            rep_flat = jnp.concatenate([p.reshape(-1) for p in rep_parts])
            n_rep = rep_flat.shape[0]
            per_chip = _round_up(-(-n_rep // N_CHIPS), PACK_W * PACK_ROWS)
            rep_slots = jnp.pad(rep_flat, (0, N_CHIPS * per_chip - n_rep)).reshape(N_CHIPS, per_chip)
            slot_grads.append(_pack(conv_slots + [rep_slots], 1))
            wire.append(F32)
        theirs = _swap_halves(slot_grads, "rs_swap")
        sums = [_add_sibling(f, th, dt, "rs_add_sibling") for f, th, dt in zip(slot_grads, theirs, wire)]
        lands = [lax.empty((N_CHIPS - 1,) + p.shape[1:], p.dtype) for p in sums]
        send_sems, recv_sems, bufs, started = _copies_start(_owner_copies, COPIES_PER_ARRAY, sums, lands, f"rs_send_start_{l}")
        reductions[l] = (send_sems, recv_sems, bufs, len(sums))
    grad_x = dx.reshape(batch, seq, d)

    reduced, after = [None] * depth, started
    for l in reversed(layers):
        send_sems, recv_sems, bufs, ns = reductions[l]
        sums, lands = _copies_wait(_owner_copies, send_sems, recv_sems, bufs, ns, after, f"rs_send_wait_{l}")
        mine = [_add_owners(p, r, "rs_add_owners") for p, r in zip(sums, lands)]
        reduced[l] = _join_halves(mine, "rs_join")
        after = reduced[l][0]
    small = _unpack(reduced[0][n_mm], [c.shape[1:] for c in conv_slots] + [(per_chip,)], 0)
    conv_grad, rep_mine = small[:-1], small[-1]
    rep_all = _all_gather_chips(rep_mine.reshape(-1, PACK_W), "gather_small_grads").reshape(-1)[:n_rep]

    grad = {}
    for k, (n, _) in enumerate(MATMUL_WEIGHTS):
        grad[n] = jnp.stack([reduced[l][k] for l in layers])
    for k, (n, _) in enumerate(CONV_WEIGHTS):
        grad[n] = jnp.stack([conv_grad[l * len(CONV_WEIGHTS) + k] for l in layers])
    off = 0
    for n in REPLICATED:
        size = math.prod(weights[n].shape)
        grad[n] = rep_all[off:off + size].reshape(weights[n].shape)
        off += size

    delta, new_m, new_v = {}, {}, {}
    for n, _ in MATMUL_WEIGHTS:
        shape = weights[n].shape
        as_rows = lambda a: a.reshape(-1, shape[-1])
        delta[n], new_m[n], new_v[n] = (
            r.reshape(shape) for r in _adamw(as_rows(weights[n]), as_rows(grad[n]), as_rows(moment1[n]), as_rows(moment2[n]), "adamw"))
    small_names = [n for n, _ in CONV_WEIGHTS] + list(REPLICATED)
    small_shapes = [weights[n].shape for n in small_names]
    packed = [_pack([src[n] for n in small_names], 0) for src in (weights, grad, moment1, moment2)]
    for dst, res in zip((delta, new_m, new_v), _adamw(*packed, "adamw_small")):
        for n, a in zip(small_names, _unpack(res, small_shapes, 0)):
            dst[n] = a

    return (loss, grad_x, *[grad[n] for n in WEIGHT_ORDER], *[delta[n] for n in WEIGHT_ORDER],
            *[new_m[n] for n in WEIGHT_ORDER], *[new_v[n] for n in WEIGHT_ORDER])
```

```python
import functools
import math

import jax
import jax.numpy as jnp
from jax import lax
from jax.experimental import pallas as pl
from jax.experimental.pallas import tpu as pltpu

F32 = jnp.float32
BF16 = jnp.bfloat16
SDS = jax.ShapeDtypeStruct

HEADS = 16
A_HEADS, B_HEADS, C_HEADS = 6, 4, 6
X_HEADS = 4
CHUNK = 128
LN_EPS = 1e-5
ADAM_LR, ADAM_B1, ADAM_B2, ADAM_EPS, ADAM_WD, ADAM_STEP = 0.001, 0.9, 0.999, 1e-08, 0.01, 10

N_CHIPS = 4
V7X_VMEM_LIMIT = 56 << 20
SUBLANES = 8
PACK_W = 1024
PACK_ROWS = 32
ROW_TILE = 512
WIDE_ROW_TILE = 256
CONV_ROWS = 256
SUB_ROWS = 64
ATT_ROWS = 512
TN_TILE = 1024
MESH = pl.DeviceIdType.MESH
ANY = pl.BlockSpec(memory_space=pl.ANY)


def _tile(n, t):
    for d in range(min(n, t), 0, -1):
        if n % d == 0 and d % (2 * SUBLANES) == 0:
            return d
    return n


def _round_up(n, m):
    return -(-n // m) * m


def _params(*sem):
    return pltpu.CompilerParams(dimension_semantics=sem or None, vmem_limit_bytes=V7X_VMEM_LIMIT)


def _dot(a, b):
    return jnp.dot(a, b, preferred_element_type=F32)


def _dot_nt(a, b):
    return lax.dot_general(a, b, (((1,), (1,)), ((), ())), preferred_element_type=F32)


def _dot_tn(a, b):
    return lax.dot_general(a, b, (((0,), (0,)), ((), ())), preferred_element_type=F32)


def _sigmoid(x):
    return 1.0 / (1.0 + jnp.exp(-x))


def _gelu(x):
    return 0.5 * x * (1.0 + lax.erf(x * (2.0 ** -0.5)))


def _gelu_grad(x):
    return 0.5 * (1.0 + lax.erf(x * (2.0 ** -0.5))) + x * jnp.exp(-0.5 * x * x) * ((2.0 * math.pi) ** -0.5)


def _ln_stats(z):
    mu = jnp.mean(z, axis=-1, keepdims=True)
    zc = z - mu
    rstd = lax.rsqrt(jnp.mean(zc * zc, axis=-1, keepdims=True) + LN_EPS)
    return zc * rstd, rstd


def _ln_bwd(dy, y, rstd, g):
    dyh = dy * g
    return rstd * (dyh - jnp.mean(dyh, axis=-1, keepdims=True) - y * jnp.mean(dyh * y, axis=-1, keepdims=True))


def _colsum(x):
    return jnp.sum(x, axis=0, keepdims=True)


def _rowwise(body, rows, consts, outs, accs=(), *, tm, name):
    t = rows[0].shape[0]
    steps = t // tm

    def kern(*refs):
        body(pl.program_id(0), steps, *refs)

    def whole(a):
        return pl.BlockSpec(a.shape, lambda i, nd=len(a.shape): (0,) * nd)

    in_specs = [pl.BlockSpec((tm, r.shape[1]), lambda i: (i, 0)) for r in rows] + [whole(c) for c in consts]
    out_shape = [SDS((t, n), dt) for n, dt in outs] + [SDS(s, dt) for s, dt in accs]
    out_specs = [pl.BlockSpec((tm, n), lambda i: (i, 0)) for n, _ in outs] + [whole(SDS(s, dt)) for s, dt in accs]
    return pl.pallas_call(kern, grid=(steps,), in_specs=in_specs, out_specs=out_specs, out_shape=out_shape,
                          compiler_params=_params("arbitrary"), name=name)(*rows, *consts)


def _in_proj(xb, w_in, widths, name):
    offs = [sum(widths[:k]) for k in range(len(widths))]

    def body(i, steps, x_ref, w_ref, *o_refs):
        x = x_ref[...]
        for o_ref, off, n in zip(o_refs, offs, widths):
            o_ref[...] = _dot(x, w_ref[:, off:off + n])

    return _rowwise(body, [xb], [w_in], [(n, F32) for n in widths], tm=_tile(xb.shape[0], ROW_TILE), name=name)


def _matmul(ab, w, name):
    def body(i, steps, a_ref, w_ref, o_ref):
        o_ref[...] = _dot(a_ref[...], w_ref[...]).astype(BF16)

    return _rowwise(body, [ab], [w], [(w.shape[1], BF16)], tm=_tile(ab.shape[0], ROW_TILE), name=name)[0]


def _ff1(xb, w, name):
    def body(i, steps, a_ref, w_ref, h_ref, r_ref):
        h = _dot(a_ref[...], w_ref[...])
        h_ref[...] = h.astype(BF16)
        r = jnp.maximum(h, 0.0)
        r_ref[...] = (r * r).astype(BF16)

    n = w.shape[1]
    return _rowwise(body, [xb], [w], [(n, BF16), (n, BF16)], tm=_tile(xb.shape[0], WIDE_ROW_TILE), name=name)


def _proj_ln(a_list, w, x, g, b, alpha, tm, name):
    widths = [a.shape[1] for a in a_list]
    offs = [sum(widths[:k]) for k in range(len(widths))]
    na = len(a_list)

    def body(i, steps, *refs):
        a_refs, (x_ref, w_ref, g_ref, b_ref, z_ref, xn_ref, xb_ref) = refs[:na], refs[na:]
        acc = alpha * x_ref[...]
        for a_ref, off, n in zip(a_refs, offs, widths):
            acc = acc + _dot(a_ref[...], w_ref[off:off + n, :])
        z_ref[...] = acc
        y, _ = _ln_stats(acc)
        xn = y * g_ref[...] + b_ref[...]
        xn_ref[...] = xn
        xb_ref[...] = xn.astype(BF16)

    d = w.shape[1]
    return _rowwise(body, [*a_list, x], [w, g, b], [(d, F32), (d, F32), (d, BF16)], tm=tm, name=name)


def _bwd_ff2(dzb, h, w_ff2, name):
    def body(i, steps, dz_ref, h_ref, w_ref, dh_ref):
        dr = _dot_nt(dz_ref[...], w_ref[...])
        dh_ref[...] = (dr * (2.0 * jnp.maximum(h_ref[...].astype(F32), 0.0))).astype(BF16)

    return _rowwise(body, [dzb, h], [w_ff2], [(h.shape[1], BF16)], tm=_tile(h.shape[0], WIDE_ROW_TILE), name=name)[0]


def _bwd_nt(g_list, w, col_offs, res, alpha, tm, name):
    ng = len(g_list)
    widths = [g.shape[1] for g in g_list]

    def body(i, steps, *refs):
        g_refs = refs[:ng]
        if res is None:
            w_ref, o_ref = refs[ng:]
            acc = None
        else:
            r_ref, w_ref, o_ref = refs[ng:]
            acc = alpha * r_ref[...]
        for g_ref, off, n in zip(g_refs, col_offs, widths):
            part = _dot_nt(g_ref[...], w_ref[:, off:off + n])
            acc = part if acc is None else acc + part
        o_ref[...] = acc.astype(o_ref.dtype)

    rows = list(g_list) + ([] if res is None else [res])
    return rows, w, body, tm, name


def _bwd_data(g_list, w, col_offs, res, alpha, out_dtype, tm, name):
    rows, w, body, tm, name = _bwd_nt(g_list, w, col_offs, res, alpha, tm, name)
    return _rowwise(body, rows, [w], [(w.shape[0], out_dtype)], tm=tm, name=name)[0]


def _bwd_data_ln(g_list, w, col_offs, res, alpha, z, gain, tm, name):
    ng = len(g_list)
    widths = [g.shape[1] for g in g_list]

    def body(i, steps, *refs):
        g_refs = refs[:ng]
        r_ref, z_ref, w_ref, gain_ref, dz_ref, dzb_ref, dg_ref, db_ref = refs[ng:]

        @pl.when(i == 0)
        def _():
            dg_ref[...] = jnp.zeros_like(dg_ref)
            db_ref[...] = jnp.zeros_like(db_ref)

        dy = alpha * r_ref[...]
        for g_ref, off, n in zip(g_refs, col_offs, widths):
            dy = dy + _dot_nt(g_ref[...], w_ref[:, off:off + n])
        y, rstd = _ln_stats(z_ref[...])
        dz = _ln_bwd(dy, y, rstd, gain_ref[...])
        dz_ref[...] = dz
        dzb_ref[...] = dz.astype(BF16)
        dg_ref[...] += _colsum(dy * y)
        db_ref[...] += _colsum(dy)

    d = z.shape[1]
    return _rowwise(body, [*g_list, res, z], [w, gain], [(d, F32), (d, BF16)], [((1, d), F32), ((1, d), F32)], tm=tm, name=name)


def _loss_head_ln(y, target, z, gain, name):
    d = y.shape[1]

    def body(i, steps, y_ref, t_ref, z_ref, gain_ref, dz_ref, dzb_ref, dg_ref, db_ref, l_ref):
        @pl.when(i == 0)
        def _():
            for r in (dg_ref, db_ref, l_ref):
                r[...] = jnp.zeros_like(r)

        err = y_ref[...] - t_ref[...]
        l_ref[...] += jnp.sum(err * err) * (0.5 / d)
        dy = err * (1.0 / d)
        yn, rstd = _ln_stats(z_ref[...])
        dz = _ln_bwd(dy, yn, rstd, gain_ref[...])
        dz_ref[...] = dz
        dzb_ref[...] = dz.astype(BF16)
        dg_ref[...] += _colsum(dy * yn)
        db_ref[...] += _colsum(dy)

    return _rowwise(body, [y, target, z], [gain], [(d, F32), (d, BF16)],
                    [((1, d), F32), ((1, d), F32), ((SUBLANES, 128), F32)], tm=_tile(y.shape[0], ROW_TILE), name=name)


def _bwd_out_proj(dzb, w_out, widths, name):
    offs = [sum(widths[:k]) for k in range(len(widths))]

    def body(i, steps, dz_ref, w_ref, *o_refs):
        dz = dz_ref[...]
        for o_ref, off, n in zip(o_refs, offs, widths):
            o_ref[...] = _dot_nt(dz, w_ref[off:off + n, :])

    return _rowwise(body, [dzb], [w_out], [(n, F32) for n in widths], tm=_tile(dzb.shape[0], ROW_TILE), name=name)


def _adamw(w, g, m, v, name):
    def body(i, steps, w_ref, g_ref, m_ref, v_ref, d_ref, nm_ref, nv_ref):
        g_ = g_ref[...]
        nm = ADAM_B1 * m_ref[...] + (1.0 - ADAM_B1) * g_
        nv = ADAM_B2 * v_ref[...] + (1.0 - ADAM_B2) * (g_ * g_)
        m_hat = nm / (1.0 - ADAM_B1 ** ADAM_STEP)
        v_hat = nv / (1.0 - ADAM_B2 ** ADAM_STEP)
        d_ref[...] = -ADAM_LR * (m_hat / (jnp.sqrt(v_hat) + ADAM_EPS) + ADAM_WD * w_ref[...])
        nm_ref[...] = nm
        nv_ref[...] = nv

    c = w.shape[1]
    return _rowwise(body, [w, g, m, v], [], [(c, F32)] * 3, tm=_tile(w.shape[0], ROW_TILE), name=name)


def _mm_tn(a, g, name, slots=1):
    t, ka = a.shape
    n = g.shape[1]
    ta, tn, tk = _tile(ka, TN_TILE), _tile(n // slots, TN_TILE), _tile(t, TN_TILE)
    per = n // slots // tn

    def kern(a_ref, g_ref, o_ref):
        @pl.when(pl.program_id(2) == 0)
        def _():
            o_ref[...] = jnp.zeros_like(o_ref)

        o_ref[0] += _dot_tn(a_ref[...], g_ref[...])

    return pl.pallas_call(
        kern, grid=(ka // ta, n // tn, t // tk),
        in_specs=[pl.BlockSpec((tk, ta), lambda i, j, k: (k, i)), pl.BlockSpec((tk, tn), lambda i, j, k: (k, j))],
        out_specs=pl.BlockSpec((1, ta, tn), lambda i, j, k: (j // per, i, j % per)), out_shape=SDS((slots, ka, n // slots), F32),
        compiler_params=_params("arbitrary", "arbitrary", "arbitrary"), name=name)(a, g)


def _mm_tn_cols(a, g_list, name):
    t, ka = a.shape
    widths = [g.shape[1] for g in g_list]
    offs = [sum(widths[:k]) for k in range(len(widths))]
    ta, tk = _tile(ka, ROW_TILE), _tile(t, TN_TILE)

    def kern(a_ref, *refs):
        g_refs, o_ref = refs[:-1], refs[-1]

        @pl.when(pl.program_id(1) == 0)
        def _():
            o_ref[...] = jnp.zeros_like(o_ref)

        a_ = a_ref[...]
        for g_ref, off, n in zip(g_refs, offs, widths):
            o_ref[:, off:off + n] += _dot_tn(a_, g_ref[...])

    return pl.pallas_call(
        kern, grid=(ka // ta, t // tk),
        in_specs=[pl.BlockSpec((tk, ta), lambda i, k: (k, i))] + [pl.BlockSpec((tk, n), lambda i, k: (k, 0)) for n in widths],
        out_specs=pl.BlockSpec((ta, sum(widths)), lambda i, k: (i, 0)), out_shape=SDS((ka, sum(widths)), F32),
        compiler_params=_params("arbitrary", "arbitrary"), name=name)(a, *g_list)


def _mm_tn_rows(a_list, g, name):
    t, n = g.shape
    widths = [a.shape[1] for a in a_list]
    offs = [sum(widths[:k]) for k in range(len(widths))]
    tn, tk = _tile(n, TN_TILE), _tile(t, TN_TILE)

    def kern(*refs):
        a_refs, g_ref, o_ref = refs[:-2], refs[-2], refs[-1]

        @pl.when(pl.program_id(1) == 0)
        def _():
            o_ref[...] = jnp.zeros_like(o_ref)

        g_ = g_ref[...]
        for a_ref, off, ka in zip(a_refs, offs, widths):
            o_ref[off:off + ka, :] += _dot_tn(a_ref[...], g_)

    return pl.pallas_call(
        kern, grid=(n // tn, t // tk),
        in_specs=[pl.BlockSpec((tk, ka), lambda j, k: (k, 0)) for ka in widths] + [pl.BlockSpec((tk, tn), lambda j, k: (k, j))],
        out_specs=pl.BlockSpec((sum(widths), tn), lambda j, k: (0, j)), out_shape=SDS((sum(widths), n), F32),
        compiler_params=_params("arbitrary", "arbitrary"), name=name)(*a_list, g)


def _conv_geometry(seq, taps):
    pad = _round_up(taps - 1, SUBLANES)
    rc = _tile(seq, CONV_ROWS)
    assert rc % pad == 0 and seq % rc == 0
    return pad, rc, seq // rc


def _chunk_spec(rc, n, nch):
    return pl.BlockSpec((rc, n), lambda b, i: (b * nch + i, 0))


def _prev_halo_spec(pad, rc, n, nch):
    per = rc // pad
    return pl.BlockSpec((pad, n), lambda b, i: (jnp.maximum((b * nch + i) * per - 1, 0), 0))


def _next_halo_spec(pad, rc, n, nch, total_rows):
    per = rc // pad
    last = total_rows // pad - 1
    return pl.BlockSpec((pad, n), lambda b, i: (jnp.minimum((b * nch + i + 1) * per, last), 0))


def _whole2(a):
    return pl.BlockSpec(a.shape, lambda b, i, nd=len(a.shape): (0,) * nd)


def _sub_rows(rc):
    return SUB_ROWS if rc % SUB_ROWS == 0 else rc


def _build_shifts(sh_ref, src_ref, offsets):
    rows = src_ref.shape[0]
    for r in sorted({o % SUBLANES for o in offsets} - {0}):
        sh_ref[r, 0:rows - SUBLANES, :] = src_ref[r:r + rows - SUBLANES, :]


def _read_shifted(sh_ref, src_ref, o, s0, sub):
    r = o % SUBLANES
    a = o - r + s0
    return src_ref[a:a + sub, :] if r == 0 else sh_ref[r, a:a + sub, :]


def _tap_sum(sh_ref, src_ref, w_ref, offsets, s0, sub):
    acc = None
    for k, o in enumerate(offsets):
        term = _read_shifted(sh_ref, src_ref, o, s0, sub) * w_ref[k:k + 1, :]
        acc = term if acc is None else acc + term
    return acc


def _row_groups(x):
    acc = x[0:SUBLANES, :]
    for g0 in range(SUBLANES, x.shape[0], SUBLANES):
        acc = acc + x[g0:g0 + SUBLANES, :]
    return acc


def _mix_a_fwd(a_val, a_gate, conv_w, conv_b, ln_g, ln_b, batch, name):
    t, da = a_val.shape
    taps = conv_w.shape[0]
    pad, rc, nch = _conv_geometry(t // batch, taps)
    sub = _sub_rows(rc)
    offs = [pad - (taps - 1) + k for k in range(taps)]

    def kern(av_ref, ag_ref, pav_ref, pag_ref, w_ref, cb_ref, g_ref, b_ref, a_ref, ac_ref, pad_ref, sh_ref):
        first = pl.program_id(1) == 0
        pad_ref[0:pad, :] = jnp.where(first, 0.0, pav_ref[...] * _sigmoid(pag_ref[...]))
        for s0 in range(0, rc, sub):
            pad_ref[pad + s0:pad + s0 + sub, :] = av_ref[s0:s0 + sub, :] * _sigmoid(ag_ref[s0:s0 + sub, :])
        _build_shifts(sh_ref, pad_ref, offs)
        for s0 in range(0, rc, sub):
            ac = _tap_sum(sh_ref, pad_ref, w_ref, offs, s0, sub) + cb_ref[...]
            ac_ref[s0:s0 + sub, :] = ac
            y, _ = _ln_stats(ac)
            aln = y * g_ref[...] + b_ref[...]
            a_ref[s0:s0 + sub, :] = (aln * _sigmoid(aln)).astype(BF16)

    chunk, halo = _chunk_spec(rc, da, nch), _prev_halo_spec(pad, rc, da, nch)
    return pl.pallas_call(
        kern, grid=(batch, nch),
        in_specs=[chunk, chunk, halo, halo, _whole2(conv_w), _whole2(conv_b), _whole2(ln_g), _whole2(ln_b)],
        out_specs=[chunk, chunk], out_shape=[SDS((t, da), BF16), SDS((t, da), F32)],
        scratch_shapes=[pltpu.VMEM((pad + rc, da), F32), pltpu.VMEM((SUBLANES, pad + rc, da), F32)],
        compiler_params=_params("arbitrary", "arbitrary"), name=name)(a_val, a_gate, a_val, a_gate, conv_w, conv_b, ln_g, ln_b)


def _mix_a_bwd(a_val, a_gate, ac, da_, conv_w, ln_g, ln_b, batch, name):
    t, da = a_val.shape
    taps = conv_w.shape[0]
    pad, rc, nch = _conv_geometry(t // batch, taps)
    sub = _sub_rows(rc)
    offs_in = [pad - (taps - 1) + k for k in range(taps)]
    offs_out = [taps - 1 - k for k in range(taps)]

    def kern(av_ref, ag_ref, pav_ref, pag_ref, ac_ref, da_ref, nac_ref, nda_ref, w_ref, g_ref, b_ref,
             dav_ref, dag_ref, dw_ref, dcb_ref, dg_ref, db_ref, gpad_ref, dpad_ref, shg_ref, shd_ref, dwacc_ref):
        i = pl.program_id(1)
        start = (pl.program_id(0) == 0) & (i == 0)
        end = (pl.program_id(0) == batch - 1) & (i == nch - 1)

        @pl.when(start)
        def _():
            for r in (dcb_ref, dg_ref, db_ref, dwacc_ref):
                r[...] = jnp.zeros_like(r)

        gain, bias = g_ref[...], b_ref[...]

        def d_conv_out(ac_, dout):
            y, rstd = _ln_stats(ac_)
            aln = y * gain + bias
            sig = _sigmoid(aln)
            daln = dout * (sig * (1.0 + aln * (1.0 - sig)))
            return _ln_bwd(daln, y, rstd, gain), daln, y

        gpad_ref[0:pad, :] = jnp.where(i == 0, 0.0, pav_ref[...] * _sigmoid(pag_ref[...]))
        dac_next, _, _ = d_conv_out(nac_ref[...], nda_ref[...])
        dpad_ref[rc:rc + pad, :] = jnp.where(i == nch - 1, 0.0, dac_next)
        for s0 in range(0, rc, sub):
            rows = slice(s0, s0 + sub)
            dac, daln, y = d_conv_out(ac_ref[rows, :], da_ref[rows, :])
            dg_ref[...] += _colsum(daln * y)
            db_ref[...] += _colsum(daln)
            dcb_ref[...] += _colsum(dac)
            dpad_ref[rows, :] = dac
            gpad_ref[pad + s0:pad + s0 + sub, :] = av_ref[rows, :] * _sigmoid(ag_ref[rows, :])
        _build_shifts(shg_ref, gpad_ref, offs_in)
        _build_shifts(shd_ref, dpad_ref, offs_out)
        for s0 in range(0, rc, sub):
            rows = slice(s0, s0 + sub)
            dac = dpad_ref[rows, :]
            for k, o in enumerate(offs_in):
                dwacc_ref[k] += _row_groups(dac * _read_shifted(shg_ref, gpad_ref, o, s0, sub))
            dgl = _tap_sum(shd_ref, dpad_ref, w_ref, offs_out, s0, sub)
            av = av_ref[rows, :]
            sig = _sigmoid(ag_ref[rows, :])
            dav_ref[rows, :] = (dgl * sig).astype(BF16)
            dag_ref[rows, :] = (dgl * av * sig * (1.0 - sig)).astype(BF16)

        @pl.when(end)
        def _():
            for k in range(taps):
                dw_ref[k:k + 1, :] = _colsum(dwacc_ref[k])

    chunk, prev = _chunk_spec(rc, da, nch), _prev_halo_spec(pad, rc, da, nch)
    nxt = _next_halo_spec(pad, rc, da, nch, t)
    vec = SDS((1, da), F32)
    return pl.pallas_call(
        kern, grid=(batch, nch),
        in_specs=[chunk, chunk, prev, prev, chunk, chunk, nxt, nxt, _whole2(conv_w), _whole2(ln_g), _whole2(ln_b)],
        out_specs=[chunk, chunk, _whole2(conv_w), _whole2(vec), _whole2(vec), _whole2(vec)],
        out_shape=[SDS((t, da), BF16), SDS((t, da), BF16), SDS(conv_w.shape, F32), vec, vec, vec],
        scratch_shapes=[pltpu.VMEM((pad + rc, da), F32), pltpu.VMEM((rc + pad, da), F32),
                        pltpu.VMEM((SUBLANES, pad + rc, da), F32), pltpu.VMEM((SUBLANES, rc + pad, da), F32),
                        pltpu.VMEM((taps, SUBLANES, da), F32)],
        compiler_params=_params("arbitrary", "arbitrary"), name=name)(
            a_val, a_gate, a_val, a_gate, ac, da_, ac, da_, conv_w, ln_g, ln_b)


def _mix_c_fwd(c_b, c_c, c_x, conv_w, batch, name):
    t, dc = c_b.shape
    taps = conv_w.shape[0]
    pad, rc, nch = _conv_geometry(t // batch, taps)
    sub = _sub_rows(rc)
    offs = [pad - (taps - 1) + k for k in range(taps)]

    def kern(cb_ref, cc_ref, cx_ref, pcc_ref, pcx_ref, w_ref, o_ref, pad_ref, sh_ref):
        pad_ref[0:pad, :] = jnp.where(pl.program_id(1) == 0, 0.0, pcc_ref[...] * pcx_ref[...])
        for s0 in range(0, rc, sub):
            pad_ref[pad + s0:pad + s0 + sub, :] = cc_ref[s0:s0 + sub, :] * cx_ref[s0:s0 + sub, :]
        _build_shifts(sh_ref, pad_ref, offs)
        for s0 in range(0, rc, sub):
            o_ref[s0:s0 + sub, :] = (cb_ref[s0:s0 + sub, :] * _tap_sum(sh_ref, pad_ref, w_ref, offs, s0, sub)).astype(BF16)

    chunk, prev = _chunk_spec(rc, dc, nch), _prev_halo_spec(pad, rc, dc, nch)
    return pl.pallas_call(
        kern, grid=(batch, nch), in_specs=[chunk, chunk, chunk, prev, prev, _whole2(conv_w)],
        out_specs=chunk, out_shape=SDS((t, dc), BF16),
        scratch_shapes=[pltpu.VMEM((pad + rc, dc), F32), pltpu.VMEM((SUBLANES, pad + rc, dc), F32)],
        compiler_params=_params("arbitrary", "arbitrary"), name=name)(c_b, c_c, c_x, c_c, c_x, conv_w)


def _mix_c_bwd(c_b, c_c, c_x, dco, conv_w, batch, name):
    t, dc = c_b.shape
    taps = conv_w.shape[0]
    pad, rc, nch = _conv_geometry(t // batch, taps)
    sub = _sub_rows(rc)
    offs_in = [pad - (taps - 1) + k for k in range(taps)]
    offs_out = [taps - 1 - k for k in range(taps)]

    def kern(cb_ref, cc_ref, cx_ref, do_ref, pcc_ref, pcx_ref, ncb_ref, ndo_ref, w_ref,
             dcb_ref, dcc_ref, dcx_ref, dw_ref, ppad_ref, dpad_ref, shp_ref, shd_ref, dwacc_ref):
        i = pl.program_id(1)

        @pl.when((pl.program_id(0) == 0) & (i == 0))
        def _():
            dwacc_ref[...] = jnp.zeros_like(dwacc_ref)

        ppad_ref[0:pad, :] = jnp.where(i == 0, 0.0, pcc_ref[...] * pcx_ref[...])
        dpad_ref[rc:rc + pad, :] = jnp.where(i == nch - 1, 0.0, ndo_ref[...] * ncb_ref[...])
        for s0 in range(0, rc, sub):
            rows = slice(s0, s0 + sub)
            ppad_ref[pad + s0:pad + s0 + sub, :] = cc_ref[rows, :] * cx_ref[rows, :]
            dpad_ref[rows, :] = do_ref[rows, :] * cb_ref[rows, :]
        _build_shifts(shp_ref, ppad_ref, offs_in)
        _build_shifts(shd_ref, dpad_ref, offs_out)
        for s0 in range(0, rc, sub):
            rows = slice(s0, s0 + sub)
            dcv = dpad_ref[rows, :]
            cv = None
            for k, o in enumerate(offs_in):
                shifted = _read_shifted(shp_ref, ppad_ref, o, s0, sub)
                dwacc_ref[k] += _row_groups(dcv * shifted)
                cv = shifted * w_ref[k:k + 1, :] if cv is None else cv + shifted * w_ref[k:k + 1, :]
            dp = _tap_sum(shd_ref, dpad_ref, w_ref, offs_out, s0, sub)
            dcb_ref[rows, :] = (do_ref[rows, :] * cv).astype(BF16)
            dcc_ref[rows, :] = (dp * cx_ref[rows, :]).astype(BF16)
            dcx_ref[rows, :] = (dp * cc_ref[rows, :]).astype(BF16)

        @pl.when((pl.program_id(0) == batch - 1) & (i == nch - 1))
        def _():
            for k in range(taps):
                dw_ref[k:k + 1, :] = _colsum(dwacc_ref[k])

    chunk, prev = _chunk_spec(rc, dc, nch), _prev_halo_spec(pad, rc, dc, nch)
    nxt = _next_halo_spec(pad, rc, dc, nch, t)
    return pl.pallas_call(
        kern, grid=(batch, nch), in_specs=[chunk, chunk, chunk, chunk, prev, prev, nxt, nxt, _whole2(conv_w)],
        out_specs=[chunk, chunk, chunk, _whole2(conv_w)],
        out_shape=[SDS((t, dc), BF16)] * 3 + [SDS(conv_w.shape, F32)],
        scratch_shapes=[pltpu.VMEM((pad + rc, dc), F32), pltpu.VMEM((rc + pad, dc), F32),
                        pltpu.VMEM((SUBLANES, pad + rc, dc), F32), pltpu.VMEM((SUBLANES, rc + pad, dc), F32),
                        pltpu.VMEM((taps, SUBLANES, dc), F32)],
        compiler_params=_params("arbitrary", "arbitrary"), name=name)(c_b, c_c, c_x, dco, c_c, c_x, c_b, dco, conv_w)


def _head_of_lane(db):
    return lax.broadcasted_iota(jnp.int32, (1, db), 1) // (db // B_HEADS)


def _tril(rows_ge_cols=True):
    r = lax.broadcasted_iota(jnp.int32, (CHUNK, CHUNK), 0)
    c = lax.broadcasted_iota(jnp.int32, (CHUNK, CHUNK), 1)
    return (r >= c) if rows_ge_cols else (r <= c)


def _spatial_mix(wm, vb, bias, head):
    mixed = bias
    for h in range(B_HEADS):
        mixed = mixed + jnp.where(head == h, _dot(wm[h], vb), 0.0)
    return mixed


def _mix_b_fwd(b_u, b_v, w_s, bias_full, ln_g, ln_b, name):
    t, db = b_u.shape
    rb = _tile(t, ROW_TILE)

    def kern(bu_ref, bv_ref, ws_ref, bias_ref, g_ref, b_ref, o_ref):
        head = _head_of_lane(db)
        wm = [jnp.where(_tril(), ws_ref[h], 0.0).astype(BF16) for h in range(B_HEADS)]
        for ch in range(rb // CHUNK):
            rows = slice(ch * CHUNK, (ch + 1) * CHUNK)
            y, _ = _ln_stats(_gelu(bv_ref[rows, :]))
            vb = (y * g_ref[...] + b_ref[...]).astype(BF16)
            mixed = _spatial_mix(wm, vb, bias_ref[...], head)
            o_ref[rows, :] = (_gelu(bu_ref[rows, :]) * mixed).astype(BF16)

    def whole(a):
        return pl.BlockSpec(a.shape, lambda i, nd=len(a.shape): (0,) * nd)

    tile = pl.BlockSpec((rb, db), lambda i: (i, 0))
    return pl.pallas_call(
        kern, grid=(t // rb,), in_specs=[tile, tile, whole(w_s), whole(bias_full), whole(ln_g), whole(ln_b)],
        out_specs=tile, out_shape=SDS((t, db), BF16), compiler_params=_params("arbitrary"), name=name)(
            b_u, b_v, w_s, bias_full, ln_g, ln_b)


def _mix_b_bwd(b_u, b_v, dbo, w_s, w_s_t, bias_full, ln_g, ln_b, name):
    t, db = b_u.shape
    rb = _tile(t, ROW_TILE)
    steps = t // rb

    def kern(bu_ref, bv_ref, do_ref, ws_ref, wst_ref, bias_ref, g_ref, b_ref,
             dbu_ref, dbv_ref, dws_ref, dbs_ref, dg_ref, dbeta_ref, dbias_ref):
        i = pl.program_id(0)

        @pl.when(i == 0)
        def _():
            for r in (dws_ref, dg_ref, dbeta_ref, dbias_ref):
                r[...] = jnp.zeros_like(r)

        head = _head_of_lane(db)
        gain = g_ref[...]
        wm = [jnp.where(_tril(), ws_ref[h], 0.0).astype(BF16) for h in range(B_HEADS)]
        wmt = [jnp.where(_tril(False), wst_ref[h], 0.0).astype(BF16) for h in range(B_HEADS)]
        for ch in range(rb // CHUNK):
            rows = slice(ch * CHUNK, (ch + 1) * CHUNK)
            bu, bv, dout = bu_ref[rows, :], bv_ref[rows, :], do_ref[rows, :]
            y, rstd = _ln_stats(_gelu(bv))
            vb = (y * gain + b_ref[...]).astype(BF16)
            mixed = _spatial_mix(wm, vb, bias_ref[...], head)
            du = dout * mixed
            dmixed = dout * _gelu(bu)
            dbias_ref[...] += dmixed
            dmb = dmixed.astype(BF16)
            dv = None
            for h in range(B_HEADS):
                dws_ref[h] += _dot_nt(jnp.where(head == h, dmb, 0.0).astype(BF16), vb)
                part = jnp.where(head == h, _dot(wmt[h], dmb), 0.0)
                dv = part if dv is None else dv + part
            dg_ref[...] += _colsum(dv * y)
            dbeta_ref[...] += _colsum(dv)
            dbv_ref[rows, :] = (_ln_bwd(dv, y, rstd, gain) * _gelu_grad(bv)).astype(BF16)
            dbu_ref[rows, :] = (du * _gelu_grad(bu)).astype(BF16)

        @pl.when(i == steps - 1)
        def _():
            for h in range(B_HEADS):
                dws_ref[h] = jnp.where(_tril(), dws_ref[h], 0.0)
                dbs_ref[h] = jnp.sum(jnp.where(head == h, dbias_ref[...], 0.0), axis=1, keepdims=True)

    def whole(a):
        return pl.BlockSpec(a.shape, lambda i, nd=len(a.shape): (0,) * nd)

    tile = pl.BlockSpec((rb, db), lambda i: (i, 0))
    vec = SDS((1, db), F32)
    dbs = SDS((B_HEADS, CHUNK, 1), F32)
    return pl.pallas_call(
        kern, grid=(steps,),
        in_specs=[tile, tile, tile, whole(w_s), whole(w_s_t), whole(bias_full), whole(ln_g), whole(ln_b)],
        out_specs=[tile, tile, whole(w_s), whole(dbs), whole(vec), whole(vec)],
        out_shape=[SDS((t, db), BF16), SDS((t, db), BF16), SDS(w_s.shape, F32), dbs, vec, vec],
        scratch_shapes=[pltpu.VMEM((CHUNK, db), F32)],
        compiler_params=_params("arbitrary"), name=name)(b_u, b_v, dbo, w_s, w_s_t, bias_full, ln_g, ln_b)


def _softmax_rows(s):
    e = jnp.exp(s - jnp.max(s, axis=-1, keepdims=True))
    return e / jnp.sum(e, axis=-1, keepdims=True)


def _attention_fwd(q, kv, batch, name):
    t, d = q.shape
    seq, mlen, hd = t // batch, kv.shape[0] // batch, d // X_HEADS
    ar = _tile(seq, ATT_ROWS)
    scale = hd ** -0.5

    def kern(q_ref, k_ref, v_ref, o_ref):
        k, v = k_ref[...], v_ref[...]
        for r0 in range(0, seq, ar):
            p = _softmax_rows(_dot_nt(q_ref[r0:r0 + ar, :], k) * scale)
            o_ref[r0:r0 + ar, :] = _dot(p.astype(BF16), v).astype(BF16)

    qs = pl.BlockSpec((seq, hd), lambda b, h: (b, h))
    return pl.pallas_call(
        kern, grid=(batch, X_HEADS),
        in_specs=[qs, pl.BlockSpec((mlen, hd), lambda b, h: (b, h)), pl.BlockSpec((mlen, hd), lambda b, h: (b, X_HEADS + h))],
        out_specs=qs, out_shape=SDS((t, d), BF16), compiler_params=_params("arbitrary", "arbitrary"), name=name)(q, kv, kv)


def _attention_bwd(q, kv, do, batch, name):
    t, d = q.shape
    seq, mlen, hd = t // batch, kv.shape[0] // batch, d // X_HEADS
    ar = _tile(seq, ATT_ROWS)
    scale = hd ** -0.5

    def kern(q_ref, k_ref, v_ref, do_ref, dq_ref, dk_ref, dv_ref):
        k, v = k_ref[...], v_ref[...]
        dk = jnp.zeros((mlen, hd), F32)
        dv = jnp.zeros((mlen, hd), F32)
        for r0 in range(0, seq, ar):
            qr, dor = q_ref[r0:r0 + ar, :], do_ref[r0:r0 + ar, :]
            p = _softmax_rows(_dot_nt(qr, k) * scale)
            dp = _dot_nt(dor, v)
            ds = (p * (dp - jnp.sum(p * dp, axis=-1, keepdims=True)) * scale).astype(BF16)
            dq_ref[r0:r0 + ar, :] = _dot(ds, k).astype(BF16)
            dk = dk + _dot_tn(ds, qr)
            dv = dv + _dot_tn(p.astype(BF16), dor)
        dk_ref[...] = dk.astype(BF16)
        dv_ref[...] = dv.astype(BF16)

    qs = pl.BlockSpec((seq, hd), lambda b, h: (b, h))
    ks = pl.BlockSpec((mlen, hd), lambda b, h: (b, h))
    dkv = SDS((kv.shape[0], d), BF16)
    return pl.pallas_call(
        kern, grid=(batch, X_HEADS),
        in_specs=[qs, ks, pl.BlockSpec((mlen, hd), lambda b, h: (b, X_HEADS + h)), qs],
        out_specs=[qs, ks, ks], out_shape=[SDS((t, d), BF16), dkv, dkv],
        compiler_params=_params("arbitrary", "arbitrary"), name=name)(q, kv, kv, do)


def _place():
    x, y, c = lax.axis_index("x"), lax.axis_index("y"), lax.axis_index("c")
    other_chips = [(1 - x, y), (x, 1 - y), (1 - x, 1 - y)]
    return x, y, c, other_chips


def _comm_call(kern, out_shape, n_pairs, name, *args):
    return pl.pallas_call(
        kern, out_shape=out_shape, in_specs=[ANY] * len(args), out_specs=jax.tree.map(lambda _: ANY, out_shape),
        scratch_shapes=[pltpu.SemaphoreType.DMA((n_pairs,)), pltpu.SemaphoreType.DMA((n_pairs,)), pltpu.SemaphoreType.DMA((n_pairs,))],
        name=name)(*args)


def _all_gather_chips(shard, name):
    r, cols = shard.shape
    rh = r // 2

    def kern(s_ref, o_ref, send_sems, recv_sems, local_sems):
        x, y, c, chips = _place()
        mine_slot = 2 * x + y
        half = pl.ds(c * rh, rh)
        other_half = pl.ds((1 - c) * rh, rh)

        def copy(k, src, dst, to):
            return pltpu.make_async_remote_copy(src_ref=src, dst_ref=dst, send_sem=send_sems.at[k], recv_sem=recv_sems.at[k],
                                                device_id=to, device_id_type=MESH)

        mine = pltpu.make_async_copy(s_ref, o_ref.at[mine_slot], local_sems.at[0])
        mine.start()
        first = [copy(j, s_ref.at[half], o_ref.at[mine_slot, half], (px, py, c)) for j, (px, py) in enumerate(chips)]
        for cp in first:
            cp.start()
        passed = []
        for j, (px, py) in enumerate(chips):
            landed = o_ref.at[2 * px + py, half]
            copy(j, landed, landed, (px, py, c)).wait_recv()
            cp = copy(3 + j, landed, landed, (x, y, 1 - c))
            cp.start()
            passed.append(cp)
        for j, (px, py) in enumerate(chips):
            theirs = o_ref.at[2 * px + py, other_half]
            copy(3 + j, theirs, theirs, (x, y, 1 - c)).wait_recv()
        for cp in first + passed:
            cp.wait_send()
        mine.wait()

    return _comm_call(kern, SDS((N_CHIPS, r, cols), shard.dtype), 6, name, shard)


def _join_halves(joined, name):
    n = len(joined)

    def kern(*refs):
        j_refs = refs[:n]
        send_sems, recv_sems = refs[2 * n:]
        x, y, c, _ = _place()
        copies = []
        for k, j_ref in enumerate(j_refs):
            rh = j_ref.shape[0] // 2
            rows = j_ref.at[pl.ds(c * rh, rh)]
            cp = pltpu.make_async_remote_copy(src_ref=rows, dst_ref=rows, send_sem=send_sems.at[k], recv_sem=recv_sems.at[k],
                                              device_id=(x, y, 1 - c), device_id_type=MESH)
            cp.start()
            copies.append(cp)
        for k, (cp, j_ref) in enumerate(zip(copies, j_refs)):
            rh = j_ref.shape[0] // 2
            theirs = j_ref.at[pl.ds((1 - c) * rh, rh)]
            pltpu.make_async_remote_copy(src_ref=theirs, dst_ref=theirs, send_sem=send_sems.at[k], recv_sem=recv_sems.at[k],
                                         device_id=(x, y, 1 - c), device_id_type=MESH).wait_recv()
            cp.wait_send()

    return pl.pallas_call(
        kern, out_shape=[SDS(j.shape, j.dtype) for j in joined], in_specs=[ANY] * n, out_specs=[ANY] * n,
        input_output_aliases={k: k for k in range(n)},
        scratch_shapes=[pltpu.SemaphoreType.DMA((n,)), pltpu.SemaphoreType.DMA((n,))], name=name)(*joined)


HBM = pl.BlockSpec(memory_space=pltpu.HBM)
SEMAPHORES = pl.BlockSpec(memory_space=pltpu.SEMAPHORE)
COPIES_PER_ARRAY = N_CHIPS - 1
GATHER_COPIES = N_CHIPS


def _in_hbm(a):
    return pltpu.with_memory_space_constraint(a, pltpu.HBM)


def _gather_copies(shard_refs, land_refs, send_sems, recv_sems):
    x, y, c, chips = _place()
    copies = []
    for k, (s_ref, land_ref) in enumerate(zip(shard_refs, land_refs)):
        half = pl.ds(c * (s_ref.shape[0] // 2), s_ref.shape[0] // 2)
        for j, (px, py) in enumerate(chips):
            n = GATHER_COPIES * k + j
            copies.append(pltpu.make_async_remote_copy(
                src_ref=s_ref.at[half], dst_ref=land_ref.at[2 * x + y, half], send_sem=send_sems.at[n], recv_sem=recv_sems.at[n],
                device_id=(px, py, c), device_id_type=MESH))
        n = GATHER_COPIES * k + N_CHIPS - 1
        copies.append(pltpu.make_async_remote_copy(
            src_ref=s_ref, dst_ref=land_ref.at[2 * x + y], send_sem=send_sems.at[n], recv_sem=recv_sems.at[n],
            device_id=(x, y, 1 - c), device_id_type=MESH))
    return copies


def _swap_copies(full_refs, land_refs, send_sems, recv_sems):
    x, y, c, _ = _place()
    copies = []
    for k, (g_ref, land_ref) in enumerate(zip(full_refs, land_refs)):
        rh = land_ref.shape[1]
        copies.append(pltpu.make_async_remote_copy(
            src_ref=g_ref.at[:, pl.ds((1 - c) * rh, rh), :], dst_ref=land_ref, send_sem=send_sems.at[k], recv_sem=recv_sems.at[k],
            device_id=(x, y, 1 - c), device_id_type=MESH))
    return copies


def _owner_copies(sum_refs, land_refs, send_sems, recv_sems):
    x, y, c, chips = _place()
    copies = []
    for k, (s_ref, land_ref) in enumerate(zip(sum_refs, land_refs)):
        for j, (px, py) in enumerate(chips):
            n = COPIES_PER_ARRAY * k + j
            copies.append(pltpu.make_async_remote_copy(
                src_ref=s_ref.at[2 * px + py], dst_ref=land_ref.at[j], send_sem=send_sems.at[n], recv_sem=recv_sems.at[n],
                device_id=(px, py, c), device_id_type=MESH))
    return copies


def _copies_start(build, per_array, sources, lands, name):
    ns, nb = len(sources), len(sources) + len(lands)
    n_copies = per_array * ns

    def kern(*refs):
        for cp in build(refs[:ns], refs[ns:nb], refs[2 * nb], refs[2 * nb + 1]):
            cp.start()
        refs[-1][...] = jnp.zeros_like(refs[-1])

    bufs = [*sources, *lands]
    res = pl.pallas_call(
        kern, name=name,
        out_shape=(*[pltpu.HBM(b.shape, b.dtype) for b in bufs], pltpu.SemaphoreType.DMA((n_copies,)), pltpu.SemaphoreType.DMA((n_copies,)),
                   SDS((SUBLANES, 128), F32)),
        in_specs=[HBM] * nb, out_specs=(*[HBM] * nb, SEMAPHORES, SEMAPHORES, pl.BlockSpec(memory_space=pltpu.VMEM)),
        input_output_aliases={i: i for i in range(nb)},
        compiler_params=pltpu.CompilerParams(has_side_effects=pltpu.SideEffectType.DATAFLOW_SIDE_EFFECTING),
    )(*[_in_hbm(b) for b in bufs])
    return res[nb], res[nb + 1], list(res[:nb]), res[-1]


def _copies_wait(build, send_sems, recv_sems, bufs, ns, after, name):
    nb = len(bufs)

    def kern(*refs):
        for cp in build(refs[:ns], refs[ns:nb], refs[nb], refs[nb + 1]):
            cp.wait_send()
            cp.wait_recv()

    res = pl.pallas_call(
        kern, name=name, out_shape=tuple(pltpu.HBM(b.shape, b.dtype) for b in bufs),
        in_specs=[HBM] * nb + [SEMAPHORES, SEMAPHORES, ANY], out_specs=tuple([HBM] * nb),
        input_output_aliases={i: i for i in range(nb)},
        compiler_params=pltpu.CompilerParams(has_side_effects=pltpu.SideEffectType.DATAFLOW_SIDE_EFFECTING),
    )(*bufs, send_sems, recv_sems, after)
    return list(res[:ns]), list(res[ns:])


def _gather_finish(lands, name):
    ns = len(lands)

    def kern(*refs):
        l_refs = refs[:ns]
        send_sems, recv_sems = refs[2 * ns:]
        x, y, c, chips = _place()
        passed = []
        for k, l_ref in enumerate(l_refs):
            rh = l_ref.shape[1] // 2
            for j, (px, py) in enumerate(chips):
                landed = l_ref.at[2 * px + py, pl.ds(c * rh, rh)]
                cp = pltpu.make_async_remote_copy(
                    src_ref=landed, dst_ref=landed, send_sem=send_sems.at[COPIES_PER_ARRAY * k + j],
                    recv_sem=recv_sems.at[COPIES_PER_ARRAY * k + j], device_id=(x, y, 1 - c), device_id_type=MESH)
                cp.start()
                passed.append(cp)
        for k, l_ref in enumerate(l_refs):
            rh = l_ref.shape[1] // 2
            for j, (px, py) in enumerate(chips):
                theirs = l_ref.at[2 * px + py, pl.ds((1 - c) * rh, rh)]
                pltpu.make_async_remote_copy(
                    src_ref=theirs, dst_ref=theirs, send_sem=send_sems.at[COPIES_PER_ARRAY * k + j],
                    recv_sem=recv_sems.at[COPIES_PER_ARRAY * k + j], device_id=(x, y, 1 - c), device_id_type=MESH).wait_recv()
        for cp in passed:
            cp.wait_send()

    return pl.pallas_call(
        kern, out_shape=[SDS(l.shape, l.dtype) for l in lands], in_specs=[ANY] * ns, out_specs=[ANY] * ns,
        input_output_aliases={k: k for k in range(ns)},
        scratch_shapes=[pltpu.SemaphoreType.DMA((COPIES_PER_ARRAY * ns,)), pltpu.SemaphoreType.DMA((COPIES_PER_ARRAY * ns,))],
        name=name)(*lands)


def _add_sibling(full, theirs, out_dtype, name):
    n, r, cols = full.shape
    rh = r // 2
    tr = _tile(rh, ROW_TILE)
    nb = rh // tr

    def kern(c_ref, a_ref, b_ref, o_ref):
        o_ref[...] = (a_ref[...] + b_ref[...]).astype(out_dtype)

    c = lax.axis_index("c").astype(jnp.int32).reshape(1)
    return pl.pallas_call(
        kern, out_shape=SDS((n, rh, cols), out_dtype),
        grid_spec=pltpu.PrefetchScalarGridSpec(
            num_scalar_prefetch=1, grid=(n, nb),
            in_specs=[pl.BlockSpec((1, tr, cols), lambda j, i, c_ref: (j, c_ref[0] * nb + i, 0)),
                      pl.BlockSpec((1, tr, cols), lambda j, i, c_ref: (j, i, 0))],
            out_specs=pl.BlockSpec((1, tr, cols), lambda j, i, c_ref: (j, i, 0))),
        compiler_params=_params("arbitrary", "arbitrary"), name=name)(c, full, theirs)


def _add_owners(partial, received, name):
    n, rh, cols = partial.shape
    tr = _tile(rh, ROW_TILE)
    nb = rh // tr

    def kern(s_ref, a_ref, b_ref, o_ref):
        acc = a_ref[0].astype(F32)
        for j in range(N_CHIPS - 1):
            acc = acc + b_ref[j].astype(F32)
        o_ref[...] = acc

    place = jnp.stack([2 * lax.axis_index("x") + lax.axis_index("y"), lax.axis_index("c")]).astype(jnp.int32)
    return pl.pallas_call(
        kern, out_shape=SDS((2 * rh, cols), F32),
        grid_spec=pltpu.PrefetchScalarGridSpec(
            num_scalar_prefetch=1, grid=(nb,),
            in_specs=[pl.BlockSpec((1, tr, cols), lambda i, s_ref: (s_ref[0], i, 0)),
                      pl.BlockSpec((N_CHIPS - 1, tr, cols), lambda i, s_ref: (0, i, 0))],
            out_specs=pl.BlockSpec((tr, cols), lambda i, s_ref: (s_ref[1] * nb + i, 0))),
        compiler_params=_params("arbitrary"), name=name)(place, partial, received)


def _pack(parts, lead):
    lead_shape = parts[0].shape[:lead]
    flat = jnp.concatenate([p.reshape(lead_shape + (-1,)) for p in parts], axis=-1)
    n = flat.shape[-1]
    rows = _round_up(-(-n // PACK_W), PACK_ROWS)
    flat = jnp.pad(flat, [(0, 0)] * lead + [(0, rows * PACK_W - n)])
    return flat.reshape(lead_shape + (rows, PACK_W))


def _unpack(buf, shapes, lead):
    lead_shape = buf.shape[:lead]
    flat = buf.reshape(lead_shape + (-1,))
    out, off = [], 0
    for s in shapes:
        n = math.prod(s)
        out.append(flat[..., off:off + n].reshape(lead_shape + tuple(s)))
        off += n
    return out


def _to_slots(full, axis):
    s = full.shape
    split = full.reshape(s[:axis] + (N_CHIPS, s[axis] // N_CHIPS) + s[axis + 1:])
    return jnp.moveaxis(split, axis, 0)


def _from_slots(slots, axis):
    moved = jnp.moveaxis(slots, 0, axis)
    s = moved.shape
    return moved.reshape(s[:axis] + (s[axis] * s[axis + 1],) + s[axis + 2:])


MATMUL_WEIGHTS = (("w_in", 1), ("w_out", 0), ("w_q", 0), ("w_kv", 1), ("w_o", 0), ("w_ff1", 1), ("w_ff2", 0))
CONV_WEIGHTS = (("conv_a_w", 1), ("conv_c_w", 1))
REPLICATED = ("conv_a_b", "ln_a_g", "ln_a_b", "ln_v_g", "ln_v_b", "w_s", "b_s",
              "ln1_g", "ln1_b", "ln2_g", "ln2_b", "ln3_g", "ln3_b")
WEIGHT_ORDER = ("w_in", "conv_a_w", "conv_a_b", "ln_a_g", "ln_a_b", "ln_v_g", "ln_v_b", "w_s", "b_s", "conv_c_w", "w_out",
                "ln1_g", "ln1_b", "w_q", "w_kv", "w_o", "ln2_g", "ln2_b", "w_ff1", "w_ff2", "ln3_g", "ln3_b")


def _row(v):
    return v.reshape(1, -1)


def kernel(x, mem, w_in, conv_a_w, conv_a_b, ln_a_g, ln_a_b, ln_v_g, ln_v_b, w_s, b_s, conv_c_w, w_out, ln1_g, ln1_b, w_q, w_kv, w_o, ln2_g, ln2_b, w_ff1, w_ff2, ln3_g, ln3_b, loss_target, m_w_in, m_conv_a_w, m_conv_a_b, m_ln_a_g, m_ln_a_b, m_ln_v_g, m_ln_v_b, m_w_s, m_b_s, m_conv_c_w, m_w_out, m_ln1_g, m_ln1_b, m_w_q, m_w_kv, m_w_o, m_ln2_g, m_ln2_b, m_w_ff1, m_w_ff2, m_ln3_g, m_ln3_b, v_w_in, v_conv_a_w, v_conv_a_b, v_ln_a_g, v_ln_a_b, v_ln_v_g, v_ln_v_b, v_w_s, v_b_s, v_conv_c_w, v_w_out, v_ln1_g, v_ln1_b, v_w_q, v_w_kv, v_w_o, v_ln2_g, v_ln2_b, v_w_ff1, v_w_ff2, v_ln3_g, v_ln3_b):
    given = dict(locals())
    weights = {n: given[n] for n in WEIGHT_ORDER}
    moment1 = {n: given["m_" + n] for n in WEIGHT_ORDER}
    moment2 = {n: given["v_" + n] for n in WEIGHT_ORDER}

    depth = w_in.shape[0]
    batch, seq, d = x.shape
    t = batch * seq
    hd = d // HEADS
    d_a, d_b, d_c = A_HEADS * hd, B_HEADS * hd, C_HEADS * hd
    widths = (d_a, d_a, d_b, d_b, d_c, d_c, d_c)
    in_offs = [sum(widths[:k]) for k in range(len(widths))]
    alpha = (2.0 * depth) ** 0.25
    layers = range(depth)
    tm = _tile(t, ROW_TILE)
    tm_wide = _tile(t, WIDE_ROW_TILE)

    conv_shards = [weights[n][l] for l in layers for n, _ in CONV_WEIGHTS]
    gathers, tokens = [], []
    for l in layers:
        shards = [weights[n][l].astype(BF16) for n, _ in MATMUL_WEIGHTS] + ([_pack(conv_shards, 0)] if l == 0 else [])
        lands = [lax.empty((N_CHIPS,) + s.shape, s.dtype) for s in shards]
        send_sems, recv_sems, bufs, token = _copies_start(_gather_copies, GATHER_COPIES, shards, lands, f"gather_start_{l}")
        gathers.append((send_sems, recv_sems, bufs, len(shards)))
        tokens.append(token[0, 0])

    def gathered(l, after):
        send_sems, recv_sems, bufs, ns = gathers[l]
        _, lands = _copies_wait(_gather_copies, send_sems, recv_sems, bufs, ns, after, f"gather_wait_{l}")
        return _gather_finish(lands, "gather_finish")

    def layer_weights(l, slots, conv_full):
        w = {}
        for k, (n, axis) in enumerate(MATMUL_WEIGHTS):
            w[n] = _from_slots(slots[k], axis)
        for k, (n, axis) in enumerate(CONV_WEIGHTS):
            w[n] = _from_slots(conv_full[l * len(CONV_WEIGHTS) + k], axis)
        for n in REPLICATED:
            w[n] = weights[n][l]
        w["w_s_t"] = jnp.swapaxes(w["w_s"], 1, 2)
        w["bias_full"] = jnp.repeat(w["b_s"].T, hd, axis=1)
        return w

    xf = x.reshape(t, d)
    xb = (xf + sum(tokens)).astype(BF16)
    memb = mem.reshape(-1, d).astype(BF16)
    saved, full, conv_full = [], [], None
    for l in layers:
        slots = gathered(l, xb)
        if l == 0:
            conv_full = _unpack(slots[-1], [s.shape for s in conv_shards], 1)
        w = layer_weights(l, slots, conv_full)
        full.append(w)
        s = {"x0b": xb}
        proj = _in_proj(xb, w["w_in"], widths, "in_proj")
        s["proj"] = proj
        a_val, a_gate, b_u, b_v, c_b, c_c, c_x = proj
        a_out, s["ac"] = _mix_a_fwd(a_val, a_gate, w["conv_a_w"], _row(w["conv_a_b"]), _row(w["ln_a_g"]), _row(w["ln_a_b"]),
                                    batch, "mix_a_fwd")
        b_out = _mix_b_fwd(b_u, b_v, w["w_s"], w["bias_full"], _row(w["ln_v_g"]), _row(w["ln_v_b"]), "mix_b_fwd")
        c_out = _mix_c_fwd(c_b, c_c, c_x, w["conv_c_w"], batch, "mix_c_fwd")
        s["cat"] = (a_out, b_out, c_out)
        s["z1"], xf, xb = _proj_ln([a_out, b_out, c_out], w["w_out"], xf, _row(w["ln1_g"]), _row(w["ln1_b"]), alpha, tm, "out_proj_ln")
        s["x1b"] = xb
        s["q"] = _matmul(xb, w["w_q"], "q_proj")
        s["kv"] = _matmul(memb, w["w_kv"], "kv_proj")
        s["o"] = _attention_fwd(s["q"], s["kv"], batch, "attention_fwd")
        s["z2"], xf, xb = _proj_ln([s["o"]], w["w_o"], xf, _row(w["ln2_g"]), _row(w["ln2_b"]), alpha, tm, "o_proj_ln")
        s["x2b"] = xb
        s["h"], s["r"] = _ff1(xb, w["w_ff1"], "ff1")
        s["z3"], xf, xb = _proj_ln([s["r"]], w["w_ff2"], xf, _row(w["ln3_g"]), _row(w["ln3_b"]), alpha, tm_wide, "ff2_ln")
        saved.append(s)

    target = loss_target.reshape(t, d)
    dz3, dz3b, top_g, top_b, loss_block = _loss_head_ln(xf, target, saved[-1]["z3"], _row(full[-1]["ln3_g"]), "loss_head_ln3")
    loss = lax.psum(loss_block[0, 0], ("x", "y", "c"))
    grads = [{} for _ in layers]
    grads[-1]["ln3_g"], grads[-1]["ln3_b"] = top_g, top_b
    reductions = [None] * depth
    swapping = None
    n_mm = len(MATMUL_WEIGHTS)
    token, last_started = 0.0, None

    def rows_to_slots(full_):
        return full_.reshape(N_CHIPS, full_.shape[1] // N_CHIPS, full_.shape[2])

    def swap_finish(after):
        nonlocal token, last_started
        k, send_sems, recv_sems, bufs, wire = swapping
        fulls, theirs = _copies_wait(_swap_copies, send_sems, recv_sems, bufs, len(wire), after, f"rs_swap_wait_{k}")
        sums = [_add_sibling(f, th, dt, "rs_add_sibling") for f, th, dt in zip(fulls, theirs, wire)]
        lands = [lax.empty((N_CHIPS - 1,) + p.shape[1:], p.dtype) for p in sums]
        send_sems, recv_sems, bufs, started = _copies_start(_owner_copies, COPIES_PER_ARRAY, sums, lands, f"rs_send_start_{k}")
        reductions[k] = (send_sems, recv_sems, bufs, len(sums))
        token, last_started = started[0, 0], started

    for l in reversed(layers):
        w, s, g = full[l], saved[l], grads[l]
        g["w_ff2"] = rows_to_slots(_mm_tn(s["r"], dz3b, "d_w_ff2"))
        dh = _bwd_ff2(dz3b, s["h"], w["w_ff2"], "d_ff_hidden")
        if swapping is not None:
            swap_finish(dh)
        g["w_ff1"] = _mm_tn(s["x2b"], dh, "d_w_ff1", slots=N_CHIPS)
        dz2, dz2b, g["ln2_g"], g["ln2_b"] = _bwd_data_ln([dh], w["w_ff1"], [0], dz3, alpha, s["z2"], _row(w["ln2_g"]) + token,
                                                         tm_wide, "d_x2_ln2")
        g["w_o"] = rows_to_slots(_mm_tn(s["o"], dz2b, "d_w_o"))
        do = _bwd_data([dz2b], w["w_o"], [0], None, alpha, BF16, tm, "d_att_out")
        dq, dk, dv = _attention_bwd(s["q"], s["kv"], do, batch, "attention_bwd")
        g["w_kv"] = _mm_tn(memb, jnp.concatenate([dk, dv], axis=1), "d_w_kv", slots=N_CHIPS)
        g["w_q"] = rows_to_slots(_mm_tn(s["x1b"], dq, "d_w_q"))
        dz1, dz1b, g["ln1_g"], g["ln1_b"] = _bwd_data_ln([dq], w["w_q"], [0], dz2, alpha, s["z1"], _row(w["ln1_g"]), tm, "d_x1_ln1")
        g["w_out"] = _mm_tn_rows(s["cat"], dz1b, "d_w_out").reshape(N_CHIPS, d // N_CHIPS, d)
        da_, dbo, dco = _bwd_out_proj(dz1b, w["w_out"], (d_a, d_b, d_c), "d_mixer_out")
        a_val, a_gate, b_u, b_v, c_b, c_c, c_x = s["proj"]
        dav, dag, g["conv_a_w"], g["conv_a_b"], g["ln_a_g"], g["ln_a_b"] = _mix_a_bwd(
            a_val, a_gate, s["ac"], da_, w["conv_a_w"], _row(w["ln_a_g"]), _row(w["ln_a_b"]), batch, "mix_a_bwd")
        dbu, dbv, g["w_s"], dbs, g["ln_v_g"], g["ln_v_b"] = _mix_b_bwd(
            b_u, b_v, dbo, w["w_s"], w["w_s_t"], w["bias_full"], _row(w["ln_v_g"]), _row(w["ln_v_b"]), "mix_b_bwd")
        g["b_s"] = dbs.reshape(B_HEADS, CHUNK)
        dcb, dcc, dcx, g["conv_c_w"] = _mix_c_bwd(c_b, c_c, c_x, dco, w["conv_c_w"], batch, "mix_c_bwd")
        dproj = [dav, dag, dbu, dbv, dcb, dcc, dcx]
        g["w_in"] = _to_slots(_mm_tn_cols(s["x0b"], dproj, "d_w_in"), 1)
        if l > 0:
            below = full[l - 1]
            dz3, dz3b, grads[l - 1]["ln3_g"], grads[l - 1]["ln3_b"] = _bwd_data_ln(
                dproj, w["w_in"], in_offs, dz1, alpha, saved[l - 1]["z3"], _row(below["ln3_g"]), tm, "d_x0_ln3")
        else:
            dx = _bwd_data(dproj, w["w_in"], in_offs, dz1, alpha, F32, tm, "d_x0")

        slot_grads = [g[n] for n, _ in MATMUL_WEIGHTS]
        wire = [BF16] * n_mm
        if l == 0:
            conv_slots = [_to_slots(grads[k][n], axis) for k in layers for n, axis in CONV_WEIGHTS]
            rep_parts = [jnp.stack([grads[k][n].reshape(weights[n].shape[1:]) for k in layers]) for n in REPLICATED]
            rep_flat = jnp.concatenate([p.reshape(-1) for p in rep_parts])
            n_rep = rep_flat.shape[0]
            per_chip = _round_up(-(-n_rep // N_CHIPS), PACK_W * PACK_ROWS)
            rep_slots = jnp.pad(rep_flat, (0, N_CHIPS * per_chip - n_rep)).reshape(N_CHIPS, per_chip)
            slot_grads.append(_pack(conv_slots + [rep_slots], 1))
            wire.append(F32)
        lands = [lax.empty((f.shape[0], f.shape[1] // 2, f.shape[2]), f.dtype) for f in slot_grads]
        send_sems, recv_sems, bufs, started = _copies_start(_swap_copies, 1, slot_grads, lands, f"rs_swap_start_{l}")
        swapping = (l, send_sems, recv_sems, bufs, wire)
        token = started[0, 0]
    swap_finish(dx)
    grad_x = dx.reshape(batch, seq, d)

    reduced, after = [None] * depth, last_started
    for l in reversed(layers):
        send_sems, recv_sems, bufs, ns = reductions[l]
        sums, lands = _copies_wait(_owner_copies, send_sems, recv_sems, bufs, ns, after, f"rs_send_wait_{l}")
        mine = [_add_owners(p, r, "rs_add_owners") for p, r in zip(sums, lands)]
        reduced[l] = _join_halves(mine, "rs_join")
        after = reduced[l][0]
    small = _unpack(reduced[0][n_mm], [c.shape[1:] for c in conv_slots] + [(per_chip,)], 0)
    conv_grad, rep_mine = small[:-1], small[-1]
    rep_all = _all_gather_chips(rep_mine.reshape(-1, PACK_W), "gather_small_grads").reshape(-1)[:n_rep]

    grad = {}
    for k, (n, _) in enumerate(MATMUL_WEIGHTS):
        grad[n] = jnp.stack([reduced[l][k] for l in layers])
    for k, (n, _) in enumerate(CONV_WEIGHTS):
        grad[n] = jnp.stack([conv_grad[l * len(CONV_WEIGHTS) + k] for l in layers])
    off = 0
    for n in REPLICATED:
        size = math.prod(weights[n].shape)
        grad[n] = rep_all[off:off + size].reshape(weights[n].shape)
        off += size

    delta, new_m, new_v = {}, {}, {}
    for n, _ in MATMUL_WEIGHTS:
        shape = weights[n].shape
        as_rows = lambda a: a.reshape(-1, shape[-1])
        delta[n], new_m[n], new_v[n] = (
            r.reshape(shape) for r in _adamw(as_rows(weights[n]), as_rows(grad[n]), as_rows(moment1[n]), as_rows(moment2[n]), "adamw"))
    small_names = [n for n, _ in CONV_WEIGHTS] + list(REPLICATED)
    small_shapes = [weights[n].shape for n in small_names]
    packed = [_pack([src[n] for n in small_names], 0) for src in (weights, grad, moment1, moment2)]
    for dst, res in zip((delta, new_m, new_v), _adamw(*packed, "adamw_small")):
        for n, a in zip(small_names, _unpack(res, small_shapes, 0)):
            dst[n] = a

    return (loss, grad_x, *[grad[n] for n in WEIGHT_ORDER], *[delta[n] for n in WEIGHT_ORDER],
            *[new_m[n] for n in WEIGHT_ORDER], *[new_v[n] for n in WEIGHT_ORDER])
```

```python
import functools
import math

import jax
import jax.numpy as jnp
from jax import lax
from jax.experimental import pallas as pl
from jax.experimental.pallas import tpu as pltpu

F32 = jnp.float32
BF16 = jnp.bfloat16
SDS = jax.ShapeDtypeStruct

HEADS = 16
A_HEADS, B_HEADS, C_HEADS = 6, 4, 6
X_HEADS = 4
CHUNK = 128
LN_EPS = 1e-5
ADAM_LR, ADAM_B1, ADAM_B2, ADAM_EPS, ADAM_WD, ADAM_STEP = 0.001, 0.9, 0.999, 1e-08, 0.01, 10

N_CHIPS = 4
V7X_VMEM_LIMIT = 56 << 20
SUBLANES = 8
PACK_W = 1024
PACK_ROWS = 32
ROW_TILE = 512
MATMUL_ROW_TILE = 1024
WIDE_ROW_TILE = 512
CONV_ROWS = 256
SUB_ROWS = 64
ATT_ROWS = 512
TN_TILE = 1024
MESH = pl.DeviceIdType.MESH
ANY = pl.BlockSpec(memory_space=pl.ANY)


def _tile(n, t):
    for d in range(min(n, t), 0, -1):
        if n % d == 0 and d % (2 * SUBLANES) == 0:
            return d
    return n


def _round_up(n, m):
    return -(-n // m) * m


def _params(*sem):
    return pltpu.CompilerParams(dimension_semantics=sem or None, vmem_limit_bytes=V7X_VMEM_LIMIT)


def _dot(a, b):
    return jnp.dot(a, b, preferred_element_type=F32)


def _dot_nt(a, b):
    return lax.dot_general(a, b, (((1,), (1,)), ((), ())), preferred_element_type=F32)


def _dot_tn(a, b):
    return lax.dot_general(a, b, (((0,), (0,)), ((), ())), preferred_element_type=F32)


def _sigmoid(x):
    return 1.0 / (1.0 + jnp.exp(-x))


def _gelu(x):
    return 0.5 * x * (1.0 + lax.erf(x * (2.0 ** -0.5)))


def _gelu_grad(x):
    return 0.5 * (1.0 + lax.erf(x * (2.0 ** -0.5))) + x * jnp.exp(-0.5 * x * x) * ((2.0 * math.pi) ** -0.5)


def _ln_stats(z):
    mu = jnp.mean(z, axis=-1, keepdims=True)
    zc = z - mu
    rstd = lax.rsqrt(jnp.mean(zc * zc, axis=-1, keepdims=True) + LN_EPS)
    return zc * rstd, rstd


def _ln_bwd(dy, y, rstd, g):
    dyh = dy * g
    return rstd * (dyh - jnp.mean(dyh, axis=-1, keepdims=True) - y * jnp.mean(dyh * y, axis=-1, keepdims=True))


def _colsum(x):
    return jnp.sum(x, axis=0, keepdims=True)


def _rowwise(body, rows, consts, outs, accs=(), *, tm, name, after=None):
    t = rows[0].shape[0]
    steps = t // tm
    n_in = len(rows) + len(consts)
    order = [] if after is None else [after]

    def kern(*refs):
        body(pl.program_id(0), steps, *refs[:n_in], *refs[n_in + len(order):])

    def whole(a, **kw):
        return pl.BlockSpec(a.shape, lambda i, nd=len(a.shape): (0,) * nd, **kw)

    in_specs = ([pl.BlockSpec((tm, r.shape[1]), lambda i: (i, 0)) for r in rows]
                + [whole(c, pipeline_mode=pl.Buffered(1)) for c in consts] + [ANY] * len(order))
    out_shape = [SDS((t, n), dt) for n, dt in outs] + [SDS(s, dt) for s, dt in accs]
    out_specs = [pl.BlockSpec((tm, n), lambda i: (i, 0)) for n, _ in outs] + [whole(SDS(s, dt)) for s, dt in accs]
    return pl.pallas_call(kern, grid=(steps,), in_specs=in_specs, out_specs=out_specs, out_shape=out_shape,
                          compiler_params=_params("arbitrary"), name=name)(*rows, *consts, *order)


def _in_proj(xb, w_in, widths, name):
    offs = [sum(widths[:k]) for k in range(len(widths))]

    def body(i, steps, x_ref, w_ref, *o_refs):
        x = x_ref[...]
        for o_ref, off, n in zip(o_refs, offs, widths):
            o_ref[...] = _dot(x, w_ref[:, off:off + n])

    return _rowwise(body, [xb], [w_in], [(n, F32) for n in widths], tm=_tile(xb.shape[0], MATMUL_ROW_TILE), name=name)


def _matmul(ab, w, name):
    def body(i, steps, a_ref, w_ref, o_ref):
        o_ref[...] = _dot(a_ref[...], w_ref[...]).astype(BF16)

    return _rowwise(body, [ab], [w], [(w.shape[1], BF16)], tm=_tile(ab.shape[0], MATMUL_ROW_TILE), name=name)[0]


def _ff1(xb, w, name):
    def body(i, steps, a_ref, w_ref, h_ref, r_ref):
        h = _dot(a_ref[...], w_ref[...])
        h_ref[...] = h.astype(BF16)
        r = jnp.maximum(h, 0.0)
        r_ref[...] = (r * r).astype(BF16)

    n = w.shape[1]
    return _rowwise(body, [xb], [w], [(n, BF16), (n, BF16)], tm=_tile(xb.shape[0], WIDE_ROW_TILE), name=name)


def _proj_ln(a_list, w, x, g, b, alpha, tm, name):
    widths = [a.shape[1] for a in a_list]
    offs = [sum(widths[:k]) for k in range(len(widths))]
    na = len(a_list)

    def body(i, steps, *refs):
        a_refs, (x_ref, w_ref, g_ref, b_ref, z_ref, xn_ref, xb_ref) = refs[:na], refs[na:]
        acc = alpha * x_ref[...]
        for a_ref, off, n in zip(a_refs, offs, widths):
            acc = acc + _dot(a_ref[...], w_ref[off:off + n, :])
        z_ref[...] = acc
        y, _ = _ln_stats(acc)
        xn = y * g_ref[...] + b_ref[...]
        xn_ref[...] = xn
        xb_ref[...] = xn.astype(BF16)

    d = w.shape[1]
    return _rowwise(body, [*a_list, x], [w, g, b], [(d, F32), (d, F32), (d, BF16)], tm=tm, name=name)


def _bwd_ff2(dzb, h, w_ff2, name, after=None):
    def body(i, steps, dz_ref, h_ref, w_ref, dh_ref):
        dr = _dot_nt(dz_ref[...], w_ref[...])
        dh_ref[...] = (dr * (2.0 * jnp.maximum(h_ref[...].astype(F32), 0.0))).astype(BF16)

    return _rowwise(body, [dzb, h], [w_ff2], [(h.shape[1], BF16)], tm=_tile(h.shape[0], WIDE_ROW_TILE), name=name, after=after)[0]


def _bwd_nt(g_list, w, col_offs, res, alpha, tm, name):
    ng = len(g_list)
    widths = [g.shape[1] for g in g_list]

    def body(i, steps, *refs):
        g_refs = refs[:ng]
        if res is None:
            w_ref, o_ref = refs[ng:]
            acc = None
        else:
            r_ref, w_ref, o_ref = refs[ng:]
            acc = alpha * r_ref[...]
        for g_ref, off, n in zip(g_refs, col_offs, widths):
            part = _dot_nt(g_ref[...], w_ref[:, off:off + n])
            acc = part if acc is None else acc + part
        o_ref[...] = acc.astype(o_ref.dtype)

    rows = list(g_list) + ([] if res is None else [res])
    return rows, w, body, tm, name


def _bwd_data(g_list, w, col_offs, res, alpha, out_dtype, tm, name):
    rows, w, body, tm, name = _bwd_nt(g_list, w, col_offs, res, alpha, tm, name)
    return _rowwise(body, rows, [w], [(w.shape[0], out_dtype)], tm=tm, name=name)[0]


def _bwd_data_ln(g_list, w, col_offs, res, alpha, z, gain, tm, name):
    ng = len(g_list)
    widths = [g.shape[1] for g in g_list]

    def body(i, steps, *refs):
        g_refs = refs[:ng]
        r_ref, z_ref, w_ref, gain_ref, dz_ref, dzb_ref, dg_ref, db_ref = refs[ng:]

        @pl.when(i == 0)
        def _():
            dg_ref[...] = jnp.zeros_like(dg_ref)
            db_ref[...] = jnp.zeros_like(db_ref)

        dy = alpha * r_ref[...]
        for g_ref, off, n in zip(g_refs, col_offs, widths):
            dy = dy + _dot_nt(g_ref[...], w_ref[:, off:off + n])
        y, rstd = _ln_stats(z_ref[...])
        dz = _ln_bwd(dy, y, rstd, gain_ref[...])
        dz_ref[...] = dz
        dzb_ref[...] = dz.astype(BF16)
        dg_ref[...] += _colsum(dy * y)
        db_ref[...] += _colsum(dy)

    d = z.shape[1]
    return _rowwise(body, [*g_list, res, z], [w, gain], [(d, F32), (d, BF16)], [((1, d), F32), ((1, d), F32)], tm=tm, name=name)


def _loss_head_ln(y, target, z, gain, name):
    d = y.shape[1]

    def body(i, steps, y_ref, t_ref, z_ref, gain_ref, dz_ref, dzb_ref, dg_ref, db_ref, l_ref):
        @pl.when(i == 0)
        def _():
            for r in (dg_ref, db_ref, l_ref):
                r[...] = jnp.zeros_like(r)

        err = y_ref[...] - t_ref[...]
        l_ref[...] += jnp.sum(err * err) * (0.5 / d)
        dy = err * (1.0 / d)
        yn, rstd = _ln_stats(z_ref[...])
        dz = _ln_bwd(dy, yn, rstd, gain_ref[...])
        dz_ref[...] = dz
        dzb_ref[...] = dz.astype(BF16)
        dg_ref[...] += _colsum(dy * yn)
        db_ref[...] += _colsum(dy)

    return _rowwise(body, [y, target, z], [gain], [(d, F32), (d, BF16)],
                    [((1, d), F32), ((1, d), F32), ((SUBLANES, 128), F32)], tm=_tile(y.shape[0], ROW_TILE), name=name)


def _bwd_out_proj(dzb, w_out, widths, name):
    offs = [sum(widths[:k]) for k in range(len(widths))]

    def body(i, steps, dz_ref, w_ref, *o_refs):
        dz = dz_ref[...]
        for o_ref, off, n in zip(o_refs, offs, widths):
            o_ref[...] = _dot_nt(dz, w_ref[off:off + n, :])

    return _rowwise(body, [dzb], [w_out], [(n, F32) for n in widths], tm=_tile(dzb.shape[0], MATMUL_ROW_TILE), name=name)


def _adamw(w, g, m, v, name):
    def body(i, steps, w_ref, g_ref, m_ref, v_ref, d_ref, nm_ref, nv_ref):
        g_ = g_ref[...]
        nm = ADAM_B1 * m_ref[...] + (1.0 - ADAM_B1) * g_
        nv = ADAM_B2 * v_ref[...] + (1.0 - ADAM_B2) * (g_ * g_)
        m_hat = nm / (1.0 - ADAM_B1 ** ADAM_STEP)
        v_hat = nv / (1.0 - ADAM_B2 ** ADAM_STEP)
        d_ref[...] = -ADAM_LR * (m_hat / (jnp.sqrt(v_hat) + ADAM_EPS) + ADAM_WD * w_ref[...])
        nm_ref[...] = nm
        nv_ref[...] = nv

    c = w.shape[1]
    return _rowwise(body, [w, g, m, v], [], [(c, F32)] * 3, tm=_tile(w.shape[0], ROW_TILE), name=name)


def _mm_tn(a, g, name, slots=1, after=None):
    t, ka = a.shape
    n = g.shape[1]
    ta, tn, tk = _tile(ka, TN_TILE), _tile(n // slots, TN_TILE), _tile(t, TN_TILE)
    per = n // slots // tn

    order = [] if after is None else [after]

    def kern(a_ref, g_ref, *refs):
        o_ref = refs[-1]

        @pl.when(pl.program_id(2) == 0)
        def _():
            o_ref[...] = jnp.zeros_like(o_ref)

        o_ref[0] += _dot_tn(a_ref[...], g_ref[...])

    return pl.pallas_call(
        kern, grid=(ka // ta, n // tn, t // tk),
        in_specs=[pl.BlockSpec((tk, ta), lambda i, j, k: (k, i)), pl.BlockSpec((tk, tn), lambda i, j, k: (k, j))] + [ANY] * len(order),
        out_specs=pl.BlockSpec((1, ta, tn), lambda i, j, k: (j // per, i, j % per)), out_shape=SDS((slots, ka, n // slots), F32),
        compiler_params=_params("arbitrary", "arbitrary", "arbitrary"), name=name)(a, g, *order)


def _mm_tn_cols(a, g_list, name):
    t, ka = a.shape
    widths = [g.shape[1] for g in g_list]
    offs = [sum(widths[:k]) for k in range(len(widths))]
    ta, tk = _tile(ka, ROW_TILE), _tile(t, TN_TILE)

    def kern(a_ref, *refs):
        g_refs, o_ref = refs[:-1], refs[-1]

        @pl.when(pl.program_id(1) == 0)
        def _():
            o_ref[...] = jnp.zeros_like(o_ref)

        a_ = a_ref[...]
        for g_ref, off, n in zip(g_refs, offs, widths):
            o_ref[:, off:off + n] += _dot_tn(a_, g_ref[...])

    return pl.pallas_call(
        kern, grid=(ka // ta, t // tk),
        in_specs=[pl.BlockSpec((tk, ta), lambda i, k: (k, i))] + [pl.BlockSpec((tk, n), lambda i, k: (k, 0)) for n in widths],
        out_specs=pl.BlockSpec((ta, sum(widths)), lambda i, k: (i, 0)), out_shape=SDS((ka, sum(widths)), F32),
        compiler_params=_params("arbitrary", "arbitrary"), name=name)(a, *g_list)


def _mm_tn_rows(a_list, g, name):
    t, n = g.shape
    widths = [a.shape[1] for a in a_list]
    offs = [sum(widths[:k]) for k in range(len(widths))]
    tn, tk = _tile(n, TN_TILE), _tile(t, TN_TILE)

    def kern(*refs):
        a_refs, g_ref, o_ref = refs[:-2], refs[-2], refs[-1]

        @pl.when(pl.program_id(1) == 0)
        def _():
            o_ref[...] = jnp.zeros_like(o_ref)

        g_ = g_ref[...]
        for a_ref, off, ka in zip(a_refs, offs, widths):
            o_ref[off:off + ka, :] += _dot_tn(a_ref[...], g_)

    return pl.pallas_call(
        kern, grid=(n // tn, t // tk),
        in_specs=[pl.BlockSpec((tk, ka), lambda j, k: (k, 0)) for ka in widths] + [pl.BlockSpec((tk, tn), lambda j, k: (k, j))],
        out_specs=pl.BlockSpec((sum(widths), tn), lambda j, k: (0, j)), out_shape=SDS((sum(widths), n), F32),
        compiler_params=_params("arbitrary", "arbitrary"), name=name)(*a_list, g)


def _conv_geometry(seq, taps):
    pad = _round_up(taps - 1, SUBLANES)
    rc = _tile(seq, CONV_ROWS)
    assert rc % pad == 0 and seq % rc == 0
    return pad, rc, seq // rc


def _chunk_spec(rc, n, nch):
    return pl.BlockSpec((rc, n), lambda b, i: (b * nch + i, 0))


def _prev_halo_spec(pad, rc, n, nch):
    per = rc // pad
    return pl.BlockSpec((pad, n), lambda b, i: (jnp.maximum((b * nch + i) * per - 1, 0), 0))


def _next_halo_spec(pad, rc, n, nch, total_rows):
    per = rc // pad
    last = total_rows // pad - 1
    return pl.BlockSpec((pad, n), lambda b, i: (jnp.minimum((b * nch + i + 1) * per, last), 0))


def _whole2(a):
    return pl.BlockSpec(a.shape, lambda b, i, nd=len(a.shape): (0,) * nd)


def _sub_rows(rc):
    return SUB_ROWS if rc % SUB_ROWS == 0 else rc


def _build_shifts(sh_ref, src_ref, offsets):
    rows = src_ref.shape[0]
    for r in sorted({o % SUBLANES for o in offsets} - {0}):
        sh_ref[r, 0:rows - SUBLANES, :] = src_ref[r:r + rows - SUBLANES, :]


def _read_shifted(sh_ref, src_ref, o, s0, sub):
    r = o % SUBLANES
    a = o - r + s0
    return src_ref[a:a + sub, :] if r == 0 else sh_ref[r, a:a + sub, :]


def _tap_sum(sh_ref, src_ref, w_ref, offsets, s0, sub):
    acc = None
    for k, o in enumerate(offsets):
        term = _read_shifted(sh_ref, src_ref, o, s0, sub) * w_ref[k:k + 1, :]
        acc = term if acc is None else acc + term
    return acc


def _row_groups(x):
    acc = x[0:SUBLANES, :]
    for g0 in range(SUBLANES, x.shape[0], SUBLANES):
        acc = acc + x[g0:g0 + SUBLANES, :]
    return acc


def _mix_a_fwd(a_val, a_gate, conv_w, conv_b, ln_g, ln_b, batch, name):
    t, da = a_val.shape
    taps = conv_w.shape[0]
    pad, rc, nch = _conv_geometry(t // batch, taps)
    sub = _sub_rows(rc)
    offs = [pad - (taps - 1) + k for k in range(taps)]

    def kern(av_ref, ag_ref, pav_ref, pag_ref, w_ref, cb_ref, g_ref, b_ref, a_ref, ac_ref, pad_ref, sh_ref):
        first = pl.program_id(1) == 0
        pad_ref[0:pad, :] = jnp.where(first, 0.0, pav_ref[...] * _sigmoid(pag_ref[...]))
        for s0 in range(0, rc, sub):
            pad_ref[pad + s0:pad + s0 + sub, :] = av_ref[s0:s0 + sub, :] * _sigmoid(ag_ref[s0:s0 + sub, :])
        _build_shifts(sh_ref, pad_ref, offs)
        for s0 in range(0, rc, sub):
            ac = _tap_sum(sh_ref, pad_ref, w_ref, offs, s0, sub) + cb_ref[...]
            ac_ref[s0:s0 + sub, :] = ac
            y, _ = _ln_stats(ac)
            aln = y * g_ref[...] + b_ref[...]
            a_ref[s0:s0 + sub, :] = (aln * _sigmoid(aln)).astype(BF16)

    chunk, halo = _chunk_spec(rc, da, nch), _prev_halo_spec(pad, rc, da, nch)
    return pl.pallas_call(
        kern, grid=(batch, nch),
        in_specs=[chunk, chunk, halo, halo, _whole2(conv_w), _whole2(conv_b), _whole2(ln_g), _whole2(ln_b)],
        out_specs=[chunk, chunk], out_shape=[SDS((t, da), BF16), SDS((t, da), F32)],
        scratch_shapes=[pltpu.VMEM((pad + rc, da), F32), pltpu.VMEM((SUBLANES, pad + rc, da), F32)],
        compiler_params=_params("arbitrary", "arbitrary"), name=name)(a_val, a_gate, a_val, a_gate, conv_w, conv_b, ln_g, ln_b)


def _mix_a_bwd(a_val, a_gate, ac, da_, conv_w, ln_g, ln_b, batch, name):
    t, da = a_val.shape
    taps = conv_w.shape[0]
    pad, rc, nch = _conv_geometry(t // batch, taps)
    sub = _sub_rows(rc)
    offs_in = [pad - (taps - 1) + k for k in range(taps)]
    offs_out = [taps - 1 - k for k in range(taps)]

    def kern(av_ref, ag_ref, pav_ref, pag_ref, ac_ref, da_ref, nac_ref, nda_ref, w_ref, g_ref, b_ref,
             dav_ref, dag_ref, dw_ref, dcb_ref, dg_ref, db_ref, gpad_ref, dpad_ref, shg_ref, shd_ref, dwacc_ref):
        i = pl.program_id(1)
        start = (pl.program_id(0) == 0) & (i == 0)
        end = (pl.program_id(0) == batch - 1) & (i == nch - 1)

        @pl.when(start)
        def _():
            for r in (dcb_ref, dg_ref, db_ref, dwacc_ref):
                r[...] = jnp.zeros_like(r)

        gain, bias = g_ref[...], b_ref[...]

        def d_conv_out(ac_, dout):
            y, rstd = _ln_stats(ac_)
            aln = y * gain + bias
            sig = _sigmoid(aln)
            daln = dout * (sig * (1.0 + aln * (1.0 - sig)))
            return _ln_bwd(daln, y, rstd, gain), daln, y

        gpad_ref[0:pad, :] = jnp.where(i == 0, 0.0, pav_ref[...] * _sigmoid(pag_ref[...]))
        dac_next, _, _ = d_conv_out(nac_ref[...], nda_ref[...])
        dpad_ref[rc:rc + pad, :] = jnp.where(i == nch - 1, 0.0, dac_next)
        for s0 in range(0, rc, sub):
            rows = slice(s0, s0 + sub)
            dac, daln, y = d_conv_out(ac_ref[rows, :], da_ref[rows, :])
            dg_ref[...] += _colsum(daln * y)
            db_ref[...] += _colsum(daln)
            dcb_ref[...] += _colsum(dac)
            dpad_ref[rows, :] = dac
            gpad_ref[pad + s0:pad + s0 + sub, :] = av_ref[rows, :] * _sigmoid(ag_ref[rows, :])
        _build_shifts(shg_ref, gpad_ref, offs_in)
        _build_shifts(shd_ref, dpad_ref, offs_out)
        for s0 in range(0, rc, sub):
            rows = slice(s0, s0 + sub)
            dac = dpad_ref[rows, :]
            for k, o in enumerate(offs_in):
                dwacc_ref[k] += _row_groups(dac * _read_shifted(shg_ref, gpad_ref, o, s0, sub))
            dgl = _tap_sum(shd_ref, dpad_ref, w_ref, offs_out, s0, sub)
            av = av_ref[rows, :]
            sig = _sigmoid(ag_ref[rows, :])
            dav_ref[rows, :] = (dgl * sig).astype(BF16)
            dag_ref[rows, :] = (dgl * av * sig * (1.0 - sig)).astype(BF16)

        @pl.when(end)
        def _():
            for k in range(taps):
                dw_ref[k:k + 1, :] = _colsum(dwacc_ref[k])

    chunk, prev = _chunk_spec(rc, da, nch), _prev_halo_spec(pad, rc, da, nch)
    nxt = _next_halo_spec(pad, rc, da, nch, t)
    vec = SDS((1, da), F32)
    return pl.pallas_call(
        kern, grid=(batch, nch),
        in_specs=[chunk, chunk, prev, prev, chunk, chunk, nxt, nxt, _whole2(conv_w), _whole2(ln_g), _whole2(ln_b)],
        out_specs=[chunk, chunk, _whole2(conv_w), _whole2(vec), _whole2(vec), _whole2(vec)],
        out_shape=[SDS((t, da), BF16), SDS((t, da), BF16), SDS(conv_w.shape, F32), vec, vec, vec],
        scratch_shapes=[pltpu.VMEM((pad + rc, da), F32), pltpu.VMEM((rc + pad, da), F32),
                        pltpu.VMEM((SUBLANES, pad + rc, da), F32), pltpu.VMEM((SUBLANES, rc + pad, da), F32),
                        pltpu.VMEM((taps, SUBLANES, da), F32)],
        compiler_params=_params("arbitrary", "arbitrary"), name=name)(
            a_val, a_gate, a_val, a_gate, ac, da_, ac, da_, conv_w, ln_g, ln_b)


def _mix_c_fwd(c_b, c_c, c_x, conv_w, batch, name):
    t, dc = c_b.shape
    taps = conv_w.shape[0]
    pad, rc, nch = _conv_geometry(t // batch, taps)
    sub = _sub_rows(rc)
    offs = [pad - (taps - 1) + k for k in range(taps)]

    def kern(cb_ref, cc_ref, cx_ref, pcc_ref, pcx_ref, w_ref, o_ref, pad_ref, sh_ref):
        pad_ref[0:pad, :] = jnp.where(pl.program_id(1) == 0, 0.0, pcc_ref[...] * pcx_ref[...])
        for s0 in range(0, rc, sub):
            pad_ref[pad + s0:pad + s0 + sub, :] = cc_ref[s0:s0 + sub, :] * cx_ref[s0:s0 + sub, :]
        _build_shifts(sh_ref, pad_ref, offs)
        for s0 in range(0, rc, sub):
            o_ref[s0:s0 + sub, :] = (cb_ref[s0:s0 + sub, :] * _tap_sum(sh_ref, pad_ref, w_ref, offs, s0, sub)).astype(BF16)

    chunk, prev = _chunk_spec(rc, dc, nch), _prev_halo_spec(pad, rc, dc, nch)
    return pl.pallas_call(
        kern, grid=(batch, nch), in_specs=[chunk, chunk, chunk, prev, prev, _whole2(conv_w)],
        out_specs=chunk, out_shape=SDS((t, dc), BF16),
        scratch_shapes=[pltpu.VMEM((pad + rc, dc), F32), pltpu.VMEM((SUBLANES, pad + rc, dc), F32)],
        compiler_params=_params("arbitrary", "arbitrary"), name=name)(c_b, c_c, c_x, c_c, c_x, conv_w)


def _mix_c_bwd(c_b, c_c, c_x, dco, conv_w, batch, name):
    t, dc = c_b.shape
    taps = conv_w.shape[0]
    pad, rc, nch = _conv_geometry(t // batch, taps)
    sub = _sub_rows(rc)
    offs_in = [pad - (taps - 1) + k for k in range(taps)]
    offs_out = [taps - 1 - k for k in range(taps)]

    def kern(cb_ref, cc_ref, cx_ref, do_ref, pcc_ref, pcx_ref, ncb_ref, ndo_ref, w_ref,
             dcb_ref, dcc_ref, dcx_ref, dw_ref, ppad_ref, dpad_ref, shp_ref, shd_ref, dwacc_ref):
        i = pl.program_id(1)

        @pl.when((pl.program_id(0) == 0) & (i == 0))
        def _():
            dwacc_ref[...] = jnp.zeros_like(dwacc_ref)

        ppad_ref[0:pad, :] = jnp.where(i == 0, 0.0, pcc_ref[...] * pcx_ref[...])
        dpad_ref[rc:rc + pad, :] = jnp.where(i == nch - 1, 0.0, ndo_ref[...] * ncb_ref[...])
        for s0 in range(0, rc, sub):
            rows = slice(s0, s0 + sub)
            ppad_ref[pad + s0:pad + s0 + sub, :] = cc_ref[rows, :] * cx_ref[rows, :]
            dpad_ref[rows, :] = do_ref[rows, :] * cb_ref[rows, :]
        _build_shifts(shp_ref, ppad_ref, offs_in)
        _build_shifts(shd_ref, dpad_ref, offs_out)
        for s0 in range(0, rc, sub):
            rows = slice(s0, s0 + sub)
            dcv = dpad_ref[rows, :]
            cv = None
            for k, o in enumerate(offs_in):
                shifted = _read_shifted(shp_ref, ppad_ref, o, s0, sub)
                dwacc_ref[k] += _row_groups(dcv * shifted)
                cv = shifted * w_ref[k:k + 1, :] if cv is None else cv + shifted * w_ref[k:k + 1, :]
            dp = _tap_sum(shd_ref, dpad_ref, w_ref, offs_out, s0, sub)
            dcb_ref[rows, :] = (do_ref[rows, :] * cv).astype(BF16)
            dcc_ref[rows, :] = (dp * cx_ref[rows, :]).astype(BF16)
            dcx_ref[rows, :] = (dp * cc_ref[rows, :]).astype(BF16)

        @pl.when((pl.program_id(0) == batch - 1) & (i == nch - 1))
        def _():
            for k in range(taps):
                dw_ref[k:k + 1, :] = _colsum(dwacc_ref[k])

    chunk, prev = _chunk_spec(rc, dc, nch), _prev_halo_spec(pad, rc, dc, nch)
    nxt = _next_halo_spec(pad, rc, dc, nch, t)
    return pl.pallas_call(
        kern, grid=(batch, nch), in_specs=[chunk, chunk, chunk, chunk, prev, prev, nxt, nxt, _whole2(conv_w)],
        out_specs=[chunk, chunk, chunk, _whole2(conv_w)],
        out_shape=[SDS((t, dc), BF16)] * 3 + [SDS(conv_w.shape, F32)],
        scratch_shapes=[pltpu.VMEM((pad + rc, dc), F32), pltpu.VMEM((rc + pad, dc), F32),
                        pltpu.VMEM((SUBLANES, pad + rc, dc), F32), pltpu.VMEM((SUBLANES, rc + pad, dc), F32),
                        pltpu.VMEM((taps, SUBLANES, dc), F32)],
        compiler_params=_params("arbitrary", "arbitrary"), name=name)(c_b, c_c, c_x, dco, c_c, c_x, c_b, dco, conv_w)


def _head_of_lane(db):
    return lax.broadcasted_iota(jnp.int32, (1, db), 1) // (db // B_HEADS)


def _tril(rows_ge_cols=True):
    r = lax.broadcasted_iota(jnp.int32, (CHUNK, CHUNK), 0)
    c = lax.broadcasted_iota(jnp.int32, (CHUNK, CHUNK), 1)
    return (r >= c) if rows_ge_cols else (r <= c)


def _spatial_mix(wm, vb, bias, head):
    mixed = bias
    for h in range(B_HEADS):
        mixed = mixed + jnp.where(head == h, _dot(wm[h], vb), 0.0)
    return mixed


def _mix_b_fwd(b_u, b_v, w_s, bias_full, ln_g, ln_b, name):
    t, db = b_u.shape
    rb = _tile(t, ROW_TILE)

    def kern(bu_ref, bv_ref, ws_ref, bias_ref, g_ref, b_ref, o_ref):
        head = _head_of_lane(db)
        wm = [jnp.where(_tril(), ws_ref[h], 0.0).astype(BF16) for h in range(B_HEADS)]
        for ch in range(rb // CHUNK):
            rows = slice(ch * CHUNK, (ch + 1) * CHUNK)
            y, _ = _ln_stats(_gelu(bv_ref[rows, :]))
            vb = (y * g_ref[...] + b_ref[...]).astype(BF16)
            mixed = _spatial_mix(wm, vb, bias_ref[...], head)
            o_ref[rows, :] = (_gelu(bu_ref[rows, :]) * mixed).astype(BF16)

    def whole(a):
        return pl.BlockSpec(a.shape, lambda i, nd=len(a.shape): (0,) * nd)

    tile = pl.BlockSpec((rb, db), lambda i: (i, 0))
    return pl.pallas_call(
        kern, grid=(t // rb,), in_specs=[tile, tile, whole(w_s), whole(bias_full), whole(ln_g), whole(ln_b)],
        out_specs=tile, out_shape=SDS((t, db), BF16), compiler_params=_params("arbitrary"), name=name)(
            b_u, b_v, w_s, bias_full, ln_g, ln_b)


def _mix_b_bwd(b_u, b_v, dbo, w_s, w_s_t, bias_full, ln_g, ln_b, name):
    t, db = b_u.shape
    rb = _tile(t, ROW_TILE)
    steps = t // rb

    def kern(bu_ref, bv_ref, do_ref, ws_ref, wst_ref, bias_ref, g_ref, b_ref,
             dbu_ref, dbv_ref, dws_ref, dbs_ref, dg_ref, dbeta_ref, dbias_ref):
        i = pl.program_id(0)

        @pl.when(i == 0)
        def _():
            for r in (dws_ref, dg_ref, dbeta_ref, dbias_ref):
                r[...] = jnp.zeros_like(r)

        head = _head_of_lane(db)
        gain = g_ref[...]
        wm = [jnp.where(_tril(), ws_ref[h], 0.0).astype(BF16) for h in range(B_HEADS)]
        wmt = [jnp.where(_tril(False), wst_ref[h], 0.0).astype(BF16) for h in range(B_HEADS)]
        for ch in range(rb // CHUNK):
            rows = slice(ch * CHUNK, (ch + 1) * CHUNK)
            bu, bv, dout = bu_ref[rows, :], bv_ref[rows, :], do_ref[rows, :]
            y, rstd = _ln_stats(_gelu(bv))
            vb = (y * gain + b_ref[...]).astype(BF16)
            mixed = _spatial_mix(wm, vb, bias_ref[...], head)
            du = dout * mixed
            dmixed = dout * _gelu(bu)
            dbias_ref[...] += dmixed
            dmb = dmixed.astype(BF16)
            dv = None
            for h in range(B_HEADS):
                dws_ref[h] += _dot_nt(jnp.where(head == h, dmb, 0.0).astype(BF16), vb)
                part = jnp.where(head == h, _dot(wmt[h], dmb), 0.0)
                dv = part if dv is None else dv + part
            dg_ref[...] += _colsum(dv * y)
            dbeta_ref[...] += _colsum(dv)
            dbv_ref[rows, :] = (_ln_bwd(dv, y, rstd, gain) * _gelu_grad(bv)).astype(BF16)
            dbu_ref[rows, :] = (du * _gelu_grad(bu)).astype(BF16)

        @pl.when(i == steps - 1)
        def _():
            for h in range(B_HEADS):
                dws_ref[h] = jnp.where(_tril(), dws_ref[h], 0.0)
                dbs_ref[h] = jnp.sum(jnp.where(head == h, dbias_ref[...], 0.0), axis=1, keepdims=True)

    def whole(a):
        return pl.BlockSpec(a.shape, lambda i, nd=len(a.shape): (0,) * nd)

    tile = pl.BlockSpec((rb, db), lambda i: (i, 0))
    vec = SDS((1, db), F32)
    dbs = SDS((B_HEADS, CHUNK, 1), F32)
    return pl.pallas_call(
        kern, grid=(steps,),
        in_specs=[tile, tile, tile, whole(w_s), whole(w_s_t), whole(bias_full), whole(ln_g), whole(ln_b)],
        out_specs=[tile, tile, whole(w_s), whole(dbs), whole(vec), whole(vec)],
        out_shape=[SDS((t, db), BF16), SDS((t, db), BF16), SDS(w_s.shape, F32), dbs, vec, vec],
        scratch_shapes=[pltpu.VMEM((CHUNK, db), F32)],
        compiler_params=_params("arbitrary"), name=name)(b_u, b_v, dbo, w_s, w_s_t, bias_full, ln_g, ln_b)


def _softmax_rows(s):
    e = jnp.exp(s - jnp.max(s, axis=-1, keepdims=True))
    return e / jnp.sum(e, axis=-1, keepdims=True)


def _attention_fwd(q, kv, batch, name):
    t, d = q.shape
    seq, mlen, hd = t // batch, kv.shape[0] // batch, d // X_HEADS
    ar = _tile(seq, ATT_ROWS)
    scale = hd ** -0.5

    def kern(q_ref, k_ref, v_ref, o_ref):
        k, v = k_ref[...], v_ref[...]
        for r0 in range(0, seq, ar):
            p = _softmax_rows(_dot_nt(q_ref[r0:r0 + ar, :], k) * scale)
            o_ref[r0:r0 + ar, :] = _dot(p.astype(BF16), v).astype(BF16)

    qs = pl.BlockSpec((seq, hd), lambda b, h: (b, h))
    return pl.pallas_call(
        kern, grid=(batch, X_HEADS),
        in_specs=[qs, pl.BlockSpec((mlen, hd), lambda b, h: (b, h)), pl.BlockSpec((mlen, hd), lambda b, h: (b, X_HEADS + h))],
        out_specs=qs, out_shape=SDS((t, d), BF16), compiler_params=_params("arbitrary", "arbitrary"), name=name)(q, kv, kv)


def _attention_bwd(q, kv, do, batch, name):
    t, d = q.shape
    seq, mlen, hd = t // batch, kv.shape[0] // batch, d // X_HEADS
    ar = _tile(seq, ATT_ROWS)
    scale = hd ** -0.5

    def kern(q_ref, k_ref, v_ref, do_ref, dq_ref, dk_ref, dv_ref):
        k, v = k_ref[...], v_ref[...]
        dk = jnp.zeros((mlen, hd), F32)
        dv = jnp.zeros((mlen, hd), F32)
        for r0 in range(0, seq, ar):
            qr, dor = q_ref[r0:r0 + ar, :], do_ref[r0:r0 + ar, :]
            p = _softmax_rows(_dot_nt(qr, k) * scale)
            dp = _dot_nt(dor, v)
            ds = (p * (dp - jnp.sum(p * dp, axis=-1, keepdims=True)) * scale).astype(BF16)
            dq_ref[r0:r0 + ar, :] = _dot(ds, k).astype(BF16)
            dk = dk + _dot_tn(ds, qr)
            dv = dv + _dot_tn(p.astype(BF16), dor)
        dk_ref[...] = dk.astype(BF16)
        dv_ref[...] = dv.astype(BF16)

    qs = pl.BlockSpec((seq, hd), lambda b, h: (b, h))
    ks = pl.BlockSpec((mlen, hd), lambda b, h: (b, h))
    dkv = SDS((kv.shape[0], d), BF16)
    return pl.pallas_call(
        kern, grid=(batch, X_HEADS),
        in_specs=[qs, ks, pl.BlockSpec((mlen, hd), lambda b, h: (b, X_HEADS + h)), qs],
        out_specs=[qs, ks, ks], out_shape=[SDS((t, d), BF16), dkv, dkv],
        compiler_params=_params("arbitrary", "arbitrary"), name=name)(q, kv, kv, do)


def _place():
    x, y, c = lax.axis_index("x"), lax.axis_index("y"), lax.axis_index("c")
    other_chips = [(1 - x, y), (x, 1 - y), (1 - x, 1 - y)]
    return x, y, c, other_chips


def _comm_call(kern, out_shape, n_pairs, name, *args):
    return pl.pallas_call(
        kern, out_shape=out_shape, in_specs=[ANY] * len(args), out_specs=jax.tree.map(lambda _: ANY, out_shape),
        scratch_shapes=[pltpu.SemaphoreType.DMA((n_pairs,)), pltpu.SemaphoreType.DMA((n_pairs,)), pltpu.SemaphoreType.DMA((n_pairs,))],
        name=name)(*args)


def _all_gather_chips(shard, name):
    r, cols = shard.shape
    rh = r // 2

    def kern(s_ref, o_ref, send_sems, recv_sems, local_sems):
        x, y, c, chips = _place()
        mine_slot = 2 * x + y
        half = pl.ds(c * rh, rh)
        other_half = pl.ds((1 - c) * rh, rh)

        def copy(k, src, dst, to):
            return pltpu.make_async_remote_copy(src_ref=src, dst_ref=dst, send_sem=send_sems.at[k], recv_sem=recv_sems.at[k],
                                                device_id=to, device_id_type=MESH)

        mine = pltpu.make_async_copy(s_ref, o_ref.at[mine_slot], local_sems.at[0])
        mine.start()
        first = [copy(j, s_ref.at[half], o_ref.at[mine_slot, half], (px, py, c)) for j, (px, py) in enumerate(chips)]
        for cp in first:
            cp.start()
        passed = []
        for j, (px, py) in enumerate(chips):
            landed = o_ref.at[2 * px + py, half]
            copy(j, landed, landed, (px, py, c)).wait_recv()
            cp = copy(3 + j, landed, landed, (x, y, 1 - c))
            cp.start()
            passed.append(cp)
        for j, (px, py) in enumerate(chips):
            theirs = o_ref.at[2 * px + py, other_half]
            copy(3 + j, theirs, theirs, (x, y, 1 - c)).wait_recv()
        for cp in first + passed:
            cp.wait_send()
        mine.wait()

    return _comm_call(kern, SDS((N_CHIPS, r, cols), shard.dtype), 6, name, shard)


def _join_halves(joined, name):
    n = len(joined)

    def kern(*refs):
        j_refs = refs[:n]
        send_sems, recv_sems = refs[2 * n:]
        x, y, c, _ = _place()
        copies = []
        for k, j_ref in enumerate(j_refs):
            rh = j_ref.shape[0] // 2
            rows = j_ref.at[pl.ds(c * rh, rh)]
            cp = pltpu.make_async_remote_copy(src_ref=rows, dst_ref=rows, send_sem=send_sems.at[k], recv_sem=recv_sems.at[k],
                                              device_id=(x, y, 1 - c), device_id_type=MESH)
            cp.start()
            copies.append(cp)
        for k, (cp, j_ref) in enumerate(zip(copies, j_refs)):
            rh = j_ref.shape[0] // 2
            theirs = j_ref.at[pl.ds((1 - c) * rh, rh)]
            pltpu.make_async_remote_copy(src_ref=theirs, dst_ref=theirs, send_sem=send_sems.at[k], recv_sem=recv_sems.at[k],
                                         device_id=(x, y, 1 - c), device_id_type=MESH).wait_recv()
            cp.wait_send()

    return pl.pallas_call(
        kern, out_shape=[SDS(j.shape, j.dtype) for j in joined], in_specs=[ANY] * n, out_specs=[ANY] * n,
        input_output_aliases={k: k for k in range(n)},
        scratch_shapes=[pltpu.SemaphoreType.DMA((n,)), pltpu.SemaphoreType.DMA((n,))], name=name)(*joined)


HBM = pl.BlockSpec(memory_space=pltpu.HBM)
SEMAPHORES = pl.BlockSpec(memory_space=pltpu.SEMAPHORE)
COPIES_PER_ARRAY = N_CHIPS - 1
GATHER_COPIES = N_CHIPS


def _in_hbm(a):
    return pltpu.with_memory_space_constraint(a, pltpu.HBM)


def _gather_copies(shard_refs, land_refs, send_sems, recv_sems):
    x, y, c, chips = _place()
    copies = []
    for k, (s_ref, land_ref) in enumerate(zip(shard_refs, land_refs)):
        half = pl.ds(c * (s_ref.shape[0] // 2), s_ref.shape[0] // 2)
        for j, (px, py) in enumerate(chips):
            n = GATHER_COPIES * k + j
            copies.append(pltpu.make_async_remote_copy(
                src_ref=s_ref.at[half], dst_ref=land_ref.at[2 * x + y, half], send_sem=send_sems.at[n], recv_sem=recv_sems.at[n],
                device_id=(px, py, c), device_id_type=MESH))
        n = GATHER_COPIES * k + N_CHIPS - 1
        copies.append(pltpu.make_async_remote_copy(
            src_ref=s_ref, dst_ref=land_ref.at[2 * x + y], send_sem=send_sems.at[n], recv_sem=recv_sems.at[n],
            device_id=(x, y, 1 - c), device_id_type=MESH))
    return copies


def _swap_copies(full_refs, land_refs, send_sems, recv_sems):
    x, y, c, _ = _place()
    copies = []
    for k, (g_ref, land_ref) in enumerate(zip(full_refs, land_refs)):
        rh = land_ref.shape[1]
        copies.append(pltpu.make_async_remote_copy(
            src_ref=g_ref.at[:, pl.ds((1 - c) * rh, rh), :], dst_ref=land_ref, send_sem=send_sems.at[k], recv_sem=recv_sems.at[k],
            device_id=(x, y, 1 - c), device_id_type=MESH))
    return copies


def _owner_copies(sum_refs, land_refs, send_sems, recv_sems):
    x, y, c, chips = _place()
    copies = []
    for k, (s_ref, land_ref) in enumerate(zip(sum_refs, land_refs)):
        for j, (px, py) in enumerate(chips):
            n = COPIES_PER_ARRAY * k + j
            copies.append(pltpu.make_async_remote_copy(
                src_ref=s_ref.at[2 * px + py], dst_ref=land_ref.at[j], send_sem=send_sems.at[n], recv_sem=recv_sems.at[n],
                device_id=(px, py, c), device_id_type=MESH))
    return copies


def _copies_start(build, per_array, sources, lands, name):
    ns, nb = len(sources), len(sources) + len(lands)
    n_copies = per_array * ns

    def kern(*refs):
        for cp in build(refs[:ns], refs[ns:nb], refs[2 * nb], refs[2 * nb + 1]):
            cp.start()
        refs[-1][...] = jnp.zeros_like(refs[-1])

    bufs = [*sources, *lands]
    res = pl.pallas_call(
        kern, name=name,
        out_shape=(*[pltpu.HBM(b.shape, b.dtype) for b in bufs], pltpu.SemaphoreType.DMA((n_copies,)), pltpu.SemaphoreType.DMA((n_copies,)),
                   SDS((SUBLANES, 128), F32)),
        in_specs=[HBM] * nb, out_specs=(*[HBM] * nb, SEMAPHORES, SEMAPHORES, pl.BlockSpec(memory_space=pltpu.VMEM)),
        input_output_aliases={i: i for i in range(nb)},
        compiler_params=pltpu.CompilerParams(has_side_effects=pltpu.SideEffectType.DATAFLOW_SIDE_EFFECTING),
    )(*[_in_hbm(b) for b in bufs])
    return res[nb], res[nb + 1], list(res[:nb]), res[-1]


def _copies_wait(build, send_sems, recv_sems, bufs, ns, after, name):
    nb = len(bufs)

    def kern(*refs):
        for cp in build(refs[:ns], refs[ns:nb], refs[nb], refs[nb + 1]):
            cp.wait_send()
            cp.wait_recv()

    res = pl.pallas_call(
        kern, name=name, out_shape=tuple(pltpu.HBM(b.shape, b.dtype) for b in bufs),
        in_specs=[HBM] * nb + [SEMAPHORES, SEMAPHORES, ANY], out_specs=tuple([HBM] * nb),
        input_output_aliases={i: i for i in range(nb)},
        compiler_params=pltpu.CompilerParams(has_side_effects=pltpu.SideEffectType.DATAFLOW_SIDE_EFFECTING),
    )(*bufs, send_sems, recv_sems, after)
    return list(res[:ns]), list(res[ns:])


def _gather_finish(lands, name):
    ns = len(lands)

    def kern(*refs):
        l_refs = refs[:ns]
        send_sems, recv_sems = refs[2 * ns:]
        x, y, c, chips = _place()
        passed = []
        for k, l_ref in enumerate(l_refs):
            rh = l_ref.shape[1] // 2
            for j, (px, py) in enumerate(chips):
                landed = l_ref.at[2 * px + py, pl.ds(c * rh, rh)]
                cp = pltpu.make_async_remote_copy(
                    src_ref=landed, dst_ref=landed, send_sem=send_sems.at[COPIES_PER_ARRAY * k + j],
                    recv_sem=recv_sems.at[COPIES_PER_ARRAY * k + j], device_id=(x, y, 1 - c), device_id_type=MESH)
                cp.start()
                passed.append(cp)
        for k, l_ref in enumerate(l_refs):
            rh = l_ref.shape[1] // 2
            for j, (px, py) in enumerate(chips):
                theirs = l_ref.at[2 * px + py, pl.ds((1 - c) * rh, rh)]
                pltpu.make_async_remote_copy(
                    src_ref=theirs, dst_ref=theirs, send_sem=send_sems.at[COPIES_PER_ARRAY * k + j],
                    recv_sem=recv_sems.at[COPIES_PER_ARRAY * k + j], device_id=(x, y, 1 - c), device_id_type=MESH).wait_recv()
        for cp in passed:
            cp.wait_send()

    return pl.pallas_call(
        kern, out_shape=[SDS(l.shape, l.dtype) for l in lands], in_specs=[ANY] * ns, out_specs=[ANY] * ns,
        input_output_aliases={k: k for k in range(ns)},
        scratch_shapes=[pltpu.SemaphoreType.DMA((COPIES_PER_ARRAY * ns,)), pltpu.SemaphoreType.DMA((COPIES_PER_ARRAY * ns,))],
        name=name)(*lands)


def _add_sibling(full, theirs, out_dtype, name):
    n, r, cols = full.shape
    rh = r // 2
    tr = _tile(rh, ROW_TILE)
    nb = rh // tr

    def kern(c_ref, a_ref, b_ref, o_ref):
        o_ref[...] = (a_ref[...] + b_ref[...]).astype(out_dtype)

    c = lax.axis_index("c").astype(jnp.int32).reshape(1)
    return pl.pallas_call(
        kern, out_shape=SDS((n, rh, cols), out_dtype),
        grid_spec=pltpu.PrefetchScalarGridSpec(
            num_scalar_prefetch=1, grid=(n, nb),
            in_specs=[pl.BlockSpec((1, tr, cols), lambda j, i, c_ref: (j, c_ref[0] * nb + i, 0)),
                      pl.BlockSpec((1, tr, cols), lambda j, i, c_ref: (j, i, 0))],
            out_specs=pl.BlockSpec((1, tr, cols), lambda j, i, c_ref: (j, i, 0))),
        compiler_params=_params("arbitrary", "arbitrary"), name=name)(c, full, theirs)


def _add_owners(partial, received, name):
    n, rh, cols = partial.shape
    tr = _tile(rh, ROW_TILE)
    nb = rh // tr

    def kern(s_ref, a_ref, b_ref, o_ref):
        acc = a_ref[0].astype(F32)
        for j in range(N_CHIPS - 1):
            acc = acc + b_ref[j].astype(F32)
        o_ref[...] = acc

    place = jnp.stack([2 * lax.axis_index("x") + lax.axis_index("y"), lax.axis_index("c")]).astype(jnp.int32)
    return pl.pallas_call(
        kern, out_shape=SDS((2 * rh, cols), F32),
        grid_spec=pltpu.PrefetchScalarGridSpec(
            num_scalar_prefetch=1, grid=(nb,),
            in_specs=[pl.BlockSpec((1, tr, cols), lambda i, s_ref: (s_ref[0], i, 0)),
                      pl.BlockSpec((N_CHIPS - 1, tr, cols), lambda i, s_ref: (0, i, 0))],
            out_specs=pl.BlockSpec((tr, cols), lambda i, s_ref: (s_ref[1] * nb + i, 0))),
        compiler_params=_params("arbitrary"), name=name)(place, partial, received)


def _pack(parts, lead):
    lead_shape = parts[0].shape[:lead]
    flat = jnp.concatenate([p.reshape(lead_shape + (-1,)) for p in parts], axis=-1)
    n = flat.shape[-1]
    rows = _round_up(-(-n // PACK_W), PACK_ROWS)
    flat = jnp.pad(flat, [(0, 0)] * lead + [(0, rows * PACK_W - n)])
    return flat.reshape(lead_shape + (rows, PACK_W))


def _unpack(buf, shapes, lead):
    lead_shape = buf.shape[:lead]
    flat = buf.reshape(lead_shape + (-1,))
    out, off = [], 0
    for s in shapes:
        n = math.prod(s)
        out.append(flat[..., off:off + n].reshape(lead_shape + tuple(s)))
        off += n
    return out


def _to_slots(full, axis):
    s = full.shape
    split = full.reshape(s[:axis] + (N_CHIPS, s[axis] // N_CHIPS) + s[axis + 1:])
    return jnp.moveaxis(split, axis, 0)


def _from_slots(slots, axis):
    moved = jnp.moveaxis(slots, 0, axis)
    s = moved.shape
    return moved.reshape(s[:axis] + (s[axis] * s[axis + 1],) + s[axis + 2:])


MATMUL_WEIGHTS = (("w_in", 1), ("w_out", 0), ("w_q", 0), ("w_kv", 1), ("w_o", 0), ("w_ff1", 1), ("w_ff2", 0))
CONV_WEIGHTS = (("conv_a_w", 1), ("conv_c_w", 1))
REPLICATED = ("conv_a_b", "ln_a_g", "ln_a_b", "ln_v_g", "ln_v_b", "w_s", "b_s",
              "ln1_g", "ln1_b", "ln2_g", "ln2_b", "ln3_g", "ln3_b")
WEIGHT_ORDER = ("w_in", "conv_a_w", "conv_a_b", "ln_a_g", "ln_a_b", "ln_v_g", "ln_v_b", "w_s", "b_s", "conv_c_w", "w_out",
                "ln1_g", "ln1_b", "w_q", "w_kv", "w_o", "ln2_g", "ln2_b", "w_ff1", "w_ff2", "ln3_g", "ln3_b")


def _row(v):
    return v.reshape(1, -1)


def kernel(x, mem, w_in, conv_a_w, conv_a_b, ln_a_g, ln_a_b, ln_v_g, ln_v_b, w_s, b_s, conv_c_w, w_out, ln1_g, ln1_b, w_q, w_kv, w_o, ln2_g, ln2_b, w_ff1, w_ff2, ln3_g, ln3_b, loss_target, m_w_in, m_conv_a_w, m_conv_a_b, m_ln_a_g, m_ln_a_b, m_ln_v_g, m_ln_v_b, m_w_s, m_b_s, m_conv_c_w, m_w_out, m_ln1_g, m_ln1_b, m_w_q, m_w_kv, m_w_o, m_ln2_g, m_ln2_b, m_w_ff1, m_w_ff2, m_ln3_g, m_ln3_b, v_w_in, v_conv_a_w, v_conv_a_b, v_ln_a_g, v_ln_a_b, v_ln_v_g, v_ln_v_b, v_w_s, v_b_s, v_conv_c_w, v_w_out, v_ln1_g, v_ln1_b, v_w_q, v_w_kv, v_w_o, v_ln2_g, v_ln2_b, v_w_ff1, v_w_ff2, v_ln3_g, v_ln3_b):
    given = dict(locals())
    weights = {n: given[n] for n in WEIGHT_ORDER}
    moment1 = {n: given["m_" + n] for n in WEIGHT_ORDER}
    moment2 = {n: given["v_" + n] for n in WEIGHT_ORDER}

    depth = w_in.shape[0]
    batch, seq, d = x.shape
    t = batch * seq
    hd = d // HEADS
    d_a, d_b, d_c = A_HEADS * hd, B_HEADS * hd, C_HEADS * hd
    widths = (d_a, d_a, d_b, d_b, d_c, d_c, d_c)
    in_offs = [sum(widths[:k]) for k in range(len(widths))]
    alpha = (2.0 * depth) ** 0.25
    layers = range(depth)
    tm = _tile(t, MATMUL_ROW_TILE)
    tm_wide = _tile(t, WIDE_ROW_TILE)

    conv_shards = [weights[n][l] for l in layers for n, _ in CONV_WEIGHTS]
    first_used = [n for n, _ in MATMUL_WEIGHTS[:1]]
    groups, tokens = [], []
    for l in layers:
        names = [n for n, _ in MATMUL_WEIGHTS]
        parts = [first_used, [n for n in names if n not in first_used]] if l == 0 else [names]
        started_groups = []
        for p, part in enumerate(parts):
            shards = [weights[n][l].astype(BF16) for n in part] + ([_pack(conv_shards, 0)] if (l, p) == (0, 0) else [])
            lands = [lax.empty((N_CHIPS,) + s.shape, s.dtype) for s in shards]
            send_sems, recv_sems, bufs, token = _copies_start(_gather_copies, GATHER_COPIES, shards, lands, f"gather_start_{l}_{p}")
            started_groups.append((part, send_sems, recv_sems, bufs, len(shards)))
            tokens.append(token[0, 0])
        groups.append(started_groups)
    axis_of = dict(MATMUL_WEIGHTS)

    def gathered(l, p, after):
        part, send_sems, recv_sems, bufs, ns = groups[l][p]
        _, lands = _copies_wait(_gather_copies, send_sems, recv_sems, bufs, ns, after, f"gather_wait_{l}_{p}")
        slots = _gather_finish(lands, "gather_finish")
        return {n: _from_slots(slots[k], axis_of[n]) for k, n in enumerate(part)}, slots[len(part):]

    xf = x.reshape(t, d)
    xb = (xf + sum(tokens)).astype(BF16)
    memb = mem.reshape(-1, d).astype(BF16)
    saved, full, conv_full = [], [], None
    for l in layers:
        w, extra = gathered(l, 0, xb)
        if l == 0:
            conv_full = _unpack(extra[0], [s.shape for s in conv_shards], 1)
        for k, (n, axis) in enumerate(CONV_WEIGHTS):
            w[n] = _from_slots(conv_full[l * len(CONV_WEIGHTS) + k], axis)
        for n in REPLICATED:
            w[n] = weights[n][l]
        w["w_s_t"] = jnp.swapaxes(w["w_s"], 1, 2)
        w["bias_full"] = jnp.repeat(w["b_s"].T, hd, axis=1)
        full.append(w)
        s = {"x0b": xb}
        proj = _in_proj(xb, w["w_in"], widths, "in_proj")
        s["proj"] = proj
        a_val, a_gate, b_u, b_v, c_b, c_c, c_x = proj
        a_out, s["ac"] = _mix_a_fwd(a_val, a_gate, w["conv_a_w"], _row(w["conv_a_b"]), _row(w["ln_a_g"]), _row(w["ln_a_b"]),
                                    batch, "mix_a_fwd")
        b_out = _mix_b_fwd(b_u, b_v, w["w_s"], w["bias_full"], _row(w["ln_v_g"]), _row(w["ln_v_b"]), "mix_b_fwd")
        c_out = _mix_c_fwd(c_b, c_c, c_x, w["conv_c_w"], batch, "mix_c_fwd")
        s["cat"] = (a_out, b_out, c_out)
        for p in range(1, len(groups[l])):
            w.update(gathered(l, p, c_out)[0])
        s["z1"], xf, xb = _proj_ln([a_out, b_out, c_out], w["w_out"], xf, _row(w["ln1_g"]), _row(w["ln1_b"]), alpha, tm, "out_proj_ln")
        s["x1b"] = xb
        s["q"] = _matmul(xb, w["w_q"], "q_proj")
        s["kv"] = _matmul(memb, w["w_kv"], "kv_proj")
        s["o"] = _attention_fwd(s["q"], s["kv"], batch, "attention_fwd")
        s["z2"], xf, xb = _proj_ln([s["o"]], w["w_o"], xf, _row(w["ln2_g"]), _row(w["ln2_b"]), alpha, tm, "o_proj_ln")
        s["x2b"] = xb
        s["h"], s["r"] = _ff1(xb, w["w_ff1"], "ff1")
        s["z3"], xf, xb = _proj_ln([s["r"]], w["w_ff2"], xf, _row(w["ln3_g"]), _row(w["ln3_b"]), alpha, tm_wide, "ff2_ln")
        saved.append(s)

    target = loss_target.reshape(t, d)
    dz3, dz3b, top_g, top_b, loss_block = _loss_head_ln(xf, target, saved[-1]["z3"], _row(full[-1]["ln3_g"]), "loss_head_ln3")
    loss = lax.psum(loss_block[0, 0], ("x", "y", "c"))
    grads = [{} for _ in layers]
    grads[-1]["ln3_g"], grads[-1]["ln3_b"] = top_g, top_b
    reductions = [None] * depth
    swapping, swap_started = None, None
    n_mm = len(MATMUL_WEIGHTS)
    token, last_started = 0.0, None

    def rows_to_slots(full_):
        return full_.reshape(N_CHIPS, full_.shape[1] // N_CHIPS, full_.shape[2])

    def swap_finish(after):
        nonlocal token, last_started
        k, send_sems, recv_sems, bufs, wire = swapping
        fulls, theirs = _copies_wait(_swap_copies, send_sems, recv_sems, bufs, len(wire), after, f"rs_swap_wait_{k}")
        sums = [_add_sibling(f, th, dt, "rs_add_sibling") for f, th, dt in zip(fulls, theirs, wire)]
        lands = [lax.empty((N_CHIPS - 1,) + p.shape[1:], p.dtype) for p in sums]
        send_sems, recv_sems, bufs, started = _copies_start(_owner_copies, COPIES_PER_ARRAY, sums, lands, f"rs_send_start_{k}")
        reductions[k] = (send_sems, recv_sems, bufs, len(sums))
        token, last_started = started[0, 0], started

    for l in reversed(layers):
        w, s, g = full[l], saved[l], grads[l]
        g["w_ff2"] = rows_to_slots(_mm_tn(s["r"], dz3b, "d_w_ff2", after=swap_started))
        dh = _bwd_ff2(dz3b, s["h"], w["w_ff2"], "d_ff_hidden", after=swap_started)
        if swapping is not None:
            swap_finish(dh)
        g["w_ff1"] = _mm_tn(s["x2b"], dh, "d_w_ff1", slots=N_CHIPS)
        dz2, dz2b, g["ln2_g"], g["ln2_b"] = _bwd_data_ln([dh], w["w_ff1"], [0], dz3, alpha, s["z2"], _row(w["ln2_g"]) + token,
                                                         tm_wide, "d_x2_ln2")
        g["w_o"] = rows_to_slots(_mm_tn(s["o"], dz2b, "d_w_o"))
        do = _bwd_data([dz2b], w["w_o"], [0], None, alpha, BF16, tm, "d_att_out")
        dq, dk, dv = _attention_bwd(s["q"], s["kv"], do, batch, "attention_bwd")
        g["w_kv"] = _mm_tn(memb, jnp.concatenate([dk, dv], axis=1), "d_w_kv", slots=N_CHIPS)
        g["w_q"] = rows_to_slots(_mm_tn(s["x1b"], dq, "d_w_q"))
        dz1, dz1b, g["ln1_g"], g["ln1_b"] = _bwd_data_ln([dq], w["w_q"], [0], dz2, alpha, s["z1"], _row(w["ln1_g"]), tm, "d_x1_ln1")
        g["w_out"] = _mm_tn_rows(s["cat"], dz1b, "d_w_out").reshape(N_CHIPS, d // N_CHIPS, d)
        da_, dbo, dco = _bwd_out_proj(dz1b, w["w_out"], (d_a, d_b, d_c), "d_mixer_out")
        a_val, a_gate, b_u, b_v, c_b, c_c, c_x = s["proj"]
        dav, dag, g["conv_a_w"], g["conv_a_b"], g["ln_a_g"], g["ln_a_b"] = _mix_a_bwd(
            a_val, a_gate, s["ac"], da_, w["conv_a_w"], _row(w["ln_a_g"]), _row(w["ln_a_b"]), batch, "mix_a_bwd")
        dbu, dbv, g["w_s"], dbs, g["ln_v_g"], g["ln_v_b"] = _mix_b_bwd(
            b_u, b_v, dbo, w["w_s"], w["w_s_t"], w["bias_full"], _row(w["ln_v_g"]), _row(w["ln_v_b"]), "mix_b_bwd")
        g["b_s"] = dbs.reshape(B_HEADS, CHUNK)
        dcb, dcc, dcx, g["conv_c_w"] = _mix_c_bwd(c_b, c_c, c_x, dco, w["conv_c_w"], batch, "mix_c_bwd")
        dproj = [dav, dag, dbu, dbv, dcb, dcc, dcx]
        g["w_in"] = _to_slots(_mm_tn_cols(s["x0b"], dproj, "d_w_in"), 1)
        if l > 0:
            below = full[l - 1]
            dz3, dz3b, grads[l - 1]["ln3_g"], grads[l - 1]["ln3_b"] = _bwd_data_ln(
                dproj, w["w_in"], in_offs, dz1, alpha, saved[l - 1]["z3"], _row(below["ln3_g"]), tm, "d_x0_ln3")
        else:
            dx = _bwd_data(dproj, w["w_in"], in_offs, dz1, alpha, F32, tm, "d_x0")

        slot_grads = [g[n] for n, _ in MATMUL_WEIGHTS]
        wire = [BF16] * n_mm
        if l == 0:
            conv_slots = [_to_slots(grads[k][n], axis) for k in layers for n, axis in CONV_WEIGHTS]
            rep_parts = [jnp.stack([grads[k][n].reshape(weights[n].shape[1:]) for k in layers]) for n in REPLICATED]
            rep_flat = jnp.concatenate([p.reshape(-1) for p in rep_parts])
            n_rep = rep_flat.shape[0]
            per_chip = _round_up(-(-n_rep // N_CHIPS), PACK_W * PACK_ROWS)
            rep_slots = jnp.pad(rep_flat, (0, N_CHIPS * per_chip - n_rep)).reshape(N_CHIPS, per_chip)
            slot_grads.append(_pack(conv_slots + [rep_slots], 1))
            wire.append(F32)
        lands = [lax.empty((f.shape[0], f.shape[1] // 2, f.shape[2]), f.dtype) for f in slot_grads]
        send_sems, recv_sems, bufs, started = _copies_start(_swap_copies, 1, slot_grads, lands, f"rs_swap_start_{l}")
        swapping, swap_started = (l, send_sems, recv_sems, bufs, wire), started
    swap_finish(dx)
    grad_x = dx.reshape(batch, seq, d)

    reduced, after = [None] * depth, last_started
    for l in reversed(layers):
        send_sems, recv_sems, bufs, ns = reductions[l]
        sums, lands = _copies_wait(_owner_copies, send_sems, recv_sems, bufs, ns, after, f"rs_send_wait_{l}")
        mine = [_add_owners(p, r, "rs_add_owners") for p, r in zip(sums, lands)]
        reduced[l] = _join_halves(mine, "rs_join")
        after = reduced[l][0]
    small = _unpack(reduced[0][n_mm], [c.shape[1:] for c in conv_slots] + [(per_chip,)], 0)
    conv_grad, rep_mine = small[:-1], small[-1]
    rep_all = _all_gather_chips(rep_mine.reshape(-1, PACK_W), "gather_small_grads").reshape(-1)[:n_rep]

    grad = {}
    for k, (n, _) in enumerate(MATMUL_WEIGHTS):
        grad[n] = jnp.stack([reduced[l][k] for l in layers])
    for k, (n, _) in enumerate(CONV_WEIGHTS):
        grad[n] = jnp.stack([conv_grad[l * len(CONV_WEIGHTS) + k] for l in layers])
    off = 0
    for n in REPLICATED:
        size = math.prod(weights[n].shape)
        grad[n] = rep_all[off:off + size].reshape(weights[n].shape)
        off += size

    delta, new_m, new_v = {}, {}, {}
    for n, _ in MATMUL_WEIGHTS:
        shape = weights[n].shape
        as_rows = lambda a: a.reshape(-1, shape[-1])
        delta[n], new_m[n], new_v[n] = (
            r.reshape(shape) for r in _adamw(as_rows(weights[n]), as_rows(grad[n]), as_rows(moment1[n]), as_rows(moment2[n]), "adamw"))
    small_names = [n for n, _ in CONV_WEIGHTS] + list(REPLICATED)
    small_shapes = [weights[n].shape for n in small_names]
    packed = [_pack([src[n] for n in small_names], 0) for src in (weights, grad, moment1, moment2)]
    for dst, res in zip((delta, new_m, new_v), _adamw(*packed, "adamw_small")):
        for n, a in zip(small_names, _unpack(res, small_shapes, 0)):
            dst[n] = a

    return (loss, grad_x, *[grad[n] for n in WEIGHT_ORDER], *[delta[n] for n in WEIGHT_ORDER],
            *[new_m[n] for n in WEIGHT_ORDER], *[new_v[n] for n in WEIGHT_ORDER])
```

```python
import functools
import math

import jax
import jax.numpy as jnp
from jax import lax
from jax.experimental import pallas as pl
from jax.experimental.pallas import tpu as pltpu

F32 = jnp.float32
BF16 = jnp.bfloat16
SDS = jax.ShapeDtypeStruct

HEADS = 16
A_HEADS, B_HEADS, C_HEADS = 6, 4, 6
X_HEADS = 4
CHUNK = 128
LN_EPS = 1e-5
ADAM_LR, ADAM_B1, ADAM_B2, ADAM_EPS, ADAM_WD, ADAM_STEP = 0.001, 0.9, 0.999, 1e-08, 0.01, 10

N_CHIPS = 4
V7X_VMEM_LIMIT = 56 << 20
SUBLANES = 8
PACK_W = 1024
PACK_ROWS = 32
ROW_TILE = 512
MATMUL_ROW_TILE = 1024
WIDE_ROW_TILE = 512
CONV_ROWS = 256
SUB_ROWS = 64
ATT_ROWS = 512
TN_TILE = 1024
MESH = pl.DeviceIdType.MESH
ANY = pl.BlockSpec(memory_space=pl.ANY)


def _tile(n, t):
    for d in range(min(n, t), 0, -1):
        if n % d == 0 and d % (2 * SUBLANES) == 0:
            return d
    return n


def _round_up(n, m):
    return -(-n // m) * m


def _params(*sem):
    return pltpu.CompilerParams(dimension_semantics=sem or None, vmem_limit_bytes=V7X_VMEM_LIMIT)


def _dot(a, b):
    return jnp.dot(a, b, preferred_element_type=F32)


def _dot_nt(a, b):
    return lax.dot_general(a, b, (((1,), (1,)), ((), ())), preferred_element_type=F32)


def _dot_tn(a, b):
    return lax.dot_general(a, b, (((0,), (0,)), ((), ())), preferred_element_type=F32)


def _sigmoid(x):
    return 1.0 / (1.0 + jnp.exp(-x))


def _gelu(x):
    return 0.5 * x * (1.0 + lax.erf(x * (2.0 ** -0.5)))


def _gelu_grad(x):
    return 0.5 * (1.0 + lax.erf(x * (2.0 ** -0.5))) + x * jnp.exp(-0.5 * x * x) * ((2.0 * math.pi) ** -0.5)


def _ln_stats(z):
    mu = jnp.mean(z, axis=-1, keepdims=True)
    zc = z - mu
    rstd = lax.rsqrt(jnp.mean(zc * zc, axis=-1, keepdims=True) + LN_EPS)
    return zc * rstd, rstd


def _ln_bwd(dy, y, rstd, g):
    dyh = dy * g
    return rstd * (dyh - jnp.mean(dyh, axis=-1, keepdims=True) - y * jnp.mean(dyh * y, axis=-1, keepdims=True))


def _colsum(x):
    return jnp.sum(x, axis=0, keepdims=True)


def _rowwise(body, rows, consts, outs, accs=(), *, tm, name, after=None):
    t = rows[0].shape[0]
    steps = t // tm
    n_in = len(rows) + len(consts)
    order = [] if after is None else [after]

    def kern(*refs):
        body(pl.program_id(0), steps, *refs[:n_in], *refs[n_in + len(order):])

    def whole(a, **kw):
        return pl.BlockSpec(a.shape, lambda i, nd=len(a.shape): (0,) * nd, **kw)

    in_specs = ([pl.BlockSpec((tm, r.shape[1]), lambda i: (i, 0)) for r in rows]
                + [whole(c, pipeline_mode=pl.Buffered(1)) for c in consts] + [ANY] * len(order))
    out_shape = [SDS((t, n), dt) for n, dt in outs] + [SDS(s, dt) for s, dt in accs]
    out_specs = [pl.BlockSpec((tm, n), lambda i: (i, 0)) for n, _ in outs] + [whole(SDS(s, dt)) for s, dt in accs]
    return pl.pallas_call(kern, grid=(steps,), in_specs=in_specs, out_specs=out_specs, out_shape=out_shape,
                          compiler_params=_params("arbitrary"), name=name)(*rows, *consts, *order)


def _in_proj(xb, w_in, widths, name, after=None):
    offs = [sum(widths[:k]) for k in range(len(widths))]

    def body(i, steps, x_ref, w_ref, *o_refs):
        x = x_ref[...]
        for o_ref, off, n in zip(o_refs, offs, widths):
            o_ref[...] = _dot(x, w_ref[:, off:off + n])

    return _rowwise(body, [xb], [w_in], [(n, F32) for n in widths], tm=_tile(xb.shape[0], MATMUL_ROW_TILE), name=name, after=after)


def _matmul(ab, w, name):
    def body(i, steps, a_ref, w_ref, o_ref):
        o_ref[...] = _dot(a_ref[...], w_ref[...]).astype(BF16)

    return _rowwise(body, [ab], [w], [(w.shape[1], BF16)], tm=_tile(ab.shape[0], MATMUL_ROW_TILE), name=name)[0]


def _ff1(xb, w, name):
    def body(i, steps, a_ref, w_ref, h_ref, r_ref):
        h = _dot(a_ref[...], w_ref[...])
        h_ref[...] = h.astype(BF16)
        r = jnp.maximum(h, 0.0)
        r_ref[...] = (r * r).astype(BF16)

    n = w.shape[1]
    return _rowwise(body, [xb], [w], [(n, BF16), (n, BF16)], tm=_tile(xb.shape[0], WIDE_ROW_TILE), name=name)


def _proj_ln(a_list, w, stream, g, b, alpha, tm, name):
    widths = [a.shape[1] for a in a_list]
    offs = [sum(widths[:k]) for k in range(len(widths))]
    na = len(a_list)
    from_ln = len(stream) == 3

    def body(i, steps, *refs):
        a_refs, rest = refs[:na], refs[na:]
        if from_ln:
            s_ref, w_ref, sg_ref, sb_ref, g_ref, b_ref, z_ref, xb_ref = rest
            x = _ln_stats(s_ref[...])[0] * sg_ref[...] + sb_ref[...]
        else:
            s_ref, w_ref, g_ref, b_ref, z_ref, xb_ref = rest
            x = s_ref[...]
        acc = alpha * x
        for a_ref, off, n in zip(a_refs, offs, widths):
            acc = acc + _dot(a_ref[...], w_ref[off:off + n, :])
        z_ref[...] = acc
        y, _ = _ln_stats(acc)
        xb_ref[...] = (y * g_ref[...] + b_ref[...]).astype(BF16)

    d = w.shape[1]
    return _rowwise(body, [*a_list, stream[0]], [w, *stream[1:], g, b], [(d, F32), (d, BF16)], tm=tm, name=name)


def _bwd_ff2(dzb, h, w_ff2, name, after=None):
    def body(i, steps, dz_ref, h_ref, w_ref, dh_ref):
        dr = _dot_nt(dz_ref[...], w_ref[...])
        dh_ref[...] = (dr * (2.0 * jnp.maximum(h_ref[...].astype(F32), 0.0))).astype(BF16)

    return _rowwise(body, [dzb, h], [w_ff2], [(h.shape[1], BF16)], tm=_tile(h.shape[0], WIDE_ROW_TILE), name=name, after=after)[0]


def _bwd_nt(g_list, w, col_offs, res, alpha, tm, name):
    ng = len(g_list)
    widths = [g.shape[1] for g in g_list]

    def body(i, steps, *refs):
        g_refs = refs[:ng]
        if res is None:
            w_ref, o_ref = refs[ng:]
            acc = None
        else:
            r_ref, w_ref, o_ref = refs[ng:]
            acc = alpha * r_ref[...]
        for g_ref, off, n in zip(g_refs, col_offs, widths):
            part = _dot_nt(g_ref[...], w_ref[:, off:off + n])
            acc = part if acc is None else acc + part
        o_ref[...] = acc.astype(o_ref.dtype)

    rows = list(g_list) + ([] if res is None else [res])
    return rows, w, body, tm, name


def _bwd_data(g_list, w, col_offs, res, alpha, out_dtype, tm, name):
    rows, w, body, tm, name = _bwd_nt(g_list, w, col_offs, res, alpha, tm, name)
    return _rowwise(body, rows, [w], [(w.shape[0], out_dtype)], tm=tm, name=name)[0]


def _bwd_data_ln(g_list, w, col_offs, res, alpha, z, gain, tm, name):
    ng = len(g_list)
    widths = [g.shape[1] for g in g_list]

    def body(i, steps, *refs):
        g_refs = refs[:ng]
        r_ref, z_ref, w_ref, gain_ref, dz_ref, dzb_ref, dg_ref, db_ref = refs[ng:]

        @pl.when(i == 0)
        def _():
            dg_ref[...] = jnp.zeros_like(dg_ref)
            db_ref[...] = jnp.zeros_like(db_ref)

        dy = alpha * r_ref[...]
        for g_ref, off, n in zip(g_refs, col_offs, widths):
            dy = dy + _dot_nt(g_ref[...], w_ref[:, off:off + n])
        y, rstd = _ln_stats(z_ref[...])
        dz = _ln_bwd(dy, y, rstd, gain_ref[...])
        dz_ref[...] = dz
        dzb_ref[...] = dz.astype(BF16)
        dg_ref[...] += _colsum(dy * y)
        db_ref[...] += _colsum(dy)

    d = z.shape[1]
    return _rowwise(body, [*g_list, res, z], [w, gain], [(d, F32), (d, BF16)], [((1, d), F32), ((1, d), F32)], tm=tm, name=name)


def _loss_head_ln(z, target, gain, bias, name):
    d = z.shape[1]

    def body(i, steps, z_ref, t_ref, gain_ref, bias_ref, dz_ref, dzb_ref, dg_ref, db_ref, l_ref):
        @pl.when(i == 0)
        def _():
            for r in (dg_ref, db_ref, l_ref):
                r[...] = jnp.zeros_like(r)

        yn, rstd = _ln_stats(z_ref[...])
        err = yn * gain_ref[...] + bias_ref[...] - t_ref[...]
        l_ref[...] += jnp.sum(err * err) * (0.5 / d)
        dy = err * (1.0 / d)
        dz = _ln_bwd(dy, yn, rstd, gain_ref[...])
        dz_ref[...] = dz
        dzb_ref[...] = dz.astype(BF16)
        dg_ref[...] += _colsum(dy * yn)
        db_ref[...] += _colsum(dy)

    return _rowwise(body, [z, target], [gain, bias], [(d, F32), (d, BF16)],
                    [((1, d), F32), ((1, d), F32), ((SUBLANES, 128), F32)], tm=_tile(z.shape[0], ROW_TILE), name=name)


def _bwd_out_proj(dzb, w_out, widths, name, after=None):
    offs = [sum(widths[:k]) for k in range(len(widths))]

    def body(i, steps, dz_ref, w_ref, *o_refs):
        dz = dz_ref[...]
        for o_ref, off, n in zip(o_refs, offs, widths):
            o_ref[...] = _dot_nt(dz, w_ref[off:off + n, :])

    return _rowwise(body, [dzb], [w_out], [(n, F32) for n in widths], tm=_tile(dzb.shape[0], MATMUL_ROW_TILE), name=name,
                    after=after)


def _adamw(w, g, m, v, name):
    def body(i, steps, w_ref, g_ref, m_ref, v_ref, d_ref, nm_ref, nv_ref):
        g_ = g_ref[...]
        nm = ADAM_B1 * m_ref[...] + (1.0 - ADAM_B1) * g_
        nv = ADAM_B2 * v_ref[...] + (1.0 - ADAM_B2) * (g_ * g_)
        m_hat = nm / (1.0 - ADAM_B1 ** ADAM_STEP)
        v_hat = nv / (1.0 - ADAM_B2 ** ADAM_STEP)
        d_ref[...] = -ADAM_LR * (m_hat / (jnp.sqrt(v_hat) + ADAM_EPS) + ADAM_WD * w_ref[...])
        nm_ref[...] = nm
        nv_ref[...] = nv

    c = w.shape[1]
    return _rowwise(body, [w, g, m, v], [], [(c, F32)] * 3, tm=_tile(w.shape[0], ROW_TILE), name=name)


def _mm_tn(a, g, name, slots=1, after=None):
    t, ka = a.shape
    n = g.shape[1]
    ta, tn, tk = _tile(ka, TN_TILE), _tile(n // slots, TN_TILE), _tile(t, TN_TILE)
    per = n // slots // tn

    order = [] if after is None else [after]

    def kern(a_ref, g_ref, *refs):
        o_ref = refs[-1]

        @pl.when(pl.program_id(2) == 0)
        def _():
            o_ref[...] = jnp.zeros_like(o_ref)

        o_ref[0] += _dot_tn(a_ref[...], g_ref[...])

    return pl.pallas_call(
        kern, grid=(ka // ta, n // tn, t // tk),
        in_specs=[pl.BlockSpec((tk, ta), lambda i, j, k: (k, i)), pl.BlockSpec((tk, tn), lambda i, j, k: (k, j))] + [ANY] * len(order),
        out_specs=pl.BlockSpec((1, ta, tn), lambda i, j, k: (j // per, i, j % per)), out_shape=SDS((slots, ka, n // slots), F32),
        compiler_params=_params("arbitrary", "arbitrary", "arbitrary"), name=name)(a, g, *order)


def _mm_tn_cols(a, g_list, name):
    t, ka = a.shape
    widths = [g.shape[1] for g in g_list]
    offs = [sum(widths[:k]) for k in range(len(widths))]
    ta, tk = _tile(ka, ROW_TILE), _tile(t, TN_TILE)

    def kern(a_ref, *refs):
        g_refs, o_ref = refs[:-1], refs[-1]

        @pl.when(pl.program_id(1) == 0)
        def _():
            o_ref[...] = jnp.zeros_like(o_ref)

        a_ = a_ref[...]
        for g_ref, off, n in zip(g_refs, offs, widths):
            o_ref[:, off:off + n] += _dot_tn(a_, g_ref[...])

    return pl.pallas_call(
        kern, grid=(ka // ta, t // tk),
        in_specs=[pl.BlockSpec((tk, ta), lambda i, k: (k, i))] + [pl.BlockSpec((tk, n), lambda i, k: (k, 0)) for n in widths],
        out_specs=pl.BlockSpec((ta, sum(widths)), lambda i, k: (i, 0)), out_shape=SDS((ka, sum(widths)), F32),
        compiler_params=_params("arbitrary", "arbitrary"), name=name)(a, *g_list)


def _mm_tn_rows(a_list, g, name):
    t, n = g.shape
    widths = [a.shape[1] for a in a_list]
    offs = [sum(widths[:k]) for k in range(len(widths))]
    tn, tk = _tile(n, TN_TILE), _tile(t, TN_TILE)

    def kern(*refs):
        a_refs, g_ref, o_ref = refs[:-2], refs[-2], refs[-1]

        @pl.when(pl.program_id(1) == 0)
        def _():
            o_ref[...] = jnp.zeros_like(o_ref)

        g_ = g_ref[...]
        for a_ref, off, ka in zip(a_refs, offs, widths):
            o_ref[off:off + ka, :] += _dot_tn(a_ref[...], g_)

    return pl.pallas_call(
        kern, grid=(n // tn, t // tk),
        in_specs=[pl.BlockSpec((tk, ka), lambda j, k: (k, 0)) for ka in widths] + [pl.BlockSpec((tk, tn), lambda j, k: (k, j))],
        out_specs=pl.BlockSpec((sum(widths), tn), lambda j, k: (0, j)), out_shape=SDS((sum(widths), n), F32),
        compiler_params=_params("arbitrary", "arbitrary"), name=name)(*a_list, g)


def _conv_geometry(seq, taps):
    pad = _round_up(taps - 1, SUBLANES)
    rc = _tile(seq, CONV_ROWS)
    assert rc % pad == 0 and seq % rc == 0
    return pad, rc, seq // rc


def _chunk_spec(rc, n, nch):
    return pl.BlockSpec((rc, n), lambda b, i: (b * nch + i, 0))


def _prev_halo_spec(pad, rc, n, nch):
    per = rc // pad
    return pl.BlockSpec((pad, n), lambda b, i: (jnp.maximum((b * nch + i) * per - 1, 0), 0))


def _next_halo_spec(pad, rc, n, nch, total_rows):
    per = rc // pad
    last = total_rows // pad - 1
    return pl.BlockSpec((pad, n), lambda b, i: (jnp.minimum((b * nch + i + 1) * per, last), 0))


def _whole2(a):
    return pl.BlockSpec(a.shape, lambda b, i, nd=len(a.shape): (0,) * nd)


def _sub_rows(rc):
    return SUB_ROWS if rc % SUB_ROWS == 0 else rc


def _build_shifts(sh_ref, src_ref, offsets):
    rows = src_ref.shape[0]
    for r in sorted({o % SUBLANES for o in offsets} - {0}):
        sh_ref[r, 0:rows - SUBLANES, :] = src_ref[r:r + rows - SUBLANES, :]


def _read_shifted(sh_ref, src_ref, o, s0, sub):
    r = o % SUBLANES
    a = o - r + s0
    return src_ref[a:a + sub, :] if r == 0 else sh_ref[r, a:a + sub, :]


def _tap_sum(sh_ref, src_ref, w_ref, offsets, s0, sub):
    acc = None
    for k, o in enumerate(offsets):
        term = _read_shifted(sh_ref, src_ref, o, s0, sub) * w_ref[k:k + 1, :]
        acc = term if acc is None else acc + term
    return acc


def _row_groups(x):
    acc = x[0:SUBLANES, :]
    for g0 in range(SUBLANES, x.shape[0], SUBLANES):
        acc = acc + x[g0:g0 + SUBLANES, :]
    return acc


def _mix_a_fwd(a_val, a_gate, conv_w, conv_b, ln_g, ln_b, batch, name):
    t, da = a_val.shape
    taps = conv_w.shape[0]
    pad, rc, nch = _conv_geometry(t // batch, taps)
    sub = _sub_rows(rc)
    offs = [pad - (taps - 1) + k for k in range(taps)]

    def kern(av_ref, ag_ref, pav_ref, pag_ref, w_ref, cb_ref, g_ref, b_ref, a_ref, ac_ref, pad_ref, sh_ref):
        first = pl.program_id(1) == 0
        pad_ref[0:pad, :] = jnp.where(first, 0.0, pav_ref[...] * _sigmoid(pag_ref[...]))
        for s0 in range(0, rc, sub):
            pad_ref[pad + s0:pad + s0 + sub, :] = av_ref[s0:s0 + sub, :] * _sigmoid(ag_ref[s0:s0 + sub, :])
        _build_shifts(sh_ref, pad_ref, offs)
        for s0 in range(0, rc, sub):
            ac = _tap_sum(sh_ref, pad_ref, w_ref, offs, s0, sub) + cb_ref[...]
            ac_ref[s0:s0 + sub, :] = ac
            y, _ = _ln_stats(ac)
            aln = y * g_ref[...] + b_ref[...]
            a_ref[s0:s0 + sub, :] = (aln * _sigmoid(aln)).astype(BF16)

    chunk, halo = _chunk_spec(rc, da, nch), _prev_halo_spec(pad, rc, da, nch)
    return pl.pallas_call(
        kern, grid=(batch, nch),
        in_specs=[chunk, chunk, halo, halo, _whole2(conv_w), _whole2(conv_b), _whole2(ln_g), _whole2(ln_b)],
        out_specs=[chunk, chunk], out_shape=[SDS((t, da), BF16), SDS((t, da), F32)],
        scratch_shapes=[pltpu.VMEM((pad + rc, da), F32), pltpu.VMEM((SUBLANES, pad + rc, da), F32)],
        compiler_params=_params("arbitrary", "arbitrary"), name=name)(a_val, a_gate, a_val, a_gate, conv_w, conv_b, ln_g, ln_b)


def _mix_a_bwd(a_val, a_gate, ac, da_, conv_w, ln_g, ln_b, batch, name):
    t, da = a_val.shape
    taps = conv_w.shape[0]
    pad, rc, nch = _conv_geometry(t // batch, taps)
    sub = _sub_rows(rc)
    offs_in = [pad - (taps - 1) + k for k in range(taps)]
    offs_out = [taps - 1 - k for k in range(taps)]

    def kern(av_ref, ag_ref, pav_ref, pag_ref, ac_ref, da_ref, nac_ref, nda_ref, w_ref, g_ref, b_ref,
             dav_ref, dag_ref, dw_ref, dcb_ref, dg_ref, db_ref, gpad_ref, dpad_ref, shg_ref, shd_ref, dwacc_ref):
        i = pl.program_id(1)
        start = (pl.program_id(0) == 0) & (i == 0)
        end = (pl.program_id(0) == batch - 1) & (i == nch - 1)

        @pl.when(start)
        def _():
            for r in (dcb_ref, dg_ref, db_ref, dwacc_ref):
                r[...] = jnp.zeros_like(r)

        gain, bias = g_ref[...], b_ref[...]

        def d_conv_out(ac_, dout):
            y, rstd = _ln_stats(ac_)
            aln = y * gain + bias
            sig = _sigmoid(aln)
            daln = dout * (sig * (1.0 + aln * (1.0 - sig)))
            return _ln_bwd(daln, y, rstd, gain), daln, y

        gpad_ref[0:pad, :] = jnp.where(i == 0, 0.0, pav_ref[...] * _sigmoid(pag_ref[...]))
        dac_next, _, _ = d_conv_out(nac_ref[...], nda_ref[...])
        dpad_ref[rc:rc + pad, :] = jnp.where(i == nch - 1, 0.0, dac_next)
        for s0 in range(0, rc, sub):
            rows = slice(s0, s0 + sub)
            dac, daln, y = d_conv_out(ac_ref[rows, :], da_ref[rows, :])
            dg_ref[...] += _colsum(daln * y)
            db_ref[...] += _colsum(daln)
            dcb_ref[...] += _colsum(dac)
            dpad_ref[rows, :] = dac
            gpad_ref[pad + s0:pad + s0 + sub, :] = av_ref[rows, :] * _sigmoid(ag_ref[rows, :])
        _build_shifts(shg_ref, gpad_ref, offs_in)
        _build_shifts(shd_ref, dpad_ref, offs_out)
        for s0 in range(0, rc, sub):
            rows = slice(s0, s0 + sub)
            dac = dpad_ref[rows, :]
            for k, o in enumerate(offs_in):
                dwacc_ref[k] += _row_groups(dac * _read_shifted(shg_ref, gpad_ref, o, s0, sub))
            dgl = _tap_sum(shd_ref, dpad_ref, w_ref, offs_out, s0, sub)
            av = av_ref[rows, :]
            sig = _sigmoid(ag_ref[rows, :])
            dav_ref[rows, :] = (dgl * sig).astype(BF16)
            dag_ref[rows, :] = (dgl * av * sig * (1.0 - sig)).astype(BF16)

        @pl.when(end)
        def _():
            for k in range(taps):
                dw_ref[k:k + 1, :] = _colsum(dwacc_ref[k])

    chunk, prev = _chunk_spec(rc, da, nch), _prev_halo_spec(pad, rc, da, nch)
    nxt = _next_halo_spec(pad, rc, da, nch, t)
    vec = SDS((1, da), F32)
    return pl.pallas_call(
        kern, grid=(batch, nch),
        in_specs=[chunk, chunk, prev, prev, chunk, chunk, nxt, nxt, _whole2(conv_w), _whole2(ln_g), _whole2(ln_b)],
        out_specs=[chunk, chunk, _whole2(conv_w), _whole2(vec), _whole2(vec), _whole2(vec)],
        out_shape=[SDS((t, da), BF16), SDS((t, da), BF16), SDS(conv_w.shape, F32), vec, vec, vec],
        scratch_shapes=[pltpu.VMEM((pad + rc, da), F32), pltpu.VMEM((rc + pad, da), F32),
                        pltpu.VMEM((SUBLANES, pad + rc, da), F32), pltpu.VMEM((SUBLANES, rc + pad, da), F32),
                        pltpu.VMEM((taps, SUBLANES, da), F32)],
        compiler_params=_params("arbitrary", "arbitrary"), name=name)(
            a_val, a_gate, a_val, a_gate, ac, da_, ac, da_, conv_w, ln_g, ln_b)


def _mix_c_fwd(c_b, c_c, c_x, conv_w, batch, name):
    t, dc = c_b.shape
    taps = conv_w.shape[0]
    pad, rc, nch = _conv_geometry(t // batch, taps)
    sub = _sub_rows(rc)
    offs = [pad - (taps - 1) + k for k in range(taps)]

    def kern(cb_ref, cc_ref, cx_ref, pcc_ref, pcx_ref, w_ref, o_ref, pad_ref, sh_ref):
        pad_ref[0:pad, :] = jnp.where(pl.program_id(1) == 0, 0.0, pcc_ref[...] * pcx_ref[...])
        for s0 in range(0, rc, sub):
            pad_ref[pad + s0:pad + s0 + sub, :] = cc_ref[s0:s0 + sub, :] * cx_ref[s0:s0 + sub, :]
        _build_shifts(sh_ref, pad_ref, offs)
        for s0 in range(0, rc, sub):
            o_ref[s0:s0 + sub, :] = (cb_ref[s0:s0 + sub, :] * _tap_sum(sh_ref, pad_ref, w_ref, offs, s0, sub)).astype(BF16)

    chunk, prev = _chunk_spec(rc, dc, nch), _prev_halo_spec(pad, rc, dc, nch)
    return pl.pallas_call(
        kern, grid=(batch, nch), in_specs=[chunk, chunk, chunk, prev, prev, _whole2(conv_w)],
        out_specs=chunk, out_shape=SDS((t, dc), BF16),
        scratch_shapes=[pltpu.VMEM((pad + rc, dc), F32), pltpu.VMEM((SUBLANES, pad + rc, dc), F32)],
        compiler_params=_params("arbitrary", "arbitrary"), name=name)(c_b, c_c, c_x, c_c, c_x, conv_w)


def _mix_c_bwd(c_b, c_c, c_x, dco, conv_w, batch, name):
    t, dc = c_b.shape
    taps = conv_w.shape[0]
    pad, rc, nch = _conv_geometry(t // batch, taps)
    sub = _sub_rows(rc)
    offs_in = [pad - (taps - 1) + k for k in range(taps)]
    offs_out = [taps - 1 - k for k in range(taps)]

    def kern(cb_ref, cc_ref, cx_ref, do_ref, pcc_ref, pcx_ref, ncb_ref, ndo_ref, w_ref,
             dcb_ref, dcc_ref, dcx_ref, dw_ref, ppad_ref, dpad_ref, shp_ref, shd_ref, dwacc_ref):
        i = pl.program_id(1)

        @pl.when((pl.program_id(0) == 0) & (i == 0))
        def _():
            dwacc_ref[...] = jnp.zeros_like(dwacc_ref)

        ppad_ref[0:pad, :] = jnp.where(i == 0, 0.0, pcc_ref[...] * pcx_ref[...])
        dpad_ref[rc:rc + pad, :] = jnp.where(i == nch - 1, 0.0, ndo_ref[...] * ncb_ref[...])
        for s0 in range(0, rc, sub):
            rows = slice(s0, s0 + sub)
            ppad_ref[pad + s0:pad + s0 + sub, :] = cc_ref[rows, :] * cx_ref[rows, :]
            dpad_ref[rows, :] = do_ref[rows, :] * cb_ref[rows, :]
        _build_shifts(shp_ref, ppad_ref, offs_in)
        _build_shifts(shd_ref, dpad_ref, offs_out)
        for s0 in range(0, rc, sub):
            rows = slice(s0, s0 + sub)
            dcv = dpad_ref[rows, :]
            cv = None
            for k, o in enumerate(offs_in):
                shifted = _read_shifted(shp_ref, ppad_ref, o, s0, sub)
                dwacc_ref[k] += _row_groups(dcv * shifted)
                cv = shifted * w_ref[k:k + 1, :] if cv is None else cv + shifted * w_ref[k:k + 1, :]
            dp = _tap_sum(shd_ref, dpad_ref, w_ref, offs_out, s0, sub)
            dcb_ref[rows, :] = (do_ref[rows, :] * cv).astype(BF16)
            dcc_ref[rows, :] = (dp * cx_ref[rows, :]).astype(BF16)
            dcx_ref[rows, :] = (dp * cc_ref[rows, :]).astype(BF16)

        @pl.when((pl.program_id(0) == batch - 1) & (i == nch - 1))
        def _():
            for k in range(taps):
                dw_ref[k:k + 1, :] = _colsum(dwacc_ref[k])

    chunk, prev = _chunk_spec(rc, dc, nch), _prev_halo_spec(pad, rc, dc, nch)
    nxt = _next_halo_spec(pad, rc, dc, nch, t)
    return pl.pallas_call(
        kern, grid=(batch, nch), in_specs=[chunk, chunk, chunk, chunk, prev, prev, nxt, nxt, _whole2(conv_w)],
        out_specs=[chunk, chunk, chunk, _whole2(conv_w)],
        out_shape=[SDS((t, dc), BF16)] * 3 + [SDS(conv_w.shape, F32)],
        scratch_shapes=[pltpu.VMEM((pad + rc, dc), F32), pltpu.VMEM((rc + pad, dc), F32),
                        pltpu.VMEM((SUBLANES, pad + rc, dc), F32), pltpu.VMEM((SUBLANES, rc + pad, dc), F32),
                        pltpu.VMEM((taps, SUBLANES, dc), F32)],
        compiler_params=_params("arbitrary", "arbitrary"), name=name)(c_b, c_c, c_x, dco, c_c, c_x, c_b, dco, conv_w)


def _head_of_lane(db):
    return lax.broadcasted_iota(jnp.int32, (1, db), 1) // (db // B_HEADS)


def _tril(rows_ge_cols=True):
    r = lax.broadcasted_iota(jnp.int32, (CHUNK, CHUNK), 0)
    c = lax.broadcasted_iota(jnp.int32, (CHUNK, CHUNK), 1)
    return (r >= c) if rows_ge_cols else (r <= c)


def _spatial_mix(wm, vb, bias, head):
    mixed = bias
    for h in range(B_HEADS):
        mixed = mixed + jnp.where(head == h, _dot(wm[h], vb), 0.0)
    return mixed


def _mix_b_fwd(b_u, b_v, w_s, bias_full, ln_g, ln_b, name):
    t, db = b_u.shape
    rb = _tile(t, ROW_TILE)

    def kern(bu_ref, bv_ref, ws_ref, bias_ref, g_ref, b_ref, o_ref):
        head = _head_of_lane(db)
        wm = [jnp.where(_tril(), ws_ref[h], 0.0).astype(BF16) for h in range(B_HEADS)]
        for ch in range(rb // CHUNK):
            rows = slice(ch * CHUNK, (ch + 1) * CHUNK)
            y, _ = _ln_stats(_gelu(bv_ref[rows, :]))
            vb = (y * g_ref[...] + b_ref[...]).astype(BF16)
            mixed = _spatial_mix(wm, vb, bias_ref[...], head)
            o_ref[rows, :] = (_gelu(bu_ref[rows, :]) * mixed).astype(BF16)

    def whole(a):
        return pl.BlockSpec(a.shape, lambda i, nd=len(a.shape): (0,) * nd)

    tile = pl.BlockSpec((rb, db), lambda i: (i, 0))
    return pl.pallas_call(
        kern, grid=(t // rb,), in_specs=[tile, tile, whole(w_s), whole(bias_full), whole(ln_g), whole(ln_b)],
        out_specs=tile, out_shape=SDS((t, db), BF16), compiler_params=_params("arbitrary"), name=name)(
            b_u, b_v, w_s, bias_full, ln_g, ln_b)


def _mix_b_bwd(b_u, b_v, dbo, w_s, w_s_t, bias_full, ln_g, ln_b, name):
    t, db = b_u.shape
    rb = _tile(t, ROW_TILE)
    steps = t // rb

    def kern(bu_ref, bv_ref, do_ref, ws_ref, wst_ref, bias_ref, g_ref, b_ref,
             dbu_ref, dbv_ref, dws_ref, dbs_ref, dg_ref, dbeta_ref, dbias_ref):
        i = pl.program_id(0)

        @pl.when(i == 0)
        def _():
            for r in (dws_ref, dg_ref, dbeta_ref, dbias_ref):
                r[...] = jnp.zeros_like(r)

        head = _head_of_lane(db)
        gain = g_ref[...]
        wm = [jnp.where(_tril(), ws_ref[h], 0.0).astype(BF16) for h in range(B_HEADS)]
        wmt = [jnp.where(_tril(False), wst_ref[h], 0.0).astype(BF16) for h in range(B_HEADS)]
        for ch in range(rb // CHUNK):
            rows = slice(ch * CHUNK, (ch + 1) * CHUNK)
            bu, bv, dout = bu_ref[rows, :], bv_ref[rows, :], do_ref[rows, :]
            y, rstd = _ln_stats(_gelu(bv))
            vb = (y * gain + b_ref[...]).astype(BF16)
            mixed = _spatial_mix(wm, vb, bias_ref[...], head)
            du = dout * mixed
            dmixed = dout * _gelu(bu)
            dbias_ref[...] += dmixed
            dmb = dmixed.astype(BF16)
            dv = None
            for h in range(B_HEADS):
                dws_ref[h] += _dot_nt(jnp.where(head == h, dmb, 0.0).astype(BF16), vb)
                part = jnp.where(head == h, _dot(wmt[h], dmb), 0.0)
                dv = part if dv is None else dv + part
            dg_ref[...] += _colsum(dv * y)
            dbeta_ref[...] += _colsum(dv)
            dbv_ref[rows, :] = (_ln_bwd(dv, y, rstd, gain) * _gelu_grad(bv)).astype(BF16)
            dbu_ref[rows, :] = (du * _gelu_grad(bu)).astype(BF16)

        @pl.when(i == steps - 1)
        def _():
            for h in range(B_HEADS):
                dws_ref[h] = jnp.where(_tril(), dws_ref[h], 0.0)
                dbs_ref[h] = jnp.sum(jnp.where(head == h, dbias_ref[...], 0.0), axis=1, keepdims=True)

    def whole(a):
        return pl.BlockSpec(a.shape, lambda i, nd=len(a.shape): (0,) * nd)

    tile = pl.BlockSpec((rb, db), lambda i: (i, 0))
    vec = SDS((1, db), F32)
    dbs = SDS((B_HEADS, CHUNK, 1), F32)
    return pl.pallas_call(
        kern, grid=(steps,),
        in_specs=[tile, tile, tile, whole(w_s), whole(w_s_t), whole(bias_full), whole(ln_g), whole(ln_b)],
        out_specs=[tile, tile, whole(w_s), whole(dbs), whole(vec), whole(vec)],
        out_shape=[SDS((t, db), BF16), SDS((t, db), BF16), SDS(w_s.shape, F32), dbs, vec, vec],
        scratch_shapes=[pltpu.VMEM((CHUNK, db), F32)],
        compiler_params=_params("arbitrary"), name=name)(b_u, b_v, dbo, w_s, w_s_t, bias_full, ln_g, ln_b)


def _softmax_rows(s):
    e = jnp.exp(s - jnp.max(s, axis=-1, keepdims=True))
    return e / jnp.sum(e, axis=-1, keepdims=True)


def _attention_fwd(q, kv, batch, name):
    t, d = q.shape
    seq, mlen, hd = t // batch, kv.shape[0] // batch, d // X_HEADS
    ar = _tile(seq, ATT_ROWS)
    scale = hd ** -0.5

    def kern(q_ref, k_ref, v_ref, o_ref):
        k, v = k_ref[...], v_ref[...]
        for r0 in range(0, seq, ar):
            p = _softmax_rows(_dot_nt(q_ref[r0:r0 + ar, :], k) * scale)
            o_ref[r0:r0 + ar, :] = _dot(p.astype(BF16), v).astype(BF16)

    qs = pl.BlockSpec((seq, hd), lambda b, h: (b, h))
    return pl.pallas_call(
        kern, grid=(batch, X_HEADS),
        in_specs=[qs, pl.BlockSpec((mlen, hd), lambda b, h: (b, h)), pl.BlockSpec((mlen, hd), lambda b, h: (b, X_HEADS + h))],
        out_specs=qs, out_shape=SDS((t, d), BF16), compiler_params=_params("arbitrary", "arbitrary"), name=name)(q, kv, kv)


def _attention_bwd(q, kv, do, batch, name):
    t, d = q.shape
    seq, mlen, hd = t // batch, kv.shape[0] // batch, d // X_HEADS
    ar = _tile(seq, ATT_ROWS)
    scale = hd ** -0.5

    def kern(q_ref, k_ref, v_ref, do_ref, dq_ref, dk_ref, dv_ref):
        k, v = k_ref[...], v_ref[...]
        dk = jnp.zeros((mlen, hd), F32)
        dv = jnp.zeros((mlen, hd), F32)
        for r0 in range(0, seq, ar):
            qr, dor = q_ref[r0:r0 + ar, :], do_ref[r0:r0 + ar, :]
            p = _softmax_rows(_dot_nt(qr, k) * scale)
            dp = _dot_nt(dor, v)
            ds = (p * (dp - jnp.sum(p * dp, axis=-1, keepdims=True)) * scale).astype(BF16)
            dq_ref[r0:r0 + ar, :] = _dot(ds, k).astype(BF16)
            dk = dk + _dot_tn(ds, qr)
            dv = dv + _dot_tn(p.astype(BF16), dor)
        dk_ref[...] = dk.astype(BF16)
        dv_ref[...] = dv.astype(BF16)

    qs = pl.BlockSpec((seq, hd), lambda b, h: (b, h))
    ks = pl.BlockSpec((mlen, hd), lambda b, h: (b, h))
    dkv = SDS((kv.shape[0], d), BF16)
    return pl.pallas_call(
        kern, grid=(batch, X_HEADS),
        in_specs=[qs, ks, pl.BlockSpec((mlen, hd), lambda b, h: (b, X_HEADS + h)), qs],
        out_specs=[qs, ks, ks], out_shape=[SDS((t, d), BF16), dkv, dkv],
        compiler_params=_params("arbitrary", "arbitrary"), name=name)(q, kv, kv, do)


def _place():
    x, y, c = lax.axis_index("x"), lax.axis_index("y"), lax.axis_index("c")
    other_chips = [(1 - x, y), (x, 1 - y), (1 - x, 1 - y)]
    return x, y, c, other_chips


def _comm_call(kern, out_shape, n_pairs, name, *args):
    return pl.pallas_call(
        kern, out_shape=out_shape, in_specs=[ANY] * len(args), out_specs=jax.tree.map(lambda _: ANY, out_shape),
        scratch_shapes=[pltpu.SemaphoreType.DMA((n_pairs,)), pltpu.SemaphoreType.DMA((n_pairs,)), pltpu.SemaphoreType.DMA((n_pairs,))],
        name=name)(*args)


def _all_gather_chips(shard, name):
    r, cols = shard.shape
    rh = r // 2

    def kern(s_ref, o_ref, send_sems, recv_sems, local_sems):
        x, y, c, chips = _place()
        mine_slot = 2 * x + y
        half = pl.ds(c * rh, rh)
        other_half = pl.ds((1 - c) * rh, rh)

        def copy(k, src, dst, to):
            return pltpu.make_async_remote_copy(src_ref=src, dst_ref=dst, send_sem=send_sems.at[k], recv_sem=recv_sems.at[k],
                                                device_id=to, device_id_type=MESH)

        mine = pltpu.make_async_copy(s_ref, o_ref.at[mine_slot], local_sems.at[0])
        mine.start()
        first = [copy(j, s_ref.at[half], o_ref.at[mine_slot, half], (px, py, c)) for j, (px, py) in enumerate(chips)]
        for cp in first:
            cp.start()
        passed = []
        for j, (px, py) in enumerate(chips):
            landed = o_ref.at[2 * px + py, half]
            copy(j, landed, landed, (px, py, c)).wait_recv()
            cp = copy(3 + j, landed, landed, (x, y, 1 - c))
            cp.start()
            passed.append(cp)
        for j, (px, py) in enumerate(chips):
            theirs = o_ref.at[2 * px + py, other_half]
            copy(3 + j, theirs, theirs, (x, y, 1 - c)).wait_recv()
        for cp in first + passed:
            cp.wait_send()
        mine.wait()

    return _comm_call(kern, SDS((N_CHIPS, r, cols), shard.dtype), 6, name, shard)


def _join_halves(joined, name):
    n = len(joined)

    def kern(*refs):
        j_refs = refs[:n]
        send_sems, recv_sems = refs[2 * n:]
        x, y, c, _ = _place()
        copies = []
        for k, j_ref in enumerate(j_refs):
            rh = j_ref.shape[0] // 2
            rows = j_ref.at[pl.ds(c * rh, rh)]
            cp = pltpu.make_async_remote_copy(src_ref=rows, dst_ref=rows, send_sem=send_sems.at[k], recv_sem=recv_sems.at[k],
                                              device_id=(x, y, 1 - c), device_id_type=MESH)
            cp.start()
            copies.append(cp)
        for k, (cp, j_ref) in enumerate(zip(copies, j_refs)):
            rh = j_ref.shape[0] // 2
            theirs = j_ref.at[pl.ds((1 - c) * rh, rh)]
            pltpu.make_async_remote_copy(src_ref=theirs, dst_ref=theirs, send_sem=send_sems.at[k], recv_sem=recv_sems.at[k],
                                         device_id=(x, y, 1 - c), device_id_type=MESH).wait_recv()
            cp.wait_send()

    return pl.pallas_call(
        kern, out_shape=[SDS(j.shape, j.dtype) for j in joined], in_specs=[ANY] * n, out_specs=[ANY] * n,
        input_output_aliases={k: k for k in range(n)},
        scratch_shapes=[pltpu.SemaphoreType.DMA((n,)), pltpu.SemaphoreType.DMA((n,))], name=name)(*joined)


HBM = pl.BlockSpec(memory_space=pltpu.HBM)
SEMAPHORES = pl.BlockSpec(memory_space=pltpu.SEMAPHORE)
COPIES_PER_ARRAY = N_CHIPS - 1
GATHER_COPIES = N_CHIPS


def _in_hbm(a):
    return pltpu.with_memory_space_constraint(a, pltpu.HBM)


def _gather_copies(shard_refs, land_refs, send_sems, recv_sems):
    x, y, c, chips = _place()
    copies = []
    for k, (s_ref, land_ref) in enumerate(zip(shard_refs, land_refs)):
        half = pl.ds(c * (s_ref.shape[0] // 2), s_ref.shape[0] // 2)
        for j, (px, py) in enumerate(chips):
            n = GATHER_COPIES * k + j
            copies.append(pltpu.make_async_remote_copy(
                src_ref=s_ref.at[half], dst_ref=land_ref.at[2 * x + y, half], send_sem=send_sems.at[n], recv_sem=recv_sems.at[n],
                device_id=(px, py, c), device_id_type=MESH))
        n = GATHER_COPIES * k + N_CHIPS - 1
        copies.append(pltpu.make_async_remote_copy(
            src_ref=s_ref, dst_ref=land_ref.at[2 * x + y], send_sem=send_sems.at[n], recv_sem=recv_sems.at[n],
            device_id=(x, y, 1 - c), device_id_type=MESH))
    return copies


def _swap_copies(full_refs, land_refs, send_sems, recv_sems):
    x, y, c, _ = _place()
    copies = []
    for k, (g_ref, land_ref) in enumerate(zip(full_refs, land_refs)):
        rh = land_ref.shape[1]
        copies.append(pltpu.make_async_remote_copy(
            src_ref=g_ref.at[:, pl.ds((1 - c) * rh, rh), :], dst_ref=land_ref, send_sem=send_sems.at[k], recv_sem=recv_sems.at[k],
            device_id=(x, y, 1 - c), device_id_type=MESH))
    return copies


def _owner_copies(sum_refs, land_refs, send_sems, recv_sems):
    x, y, c, chips = _place()
    copies = []
    for k, (s_ref, land_ref) in enumerate(zip(sum_refs, land_refs)):
        for j, (px, py) in enumerate(chips):
            n = COPIES_PER_ARRAY * k + j
            copies.append(pltpu.make_async_remote_copy(
                src_ref=s_ref.at[2 * px + py], dst_ref=land_ref.at[j], send_sem=send_sems.at[n], recv_sem=recv_sems.at[n],
                device_id=(px, py, c), device_id_type=MESH))
    return copies


def _copies_start(build, per_array, sources, lands, name, after=None):
    ns, nb = len(sources), len(sources) + len(lands)
    n_copies = per_array * ns
    order = [] if after is None else [after]
    n_in = nb + len(order)

    def kern(*refs):
        for cp in build(refs[:ns], refs[ns:nb], refs[n_in + nb], refs[n_in + nb + 1]):
            cp.start()
        refs[-1][...] = jnp.zeros_like(refs[-1])

    bufs = [*sources, *lands]
    res = pl.pallas_call(
        kern, name=name,
        out_shape=(*[pltpu.HBM(b.shape, b.dtype) for b in bufs], pltpu.SemaphoreType.DMA((n_copies,)), pltpu.SemaphoreType.DMA((n_copies,)),
                   SDS((SUBLANES, 128), F32)),
        in_specs=[HBM] * nb + [ANY] * len(order),
        out_specs=(*[HBM] * nb, SEMAPHORES, SEMAPHORES, pl.BlockSpec(memory_space=pltpu.VMEM)),
        input_output_aliases={i: i for i in range(nb)},
        compiler_params=pltpu.CompilerParams(has_side_effects=pltpu.SideEffectType.DATAFLOW_SIDE_EFFECTING),
    )(*[_in_hbm(b) for b in bufs], *order)
    return res[nb], res[nb + 1], list(res[:nb]), res[-1]


def _copies_wait(build, send_sems, recv_sems, bufs, ns, after, name):
    nb = len(bufs)

    def kern(*refs):
        for cp in build(refs[:ns], refs[ns:nb], refs[nb], refs[nb + 1]):
            cp.wait_send()
            cp.wait_recv()

    res = pl.pallas_call(
        kern, name=name, out_shape=tuple(pltpu.HBM(b.shape, b.dtype) for b in bufs),
        in_specs=[HBM] * nb + [SEMAPHORES, SEMAPHORES, ANY], out_specs=tuple([HBM] * nb),
        input_output_aliases={i: i for i in range(nb)},
        compiler_params=pltpu.CompilerParams(has_side_effects=pltpu.SideEffectType.DATAFLOW_SIDE_EFFECTING),
    )(*bufs, send_sems, recv_sems, after)
    return list(res[:ns]), list(res[ns:])


def _gather_finish(lands, name):
    ns = len(lands)

    def kern(*refs):
        l_refs = refs[:ns]
        send_sems, recv_sems = refs[2 * ns:]
        x, y, c, chips = _place()
        passed = []
        for k, l_ref in enumerate(l_refs):
            rh = l_ref.shape[1] // 2
            for j, (px, py) in enumerate(chips):
                landed = l_ref.at[2 * px + py, pl.ds(c * rh, rh)]
                cp = pltpu.make_async_remote_copy(
                    src_ref=landed, dst_ref=landed, send_sem=send_sems.at[COPIES_PER_ARRAY * k + j],
                    recv_sem=recv_sems.at[COPIES_PER_ARRAY * k + j], device_id=(x, y, 1 - c), device_id_type=MESH)
                cp.start()
                passed.append(cp)
        for k, l_ref in enumerate(l_refs):
            rh = l_ref.shape[1] // 2
            for j, (px, py) in enumerate(chips):
                theirs = l_ref.at[2 * px + py, pl.ds((1 - c) * rh, rh)]
                pltpu.make_async_remote_copy(
                    src_ref=theirs, dst_ref=theirs, send_sem=send_sems.at[COPIES_PER_ARRAY * k + j],
                    recv_sem=recv_sems.at[COPIES_PER_ARRAY * k + j], device_id=(x, y, 1 - c), device_id_type=MESH).wait_recv()
        for cp in passed:
            cp.wait_send()

    return pl.pallas_call(
        kern, out_shape=[SDS(l.shape, l.dtype) for l in lands], in_specs=[ANY] * ns, out_specs=[ANY] * ns,
        input_output_aliases={k: k for k in range(ns)},
        scratch_shapes=[pltpu.SemaphoreType.DMA((COPIES_PER_ARRAY * ns,)), pltpu.SemaphoreType.DMA((COPIES_PER_ARRAY * ns,))],
        name=name)(*lands)


def _add_sibling(full, theirs, out_dtype, name):
    n, r, cols = full.shape
    rh = r // 2
    tr = _tile(rh, ROW_TILE)
    nb = rh // tr

    def kern(c_ref, a_ref, b_ref, o_ref):
        o_ref[...] = (a_ref[...] + b_ref[...]).astype(out_dtype)

    c = lax.axis_index("c").astype(jnp.int32).reshape(1)
    return pl.pallas_call(
        kern, out_shape=SDS((n, rh, cols), out_dtype),
        grid_spec=pltpu.PrefetchScalarGridSpec(
            num_scalar_prefetch=1, grid=(n, nb),
            in_specs=[pl.BlockSpec((1, tr, cols), lambda j, i, c_ref: (j, c_ref[0] * nb + i, 0)),
                      pl.BlockSpec((1, tr, cols), lambda j, i, c_ref: (j, i, 0))],
            out_specs=pl.BlockSpec((1, tr, cols), lambda j, i, c_ref: (j, i, 0))),
        compiler_params=_params("arbitrary", "arbitrary"), name=name)(c, full, theirs)


def _add_owners(partial, received, name):
    n, rh, cols = partial.shape
    tr = _tile(rh, ROW_TILE)
    nb = rh // tr

    def kern(s_ref, a_ref, b_ref, o_ref):
        acc = a_ref[0].astype(F32)
        for j in range(N_CHIPS - 1):
            acc = acc + b_ref[j].astype(F32)
        o_ref[...] = acc

    place = jnp.stack([2 * lax.axis_index("x") + lax.axis_index("y"), lax.axis_index("c")]).astype(jnp.int32)
    return pl.pallas_call(
        kern, out_shape=SDS((2 * rh, cols), F32),
        grid_spec=pltpu.PrefetchScalarGridSpec(
            num_scalar_prefetch=1, grid=(nb,),
            in_specs=[pl.BlockSpec((1, tr, cols), lambda i, s_ref: (s_ref[0], i, 0)),
                      pl.BlockSpec((N_CHIPS - 1, tr, cols), lambda i, s_ref: (0, i, 0))],
            out_specs=pl.BlockSpec((tr, cols), lambda i, s_ref: (s_ref[1] * nb + i, 0))),
        compiler_params=_params("arbitrary"), name=name)(place, partial, received)


def _pack(parts, lead):
    lead_shape = parts[0].shape[:lead]
    flat = jnp.concatenate([p.reshape(lead_shape + (-1,)) for p in parts], axis=-1)
    n = flat.shape[-1]
    rows = _round_up(-(-n // PACK_W), PACK_ROWS)
    flat = jnp.pad(flat, [(0, 0)] * lead + [(0, rows * PACK_W - n)])
    return flat.reshape(lead_shape + (rows, PACK_W))


def _unpack(buf, shapes, lead):
    lead_shape = buf.shape[:lead]
    flat = buf.reshape(lead_shape + (-1,))
    out, off = [], 0
    for s in shapes:
        n = math.prod(s)
        out.append(flat[..., off:off + n].reshape(lead_shape + tuple(s)))
        off += n
    return out


def _to_slots(full, axis):
    s = full.shape
    split = full.reshape(s[:axis] + (N_CHIPS, s[axis] // N_CHIPS) + s[axis + 1:])
    return jnp.moveaxis(split, axis, 0)


def _from_slots(slots, axis):
    moved = jnp.moveaxis(slots, 0, axis)
    s = moved.shape
    return moved.reshape(s[:axis] + (s[axis] * s[axis + 1],) + s[axis + 2:])


MATMUL_WEIGHTS = (("w_in", 1), ("w_out", 0), ("w_q", 0), ("w_kv", 1), ("w_o", 0), ("w_ff1", 1), ("w_ff2", 0))
CONV_WEIGHTS = (("conv_a_w", 1), ("conv_c_w", 1))
REPLICATED = ("conv_a_b", "ln_a_g", "ln_a_b", "ln_v_g", "ln_v_b", "w_s", "b_s",
              "ln1_g", "ln1_b", "ln2_g", "ln2_b", "ln3_g", "ln3_b")
WEIGHT_ORDER = ("w_in", "conv_a_w", "conv_a_b", "ln_a_g", "ln_a_b", "ln_v_g", "ln_v_b", "w_s", "b_s", "conv_c_w", "w_out",
                "ln1_g", "ln1_b", "w_q", "w_kv", "w_o", "ln2_g", "ln2_b", "w_ff1", "w_ff2", "ln3_g", "ln3_b")


def _row(v):
    return v.reshape(1, -1)


def kernel(x, mem, w_in, conv_a_w, conv_a_b, ln_a_g, ln_a_b, ln_v_g, ln_v_b, w_s, b_s, conv_c_w, w_out, ln1_g, ln1_b, w_q, w_kv, w_o, ln2_g, ln2_b, w_ff1, w_ff2, ln3_g, ln3_b, loss_target, m_w_in, m_conv_a_w, m_conv_a_b, m_ln_a_g, m_ln_a_b, m_ln_v_g, m_ln_v_b, m_w_s, m_b_s, m_conv_c_w, m_w_out, m_ln1_g, m_ln1_b, m_w_q, m_w_kv, m_w_o, m_ln2_g, m_ln2_b, m_w_ff1, m_w_ff2, m_ln3_g, m_ln3_b, v_w_in, v_conv_a_w, v_conv_a_b, v_ln_a_g, v_ln_a_b, v_ln_v_g, v_ln_v_b, v_w_s, v_b_s, v_conv_c_w, v_w_out, v_ln1_g, v_ln1_b, v_w_q, v_w_kv, v_w_o, v_ln2_g, v_ln2_b, v_w_ff1, v_w_ff2, v_ln3_g, v_ln3_b):
    given = dict(locals())
    weights = {n: given[n] for n in WEIGHT_ORDER}
    moment1 = {n: given["m_" + n] for n in WEIGHT_ORDER}
    moment2 = {n: given["v_" + n] for n in WEIGHT_ORDER}

    depth = w_in.shape[0]
    batch, seq, d = x.shape
    t = batch * seq
    hd = d // HEADS
    d_a, d_b, d_c = A_HEADS * hd, B_HEADS * hd, C_HEADS * hd
    widths = (d_a, d_a, d_b, d_b, d_c, d_c, d_c)
    in_offs = [sum(widths[:k]) for k in range(len(widths))]
    alpha = (2.0 * depth) ** 0.25
    layers = range(depth)
    tm = _tile(t, MATMUL_ROW_TILE)
    tm_wide = _tile(t, WIDE_ROW_TILE)

    conv_shards = [weights[n][l] for l in layers for n, _ in CONV_WEIGHTS]
    names = [n for n, _ in MATMUL_WEIGHTS]
    first_used = names[:1]
    parts = {l: [first_used, [n for n in names if n not in first_used]] if l == 0 else [names] for l in layers}
    axis_of = dict(MATMUL_WEIGHTS)
    groups = {}

    def gather_start(l, p, after):
        part = parts[l][p]
        shards = [weights[n][l].astype(BF16) for n in part] + ([_pack(conv_shards, 0)] if (l, p) == (0, 0) else [])
        lands = [lax.empty((N_CHIPS,) + s.shape, s.dtype) for s in shards]
        send_sems, recv_sems, bufs, token = _copies_start(_gather_copies, GATHER_COPIES, shards, lands, f"gather_start_{l}_{p}", after)
        groups[l, p] = (part, send_sems, recv_sems, bufs, len(shards))
        return token

    def gathered(l, p, after):
        part, send_sems, recv_sems, bufs, ns = groups[l, p]
        _, lands = _copies_wait(_gather_copies, send_sems, recv_sems, bufs, ns, after, f"gather_wait_{l}_{p}")
        slots = _gather_finish(lands, "gather_finish")
        return {n: _from_slots(slots[k], axis_of[n]) for k, n in enumerate(part)}, slots[len(part):]

    xf = x.reshape(t, d)
    xb = (xf + gather_start(0, 0, None)[0, 0]).astype(BF16)
    memb = mem.reshape(-1, d).astype(BF16)
    saved, full, conv_full = [], [], None
    stream = (xf,)
    for l in layers:
        w, extra = gathered(l, 0, xb)
        anchor = None
        if l == 0:
            conv_full = _unpack(extra[0], [s.shape for s in conv_shards], 1)
            anchor = extra[0]
            for k in layers:
                for p in range(len(parts[k])):
                    if (k, p) != (0, 0):
                        anchor = gather_start(k, p, anchor)
        for k, (n, axis) in enumerate(CONV_WEIGHTS):
            w[n] = _from_slots(conv_full[l * len(CONV_WEIGHTS) + k], axis)
        for n in REPLICATED:
            w[n] = weights[n][l]
        w["w_s_t"] = jnp.swapaxes(w["w_s"], 1, 2)
        w["bias_full"] = jnp.repeat(w["b_s"].T, hd, axis=1)
        full.append(w)
        s = {"x0b": xb}
        proj = _in_proj(xb, w["w_in"], widths, "in_proj", after=anchor)
        s["proj"] = proj
        a_val, a_gate, b_u, b_v, c_b, c_c, c_x = proj
        a_out, s["ac"] = _mix_a_fwd(a_val, a_gate, w["conv_a_w"], _row(w["conv_a_b"]), _row(w["ln_a_g"]), _row(w["ln_a_b"]),
                                    batch, "mix_a_fwd")
        b_out = _mix_b_fwd(b_u, b_v, w["w_s"], w["bias_full"], _row(w["ln_v_g"]), _row(w["ln_v_b"]), "mix_b_fwd")
        c_out = _mix_c_fwd(c_b, c_c, c_x, w["conv_c_w"], batch, "mix_c_fwd")
        s["cat"] = (a_out, b_out, c_out)
        for p in range(1, len(parts[l])):
            w.update(gathered(l, p, c_out)[0])
        s["z1"], xb = _proj_ln([a_out, b_out, c_out], w["w_out"], stream, _row(w["ln1_g"]), _row(w["ln1_b"]), alpha, tm, "out_proj_ln")
        stream = (s["z1"], _row(w["ln1_g"]), _row(w["ln1_b"]))
        s["x1b"] = xb
        s["q"] = _matmul(xb, w["w_q"], "q_proj")
        s["kv"] = _matmul(memb, w["w_kv"], "kv_proj")
        s["o"] = _attention_fwd(s["q"], s["kv"], batch, "attention_fwd")
        s["z2"], xb = _proj_ln([s["o"]], w["w_o"], stream, _row(w["ln2_g"]), _row(w["ln2_b"]), alpha, tm, "o_proj_ln")
        stream = (s["z2"], _row(w["ln2_g"]), _row(w["ln2_b"]))
        s["x2b"] = xb
        s["h"], s["r"] = _ff1(xb, w["w_ff1"], "ff1")
        s["z3"], xb = _proj_ln([s["r"]], w["w_ff2"], stream, _row(w["ln3_g"]), _row(w["ln3_b"]), alpha, tm_wide, "ff2_ln")
        stream = (s["z3"], _row(w["ln3_g"]), _row(w["ln3_b"]))
        saved.append(s)

    target = loss_target.reshape(t, d)
    dz3, dz3b, top_g, top_b, loss_block = _loss_head_ln(saved[-1]["z3"], target, _row(full[-1]["ln3_g"]), _row(full[-1]["ln3_b"]),
                                                        "loss_head_ln3")
    loss = lax.psum(loss_block[0, 0], ("x", "y", "c"))
    grads = [{} for _ in layers]
    grads[-1]["ln3_g"], grads[-1]["ln3_b"] = top_g, top_b
    reductions = []
    swapping = {}
    token, last_started = 0.0, None

    def rows_to_slots(full_):
        return full_.reshape(N_CHIPS, full_.shape[1] // N_CHIPS, full_.shape[2])

    def swap_start(key, named, wire):
        arrays = [a for _, a in named]
        lands = [lax.empty((f.shape[0], f.shape[1] // 2, f.shape[2]), f.dtype) for f in arrays]
        send_sems, recv_sems, bufs, started = _copies_start(_swap_copies, 1, arrays, lands, f"rs_swap_start_{key}")
        swapping[key] = ([n for n, _ in named], send_sems, recv_sems, bufs, wire)
        return started

    def swap_finish(key, after):
        nonlocal token, last_started
        part, send_sems, recv_sems, bufs, wire = swapping.pop(key)
        fulls, theirs = _copies_wait(_swap_copies, send_sems, recv_sems, bufs, len(wire), after, f"rs_swap_wait_{key}")
        sums = [_add_sibling(f, th, dt, "rs_add_sibling") for f, th, dt in zip(fulls, theirs, wire)]
        lands = [lax.empty((N_CHIPS - 1,) + p.shape[1:], p.dtype) for p in sums]
        send_sems, recv_sems, bufs, started = _copies_start(_owner_copies, COPIES_PER_ARRAY, sums, lands, f"rs_send_start_{key}")
        reductions.append((key, part, send_sems, recv_sems, bufs))
        token, last_started = started[0, 0], started

    early = ("w_ff2", "w_ff1", "w_o", "w_kv", "w_q")
    swap_started, pending = None, None
    for l in reversed(layers):
        w, s, g = full[l], saved[l], grads[l]
        g["w_ff2"] = rows_to_slots(_mm_tn(s["r"], dz3b, "d_w_ff2", after=swap_started))
        dh = _bwd_ff2(dz3b, s["h"], w["w_ff2"], "d_ff_hidden", after=swap_started)
        if pending is not None:
            swap_finish(pending, dh)
        g["w_ff1"] = _mm_tn(s["x2b"], dh, "d_w_ff1", slots=N_CHIPS)
        dz2, dz2b, g["ln2_g"], g["ln2_b"] = _bwd_data_ln([dh], w["w_ff1"], [0], dz3, alpha, s["z2"], _row(w["ln2_g"]) + token,
                                                         tm_wide, "d_x2_ln2")
        g["w_o"] = rows_to_slots(_mm_tn(s["o"], dz2b, "d_w_o"))
        do = _bwd_data([dz2b], w["w_o"], [0], None, alpha, BF16, tm, "d_att_out")
        dq, dk, dv = _attention_bwd(s["q"], s["kv"], do, batch, "attention_bwd")
        g["w_kv"] = _mm_tn(memb, jnp.concatenate([dk, dv], axis=1), "d_w_kv", slots=N_CHIPS)
        g["w_q"] = rows_to_slots(_mm_tn(s["x1b"], dq, "d_w_q"))
        dz1, dz1b, g["ln1_g"], g["ln1_b"] = _bwd_data_ln([dq], w["w_q"], [0], dz2, alpha, s["z1"], _row(w["ln1_g"]), tm, "d_x1_ln1")
        early_started = swap_start("0a", [(n, g[n]) for n in early], [BF16] * len(early)) if l == 0 else None
        g["w_out"] = _mm_tn_rows(s["cat"], dz1b, "d_w_out").reshape(N_CHIPS, d // N_CHIPS, d)
        da_, dbo, dco = _bwd_out_proj(dz1b, w["w_out"], (d_a, d_b, d_c), "d_mixer_out", after=early_started)
        a_val, a_gate, b_u, b_v, c_b, c_c, c_x = s["proj"]
        dav, dag, g["conv_a_w"], g["conv_a_b"], g["ln_a_g"], g["ln_a_b"] = _mix_a_bwd(
            a_val, a_gate, s["ac"], da_, w["conv_a_w"], _row(w["ln_a_g"]), _row(w["ln_a_b"]), batch, "mix_a_bwd")
        if l == 0:
            swap_finish("0a", dav)
        dbu, dbv, g["w_s"], dbs, g["ln_v_g"], g["ln_v_b"] = _mix_b_bwd(
            b_u, b_v, dbo, w["w_s"], w["w_s_t"], w["bias_full"], _row(w["ln_v_g"]) + token, _row(w["ln_v_b"]), "mix_b_bwd")
        g["b_s"] = dbs.reshape(B_HEADS, CHUNK)
        dcb, dcc, dcx, g["conv_c_w"] = _mix_c_bwd(c_b, c_c, c_x, dco, w["conv_c_w"], batch, "mix_c_bwd")
        dproj = [dav, dag, dbu, dbv, dcb, dcc, dcx]
        g["w_in"] = _to_slots(_mm_tn_cols(s["x0b"], dproj, "d_w_in"), 1)
        if l > 0:
            below = full[l - 1]
            dz3, dz3b, grads[l - 1]["ln3_g"], grads[l - 1]["ln3_b"] = _bwd_data_ln(
                dproj, w["w_in"], in_offs, dz1, alpha, saved[l - 1]["z3"], _row(below["ln3_g"]), tm, "d_x0_ln3")
            pending = str(l)
            swap_started = swap_start(pending, [(n, g[n]) for n, _ in MATMUL_WEIGHTS], [BF16] * len(MATMUL_WEIGHTS))
        else:
            dx = _bwd_data(dproj, w["w_in"], in_offs, dz1, alpha, F32, tm, "d_x0")
            conv_slots = [_to_slots(grads[k][n], axis) for k in layers for n, axis in CONV_WEIGHTS]
            rep_parts = [jnp.stack([grads[k][n].reshape(weights[n].shape[1:]) for k in layers]) for n in REPLICATED]
            rep_flat = jnp.concatenate([p.reshape(-1) for p in rep_parts])
            n_rep = rep_flat.shape[0]
            per_chip = _round_up(-(-n_rep // N_CHIPS), PACK_W * PACK_ROWS)
            rep_slots = jnp.pad(rep_flat, (0, N_CHIPS * per_chip - n_rep)).reshape(N_CHIPS, per_chip)
            late = [(n, g[n]) for n, _ in MATMUL_WEIGHTS if n not in early] + [("small", _pack(conv_slots + [rep_slots], 1))]
            swap_start("0b", late, [BF16] * (len(late) - 1) + [F32])
            swap_finish("0b", dx)
    grad_x = dx.reshape(batch, seq, d)

    reduced, after = {}, last_started
    for key, part, send_sems, recv_sems, bufs in reductions:
        sums, lands = _copies_wait(_owner_copies, send_sems, recv_sems, bufs, len(part), after, f"rs_send_wait_{key}")
        mine = [_add_owners(p, r, "rs_add_owners") for p, r in zip(sums, lands)]
        joined = _join_halves(mine, "rs_join")
        for n, j in zip(part, joined):
            reduced[int(key[0]), n] = j
        after = joined[0]
    small = _unpack(reduced[0, "small"], [c.shape[1:] for c in conv_slots] + [(per_chip,)], 0)
    conv_grad, rep_mine = small[:-1], small[-1]
    rep_all = _all_gather_chips(rep_mine.reshape(-1, PACK_W), "gather_small_grads").reshape(-1)[:n_rep]

    grad = {}
    for n, _ in MATMUL_WEIGHTS:
        grad[n] = jnp.stack([reduced[l, n] for l in layers])
    for k, (n, _) in enumerate(CONV_WEIGHTS):
        grad[n] = jnp.stack([conv_grad[l * len(CONV_WEIGHTS) + k] for l in layers])
    off = 0
    for n in REPLICATED:
        size = math.prod(weights[n].shape)
        grad[n] = rep_all[off:off + size].reshape(weights[n].shape)
        off += size

    delta, new_m, new_v = {}, {}, {}
    for n, _ in MATMUL_WEIGHTS:
        shape = weights[n].shape
        as_rows = lambda a: a.reshape(-1, shape[-1])
        delta[n], new_m[n], new_v[n] = (
            r.reshape(shape) for r in _adamw(as_rows(weights[n]), as_rows(grad[n]), as_rows(moment1[n]), as_rows(moment2[n]), "adamw"))
    small_names = [n for n, _ in CONV_WEIGHTS] + list(REPLICATED)
    small_shapes = [weights[n].shape for n in small_names]
    packed = [_pack([src[n] for n in small_names], 0) for src in (weights, grad, moment1, moment2)]
    for dst, res in zip((delta, new_m, new_v), _adamw(*packed, "adamw_small")):
        for n, a in zip(small_names, _unpack(res, small_shapes, 0)):
            dst[n] = a

    return (loss, grad_x, *[grad[n] for n in WEIGHT_ORDER], *[delta[n] for n in WEIGHT_ORDER],
            *[new_m[n] for n in WEIGHT_ORDER], *[new_v[n] for n in WEIGHT_ORDER])
```

```python
import functools
import math

import jax
import jax.numpy as jnp
from jax import lax
from jax.experimental import pallas as pl
from jax.experimental.pallas import tpu as pltpu

F32 = jnp.float32
BF16 = jnp.bfloat16
SDS = jax.ShapeDtypeStruct

HEADS = 16
A_HEADS, B_HEADS, C_HEADS = 6, 4, 6
X_HEADS = 4
CHUNK = 128
LN_EPS = 1e-5
ADAM_LR, ADAM_B1, ADAM_B2, ADAM_EPS, ADAM_WD, ADAM_STEP = 0.001, 0.9, 0.999, 1e-08, 0.01, 10

N_CHIPS = 4
V7X_VMEM_LIMIT = 56 << 20
SUBLANES = 8
PACK_W = 1024
PACK_ROWS = 32
ROW_TILE = 512
MATMUL_ROW_TILE = 1024
WIDE_ROW_TILE = 512
CONV_ROWS = 256
SUB_ROWS = 64
ATT_ROWS = 512
TN_TILE = 1024
MESH = pl.DeviceIdType.MESH
ANY = pl.BlockSpec(memory_space=pl.ANY)


def _tile(n, t):
    for d in range(min(n, t), 0, -1):
        if n % d == 0 and d % (2 * SUBLANES) == 0:
            return d
    return n


def _round_up(n, m):
    return -(-n // m) * m


def _params(*sem):
    return pltpu.CompilerParams(dimension_semantics=sem or None, vmem_limit_bytes=V7X_VMEM_LIMIT)


def _dot(a, b):
    return jnp.dot(a, b, preferred_element_type=F32)


def _dot_nt(a, b):
    return lax.dot_general(a, b, (((1,), (1,)), ((), ())), preferred_element_type=F32)


def _dot_tn(a, b):
    return lax.dot_general(a, b, (((0,), (0,)), ((), ())), preferred_element_type=F32)


def _sigmoid(x):
    return 1.0 / (1.0 + jnp.exp(-x))


def _gelu(x):
    return 0.5 * x * (1.0 + lax.erf(x * (2.0 ** -0.5)))


def _gelu_grad(x):
    return 0.5 * (1.0 + lax.erf(x * (2.0 ** -0.5))) + x * jnp.exp(-0.5 * x * x) * ((2.0 * math.pi) ** -0.5)


def _ln_stats(z):
    mu = jnp.mean(z, axis=-1, keepdims=True)
    zc = z - mu
    rstd = lax.rsqrt(jnp.mean(zc * zc, axis=-1, keepdims=True) + LN_EPS)
    return zc * rstd, rstd


def _ln_bwd(dy, y, rstd, g):
    dyh = dy * g
    return rstd * (dyh - jnp.mean(dyh, axis=-1, keepdims=True) - y * jnp.mean(dyh * y, axis=-1, keepdims=True))


def _colsum(x):
    return jnp.sum(x, axis=0, keepdims=True)


def _rowwise(body, rows, consts, outs, accs=(), *, tm, name, after=None):
    t = rows[0].shape[0]
    steps = t // tm
    n_in = len(rows) + len(consts)
    order = [] if after is None else [after]

    def kern(*refs):
        body(pl.program_id(0), steps, *refs[:n_in], *refs[n_in + len(order):])

    def whole(a, **kw):
        return pl.BlockSpec(a.shape, lambda i, nd=len(a.shape): (0,) * nd, **kw)

    in_specs = ([pl.BlockSpec((tm, r.shape[1]), lambda i: (i, 0)) for r in rows]
                + [whole(c, pipeline_mode=pl.Buffered(1)) for c in consts] + [ANY] * len(order))
    out_shape = [SDS((t, n), dt) for n, dt in outs] + [SDS(s, dt) for s, dt in accs]
    out_specs = [pl.BlockSpec((tm, n), lambda i: (i, 0)) for n, _ in outs] + [whole(SDS(s, dt)) for s, dt in accs]
    return pl.pallas_call(kern, grid=(steps,), in_specs=in_specs, out_specs=out_specs, out_shape=out_shape,
                          compiler_params=_params("arbitrary"), name=name)(*rows, *consts, *order)


def _in_proj(xb, w_in, widths, name, after=None):
    offs = [sum(widths[:k]) for k in range(len(widths))]

    def body(i, steps, x_ref, w_ref, *o_refs):
        x = x_ref[...]
        for o_ref, off, n in zip(o_refs, offs, widths):
            o_ref[...] = _dot(x, w_ref[:, off:off + n])

    return _rowwise(body, [xb], [w_in], [(n, F32) for n in widths], tm=_tile(xb.shape[0], MATMUL_ROW_TILE), name=name, after=after)


def _matmul(ab, w, name):
    def body(i, steps, a_ref, w_ref, o_ref):
        o_ref[...] = _dot(a_ref[...], w_ref[...]).astype(BF16)

    return _rowwise(body, [ab], [w], [(w.shape[1], BF16)], tm=_tile(ab.shape[0], MATMUL_ROW_TILE), name=name)[0]


def _ff1(xb, w, name):
    def body(i, steps, a_ref, w_ref, h_ref, r_ref):
        h = _dot(a_ref[...], w_ref[...])
        h_ref[...] = h.astype(BF16)
        r = jnp.maximum(h, 0.0)
        r_ref[...] = (r * r).astype(BF16)

    n = w.shape[1]
    return _rowwise(body, [xb], [w], [(n, BF16), (n, BF16)], tm=_tile(xb.shape[0], WIDE_ROW_TILE), name=name)


def _proj_ln(a_list, w, stream, g, b, alpha, tm, name, after=None):
    widths = [a.shape[1] for a in a_list]
    offs = [sum(widths[:k]) for k in range(len(widths))]
    na = len(a_list)
    from_ln = len(stream) == 3

    def body(i, steps, *refs):
        a_refs, rest = refs[:na], refs[na:]
        if from_ln:
            s_ref, w_ref, sg_ref, sb_ref, g_ref, b_ref, z_ref, xb_ref = rest
            x = _ln_stats(s_ref[...])[0] * sg_ref[...] + sb_ref[...]
        else:
            s_ref, w_ref, g_ref, b_ref, z_ref, xb_ref = rest
            x = s_ref[...]
        acc = alpha * x
        for a_ref, off, n in zip(a_refs, offs, widths):
            acc = acc + _dot(a_ref[...], w_ref[off:off + n, :])
        z_ref[...] = acc
        y, _ = _ln_stats(acc)
        xb_ref[...] = (y * g_ref[...] + b_ref[...]).astype(BF16)

    d = w.shape[1]
    return _rowwise(body, [*a_list, stream[0]], [w, *stream[1:], g, b], [(d, F32), (d, BF16)], tm=tm, name=name, after=after)


def _bwd_ff2(dzb, h, w_ff2, name, after=None):
    def body(i, steps, dz_ref, h_ref, w_ref, dh_ref):
        dr = _dot_nt(dz_ref[...], w_ref[...])
        dh_ref[...] = (dr * (2.0 * jnp.maximum(h_ref[...].astype(F32), 0.0))).astype(BF16)

    return _rowwise(body, [dzb, h], [w_ff2], [(h.shape[1], BF16)], tm=_tile(h.shape[0], WIDE_ROW_TILE), name=name, after=after)[0]


def _bwd_nt(g_list, w, col_offs, res, alpha, tm, name):
    ng = len(g_list)
    widths = [g.shape[1] for g in g_list]

    def body(i, steps, *refs):
        g_refs = refs[:ng]
        if res is None:
            w_ref, o_ref = refs[ng:]
            acc = None
        else:
            r_ref, w_ref, o_ref = refs[ng:]
            acc = alpha * r_ref[...]
        for g_ref, off, n in zip(g_refs, col_offs, widths):
            part = _dot_nt(g_ref[...], w_ref[:, off:off + n])
            acc = part if acc is None else acc + part
        o_ref[...] = acc.astype(o_ref.dtype)

    rows = list(g_list) + ([] if res is None else [res])
    return rows, w, body, tm, name


def _bwd_data(g_list, w, col_offs, res, alpha, out_dtype, tm, name):
    rows, w, body, tm, name = _bwd_nt(g_list, w, col_offs, res, alpha, tm, name)
    return _rowwise(body, rows, [w], [(w.shape[0], out_dtype)], tm=tm, name=name)[0]


def _bwd_data_ln(g_list, w, col_offs, res, alpha, z, gain, tm, name):
    ng = len(g_list)
    widths = [g.shape[1] for g in g_list]

    def body(i, steps, *refs):
        g_refs = refs[:ng]
        r_ref, z_ref, w_ref, gain_ref, dz_ref, dzb_ref, dg_ref, db_ref = refs[ng:]

        @pl.when(i == 0)
        def _():
            dg_ref[...] = jnp.zeros_like(dg_ref)
            db_ref[...] = jnp.zeros_like(db_ref)

        dy = alpha * r_ref[...]
        for g_ref, off, n in zip(g_refs, col_offs, widths):
            dy = dy + _dot_nt(g_ref[...], w_ref[:, off:off + n])
        y, rstd = _ln_stats(z_ref[...])
        dz = _ln_bwd(dy, y, rstd, gain_ref[...])
        dz_ref[...] = dz
        dzb_ref[...] = dz.astype(BF16)
        dg_ref[...] += _colsum(dy * y)
        db_ref[...] += _colsum(dy)

    d = z.shape[1]
    return _rowwise(body, [*g_list, res, z], [w, gain], [(d, F32), (d, BF16)], [((1, d), F32), ((1, d), F32)], tm=tm, name=name)


def _loss_head_ln(z, target, gain, bias, name):
    d = z.shape[1]

    def body(i, steps, z_ref, t_ref, gain_ref, bias_ref, dz_ref, dzb_ref, dg_ref, db_ref, l_ref):
        @pl.when(i == 0)
        def _():
            for r in (dg_ref, db_ref, l_ref):
                r[...] = jnp.zeros_like(r)

        yn, rstd = _ln_stats(z_ref[...])
        err = yn * gain_ref[...] + bias_ref[...] - t_ref[...]
        l_ref[...] += jnp.sum(err * err) * (0.5 / d)
        dy = err * (1.0 / d)
        dz = _ln_bwd(dy, yn, rstd, gain_ref[...])
        dz_ref[...] = dz
        dzb_ref[...] = dz.astype(BF16)
        dg_ref[...] += _colsum(dy * yn)
        db_ref[...] += _colsum(dy)

    return _rowwise(body, [z, target], [gain, bias], [(d, F32), (d, BF16)],
                    [((1, d), F32), ((1, d), F32), ((SUBLANES, 128), F32)], tm=_tile(z.shape[0], ROW_TILE), name=name)


def _bwd_out_proj(dzb, w_out, widths, name, after=None):
    offs = [sum(widths[:k]) for k in range(len(widths))]

    def body(i, steps, dz_ref, w_ref, *o_refs):
        dz = dz_ref[...]
        for o_ref, off, n in zip(o_refs, offs, widths):
            o_ref[...] = _dot_nt(dz, w_ref[off:off + n, :])

    return _rowwise(body, [dzb], [w_out], [(n, F32) for n in widths], tm=_tile(dzb.shape[0], MATMUL_ROW_TILE), name=name,
                    after=after)


def _adamw(w, g, m, v, name):
    def body(i, steps, w_ref, g_ref, m_ref, v_ref, d_ref, nm_ref, nv_ref):
        g_ = g_ref[...]
        nm = ADAM_B1 * m_ref[...] + (1.0 - ADAM_B1) * g_
        nv = ADAM_B2 * v_ref[...] + (1.0 - ADAM_B2) * (g_ * g_)
        m_hat = nm / (1.0 - ADAM_B1 ** ADAM_STEP)
        v_hat = nv / (1.0 - ADAM_B2 ** ADAM_STEP)
        d_ref[...] = -ADAM_LR * (m_hat / (jnp.sqrt(v_hat) + ADAM_EPS) + ADAM_WD * w_ref[...])
        nm_ref[...] = nm
        nv_ref[...] = nv

    c = w.shape[1]
    return _rowwise(body, [w, g, m, v], [], [(c, F32)] * 3, tm=_tile(w.shape[0], ROW_TILE), name=name)


def _mm_tn(a, g, name, slots=1, after=None):
    t, ka = a.shape
    n = g.shape[1]
    ta, tn, tk = _tile(ka, TN_TILE), _tile(n // slots, TN_TILE), _tile(t, TN_TILE)
    per = n // slots // tn

    order = [] if after is None else [after]

    def kern(a_ref, g_ref, *refs):
        o_ref = refs[-1]

        @pl.when(pl.program_id(2) == 0)
        def _():
            o_ref[...] = jnp.zeros_like(o_ref)

        o_ref[0] += _dot_tn(a_ref[...], g_ref[...])

    return pl.pallas_call(
        kern, grid=(ka // ta, n // tn, t // tk),
        in_specs=[pl.BlockSpec((tk, ta), lambda i, j, k: (k, i)), pl.BlockSpec((tk, tn), lambda i, j, k: (k, j))] + [ANY] * len(order),
        out_specs=pl.BlockSpec((1, ta, tn), lambda i, j, k: (j // per, i, j % per)), out_shape=SDS((slots, ka, n // slots), F32),
        compiler_params=_params("arbitrary", "arbitrary", "arbitrary"), name=name)(a, g, *order)


def _mm_tn_cols(a, g_list, name):
    t, ka = a.shape
    widths = [g.shape[1] for g in g_list]
    offs = [sum(widths[:k]) for k in range(len(widths))]
    ta, tk = _tile(ka, ROW_TILE), _tile(t, TN_TILE)

    def kern(a_ref, *refs):
        g_refs, o_ref = refs[:-1], refs[-1]

        @pl.when(pl.program_id(1) == 0)
        def _():
            o_ref[...] = jnp.zeros_like(o_ref)

        a_ = a_ref[...]
        for g_ref, off, n in zip(g_refs, offs, widths):
            o_ref[:, off:off + n] += _dot_tn(a_, g_ref[...])

    return pl.pallas_call(
        kern, grid=(ka // ta, t // tk),
        in_specs=[pl.BlockSpec((tk, ta), lambda i, k: (k, i))] + [pl.BlockSpec((tk, n), lambda i, k: (k, 0)) for n in widths],
        out_specs=pl.BlockSpec((ta, sum(widths)), lambda i, k: (i, 0)), out_shape=SDS((ka, sum(widths)), F32),
        compiler_params=_params("arbitrary", "arbitrary"), name=name)(a, *g_list)


def _mm_tn_rows(a_list, g, name):
    t, n = g.shape
    widths = [a.shape[1] for a in a_list]
    offs = [sum(widths[:k]) for k in range(len(widths))]
    tn, tk = _tile(n, TN_TILE), _tile(t, TN_TILE)

    def kern(*refs):
        a_refs, g_ref, o_ref = refs[:-2], refs[-2], refs[-1]

        @pl.when(pl.program_id(1) == 0)
        def _():
            o_ref[...] = jnp.zeros_like(o_ref)

        g_ = g_ref[...]
        for a_ref, off, ka in zip(a_refs, offs, widths):
            o_ref[off:off + ka, :] += _dot_tn(a_ref[...], g_)

    return pl.pallas_call(
        kern, grid=(n // tn, t // tk),
        in_specs=[pl.BlockSpec((tk, ka), lambda j, k: (k, 0)) for ka in widths] + [pl.BlockSpec((tk, tn), lambda j, k: (k, j))],
        out_specs=pl.BlockSpec((sum(widths), tn), lambda j, k: (0, j)), out_shape=SDS((sum(widths), n), F32),
        compiler_params=_params("arbitrary", "arbitrary"), name=name)(*a_list, g)


def _conv_geometry(seq, taps):
    pad = _round_up(taps - 1, SUBLANES)
    rc = _tile(seq, CONV_ROWS)
    assert rc % pad == 0 and seq % rc == 0
    return pad, rc, seq // rc


def _chunk_spec(rc, n, nch):
    return pl.BlockSpec((rc, n), lambda b, i: (b * nch + i, 0))


def _prev_halo_spec(pad, rc, n, nch):
    per = rc // pad
    return pl.BlockSpec((pad, n), lambda b, i: (jnp.maximum((b * nch + i) * per - 1, 0), 0))


def _next_halo_spec(pad, rc, n, nch, total_rows):
    per = rc // pad
    last = total_rows // pad - 1
    return pl.BlockSpec((pad, n), lambda b, i: (jnp.minimum((b * nch + i + 1) * per, last), 0))


def _whole2(a):
    return pl.BlockSpec(a.shape, lambda b, i, nd=len(a.shape): (0,) * nd)


def _sub_rows(rc):
    return SUB_ROWS if rc % SUB_ROWS == 0 else rc


def _build_shifts(sh_ref, src_ref, offsets):
    rows = src_ref.shape[0]
    for r in sorted({o % SUBLANES for o in offsets} - {0}):
        sh_ref[r, 0:rows - SUBLANES, :] = src_ref[r:r + rows - SUBLANES, :]


def _read_shifted(sh_ref, src_ref, o, s0, sub):
    r = o % SUBLANES
    a = o - r + s0
    return src_ref[a:a + sub, :] if r == 0 else sh_ref[r, a:a + sub, :]


def _tap_sum(sh_ref, src_ref, w_ref, offsets, s0, sub):
    acc = None
    for k, o in enumerate(offsets):
        term = _read_shifted(sh_ref, src_ref, o, s0, sub) * w_ref[k:k + 1, :]
        acc = term if acc is None else acc + term
    return acc


def _row_groups(x):
    acc = x[0:SUBLANES, :]
    for g0 in range(SUBLANES, x.shape[0], SUBLANES):
        acc = acc + x[g0:g0 + SUBLANES, :]
    return acc


def _mix_a_fwd(a_val, a_gate, conv_w, conv_b, ln_g, ln_b, batch, name):
    t, da = a_val.shape
    taps = conv_w.shape[0]
    pad, rc, nch = _conv_geometry(t // batch, taps)
    sub = _sub_rows(rc)
    offs = [pad - (taps - 1) + k for k in range(taps)]

    def kern(av_ref, ag_ref, pav_ref, pag_ref, w_ref, cb_ref, g_ref, b_ref, a_ref, ac_ref, pad_ref, sh_ref):
        first = pl.program_id(1) == 0
        pad_ref[0:pad, :] = jnp.where(first, 0.0, pav_ref[...] * _sigmoid(pag_ref[...]))
        for s0 in range(0, rc, sub):
            pad_ref[pad + s0:pad + s0 + sub, :] = av_ref[s0:s0 + sub, :] * _sigmoid(ag_ref[s0:s0 + sub, :])
        _build_shifts(sh_ref, pad_ref, offs)
        for s0 in range(0, rc, sub):
            ac = _tap_sum(sh_ref, pad_ref, w_ref, offs, s0, sub) + cb_ref[...]
            ac_ref[s0:s0 + sub, :] = ac
            y, _ = _ln_stats(ac)
            aln = y * g_ref[...] + b_ref[...]
            a_ref[s0:s0 + sub, :] = (aln * _sigmoid(aln)).astype(BF16)

    chunk, halo = _chunk_spec(rc, da, nch), _prev_halo_spec(pad, rc, da, nch)
    return pl.pallas_call(
        kern, grid=(batch, nch),
        in_specs=[chunk, chunk, halo, halo, _whole2(conv_w), _whole2(conv_b), _whole2(ln_g), _whole2(ln_b)],
        out_specs=[chunk, chunk], out_shape=[SDS((t, da), BF16), SDS((t, da), F32)],
        scratch_shapes=[pltpu.VMEM((pad + rc, da), F32), pltpu.VMEM((SUBLANES, pad + rc, da), F32)],
        compiler_params=_params("arbitrary", "arbitrary"), name=name)(a_val, a_gate, a_val, a_gate, conv_w, conv_b, ln_g, ln_b)


def _mix_a_bwd(a_val, a_gate, ac, da_, conv_w, ln_g, ln_b, batch, name):
    t, da = a_val.shape
    taps = conv_w.shape[0]
    pad, rc, nch = _conv_geometry(t // batch, taps)
    sub = _sub_rows(rc)
    offs_in = [pad - (taps - 1) + k for k in range(taps)]
    offs_out = [taps - 1 - k for k in range(taps)]

    def kern(av_ref, ag_ref, pav_ref, pag_ref, ac_ref, da_ref, nac_ref, nda_ref, w_ref, g_ref, b_ref,
             dav_ref, dag_ref, dw_ref, dcb_ref, dg_ref, db_ref, gpad_ref, dpad_ref, shg_ref, shd_ref, dwacc_ref):
        i = pl.program_id(1)
        start = (pl.program_id(0) == 0) & (i == 0)
        end = (pl.program_id(0) == batch - 1) & (i == nch - 1)

        @pl.when(start)
        def _():
            for r in (dcb_ref, dg_ref, db_ref, dwacc_ref):
                r[...] = jnp.zeros_like(r)

        gain, bias = g_ref[...], b_ref[...]

        def d_conv_out(ac_, dout):
            y, rstd = _ln_stats(ac_)
            aln = y * gain + bias
            sig = _sigmoid(aln)
            daln = dout * (sig * (1.0 + aln * (1.0 - sig)))
            return _ln_bwd(daln, y, rstd, gain), daln, y

        gpad_ref[0:pad, :] = jnp.where(i == 0, 0.0, pav_ref[...] * _sigmoid(pag_ref[...]))
        dac_next, _, _ = d_conv_out(nac_ref[...], nda_ref[...])
        dpad_ref[rc:rc + pad, :] = jnp.where(i == nch - 1, 0.0, dac_next)
        for s0 in range(0, rc, sub):
            rows = slice(s0, s0 + sub)
            dac, daln, y = d_conv_out(ac_ref[rows, :], da_ref[rows, :])
            dg_ref[...] += _colsum(daln * y)
            db_ref[...] += _colsum(daln)
            dcb_ref[...] += _colsum(dac)
            dpad_ref[rows, :] = dac
            gpad_ref[pad + s0:pad + s0 + sub, :] = av_ref[rows, :] * _sigmoid(ag_ref[rows, :])
        _build_shifts(shg_ref, gpad_ref, offs_in)
        _build_shifts(shd_ref, dpad_ref, offs_out)
        for s0 in range(0, rc, sub):
            rows = slice(s0, s0 + sub)
            dac = dpad_ref[rows, :]
            for k, o in enumerate(offs_in):
                dwacc_ref[k] += _row_groups(dac * _read_shifted(shg_ref, gpad_ref, o, s0, sub))
            dgl = _tap_sum(shd_ref, dpad_ref, w_ref, offs_out, s0, sub)
            av = av_ref[rows, :]
            sig = _sigmoid(ag_ref[rows, :])
            dav_ref[rows, :] = (dgl * sig).astype(BF16)
            dag_ref[rows, :] = (dgl * av * sig * (1.0 - sig)).astype(BF16)

        @pl.when(end)
        def _():
            for k in range(taps):
                dw_ref[k:k + 1, :] = _colsum(dwacc_ref[k])

    chunk, prev = _chunk_spec(rc, da, nch), _prev_halo_spec(pad, rc, da, nch)
    nxt = _next_halo_spec(pad, rc, da, nch, t)
    vec = SDS((1, da), F32)
    return pl.pallas_call(
        kern, grid=(batch, nch),
        in_specs=[chunk, chunk, prev, prev, chunk, chunk, nxt, nxt, _whole2(conv_w), _whole2(ln_g), _whole2(ln_b)],
        out_specs=[chunk, chunk, _whole2(conv_w), _whole2(vec), _whole2(vec), _whole2(vec)],
        out_shape=[SDS((t, da), BF16), SDS((t, da), BF16), SDS(conv_w.shape, F32), vec, vec, vec],
        scratch_shapes=[pltpu.VMEM((pad + rc, da), F32), pltpu.VMEM((rc + pad, da), F32),
                        pltpu.VMEM((SUBLANES, pad + rc, da), F32), pltpu.VMEM((SUBLANES, rc + pad, da), F32),
                        pltpu.VMEM((taps, SUBLANES, da), F32)],
        compiler_params=_params("arbitrary", "arbitrary"), name=name)(
            a_val, a_gate, a_val, a_gate, ac, da_, ac, da_, conv_w, ln_g, ln_b)


def _mix_c_fwd(c_b, c_c, c_x, conv_w, batch, name):
    t, dc = c_b.shape
    taps = conv_w.shape[0]
    pad, rc, nch = _conv_geometry(t // batch, taps)
    sub = _sub_rows(rc)
    offs = [pad - (taps - 1) + k for k in range(taps)]

    def kern(cb_ref, cc_ref, cx_ref, pcc_ref, pcx_ref, w_ref, o_ref, pad_ref, sh_ref):
        pad_ref[0:pad, :] = jnp.where(pl.program_id(1) == 0, 0.0, pcc_ref[...] * pcx_ref[...])
        for s0 in range(0, rc, sub):
            pad_ref[pad + s0:pad + s0 + sub, :] = cc_ref[s0:s0 + sub, :] * cx_ref[s0:s0 + sub, :]
        _build_shifts(sh_ref, pad_ref, offs)
        for s0 in range(0, rc, sub):
            o_ref[s0:s0 + sub, :] = (cb_ref[s0:s0 + sub, :] * _tap_sum(sh_ref, pad_ref, w_ref, offs, s0, sub)).astype(BF16)

    chunk, prev = _chunk_spec(rc, dc, nch), _prev_halo_spec(pad, rc, dc, nch)
    return pl.pallas_call(
        kern, grid=(batch, nch), in_specs=[chunk, chunk, chunk, prev, prev, _whole2(conv_w)],
        out_specs=chunk, out_shape=SDS((t, dc), BF16),
        scratch_shapes=[pltpu.VMEM((pad + rc, dc), F32), pltpu.VMEM((SUBLANES, pad + rc, dc), F32)],
        compiler_params=_params("arbitrary", "arbitrary"), name=name)(c_b, c_c, c_x, c_c, c_x, conv_w)


def _mix_c_bwd(c_b, c_c, c_x, dco, conv_w, batch, name):
    t, dc = c_b.shape
    taps = conv_w.shape[0]
    pad, rc, nch = _conv_geometry(t // batch, taps)
    sub = _sub_rows(rc)
    offs_in = [pad - (taps - 1) + k for k in range(taps)]
    offs_out = [taps - 1 - k for k in range(taps)]

    def kern(cb_ref, cc_ref, cx_ref, do_ref, pcc_ref, pcx_ref, ncb_ref, ndo_ref, w_ref,
             dcb_ref, dcc_ref, dcx_ref, dw_ref, ppad_ref, dpad_ref, shp_ref, shd_ref, dwacc_ref):
        i = pl.program_id(1)

        @pl.when((pl.program_id(0) == 0) & (i == 0))
        def _():
            dwacc_ref[...] = jnp.zeros_like(dwacc_ref)

        ppad_ref[0:pad, :] = jnp.where(i == 0, 0.0, pcc_ref[...] * pcx_ref[...])
        dpad_ref[rc:rc + pad, :] = jnp.where(i == nch - 1, 0.0, ndo_ref[...] * ncb_ref[...])
        for s0 in range(0, rc, sub):
            rows = slice(s0, s0 + sub)
            ppad_ref[pad + s0:pad + s0 + sub, :] = cc_ref[rows, :] * cx_ref[rows, :]
            dpad_ref[rows, :] = do_ref[rows, :] * cb_ref[rows, :]
        _build_shifts(shp_ref, ppad_ref, offs_in)
        _build_shifts(shd_ref, dpad_ref, offs_out)
        for s0 in range(0, rc, sub):
            rows = slice(s0, s0 + sub)
            dcv = dpad_ref[rows, :]
            cv = None
            for k, o in enumerate(offs_in):
                shifted = _read_shifted(shp_ref, ppad_ref, o, s0, sub)
                dwacc_ref[k] += _row_groups(dcv * shifted)
                cv = shifted * w_ref[k:k + 1, :] if cv is None else cv + shifted * w_ref[k:k + 1, :]
            dp = _tap_sum(shd_ref, dpad_ref, w_ref, offs_out, s0, sub)
            dcb_ref[rows, :] = (do_ref[rows, :] * cv).astype(BF16)
            dcc_ref[rows, :] = (dp * cx_ref[rows, :]).astype(BF16)
            dcx_ref[rows, :] = (dp * cc_ref[rows, :]).astype(BF16)

        @pl.when((pl.program_id(0) == batch - 1) & (i == nch - 1))
        def _():
            for k in range(taps):
                dw_ref[k:k + 1, :] = _colsum(dwacc_ref[k])

    chunk, prev = _chunk_spec(rc, dc, nch), _prev_halo_spec(pad, rc, dc, nch)
    nxt = _next_halo_spec(pad, rc, dc, nch, t)
    return pl.pallas_call(
        kern, grid=(batch, nch), in_specs=[chunk, chunk, chunk, chunk, prev, prev, nxt, nxt, _whole2(conv_w)],
        out_specs=[chunk, chunk, chunk, _whole2(conv_w)],
        out_shape=[SDS((t, dc), BF16)] * 3 + [SDS(conv_w.shape, F32)],
        scratch_shapes=[pltpu.VMEM((pad + rc, dc), F32), pltpu.VMEM((rc + pad, dc), F32),
                        pltpu.VMEM((SUBLANES, pad + rc, dc), F32), pltpu.VMEM((SUBLANES, rc + pad, dc), F32),
                        pltpu.VMEM((taps, SUBLANES, dc), F32)],
        compiler_params=_params("arbitrary", "arbitrary"), name=name)(c_b, c_c, c_x, dco, c_c, c_x, c_b, dco, conv_w)


def _head_of_lane(db):
    return lax.broadcasted_iota(jnp.int32, (1, db), 1) // (db // B_HEADS)


def _tril(rows_ge_cols=True):
    r = lax.broadcasted_iota(jnp.int32, (CHUNK, CHUNK), 0)
    c = lax.broadcasted_iota(jnp.int32, (CHUNK, CHUNK), 1)
    return (r >= c) if rows_ge_cols else (r <= c)


def _spatial_mix(wm, vb, bias, head):
    mixed = bias
    for h in range(B_HEADS):
        mixed = mixed + jnp.where(head == h, _dot(wm[h], vb), 0.0)
    return mixed


def _mix_b_fwd(b_u, b_v, w_s, bias_full, ln_g, ln_b, name):
    t, db = b_u.shape
    rb = _tile(t, ROW_TILE)

    def kern(bu_ref, bv_ref, ws_ref, bias_ref, g_ref, b_ref, o_ref):
        head = _head_of_lane(db)
        wm = [jnp.where(_tril(), ws_ref[h], 0.0).astype(BF16) for h in range(B_HEADS)]
        for ch in range(rb // CHUNK):
            rows = slice(ch * CHUNK, (ch + 1) * CHUNK)
            y, _ = _ln_stats(_gelu(bv_ref[rows, :]))
            vb = (y * g_ref[...] + b_ref[...]).astype(BF16)
            mixed = _spatial_mix(wm, vb, bias_ref[...], head)
            o_ref[rows, :] = (_gelu(bu_ref[rows, :]) * mixed).astype(BF16)

    def whole(a):
        return pl.BlockSpec(a.shape, lambda i, nd=len(a.shape): (0,) * nd)

    tile = pl.BlockSpec((rb, db), lambda i: (i, 0))
    return pl.pallas_call(
        kern, grid=(t // rb,), in_specs=[tile, tile, whole(w_s), whole(bias_full), whole(ln_g), whole(ln_b)],
        out_specs=tile, out_shape=SDS((t, db), BF16), compiler_params=_params("arbitrary"), name=name)(
            b_u, b_v, w_s, bias_full, ln_g, ln_b)


def _mix_b_bwd(b_u, b_v, dbo, w_s, w_s_t, bias_full, ln_g, ln_b, name):
    t, db = b_u.shape
    rb = _tile(t, ROW_TILE)
    steps = t // rb

    def kern(bu_ref, bv_ref, do_ref, ws_ref, wst_ref, bias_ref, g_ref, b_ref,
             dbu_ref, dbv_ref, dws_ref, dbs_ref, dg_ref, dbeta_ref, dbias_ref):
        i = pl.program_id(0)

        @pl.when(i == 0)
        def _():
            for r in (dws_ref, dg_ref, dbeta_ref, dbias_ref):
                r[...] = jnp.zeros_like(r)

        head = _head_of_lane(db)
        gain = g_ref[...]
        wm = [jnp.where(_tril(), ws_ref[h], 0.0).astype(BF16) for h in range(B_HEADS)]
        wmt = [jnp.where(_tril(False), wst_ref[h], 0.0).astype(BF16) for h in range(B_HEADS)]
        for ch in range(rb // CHUNK):
            rows = slice(ch * CHUNK, (ch + 1) * CHUNK)
            bu, bv, dout = bu_ref[rows, :], bv_ref[rows, :], do_ref[rows, :]
            y, rstd = _ln_stats(_gelu(bv))
            vb = (y * gain + b_ref[...]).astype(BF16)
            mixed = _spatial_mix(wm, vb, bias_ref[...], head)
            du = dout * mixed
            dmixed = dout * _gelu(bu)
            dbias_ref[...] += dmixed
            dmb = dmixed.astype(BF16)
            dv = None
            for h in range(B_HEADS):
                dws_ref[h] += _dot_nt(jnp.where(head == h, dmb, 0.0).astype(BF16), vb)
                part = jnp.where(head == h, _dot(wmt[h], dmb), 0.0)
                dv = part if dv is None else dv + part
            dg_ref[...] += _colsum(dv * y)
            dbeta_ref[...] += _colsum(dv)
            dbv_ref[rows, :] = (_ln_bwd(dv, y, rstd, gain) * _gelu_grad(bv)).astype(BF16)
            dbu_ref[rows, :] = (du * _gelu_grad(bu)).astype(BF16)

        @pl.when(i == steps - 1)
        def _():
            for h in range(B_HEADS):
                dws_ref[h] = jnp.where(_tril(), dws_ref[h], 0.0)
                dbs_ref[h] = jnp.sum(jnp.where(head == h, dbias_ref[...], 0.0), axis=1, keepdims=True)

    def whole(a):
        return pl.BlockSpec(a.shape, lambda i, nd=len(a.shape): (0,) * nd)

    tile = pl.BlockSpec((rb, db), lambda i: (i, 0))
    vec = SDS((1, db), F32)
    dbs = SDS((B_HEADS, CHUNK, 1), F32)
    return pl.pallas_call(
        kern, grid=(steps,),
        in_specs=[tile, tile, tile, whole(w_s), whole(w_s_t), whole(bias_full), whole(ln_g), whole(ln_b)],
        out_specs=[tile, tile, whole(w_s), whole(dbs), whole(vec), whole(vec)],
        out_shape=[SDS((t, db), BF16), SDS((t, db), BF16), SDS(w_s.shape, F32), dbs, vec, vec],
        scratch_shapes=[pltpu.VMEM((CHUNK, db), F32)],
        compiler_params=_params("arbitrary"), name=name)(b_u, b_v, dbo, w_s, w_s_t, bias_full, ln_g, ln_b)


def _softmax_rows(s):
    e = jnp.exp(s - jnp.max(s, axis=-1, keepdims=True))
    return e / jnp.sum(e, axis=-1, keepdims=True)


def _attention_fwd(q, kv, batch, name):
    t, d = q.shape
    seq, mlen, hd = t // batch, kv.shape[0] // batch, d // X_HEADS
    ar = _tile(seq, ATT_ROWS)
    scale = hd ** -0.5

    def kern(q_ref, k_ref, v_ref, o_ref):
        k, v = k_ref[...], v_ref[...]
        for r0 in range(0, seq, ar):
            p = _softmax_rows(_dot_nt(q_ref[r0:r0 + ar, :], k) * scale)
            o_ref[r0:r0 + ar, :] = _dot(p.astype(BF16), v).astype(BF16)

    qs = pl.BlockSpec((seq, hd), lambda b, h: (b, h))
    return pl.pallas_call(
        kern, grid=(batch, X_HEADS),
        in_specs=[qs, pl.BlockSpec((mlen, hd), lambda b, h: (b, h)), pl.BlockSpec((mlen, hd), lambda b, h: (b, X_HEADS + h))],
        out_specs=qs, out_shape=SDS((t, d), BF16), compiler_params=_params("arbitrary", "arbitrary"), name=name)(q, kv, kv)


def _attention_bwd(q, kv, do, batch, name):
    t, d = q.shape
    seq, mlen, hd = t // batch, kv.shape[0] // batch, d // X_HEADS
    ar = _tile(seq, ATT_ROWS)
    scale = hd ** -0.5

    def kern(q_ref, k_ref, v_ref, do_ref, dq_ref, dk_ref, dv_ref):
        k, v = k_ref[...], v_ref[...]
        dk = jnp.zeros((mlen, hd), F32)
        dv = jnp.zeros((mlen, hd), F32)
        for r0 in range(0, seq, ar):
            qr, dor = q_ref[r0:r0 + ar, :], do_ref[r0:r0 + ar, :]
            p = _softmax_rows(_dot_nt(qr, k) * scale)
            dp = _dot_nt(dor, v)
            ds = (p * (dp - jnp.sum(p * dp, axis=-1, keepdims=True)) * scale).astype(BF16)
            dq_ref[r0:r0 + ar, :] = _dot(ds, k).astype(BF16)
            dk = dk + _dot_tn(ds, qr)
            dv = dv + _dot_tn(p.astype(BF16), dor)
        dk_ref[...] = dk.astype(BF16)
        dv_ref[...] = dv.astype(BF16)

    qs = pl.BlockSpec((seq, hd), lambda b, h: (b, h))
    ks = pl.BlockSpec((mlen, hd), lambda b, h: (b, h))
    dkv = SDS((kv.shape[0], d), BF16)
    return pl.pallas_call(
        kern, grid=(batch, X_HEADS),
        in_specs=[qs, ks, pl.BlockSpec((mlen, hd), lambda b, h: (b, X_HEADS + h)), qs],
        out_specs=[qs, ks, ks], out_shape=[SDS((t, d), BF16), dkv, dkv],
        compiler_params=_params("arbitrary", "arbitrary"), name=name)(q, kv, kv, do)


def _place():
    x, y, c = lax.axis_index("x"), lax.axis_index("y"), lax.axis_index("c")
    other_chips = [(1 - x, y), (x, 1 - y), (1 - x, 1 - y)]
    return x, y, c, other_chips


def _comm_call(kern, out_shape, n_pairs, name, *args):
    return pl.pallas_call(
        kern, out_shape=out_shape, in_specs=[ANY] * len(args), out_specs=jax.tree.map(lambda _: ANY, out_shape),
        scratch_shapes=[pltpu.SemaphoreType.DMA((n_pairs,)), pltpu.SemaphoreType.DMA((n_pairs,)), pltpu.SemaphoreType.DMA((n_pairs,))],
        name=name)(*args)


def _all_gather_chips(shard, name):
    r, cols = shard.shape
    rh = r // 2

    def kern(s_ref, o_ref, send_sems, recv_sems, local_sems):
        x, y, c, chips = _place()
        mine_slot = 2 * x + y
        half = pl.ds(c * rh, rh)
        other_half = pl.ds((1 - c) * rh, rh)

        def copy(k, src, dst, to):
            return pltpu.make_async_remote_copy(src_ref=src, dst_ref=dst, send_sem=send_sems.at[k], recv_sem=recv_sems.at[k],
                                                device_id=to, device_id_type=MESH)

        mine = pltpu.make_async_copy(s_ref, o_ref.at[mine_slot], local_sems.at[0])
        mine.start()
        first = [copy(j, s_ref.at[half], o_ref.at[mine_slot, half], (px, py, c)) for j, (px, py) in enumerate(chips)]
        for cp in first:
            cp.start()
        passed = []
        for j, (px, py) in enumerate(chips):
            landed = o_ref.at[2 * px + py, half]
            copy(j, landed, landed, (px, py, c)).wait_recv()
            cp = copy(3 + j, landed, landed, (x, y, 1 - c))
            cp.start()
            passed.append(cp)
        for j, (px, py) in enumerate(chips):
            theirs = o_ref.at[2 * px + py, other_half]
            copy(3 + j, theirs, theirs, (x, y, 1 - c)).wait_recv()
        for cp in first + passed:
            cp.wait_send()
        mine.wait()

    return _comm_call(kern, SDS((N_CHIPS, r, cols), shard.dtype), 6, name, shard)


def _join_halves(joined, name):
    n = len(joined)

    def kern(*refs):
        j_refs = refs[:n]
        send_sems, recv_sems = refs[2 * n:]
        x, y, c, _ = _place()
        copies = []
        for k, j_ref in enumerate(j_refs):
            rh = j_ref.shape[1] // 2
            rows = j_ref.at[:, pl.ds(c * rh, rh)]
            cp = pltpu.make_async_remote_copy(src_ref=rows, dst_ref=rows, send_sem=send_sems.at[k], recv_sem=recv_sems.at[k],
                                              device_id=(x, y, 1 - c), device_id_type=MESH)
            cp.start()
            copies.append(cp)
        for k, (cp, j_ref) in enumerate(zip(copies, j_refs)):
            rh = j_ref.shape[1] // 2
            theirs = j_ref.at[:, pl.ds((1 - c) * rh, rh)]
            pltpu.make_async_remote_copy(src_ref=theirs, dst_ref=theirs, send_sem=send_sems.at[k], recv_sem=recv_sems.at[k],
                                         device_id=(x, y, 1 - c), device_id_type=MESH).wait_recv()
            cp.wait_send()

    return pl.pallas_call(
        kern, out_shape=[SDS(j.shape, j.dtype) for j in joined], in_specs=[ANY] * n, out_specs=[ANY] * n,
        input_output_aliases={k: k for k in range(n)},
        scratch_shapes=[pltpu.SemaphoreType.DMA((n,)), pltpu.SemaphoreType.DMA((n,))], name=name)(*joined)


HBM = pl.BlockSpec(memory_space=pltpu.HBM)
SEMAPHORES = pl.BlockSpec(memory_space=pltpu.SEMAPHORE)
COPIES_PER_ARRAY = N_CHIPS - 1
GATHER_COPIES = N_CHIPS


def _in_hbm(a):
    return pltpu.with_memory_space_constraint(a, pltpu.HBM)


def _gather_copies(shard_refs, land_refs, send_sems, recv_sems):
    x, y, c, chips = _place()
    copies = []
    for k, (s_ref, land_ref) in enumerate(zip(shard_refs, land_refs)):
        half = pl.ds(c * (s_ref.shape[0] // 2), s_ref.shape[0] // 2)
        for j, (px, py) in enumerate(chips):
            n = GATHER_COPIES * k + j
            copies.append(pltpu.make_async_remote_copy(
                src_ref=s_ref.at[half], dst_ref=land_ref.at[2 * x + y, half], send_sem=send_sems.at[n], recv_sem=recv_sems.at[n],
                device_id=(px, py, c), device_id_type=MESH))
        n = GATHER_COPIES * k + N_CHIPS - 1
        copies.append(pltpu.make_async_remote_copy(
            src_ref=s_ref, dst_ref=land_ref.at[2 * x + y], send_sem=send_sems.at[n], recv_sem=recv_sems.at[n],
            device_id=(x, y, 1 - c), device_id_type=MESH))
    return copies


def _swap_copies(full_refs, land_refs, send_sems, recv_sems):
    x, y, c, _ = _place()
    copies = []
    for k, (g_ref, land_ref) in enumerate(zip(full_refs, land_refs)):
        rh = land_ref.shape[1]
        copies.append(pltpu.make_async_remote_copy(
            src_ref=g_ref.at[:, pl.ds((1 - c) * rh, rh), :], dst_ref=land_ref, send_sem=send_sems.at[k], recv_sem=recv_sems.at[k],
            device_id=(x, y, 1 - c), device_id_type=MESH))
    return copies


def _owner_copies(sum_refs, land_refs, send_sems, recv_sems):
    x, y, c, chips = _place()
    copies = []
    for k, (s_ref, land_ref) in enumerate(zip(sum_refs, land_refs)):
        for j, (px, py) in enumerate(chips):
            n = COPIES_PER_ARRAY * k + j
            copies.append(pltpu.make_async_remote_copy(
                src_ref=s_ref.at[2 * px + py], dst_ref=land_ref.at[j], send_sem=send_sems.at[n], recv_sem=recv_sems.at[n],
                device_id=(px, py, c), device_id_type=MESH))
    return copies


def _copies_start(build, per_array, sources, lands, name, after=None):
    ns, nb = len(sources), len(sources) + len(lands)
    n_copies = per_array * ns
    order = [] if after is None else [after]
    n_in = nb + len(order)

    def kern(*refs):
        for cp in build(refs[:ns], refs[ns:nb], refs[n_in + nb], refs[n_in + nb + 1]):
            cp.start()
        refs[-1][...] = jnp.zeros_like(refs[-1])

    bufs = [*sources, *lands]
    res = pl.pallas_call(
        kern, name=name,
        out_shape=(*[pltpu.HBM(b.shape, b.dtype) for b in bufs], pltpu.SemaphoreType.DMA((n_copies,)), pltpu.SemaphoreType.DMA((n_copies,)),
                   SDS((SUBLANES, 128), F32)),
        in_specs=[HBM] * nb + [ANY] * len(order),
        out_specs=(*[HBM] * nb, SEMAPHORES, SEMAPHORES, pl.BlockSpec(memory_space=pltpu.VMEM)),
        input_output_aliases={i: i for i in range(nb)},
        compiler_params=pltpu.CompilerParams(has_side_effects=pltpu.SideEffectType.DATAFLOW_SIDE_EFFECTING),
    )(*[_in_hbm(b) for b in bufs], *order)
    return res[nb], res[nb + 1], list(res[:nb]), res[-1]


def _copies_wait(build, send_sems, recv_sems, bufs, ns, after, name):
    nb = len(bufs)

    def kern(*refs):
        for cp in build(refs[:ns], refs[ns:nb], refs[nb], refs[nb + 1]):
            cp.wait_send()
            cp.wait_recv()

    res = pl.pallas_call(
        kern, name=name, out_shape=tuple(pltpu.HBM(b.shape, b.dtype) for b in bufs),
        in_specs=[HBM] * nb + [SEMAPHORES, SEMAPHORES, ANY], out_specs=tuple([HBM] * nb),
        input_output_aliases={i: i for i in range(nb)},
        compiler_params=pltpu.CompilerParams(has_side_effects=pltpu.SideEffectType.DATAFLOW_SIDE_EFFECTING),
    )(*bufs, send_sems, recv_sems, after)
    return list(res[:ns]), list(res[ns:])


def _gather_finish(lands, name):
    ns = len(lands)

    def kern(*refs):
        l_refs = refs[:ns]
        send_sems, recv_sems = refs[2 * ns:]
        x, y, c, chips = _place()
        passed = []
        for k, l_ref in enumerate(l_refs):
            rh = l_ref.shape[1] // 2
            for j, (px, py) in enumerate(chips):
                landed = l_ref.at[2 * px + py, pl.ds(c * rh, rh)]
                cp = pltpu.make_async_remote_copy(
                    src_ref=landed, dst_ref=landed, send_sem=send_sems.at[COPIES_PER_ARRAY * k + j],
                    recv_sem=recv_sems.at[COPIES_PER_ARRAY * k + j], device_id=(x, y, 1 - c), device_id_type=MESH)
                cp.start()
                passed.append(cp)
        for k, l_ref in enumerate(l_refs):
            rh = l_ref.shape[1] // 2
            for j, (px, py) in enumerate(chips):
                theirs = l_ref.at[2 * px + py, pl.ds((1 - c) * rh, rh)]
                pltpu.make_async_remote_copy(
                    src_ref=theirs, dst_ref=theirs, send_sem=send_sems.at[COPIES_PER_ARRAY * k + j],
                    recv_sem=recv_sems.at[COPIES_PER_ARRAY * k + j], device_id=(x, y, 1 - c), device_id_type=MESH).wait_recv()
        for cp in passed:
            cp.wait_send()

    return pl.pallas_call(
        kern, out_shape=[SDS(l.shape, l.dtype) for l in lands], in_specs=[ANY] * ns, out_specs=[ANY] * ns,
        input_output_aliases={k: k for k in range(ns)},
        scratch_shapes=[pltpu.SemaphoreType.DMA((COPIES_PER_ARRAY * ns,)), pltpu.SemaphoreType.DMA((COPIES_PER_ARRAY * ns,))],
        name=name)(*lands)


def _add_sibling(full, theirs, out_dtype, name):
    n, r, cols = full.shape
    rh = r // 2
    tr = _tile(rh, ROW_TILE)
    nb = rh // tr

    def kern(c_ref, a_ref, b_ref, o_ref):
        o_ref[...] = (a_ref[...] + b_ref[...]).astype(out_dtype)

    c = lax.axis_index("c").astype(jnp.int32).reshape(1)
    return pl.pallas_call(
        kern, out_shape=SDS((n, rh, cols), out_dtype),
        grid_spec=pltpu.PrefetchScalarGridSpec(
            num_scalar_prefetch=1, grid=(n, nb),
            in_specs=[pl.BlockSpec((1, tr, cols), lambda j, i, c_ref: (j, c_ref[0] * nb + i, 0)),
                      pl.BlockSpec((1, tr, cols), lambda j, i, c_ref: (j, i, 0))],
            out_specs=pl.BlockSpec((1, tr, cols), lambda j, i, c_ref: (j, i, 0))),
        compiler_params=_params("arbitrary", "arbitrary"), name=name)(c, full, theirs)


def _add_owners(partial, received, name, into=None, layer=0, depth=1):
    n, rh, cols = partial.shape
    tr = _tile(rh, ROW_TILE)
    nb = rh // tr
    stacked = [] if into is None else [into]

    def kern(s_ref, a_ref, b_ref, *refs):
        acc = a_ref[0].astype(F32)
        for j in range(N_CHIPS - 1):
            acc = acc + b_ref[j].astype(F32)
        refs[-1][0] = acc

    place = jnp.stack([2 * lax.axis_index("x") + lax.axis_index("y"), lax.axis_index("c")]).astype(jnp.int32)
    return pl.pallas_call(
        kern, out_shape=SDS((depth, 2 * rh, cols), F32),
        grid_spec=pltpu.PrefetchScalarGridSpec(
            num_scalar_prefetch=1, grid=(nb,),
            in_specs=[pl.BlockSpec((1, tr, cols), lambda i, s_ref: (s_ref[0], i, 0)),
                      pl.BlockSpec((N_CHIPS - 1, tr, cols), lambda i, s_ref: (0, i, 0))] + [ANY] * len(stacked),
            out_specs=pl.BlockSpec((1, tr, cols), lambda i, s_ref: (layer, s_ref[1] * nb + i, 0))),
        input_output_aliases={3: 0} if stacked else {},
        compiler_params=_params("arbitrary"), name=name)(place, partial, received, *stacked)


def _pack(parts, lead):
    lead_shape = parts[0].shape[:lead]
    flat = jnp.concatenate([p.reshape(lead_shape + (-1,)) for p in parts], axis=-1)
    n = flat.shape[-1]
    rows = _round_up(-(-n // PACK_W), PACK_ROWS)
    flat = jnp.pad(flat, [(0, 0)] * lead + [(0, rows * PACK_W - n)])
    return flat.reshape(lead_shape + (rows, PACK_W))


def _unpack(buf, shapes, lead):
    lead_shape = buf.shape[:lead]
    flat = buf.reshape(lead_shape + (-1,))
    out, off = [], 0
    for s in shapes:
        n = math.prod(s)
        out.append(flat[..., off:off + n].reshape(lead_shape + tuple(s)))
        off += n
    return out


def _to_slots(full, axis):
    s = full.shape
    split = full.reshape(s[:axis] + (N_CHIPS, s[axis] // N_CHIPS) + s[axis + 1:])
    return jnp.moveaxis(split, axis, 0)


def _from_slots(slots, axis):
    moved = jnp.moveaxis(slots, 0, axis)
    s = moved.shape
    return moved.reshape(s[:axis] + (s[axis] * s[axis + 1],) + s[axis + 2:])


MATMUL_WEIGHTS = (("w_in", 1), ("w_out", 0), ("w_q", 0), ("w_kv", 1), ("w_o", 0), ("w_ff1", 1), ("w_ff2", 0))
CONV_WEIGHTS = (("conv_a_w", 1), ("conv_c_w", 1))
REPLICATED = ("conv_a_b", "ln_a_g", "ln_a_b", "ln_v_g", "ln_v_b", "w_s", "b_s",
              "ln1_g", "ln1_b", "ln2_g", "ln2_b", "ln3_g", "ln3_b")
WEIGHT_ORDER = ("w_in", "conv_a_w", "conv_a_b", "ln_a_g", "ln_a_b", "ln_v_g", "ln_v_b", "w_s", "b_s", "conv_c_w", "w_out",
                "ln1_g", "ln1_b", "w_q", "w_kv", "w_o", "ln2_g", "ln2_b", "w_ff1", "w_ff2", "ln3_g", "ln3_b")


def _row(v):
    return v.reshape(1, -1)


def kernel(x, mem, w_in, conv_a_w, conv_a_b, ln_a_g, ln_a_b, ln_v_g, ln_v_b, w_s, b_s, conv_c_w, w_out, ln1_g, ln1_b, w_q, w_kv, w_o, ln2_g, ln2_b, w_ff1, w_ff2, ln3_g, ln3_b, loss_target, m_w_in, m_conv_a_w, m_conv_a_b, m_ln_a_g, m_ln_a_b, m_ln_v_g, m_ln_v_b, m_w_s, m_b_s, m_conv_c_w, m_w_out, m_ln1_g, m_ln1_b, m_w_q, m_w_kv, m_w_o, m_ln2_g, m_ln2_b, m_w_ff1, m_w_ff2, m_ln3_g, m_ln3_b, v_w_in, v_conv_a_w, v_conv_a_b, v_ln_a_g, v_ln_a_b, v_ln_v_g, v_ln_v_b, v_w_s, v_b_s, v_conv_c_w, v_w_out, v_ln1_g, v_ln1_b, v_w_q, v_w_kv, v_w_o, v_ln2_g, v_ln2_b, v_w_ff1, v_w_ff2, v_ln3_g, v_ln3_b):
    given = dict(locals())
    weights = {n: given[n] for n in WEIGHT_ORDER}
    moment1 = {n: given["m_" + n] for n in WEIGHT_ORDER}
    moment2 = {n: given["v_" + n] for n in WEIGHT_ORDER}

    depth = w_in.shape[0]
    batch, seq, d = x.shape
    t = batch * seq
    hd = d // HEADS
    d_a, d_b, d_c = A_HEADS * hd, B_HEADS * hd, C_HEADS * hd
    widths = (d_a, d_a, d_b, d_b, d_c, d_c, d_c)
    in_offs = [sum(widths[:k]) for k in range(len(widths))]
    alpha = (2.0 * depth) ** 0.25
    layers = range(depth)
    tm = _tile(t, MATMUL_ROW_TILE)
    tm_wide = _tile(t, WIDE_ROW_TILE)

    conv_shards = [weights[n][l] for l in layers for n, _ in CONV_WEIGHTS]
    names = [n for n, _ in MATMUL_WEIGHTS]
    first_used = names[:1]
    parts = {l: [first_used, [n for n in names if n not in first_used]] if l == 0 else [names] for l in layers}
    axis_of = dict(MATMUL_WEIGHTS)
    groups = {}

    def gather_start(l, p, after):
        part = parts[l][p]
        shards = [weights[n][l].astype(BF16) for n in part] + ([_pack(conv_shards, 0)] if (l, p) == (0, 0) else [])
        lands = [lax.empty((N_CHIPS,) + s.shape, s.dtype) for s in shards]
        send_sems, recv_sems, bufs, token = _copies_start(_gather_copies, GATHER_COPIES, shards, lands, f"gather_start_{l}_{p}", after)
        groups[l, p] = (part, send_sems, recv_sems, bufs, len(shards))
        return token

    sequence = [(l, p) for l in layers for p in range(len(parts[l]))]

    def gathered(l, p, after):
        part, send_sems, recv_sems, bufs, ns = groups[l, p]
        _, lands = _copies_wait(_gather_copies, send_sems, recv_sems, bufs, ns, after, f"gather_wait_{l}_{p}")
        slots = _gather_finish(lands, "gather_finish")
        nxt = sequence.index((l, p)) + 1
        started = gather_start(*sequence[nxt], slots[0]) if nxt < len(sequence) else None
        return {n: _from_slots(slots[k], axis_of[n]) for k, n in enumerate(part)}, slots[len(part):], started

    xf = x.reshape(t, d)
    xb = (xf + gather_start(0, 0, None)[0, 0]).astype(BF16)
    memb = mem.reshape(-1, d).astype(BF16)
    saved, full, conv_full = [], [], None
    stream = (xf,)
    for l in layers:
        w, extra, anchor = gathered(l, 0, xb)
        if l == 0:
            conv_full = _unpack(extra[0], [s.shape for s in conv_shards], 1)
        for k, (n, axis) in enumerate(CONV_WEIGHTS):
            w[n] = _from_slots(conv_full[l * len(CONV_WEIGHTS) + k], axis)
        for n in REPLICATED:
            w[n] = weights[n][l]
        w["w_s_t"] = jnp.swapaxes(w["w_s"], 1, 2)
        w["bias_full"] = jnp.repeat(w["b_s"].T, hd, axis=1)
        full.append(w)
        s = {"x0b": xb}
        proj = _in_proj(xb, w["w_in"], widths, "in_proj", after=anchor)
        s["proj"] = proj
        a_val, a_gate, b_u, b_v, c_b, c_c, c_x = proj
        a_out, s["ac"] = _mix_a_fwd(a_val, a_gate, w["conv_a_w"], _row(w["conv_a_b"]), _row(w["ln_a_g"]), _row(w["ln_a_b"]),
                                    batch, "mix_a_fwd")
        b_out = _mix_b_fwd(b_u, b_v, w["w_s"], w["bias_full"], _row(w["ln_v_g"]), _row(w["ln_v_b"]), "mix_b_fwd")
        c_out = _mix_c_fwd(c_b, c_c, c_x, w["conv_c_w"], batch, "mix_c_fwd")
        s["cat"] = (a_out, b_out, c_out)
        anchor = None
        for p in range(1, len(parts[l])):
            more, _, anchor = gathered(l, p, c_out)
            w.update(more)
        s["z1"], xb = _proj_ln([a_out, b_out, c_out], w["w_out"], stream, _row(w["ln1_g"]), _row(w["ln1_b"]), alpha, tm, "out_proj_ln",
                               after=anchor)
        stream = (s["z1"], _row(w["ln1_g"]), _row(w["ln1_b"]))
        s["x1b"] = xb
        s["q"] = _matmul(xb, w["w_q"], "q_proj")
        s["kv"] = _matmul(memb, w["w_kv"], "kv_proj")
        s["o"] = _attention_fwd(s["q"], s["kv"], batch, "attention_fwd")
        s["z2"], xb = _proj_ln([s["o"]], w["w_o"], stream, _row(w["ln2_g"]), _row(w["ln2_b"]), alpha, tm, "o_proj_ln")
        stream = (s["z2"], _row(w["ln2_g"]), _row(w["ln2_b"]))
        s["x2b"] = xb
        s["h"], s["r"] = _ff1(xb, w["w_ff1"], "ff1")
        s["z3"], xb = _proj_ln([s["r"]], w["w_ff2"], stream, _row(w["ln3_g"]), _row(w["ln3_b"]), alpha, tm_wide, "ff2_ln")
        stream = (s["z3"], _row(w["ln3_g"]), _row(w["ln3_b"]))
        saved.append(s)

    target = loss_target.reshape(t, d)
    dz3, dz3b, top_g, top_b, loss_block = _loss_head_ln(saved[-1]["z3"], target, _row(full[-1]["ln3_g"]), _row(full[-1]["ln3_b"]),
                                                        "loss_head_ln3")
    loss = lax.psum(loss_block[0, 0], ("x", "y", "c"))
    grads = [{} for _ in layers]
    grads[-1]["ln3_g"], grads[-1]["ln3_b"] = top_g, top_b
    reductions = []
    swapping = {}
    token, last_started = 0.0, None

    def rows_to_slots(full_):
        return full_.reshape(N_CHIPS, full_.shape[1] // N_CHIPS, full_.shape[2])

    def swap_start(key, named, wire):
        arrays = [a for _, a in named]
        lands = [lax.empty((f.shape[0], f.shape[1] // 2, f.shape[2]), f.dtype) for f in arrays]
        send_sems, recv_sems, bufs, started = _copies_start(_swap_copies, 1, arrays, lands, f"rs_swap_start_{key}")
        swapping[key] = ([n for n, _ in named], send_sems, recv_sems, bufs, wire)
        return started

    def swap_finish(key, after):
        nonlocal token, last_started
        part, send_sems, recv_sems, bufs, wire = swapping.pop(key)
        fulls, theirs = _copies_wait(_swap_copies, send_sems, recv_sems, bufs, len(wire), after, f"rs_swap_wait_{key}")
        sums = [_add_sibling(f, th, dt, "rs_add_sibling") for f, th, dt in zip(fulls, theirs, wire)]
        lands = [lax.empty((N_CHIPS - 1,) + p.shape[1:], p.dtype) for p in sums]
        send_sems, recv_sems, bufs, started = _copies_start(_owner_copies, COPIES_PER_ARRAY, sums, lands, f"rs_send_start_{key}")
        reductions.append((key, part, send_sems, recv_sems, bufs))
        token, last_started = started[0, 0], started

    early = ("w_ff2", "w_ff1", "w_o", "w_kv", "w_q")
    swap_started, pending = None, None
    for l in reversed(layers):
        w, s, g = full[l], saved[l], grads[l]
        g["w_ff2"] = rows_to_slots(_mm_tn(s["r"], dz3b, "d_w_ff2", after=swap_started))
        dh = _bwd_ff2(dz3b, s["h"], w["w_ff2"], "d_ff_hidden", after=swap_started)
        if pending is not None:
            swap_finish(pending, dh)
        g["w_ff1"] = _mm_tn(s["x2b"], dh, "d_w_ff1", slots=N_CHIPS)
        dz2, dz2b, g["ln2_g"], g["ln2_b"] = _bwd_data_ln([dh], w["w_ff1"], [0], dz3, alpha, s["z2"], _row(w["ln2_g"]) + token,
                                                         tm_wide, "d_x2_ln2")
        g["w_o"] = rows_to_slots(_mm_tn(s["o"], dz2b, "d_w_o"))
        do = _bwd_data([dz2b], w["w_o"], [0], None, alpha, BF16, tm, "d_att_out")
        dq, dk, dv = _attention_bwd(s["q"], s["kv"], do, batch, "attention_bwd")
        g["w_kv"] = _mm_tn(memb, jnp.concatenate([dk, dv], axis=1), "d_w_kv", slots=N_CHIPS)
        g["w_q"] = rows_to_slots(_mm_tn(s["x1b"], dq, "d_w_q"))
        dz1, dz1b, g["ln1_g"], g["ln1_b"] = _bwd_data_ln([dq], w["w_q"], [0], dz2, alpha, s["z1"], _row(w["ln1_g"]), tm, "d_x1_ln1")
        early_started = swap_start("0a", [(n, g[n]) for n in early], [BF16] * len(early)) if l == 0 else None
        g["w_out"] = _mm_tn_rows(s["cat"], dz1b, "d_w_out").reshape(N_CHIPS, d // N_CHIPS, d)
        da_, dbo, dco = _bwd_out_proj(dz1b, w["w_out"], (d_a, d_b, d_c), "d_mixer_out", after=early_started)
        a_val, a_gate, b_u, b_v, c_b, c_c, c_x = s["proj"]
        dav, dag, g["conv_a_w"], g["conv_a_b"], g["ln_a_g"], g["ln_a_b"] = _mix_a_bwd(
            a_val, a_gate, s["ac"], da_, w["conv_a_w"], _row(w["ln_a_g"]), _row(w["ln_a_b"]), batch, "mix_a_bwd")
        if l == 0:
            swap_finish("0a", dav)
        dbu, dbv, g["w_s"], dbs, g["ln_v_g"], g["ln_v_b"] = _mix_b_bwd(
            b_u, b_v, dbo, w["w_s"], w["w_s_t"], w["bias_full"], _row(w["ln_v_g"]) + token, _row(w["ln_v_b"]), "mix_b_bwd")
        g["b_s"] = dbs.reshape(B_HEADS, CHUNK)
        dcb, dcc, dcx, g["conv_c_w"] = _mix_c_bwd(c_b, c_c, c_x, dco, w["conv_c_w"], batch, "mix_c_bwd")
        dproj = [dav, dag, dbu, dbv, dcb, dcc, dcx]
        g["w_in"] = _to_slots(_mm_tn_cols(s["x0b"], dproj, "d_w_in"), 1)
        if l > 0:
            below = full[l - 1]
            dz3, dz3b, grads[l - 1]["ln3_g"], grads[l - 1]["ln3_b"] = _bwd_data_ln(
                dproj, w["w_in"], in_offs, dz1, alpha, saved[l - 1]["z3"], _row(below["ln3_g"]), tm, "d_x0_ln3")
            pending = str(l)
            swap_started = swap_start(pending, [(n, g[n]) for n, _ in MATMUL_WEIGHTS], [BF16] * len(MATMUL_WEIGHTS))
        else:
            dx = _bwd_data(dproj, w["w_in"], in_offs, dz1, alpha, F32, tm, "d_x0")
            conv_slots = [_to_slots(grads[k][n], axis) for k in layers for n, axis in CONV_WEIGHTS]
            rep_parts = [jnp.stack([grads[k][n].reshape(weights[n].shape[1:]) for k in layers]) for n in REPLICATED]
            rep_flat = jnp.concatenate([p.reshape(-1) for p in rep_parts])
            n_rep = rep_flat.shape[0]
            per_chip = _round_up(-(-n_rep // N_CHIPS), PACK_W * PACK_ROWS)
            rep_slots = jnp.pad(rep_flat, (0, N_CHIPS * per_chip - n_rep)).reshape(N_CHIPS, per_chip)
            late = [(n, g[n]) for n, _ in MATMUL_WEIGHTS if n not in early] + [("small", _pack(conv_slots + [rep_slots], 1))]
            swap_start("0b", late, [BF16] * (len(late) - 1) + [F32])
            swap_finish("0b", dx)
    grad_x = dx.reshape(batch, seq, d)

    reduced, after = {}, last_started
    for key, part, send_sems, recv_sems, bufs in reductions:
        sums, lands = _copies_wait(_owner_copies, send_sems, recv_sems, bufs, len(part), after, f"rs_send_wait_{key}")
        for n, p, r in zip(part, sums, lands):
            stacked = n != "small"
            reduced[n] = _add_owners(p, r, "rs_add_owners", reduced.get(n), int(key[0]) if stacked else 0, depth if stacked else 1)
            after = reduced[n]
    reduced = dict(zip(reduced, _join_halves(list(reduced.values()), "rs_join")))
    small = _unpack(reduced["small"][0], [c.shape[1:] for c in conv_slots] + [(per_chip,)], 0)
    conv_grad, rep_mine = small[:-1], small[-1]
    rep_all = _all_gather_chips(rep_mine.reshape(-1, PACK_W), "gather_small_grads").reshape(-1)[:n_rep]

    grad = {}
    for n, _ in MATMUL_WEIGHTS:
        grad[n] = reduced[n]
    for k, (n, _) in enumerate(CONV_WEIGHTS):
        grad[n] = jnp.stack([conv_grad[l * len(CONV_WEIGHTS) + k] for l in layers])
    off = 0
    for n in REPLICATED:
        size = math.prod(weights[n].shape)
        grad[n] = rep_all[off:off + size].reshape(weights[n].shape)
        off += size

    delta, new_m, new_v = {}, {}, {}
    for n, _ in MATMUL_WEIGHTS:
        shape = weights[n].shape
        as_rows = lambda a: a.reshape(-1, shape[-1])
        delta[n], new_m[n], new_v[n] = (
            r.reshape(shape) for r in _adamw(as_rows(weights[n]), as_rows(grad[n]), as_rows(moment1[n]), as_rows(moment2[n]), "adamw"))
    small_names = [n for n, _ in CONV_WEIGHTS] + list(REPLICATED)
    small_shapes = [weights[n].shape for n in small_names]
    packed = [_pack([src[n] for n in small_names], 0) for src in (weights, grad, moment1, moment2)]
    for dst, res in zip((delta, new_m, new_v), _adamw(*packed, "adamw_small")):
        for n, a in zip(small_names, _unpack(res, small_shapes, 0)):
            dst[n] = a

    return (loss, grad_x, *[grad[n] for n in WEIGHT_ORDER], *[delta[n] for n in WEIGHT_ORDER],
            *[new_m[n] for n in WEIGHT_ORDER], *[new_v[n] for n in WEIGHT_ORDER])
```

```python
import functools
import math

import jax
import jax.numpy as jnp
from jax import lax
from jax.experimental import pallas as pl
from jax.experimental.pallas import tpu as pltpu

F32 = jnp.float32
BF16 = jnp.bfloat16
SDS = jax.ShapeDtypeStruct

HEADS = 16
A_HEADS, B_HEADS, C_HEADS = 6, 4, 6
X_HEADS = 4
CHUNK = 128
LN_EPS = 1e-5
ADAM_LR, ADAM_B1, ADAM_B2, ADAM_EPS, ADAM_WD, ADAM_STEP = 0.001, 0.9, 0.999, 1e-08, 0.01, 10

N_CHIPS = 4
V7X_VMEM_LIMIT = 56 << 20
SUBLANES = 8
PACK_W = 1024
PACK_ROWS = 32
ROW_TILE = 512
MATMUL_ROW_TILE = 1024
WIDE_ROW_TILE = 512
CONV_ROWS = 256
SUB_ROWS = 64
ATT_ROWS = 512
TN_TILE = 1024
TN_TOKENS = 2048
MESH = pl.DeviceIdType.MESH
ANY = pl.BlockSpec(memory_space=pl.ANY)


def _tile(n, t):
    for d in range(min(n, t), 0, -1):
        if n % d == 0 and d % (2 * SUBLANES) == 0:
            return d
    return n


def _round_up(n, m):
    return -(-n // m) * m


def _params(*sem):
    return pltpu.CompilerParams(dimension_semantics=sem or None, vmem_limit_bytes=V7X_VMEM_LIMIT)


def _dot(a, b):
    return jnp.dot(a, b, preferred_element_type=F32)


def _dot_nt(a, b):
    return lax.dot_general(a, b, (((1,), (1,)), ((), ())), preferred_element_type=F32)


def _dot_tn(a, b):
    return lax.dot_general(a, b, (((0,), (0,)), ((), ())), preferred_element_type=F32)


def _sigmoid(x):
    return 1.0 / (1.0 + jnp.exp(-x))


def _gelu(x):
    return 0.5 * x * (1.0 + lax.erf(x * (2.0 ** -0.5)))


def _gelu_grad(x):
    return 0.5 * (1.0 + lax.erf(x * (2.0 ** -0.5))) + x * jnp.exp(-0.5 * x * x) * ((2.0 * math.pi) ** -0.5)


def _ln_stats(z):
    mu = jnp.mean(z, axis=-1, keepdims=True)
    zc = z - mu
    rstd = lax.rsqrt(jnp.mean(zc * zc, axis=-1, keepdims=True) + LN_EPS)
    return zc * rstd, rstd


def _ln_bwd(dy, y, rstd, g):
    dyh = dy * g
    return rstd * (dyh - jnp.mean(dyh, axis=-1, keepdims=True) - y * jnp.mean(dyh * y, axis=-1, keepdims=True))


def _colsum(x):
    return jnp.sum(x, axis=0, keepdims=True)


def _rowwise(body, rows, consts, outs, accs=(), *, tm, name, after=None):
    t = rows[0].shape[0]
    steps = t // tm
    n_in = len(rows) + len(consts)
    order = [] if after is None else [after]

    def kern(*refs):
        body(pl.program_id(0), steps, *refs[:n_in], *refs[n_in + len(order):])

    def whole(a, **kw):
        return pl.BlockSpec(a.shape, lambda i, nd=len(a.shape): (0,) * nd, **kw)

    in_specs = ([pl.BlockSpec((tm, r.shape[1]), lambda i: (i, 0)) for r in rows]
                + [whole(c, pipeline_mode=pl.Buffered(1)) for c in consts] + [ANY] * len(order))
    out_shape = [SDS((t, n), dt) for n, dt in outs] + [SDS(s, dt) for s, dt in accs]
    out_specs = [pl.BlockSpec((tm, n), lambda i: (i, 0)) for n, _ in outs] + [whole(SDS(s, dt)) for s, dt in accs]
    return pl.pallas_call(kern, grid=(steps,), in_specs=in_specs, out_specs=out_specs, out_shape=out_shape,
                          compiler_params=_params("arbitrary"), name=name)(*rows, *consts, *order)


def _in_proj(xb, w_in, widths, name, after=None):
    offs = [sum(widths[:k]) for k in range(len(widths))]

    def body(i, steps, x_ref, w_ref, *o_refs):
        x = x_ref[...]
        for o_ref, off, n in zip(o_refs, offs, widths):
            o_ref[...] = _dot(x, w_ref[:, off:off + n])

    return _rowwise(body, [xb], [w_in], [(n, F32) for n in widths], tm=_tile(xb.shape[0], MATMUL_ROW_TILE), name=name, after=after)


def _matmul(ab, w, name):
    def body(i, steps, a_ref, w_ref, o_ref):
        o_ref[...] = _dot(a_ref[...], w_ref[...]).astype(BF16)

    return _rowwise(body, [ab], [w], [(w.shape[1], BF16)], tm=_tile(ab.shape[0], MATMUL_ROW_TILE), name=name)[0]


def _ff1(xb, w, name):
    def body(i, steps, a_ref, w_ref, h_ref, r_ref):
        h = _dot(a_ref[...], w_ref[...])
        h_ref[...] = h.astype(BF16)
        r = jnp.maximum(h, 0.0)
        r_ref[...] = (r * r).astype(BF16)

    n = w.shape[1]
    return _rowwise(body, [xb], [w], [(n, BF16), (n, BF16)], tm=_tile(xb.shape[0], MATMUL_ROW_TILE), name=name)


def _proj_ln(a_list, w, stream, g, b, alpha, tm, name, after=None):
    widths = [a.shape[1] for a in a_list]
    offs = [sum(widths[:k]) for k in range(len(widths))]
    na = len(a_list)
    from_ln = len(stream) == 3

    def body(i, steps, *refs):
        a_refs, rest = refs[:na], refs[na:]
        if from_ln:
            s_ref, w_ref, sg_ref, sb_ref, g_ref, b_ref, z_ref, xb_ref = rest
            x = _ln_stats(s_ref[...])[0] * sg_ref[...] + sb_ref[...]
        else:
            s_ref, w_ref, g_ref, b_ref, z_ref, xb_ref = rest
            x = s_ref[...]
        acc = alpha * x
        for a_ref, off, n in zip(a_refs, offs, widths):
            acc = acc + _dot(a_ref[...], w_ref[off:off + n, :])
        z_ref[...] = acc
        y, _ = _ln_stats(acc)
        xb_ref[...] = (y * g_ref[...] + b_ref[...]).astype(BF16)

    d = w.shape[1]
    return _rowwise(body, [*a_list, stream[0]], [w, *stream[1:], g, b], [(d, F32), (d, BF16)], tm=tm, name=name, after=after)


def _bwd_ff2(dzb, h, w_ff2, name, after=None):
    def body(i, steps, dz_ref, h_ref, w_ref, dh_ref):
        dr = _dot_nt(dz_ref[...], w_ref[...])
        dh_ref[...] = (dr * (2.0 * jnp.maximum(h_ref[...].astype(F32), 0.0))).astype(BF16)

    return _rowwise(body, [dzb, h], [w_ff2], [(h.shape[1], BF16)], tm=_tile(h.shape[0], MATMUL_ROW_TILE), name=name, after=after)[0]


def _bwd_nt(g_list, w, col_offs, res, alpha, tm, name):
    ng = len(g_list)
    widths = [g.shape[1] for g in g_list]

    def body(i, steps, *refs):
        g_refs = refs[:ng]
        if res is None:
            w_ref, o_ref = refs[ng:]
            acc = None
        else:
            r_ref, w_ref, o_ref = refs[ng:]
            acc = alpha * r_ref[...]
        for g_ref, off, n in zip(g_refs, col_offs, widths):
            part = _dot_nt(g_ref[...], w_ref[:, off:off + n])
            acc = part if acc is None else acc + part
        o_ref[...] = acc.astype(o_ref.dtype)

    rows = list(g_list) + ([] if res is None else [res])
    return rows, w, body, tm, name


def _bwd_data(g_list, w, col_offs, res, alpha, out_dtype, tm, name):
    rows, w, body, tm, name = _bwd_nt(g_list, w, col_offs, res, alpha, tm, name)
    return _rowwise(body, rows, [w], [(w.shape[0], out_dtype)], tm=tm, name=name)[0]


def _bwd_data_ln(g_list, w, col_offs, res, alpha, z, gain, tm, name):
    ng = len(g_list)
    widths = [g.shape[1] for g in g_list]

    def body(i, steps, *refs):
        g_refs = refs[:ng]
        r_ref, z_ref, w_ref, gain_ref, dz_ref, dzb_ref, dg_ref, db_ref = refs[ng:]

        @pl.when(i == 0)
        def _():
            dg_ref[...] = jnp.zeros_like(dg_ref)
            db_ref[...] = jnp.zeros_like(db_ref)

        dy = alpha * r_ref[...]
        for g_ref, off, n in zip(g_refs, col_offs, widths):
            dy = dy + _dot_nt(g_ref[...], w_ref[:, off:off + n])
        y, rstd = _ln_stats(z_ref[...])
        dz = _ln_bwd(dy, y, rstd, gain_ref[...])
        dz_ref[...] = dz
        dzb_ref[...] = dz.astype(BF16)
        dg_ref[...] += _colsum(dy * y)
        db_ref[...] += _colsum(dy)

    d = z.shape[1]
    return _rowwise(body, [*g_list, res, z], [w, gain], [(d, F32), (d, BF16)], [((1, d), F32), ((1, d), F32)], tm=tm, name=name)


def _loss_head_ln(z, target, gain, bias, name):
    d = z.shape[1]

    def body(i, steps, z_ref, t_ref, gain_ref, bias_ref, dz_ref, dzb_ref, dg_ref, db_ref, l_ref):
        @pl.when(i == 0)
        def _():
            for r in (dg_ref, db_ref, l_ref):
                r[...] = jnp.zeros_like(r)

        yn, rstd = _ln_stats(z_ref[...])
        err = yn * gain_ref[...] + bias_ref[...] - t_ref[...]
        l_ref[...] += jnp.sum(err * err) * (0.5 / d)
        dy = err * (1.0 / d)
        dz = _ln_bwd(dy, yn, rstd, gain_ref[...])
        dz_ref[...] = dz
        dzb_ref[...] = dz.astype(BF16)
        dg_ref[...] += _colsum(dy * yn)
        db_ref[...] += _colsum(dy)

    return _rowwise(body, [z, target], [gain, bias], [(d, F32), (d, BF16)],
                    [((1, d), F32), ((1, d), F32), ((SUBLANES, 128), F32)], tm=_tile(z.shape[0], ROW_TILE), name=name)


def _bwd_out_proj(dzb, w_out, widths, name, after=None):
    offs = [sum(widths[:k]) for k in range(len(widths))]

    def body(i, steps, dz_ref, w_ref, *o_refs):
        dz = dz_ref[...]
        for o_ref, off, n in zip(o_refs, offs, widths):
            o_ref[...] = _dot_nt(dz, w_ref[off:off + n, :])

    return _rowwise(body, [dzb], [w_out], [(n, F32) for n in widths], tm=_tile(dzb.shape[0], MATMUL_ROW_TILE), name=name,
                    after=after)


def _adamw(w, g, m, v, name):
    def body(i, steps, w_ref, g_ref, m_ref, v_ref, d_ref, nm_ref, nv_ref):
        g_ = g_ref[...]
        nm = ADAM_B1 * m_ref[...] + (1.0 - ADAM_B1) * g_
        nv = ADAM_B2 * v_ref[...] + (1.0 - ADAM_B2) * (g_ * g_)
        m_hat = nm / (1.0 - ADAM_B1 ** ADAM_STEP)
        v_hat = nv / (1.0 - ADAM_B2 ** ADAM_STEP)
        d_ref[...] = -ADAM_LR * (m_hat / (jnp.sqrt(v_hat) + ADAM_EPS) + ADAM_WD * w_ref[...])
        nm_ref[...] = nm
        nv_ref[...] = nv

    c = w.shape[1]
    return _rowwise(body, [w, g, m, v], [], [(c, F32)] * 3, tm=_tile(w.shape[0], ROW_TILE), name=name)


def _mm_tn(a, g, name, slots=1, after=None):
    t, ka = a.shape
    n = g.shape[1]
    ta, tn, tk = _tile(ka, TN_TILE), _tile(n // slots, TN_TILE), _tile(t, TN_TOKENS)
    per = n // slots // tn

    order = [] if after is None else [after]

    def kern(a_ref, g_ref, *refs):
        o_ref = refs[-1]

        @pl.when(pl.program_id(2) == 0)
        def _():
            o_ref[...] = jnp.zeros_like(o_ref)

        o_ref[0] += _dot_tn(a_ref[...], g_ref[...])

    return pl.pallas_call(
        kern, grid=(ka // ta, n // tn, t // tk),
        in_specs=[pl.BlockSpec((tk, ta), lambda i, j, k: (k, i)), pl.BlockSpec((tk, tn), lambda i, j, k: (k, j))] + [ANY] * len(order),
        out_specs=pl.BlockSpec((1, ta, tn), lambda i, j, k: (j // per, i, j % per)), out_shape=SDS((slots, ka, n // slots), F32),
        compiler_params=_params("arbitrary", "arbitrary", "arbitrary"), name=name)(a, g, *order)


def _mm_tn_cols(a, g_list, name):
    t, ka = a.shape
    widths = [g.shape[1] for g in g_list]
    offs = [sum(widths[:k]) for k in range(len(widths))]
    ta, tk = _tile(ka, ROW_TILE), _tile(t, TN_TOKENS)

    def kern(a_ref, *refs):
        g_refs, o_ref = refs[:-1], refs[-1]

        @pl.when(pl.program_id(1) == 0)
        def _():
            o_ref[...] = jnp.zeros_like(o_ref)

        a_ = a_ref[...]
        for g_ref, off, n in zip(g_refs, offs, widths):
            o_ref[:, off:off + n] += _dot_tn(a_, g_ref[...])

    return pl.pallas_call(
        kern, grid=(ka // ta, t // tk),
        in_specs=[pl.BlockSpec((tk, ta), lambda i, k: (k, i))] + [pl.BlockSpec((tk, n), lambda i, k: (k, 0)) for n in widths],
        out_specs=pl.BlockSpec((ta, sum(widths)), lambda i, k: (i, 0)), out_shape=SDS((ka, sum(widths)), F32),
        compiler_params=_params("arbitrary", "arbitrary"), name=name)(a, *g_list)


def _mm_tn_rows(a_list, g, name):
    t, n = g.shape
    widths = [a.shape[1] for a in a_list]
    offs = [sum(widths[:k]) for k in range(len(widths))]
    tn, tk = _tile(n, TN_TILE), _tile(t, TN_TOKENS)

    def kern(*refs):
        a_refs, g_ref, o_ref = refs[:-2], refs[-2], refs[-1]

        @pl.when(pl.program_id(1) == 0)
        def _():
            o_ref[...] = jnp.zeros_like(o_ref)

        g_ = g_ref[...]
        for a_ref, off, ka in zip(a_refs, offs, widths):
            o_ref[off:off + ka, :] += _dot_tn(a_ref[...], g_)

    return pl.pallas_call(
        kern, grid=(n // tn, t // tk),
        in_specs=[pl.BlockSpec((tk, ka), lambda j, k: (k, 0)) for ka in widths] + [pl.BlockSpec((tk, tn), lambda j, k: (k, j))],
        out_specs=pl.BlockSpec((sum(widths), tn), lambda j, k: (0, j)), out_shape=SDS((sum(widths), n), F32),
        compiler_params=_params("arbitrary", "arbitrary"), name=name)(*a_list, g)


def _conv_geometry(seq, taps):
    pad = _round_up(taps - 1, SUBLANES)
    rc = _tile(seq, CONV_ROWS)
    assert rc % pad == 0 and seq % rc == 0
    return pad, rc, seq // rc


def _chunk_spec(rc, n, nch):
    return pl.BlockSpec((rc, n), lambda b, i: (b * nch + i, 0))


def _prev_halo_spec(pad, rc, n, nch):
    per = rc // pad
    return pl.BlockSpec((pad, n), lambda b, i: (jnp.maximum((b * nch + i) * per - 1, 0), 0))


def _next_halo_spec(pad, rc, n, nch, total_rows):
    per = rc // pad
    last = total_rows // pad - 1
    return pl.BlockSpec((pad, n), lambda b, i: (jnp.minimum((b * nch + i + 1) * per, last), 0))


def _whole2(a):
    return pl.BlockSpec(a.shape, lambda b, i, nd=len(a.shape): (0,) * nd)


def _sub_rows(rc):
    return SUB_ROWS if rc % SUB_ROWS == 0 else rc


def _build_shifts(sh_ref, src_ref, offsets):
    rows = src_ref.shape[0]
    for r in sorted({o % SUBLANES for o in offsets} - {0}):
        sh_ref[r, 0:rows - SUBLANES, :] = src_ref[r:r + rows - SUBLANES, :]


def _read_shifted(sh_ref, src_ref, o, s0, sub):
    r = o % SUBLANES
    a = o - r + s0
    return src_ref[a:a + sub, :] if r == 0 else sh_ref[r, a:a + sub, :]


def _tap_sum(sh_ref, src_ref, w_ref, offsets, s0, sub):
    acc = None
    for k, o in enumerate(offsets):
        term = _read_shifted(sh_ref, src_ref, o, s0, sub) * w_ref[k:k + 1, :]
        acc = term if acc is None else acc + term
    return acc


def _row_groups(x):
    acc = x[0:SUBLANES, :]
    for g0 in range(SUBLANES, x.shape[0], SUBLANES):
        acc = acc + x[g0:g0 + SUBLANES, :]
    return acc


def _mix_a_fwd(a_val, a_gate, conv_w, conv_b, ln_g, ln_b, batch, name):
    t, da = a_val.shape
    taps = conv_w.shape[0]
    pad, rc, nch = _conv_geometry(t // batch, taps)
    sub = _sub_rows(rc)
    offs = [pad - (taps - 1) + k for k in range(taps)]

    def kern(av_ref, ag_ref, pav_ref, pag_ref, w_ref, cb_ref, g_ref, b_ref, a_ref, ac_ref, pad_ref, sh_ref):
        first = pl.program_id(1) == 0
        pad_ref[0:pad, :] = jnp.where(first, 0.0, pav_ref[...] * _sigmoid(pag_ref[...]))
        for s0 in range(0, rc, sub):
            pad_ref[pad + s0:pad + s0 + sub, :] = av_ref[s0:s0 + sub, :] * _sigmoid(ag_ref[s0:s0 + sub, :])
        _build_shifts(sh_ref, pad_ref, offs)
        for s0 in range(0, rc, sub):
            ac = _tap_sum(sh_ref, pad_ref, w_ref, offs, s0, sub) + cb_ref[...]
            ac_ref[s0:s0 + sub, :] = ac
            y, _ = _ln_stats(ac)
            aln = y * g_ref[...] + b_ref[...]
            a_ref[s0:s0 + sub, :] = (aln * _sigmoid(aln)).astype(BF16)

    chunk, halo = _chunk_spec(rc, da, nch), _prev_halo_spec(pad, rc, da, nch)
    return pl.pallas_call(
        kern, grid=(batch, nch),
        in_specs=[chunk, chunk, halo, halo, _whole2(conv_w), _whole2(conv_b), _whole2(ln_g), _whole2(ln_b)],
        out_specs=[chunk, chunk], out_shape=[SDS((t, da), BF16), SDS((t, da), F32)],
        scratch_shapes=[pltpu.VMEM((pad + rc, da), F32), pltpu.VMEM((SUBLANES, pad + rc, da), F32)],
        compiler_params=_params("arbitrary", "arbitrary"), name=name)(a_val, a_gate, a_val, a_gate, conv_w, conv_b, ln_g, ln_b)


def _mix_a_bwd(a_val, a_gate, ac, da_, conv_w, ln_g, ln_b, batch, name):
    t, da = a_val.shape
    taps = conv_w.shape[0]
    pad, rc, nch = _conv_geometry(t // batch, taps)
    sub = _sub_rows(rc)
    offs_in = [pad - (taps - 1) + k for k in range(taps)]
    offs_out = [taps - 1 - k for k in range(taps)]

    def kern(av_ref, ag_ref, pav_ref, pag_ref, ac_ref, da_ref, nac_ref, nda_ref, w_ref, g_ref, b_ref,
             dav_ref, dag_ref, dw_ref, dcb_ref, dg_ref, db_ref, gpad_ref, dpad_ref, shg_ref, shd_ref, dwacc_ref):
        i = pl.program_id(1)
        start = (pl.program_id(0) == 0) & (i == 0)
        end = (pl.program_id(0) == batch - 1) & (i == nch - 1)

        @pl.when(start)
        def _():
            for r in (dcb_ref, dg_ref, db_ref, dwacc_ref):
                r[...] = jnp.zeros_like(r)

        gain, bias = g_ref[...], b_ref[...]

        def d_conv_out(ac_, dout):
            y, rstd = _ln_stats(ac_)
            aln = y * gain + bias
            sig = _sigmoid(aln)
            daln = dout * (sig * (1.0 + aln * (1.0 - sig)))
            return _ln_bwd(daln, y, rstd, gain), daln, y

        gpad_ref[0:pad, :] = jnp.where(i == 0, 0.0, pav_ref[...] * _sigmoid(pag_ref[...]))
        dac_next, _, _ = d_conv_out(nac_ref[...], nda_ref[...])
        dpad_ref[rc:rc + pad, :] = jnp.where(i == nch - 1, 0.0, dac_next)
        for s0 in range(0, rc, sub):
            rows = slice(s0, s0 + sub)
            dac, daln, y = d_conv_out(ac_ref[rows, :], da_ref[rows, :])
            dg_ref[...] += _colsum(daln * y)
            db_ref[...] += _colsum(daln)
            dcb_ref[...] += _colsum(dac)
            dpad_ref[rows, :] = dac
            gpad_ref[pad + s0:pad + s0 + sub, :] = av_ref[rows, :] * _sigmoid(ag_ref[rows, :])
        _build_shifts(shg_ref, gpad_ref, offs_in)
        _build_shifts(shd_ref, dpad_ref, offs_out)
        for s0 in range(0, rc, sub):
            rows = slice(s0, s0 + sub)
            dac = dpad_ref[rows, :]
            for k, o in enumerate(offs_in):
                dwacc_ref[k] += _row_groups(dac * _read_shifted(shg_ref, gpad_ref, o, s0, sub))
            dgl = _tap_sum(shd_ref, dpad_ref, w_ref, offs_out, s0, sub)
            av = av_ref[rows, :]
            sig = _sigmoid(ag_ref[rows, :])
            dav_ref[rows, :] = (dgl * sig).astype(BF16)
            dag_ref[rows, :] = (dgl * av * sig * (1.0 - sig)).astype(BF16)

        @pl.when(end)
        def _():
            for k in range(taps):
                dw_ref[k:k + 1, :] = _colsum(dwacc_ref[k])

    chunk, prev = _chunk_spec(rc, da, nch), _prev_halo_spec(pad, rc, da, nch)
    nxt = _next_halo_spec(pad, rc, da, nch, t)
    vec = SDS((1, da), F32)
    return pl.pallas_call(
        kern, grid=(batch, nch),
        in_specs=[chunk, chunk, prev, prev, chunk, chunk, nxt, nxt, _whole2(conv_w), _whole2(ln_g), _whole2(ln_b)],
        out_specs=[chunk, chunk, _whole2(conv_w), _whole2(vec), _whole2(vec), _whole2(vec)],
        out_shape=[SDS((t, da), BF16), SDS((t, da), BF16), SDS(conv_w.shape, F32), vec, vec, vec],
        scratch_shapes=[pltpu.VMEM((pad + rc, da), F32), pltpu.VMEM((rc + pad, da), F32),
                        pltpu.VMEM((SUBLANES, pad + rc, da), F32), pltpu.VMEM((SUBLANES, rc + pad, da), F32),
                        pltpu.VMEM((taps, SUBLANES, da), F32)],
        compiler_params=_params("arbitrary", "arbitrary"), name=name)(
            a_val, a_gate, a_val, a_gate, ac, da_, ac, da_, conv_w, ln_g, ln_b)


def _mix_c_fwd(c_b, c_c, c_x, conv_w, batch, name):
    t, dc = c_b.shape
    taps = conv_w.shape[0]
    pad, rc, nch = _conv_geometry(t // batch, taps)
    sub = _sub_rows(rc)
    offs = [pad - (taps - 1) + k for k in range(taps)]

    def kern(cb_ref, cc_ref, cx_ref, pcc_ref, pcx_ref, w_ref, o_ref, pad_ref, sh_ref):
        pad_ref[0:pad, :] = jnp.where(pl.program_id(1) == 0, 0.0, pcc_ref[...] * pcx_ref[...])
        for s0 in range(0, rc, sub):
            pad_ref[pad + s0:pad + s0 + sub, :] = cc_ref[s0:s0 + sub, :] * cx_ref[s0:s0 + sub, :]
        _build_shifts(sh_ref, pad_ref, offs)
        for s0 in range(0, rc, sub):
            o_ref[s0:s0 + sub, :] = (cb_ref[s0:s0 + sub, :] * _tap_sum(sh_ref, pad_ref, w_ref, offs, s0, sub)).astype(BF16)

    chunk, prev = _chunk_spec(rc, dc, nch), _prev_halo_spec(pad, rc, dc, nch)
    return pl.pallas_call(
        kern, grid=(batch, nch), in_specs=[chunk, chunk, chunk, prev, prev, _whole2(conv_w)],
        out_specs=chunk, out_shape=SDS((t, dc), BF16),
        scratch_shapes=[pltpu.VMEM((pad + rc, dc), F32), pltpu.VMEM((SUBLANES, pad + rc, dc), F32)],
        compiler_params=_params("arbitrary", "arbitrary"), name=name)(c_b, c_c, c_x, c_c, c_x, conv_w)


def _mix_c_bwd(c_b, c_c, c_x, dco, conv_w, batch, name):
    t, dc = c_b.shape
    taps = conv_w.shape[0]
    pad, rc, nch = _conv_geometry(t // batch, taps)
    sub = _sub_rows(rc)
    offs_in = [pad - (taps - 1) + k for k in range(taps)]
    offs_out = [taps - 1 - k for k in range(taps)]

    def kern(cb_ref, cc_ref, cx_ref, do_ref, pcc_ref, pcx_ref, ncb_ref, ndo_ref, w_ref,
             dcb_ref, dcc_ref, dcx_ref, dw_ref, ppad_ref, dpad_ref, shp_ref, shd_ref, dwacc_ref):
        i = pl.program_id(1)

        @pl.when((pl.program_id(0) == 0) & (i == 0))
        def _():
            dwacc_ref[...] = jnp.zeros_like(dwacc_ref)

        ppad_ref[0:pad, :] = jnp.where(i == 0, 0.0, pcc_ref[...] * pcx_ref[...])
        dpad_ref[rc:rc + pad, :] = jnp.where(i == nch - 1, 0.0, ndo_ref[...] * ncb_ref[...])
        for s0 in range(0, rc, sub):
            rows = slice(s0, s0 + sub)
            ppad_ref[pad + s0:pad + s0 + sub, :] = cc_ref[rows, :] * cx_ref[rows, :]
            dpad_ref[rows, :] = do_ref[rows, :] * cb_ref[rows, :]
        _build_shifts(shp_ref, ppad_ref, offs_in)
        _build_shifts(shd_ref, dpad_ref, offs_out)
        for s0 in range(0, rc, sub):
            rows = slice(s0, s0 + sub)
            dcv = dpad_ref[rows, :]
            cv = None
            for k, o in enumerate(offs_in):
                shifted = _read_shifted(shp_ref, ppad_ref, o, s0, sub)
                dwacc_ref[k] += _row_groups(dcv * shifted)
                cv = shifted * w_ref[k:k + 1, :] if cv is None else cv + shifted * w_ref[k:k + 1, :]
            dp = _tap_sum(shd_ref, dpad_ref, w_ref, offs_out, s0, sub)
            dcb_ref[rows, :] = (do_ref[rows, :] * cv).astype(BF16)
            dcc_ref[rows, :] = (dp * cx_ref[rows, :]).astype(BF16)
            dcx_ref[rows, :] = (dp * cc_ref[rows, :]).astype(BF16)

        @pl.when((pl.program_id(0) == batch - 1) & (i == nch - 1))
        def _():
            for k in range(taps):
                dw_ref[k:k + 1, :] = _colsum(dwacc_ref[k])

    chunk, prev = _chunk_spec(rc, dc, nch), _prev_halo_spec(pad, rc, dc, nch)
    nxt = _next_halo_spec(pad, rc, dc, nch, t)
    return pl.pallas_call(
        kern, grid=(batch, nch), in_specs=[chunk, chunk, chunk, chunk, prev, prev, nxt, nxt, _whole2(conv_w)],
        out_specs=[chunk, chunk, chunk, _whole2(conv_w)],
        out_shape=[SDS((t, dc), BF16)] * 3 + [SDS(conv_w.shape, F32)],
        scratch_shapes=[pltpu.VMEM((pad + rc, dc), F32), pltpu.VMEM((rc + pad, dc), F32),
                        pltpu.VMEM((SUBLANES, pad + rc, dc), F32), pltpu.VMEM((SUBLANES, rc + pad, dc), F32),
                        pltpu.VMEM((taps, SUBLANES, dc), F32)],
        compiler_params=_params("arbitrary", "arbitrary"), name=name)(c_b, c_c, c_x, dco, c_c, c_x, c_b, dco, conv_w)


def _head_of_lane(db):
    return lax.broadcasted_iota(jnp.int32, (1, db), 1) // (db // B_HEADS)


def _tril(rows_ge_cols=True):
    r = lax.broadcasted_iota(jnp.int32, (CHUNK, CHUNK), 0)
    c = lax.broadcasted_iota(jnp.int32, (CHUNK, CHUNK), 1)
    return (r >= c) if rows_ge_cols else (r <= c)


def _spatial_mix(wm, vb, bias, head):
    mixed = bias
    for h in range(B_HEADS):
        mixed = mixed + jnp.where(head == h, _dot(wm[h], vb), 0.0)
    return mixed


def _mix_b_fwd(b_u, b_v, w_s, bias_full, ln_g, ln_b, name):
    t, db = b_u.shape
    rb = _tile(t, ROW_TILE)

    def kern(bu_ref, bv_ref, ws_ref, bias_ref, g_ref, b_ref, o_ref):
        head = _head_of_lane(db)
        wm = [jnp.where(_tril(), ws_ref[h], 0.0).astype(BF16) for h in range(B_HEADS)]
        for ch in range(rb // CHUNK):
            rows = slice(ch * CHUNK, (ch + 1) * CHUNK)
            y, _ = _ln_stats(_gelu(bv_ref[rows, :]))
            vb = (y * g_ref[...] + b_ref[...]).astype(BF16)
            mixed = _spatial_mix(wm, vb, bias_ref[...], head)
            o_ref[rows, :] = (_gelu(bu_ref[rows, :]) * mixed).astype(BF16)

    def whole(a):
        return pl.BlockSpec(a.shape, lambda i, nd=len(a.shape): (0,) * nd)

    tile = pl.BlockSpec((rb, db), lambda i: (i, 0))
    return pl.pallas_call(
        kern, grid=(t // rb,), in_specs=[tile, tile, whole(w_s), whole(bias_full), whole(ln_g), whole(ln_b)],
        out_specs=tile, out_shape=SDS((t, db), BF16), compiler_params=_params("arbitrary"), name=name)(
            b_u, b_v, w_s, bias_full, ln_g, ln_b)


def _mix_b_bwd(b_u, b_v, dbo, w_s, w_s_t, bias_full, ln_g, ln_b, name):
    t, db = b_u.shape
    rb = _tile(t, ROW_TILE)
    steps = t // rb

    def kern(bu_ref, bv_ref, do_ref, ws_ref, wst_ref, bias_ref, g_ref, b_ref,
             dbu_ref, dbv_ref, dws_ref, dbs_ref, dg_ref, dbeta_ref, dbias_ref):
        i = pl.program_id(0)

        @pl.when(i == 0)
        def _():
            for r in (dws_ref, dg_ref, dbeta_ref, dbias_ref):
                r[...] = jnp.zeros_like(r)

        head = _head_of_lane(db)
        gain = g_ref[...]
        wm = [jnp.where(_tril(), ws_ref[h], 0.0).astype(BF16) for h in range(B_HEADS)]
        wmt = [jnp.where(_tril(False), wst_ref[h], 0.0).astype(BF16) for h in range(B_HEADS)]
        for ch in range(rb // CHUNK):
            rows = slice(ch * CHUNK, (ch + 1) * CHUNK)
            bu, bv, dout = bu_ref[rows, :], bv_ref[rows, :], do_ref[rows, :]
            y, rstd = _ln_stats(_gelu(bv))
            vb = (y * gain + b_ref[...]).astype(BF16)
            mixed = _spatial_mix(wm, vb, bias_ref[...], head)
            du = dout * mixed
            dmixed = dout * _gelu(bu)
            dbias_ref[...] += dmixed
            dmb = dmixed.astype(BF16)
            dv = None
            for h in range(B_HEADS):
                dws_ref[h] += _dot_nt(jnp.where(head == h, dmb, 0.0).astype(BF16), vb)
                part = jnp.where(head == h, _dot(wmt[h], dmb), 0.0)
                dv = part if dv is None else dv + part
            dg_ref[...] += _colsum(dv * y)
            dbeta_ref[...] += _colsum(dv)
            dbv_ref[rows, :] = (_ln_bwd(dv, y, rstd, gain) * _gelu_grad(bv)).astype(BF16)
            dbu_ref[rows, :] = (du * _gelu_grad(bu)).astype(BF16)

        @pl.when(i == steps - 1)
        def _():
            for h in range(B_HEADS):
                dws_ref[h] = jnp.where(_tril(), dws_ref[h], 0.0)
                dbs_ref[h] = jnp.sum(jnp.where(head == h, dbias_ref[...], 0.0), axis=1, keepdims=True)

    def whole(a):
        return pl.BlockSpec(a.shape, lambda i, nd=len(a.shape): (0,) * nd)

    tile = pl.BlockSpec((rb, db), lambda i: (i, 0))
    vec = SDS((1, db), F32)
    dbs = SDS((B_HEADS, CHUNK, 1), F32)
    return pl.pallas_call(
        kern, grid=(steps,),
        in_specs=[tile, tile, tile, whole(w_s), whole(w_s_t), whole(bias_full), whole(ln_g), whole(ln_b)],
        out_specs=[tile, tile, whole(w_s), whole(dbs), whole(vec), whole(vec)],
        out_shape=[SDS((t, db), BF16), SDS((t, db), BF16), SDS(w_s.shape, F32), dbs, vec, vec],
        scratch_shapes=[pltpu.VMEM((CHUNK, db), F32)],
        compiler_params=_params("arbitrary"), name=name)(b_u, b_v, dbo, w_s, w_s_t, bias_full, ln_g, ln_b)


def _softmax_rows(s):
    e = jnp.exp(s - jnp.max(s, axis=-1, keepdims=True))
    return e / jnp.sum(e, axis=-1, keepdims=True)


def _attention_fwd(q, kv, batch, name):
    t, d = q.shape
    seq, mlen, hd = t // batch, kv.shape[0] // batch, d // X_HEADS
    ar = _tile(seq, ATT_ROWS)
    scale = hd ** -0.5

    def kern(q_ref, k_ref, v_ref, o_ref):
        k, v = k_ref[...], v_ref[...]
        for r0 in range(0, seq, ar):
            p = _softmax_rows(_dot_nt(q_ref[r0:r0 + ar, :], k) * scale)
            o_ref[r0:r0 + ar, :] = _dot(p.astype(BF16), v).astype(BF16)

    qs = pl.BlockSpec((seq, hd), lambda b, h: (b, h))
    return pl.pallas_call(
        kern, grid=(batch, X_HEADS),
        in_specs=[qs, pl.BlockSpec((mlen, hd), lambda b, h: (b, h)), pl.BlockSpec((mlen, hd), lambda b, h: (b, X_HEADS + h))],
        out_specs=qs, out_shape=SDS((t, d), BF16), compiler_params=_params("arbitrary", "arbitrary"), name=name)(q, kv, kv)


def _attention_bwd(q, kv, do, batch, name):
    t, d = q.shape
    seq, mlen, hd = t // batch, kv.shape[0] // batch, d // X_HEADS
    ar = _tile(seq, ATT_ROWS)
    scale = hd ** -0.5

    def kern(q_ref, k_ref, v_ref, do_ref, dq_ref, dk_ref, dv_ref):
        k, v = k_ref[...], v_ref[...]
        dk = jnp.zeros((mlen, hd), F32)
        dv = jnp.zeros((mlen, hd), F32)
        for r0 in range(0, seq, ar):
            qr, dor = q_ref[r0:r0 + ar, :], do_ref[r0:r0 + ar, :]
            p = _softmax_rows(_dot_nt(qr, k) * scale)
            dp = _dot_nt(dor, v)
            ds = (p * (dp - jnp.sum(p * dp, axis=-1, keepdims=True)) * scale).astype(BF16)
            dq_ref[r0:r0 + ar, :] = _dot(ds, k).astype(BF16)
            dk = dk + _dot_tn(ds, qr)
            dv = dv + _dot_tn(p.astype(BF16), dor)
        dk_ref[...] = dk.astype(BF16)
        dv_ref[...] = dv.astype(BF16)

    qs = pl.BlockSpec((seq, hd), lambda b, h: (b, h))
    ks = pl.BlockSpec((mlen, hd), lambda b, h: (b, h))
    dkv = SDS((kv.shape[0], d), BF16)
    return pl.pallas_call(
        kern, grid=(batch, X_HEADS),
        in_specs=[qs, ks, pl.BlockSpec((mlen, hd), lambda b, h: (b, X_HEADS + h)), qs],
        out_specs=[qs, ks, ks], out_shape=[SDS((t, d), BF16), dkv, dkv],
        compiler_params=_params("arbitrary", "arbitrary"), name=name)(q, kv, kv, do)


def _place():
    x, y, c = lax.axis_index("x"), lax.axis_index("y"), lax.axis_index("c")
    other_chips = [(1 - x, y), (x, 1 - y), (1 - x, 1 - y)]
    return x, y, c, other_chips


def _comm_call(kern, out_shape, n_pairs, name, *args):
    return pl.pallas_call(
        kern, out_shape=out_shape, in_specs=[ANY] * len(args), out_specs=jax.tree.map(lambda _: ANY, out_shape),
        scratch_shapes=[pltpu.SemaphoreType.DMA((n_pairs,)), pltpu.SemaphoreType.DMA((n_pairs,)), pltpu.SemaphoreType.DMA((n_pairs,))],
        name=name)(*args)


def _all_gather_chips(shard, name):
    r, cols = shard.shape
    rh = r // 2

    def kern(s_ref, o_ref, send_sems, recv_sems, local_sems):
        x, y, c, chips = _place()
        mine_slot = 2 * x + y
        half = pl.ds(c * rh, rh)
        other_half = pl.ds((1 - c) * rh, rh)

        def copy(k, src, dst, to):
            return pltpu.make_async_remote_copy(src_ref=src, dst_ref=dst, send_sem=send_sems.at[k], recv_sem=recv_sems.at[k],
                                                device_id=to, device_id_type=MESH)

        mine = pltpu.make_async_copy(s_ref, o_ref.at[mine_slot], local_sems.at[0])
        mine.start()
        first = [copy(j, s_ref.at[half], o_ref.at[mine_slot, half], (px, py, c)) for j, (px, py) in enumerate(chips)]
        for cp in first:
            cp.start()
        passed = []
        for j, (px, py) in enumerate(chips):
            landed = o_ref.at[2 * px + py, half]
            copy(j, landed, landed, (px, py, c)).wait_recv()
            cp = copy(3 + j, landed, landed, (x, y, 1 - c))
            cp.start()
            passed.append(cp)
        for j, (px, py) in enumerate(chips):
            theirs = o_ref.at[2 * px + py, other_half]
            copy(3 + j, theirs, theirs, (x, y, 1 - c)).wait_recv()
        for cp in first + passed:
            cp.wait_send()
        mine.wait()

    return _comm_call(kern, SDS((N_CHIPS, r, cols), shard.dtype), 6, name, shard)


def _join_halves(joined, name):
    n = len(joined)

    def kern(*refs):
        j_refs = refs[:n]
        send_sems, recv_sems = refs[2 * n:]
        x, y, c, _ = _place()
        copies = []
        for k, j_ref in enumerate(j_refs):
            rh = j_ref.shape[1] // 2
            rows = j_ref.at[:, pl.ds(c * rh, rh)]
            cp = pltpu.make_async_remote_copy(src_ref=rows, dst_ref=rows, send_sem=send_sems.at[k], recv_sem=recv_sems.at[k],
                                              device_id=(x, y, 1 - c), device_id_type=MESH)
            cp.start()
            copies.append(cp)
        for k, (cp, j_ref) in enumerate(zip(copies, j_refs)):
            rh = j_ref.shape[1] // 2
            theirs = j_ref.at[:, pl.ds((1 - c) * rh, rh)]
            pltpu.make_async_remote_copy(src_ref=theirs, dst_ref=theirs, send_sem=send_sems.at[k], recv_sem=recv_sems.at[k],
                                         device_id=(x, y, 1 - c), device_id_type=MESH).wait_recv()
            cp.wait_send()

    return pl.pallas_call(
        kern, out_shape=[SDS(j.shape, j.dtype) for j in joined], in_specs=[ANY] * n, out_specs=[ANY] * n,
        input_output_aliases={k: k for k in range(n)},
        scratch_shapes=[pltpu.SemaphoreType.DMA((n,)), pltpu.SemaphoreType.DMA((n,))], name=name)(*joined)


HBM = pl.BlockSpec(memory_space=pltpu.HBM)
SEMAPHORES = pl.BlockSpec(memory_space=pltpu.SEMAPHORE)
COPIES_PER_ARRAY = N_CHIPS - 1
GATHER_COPIES = N_CHIPS


def _in_hbm(a):
    return pltpu.with_memory_space_constraint(a, pltpu.HBM)


def _gather_copies(shard_refs, land_refs, send_sems, recv_sems):
    x, y, c, chips = _place()
    copies = []
    for k, (s_ref, land_ref) in enumerate(zip(shard_refs, land_refs)):
        half = pl.ds(c * (s_ref.shape[0] // 2), s_ref.shape[0] // 2)
        for j, (px, py) in enumerate(chips):
            n = GATHER_COPIES * k + j
            copies.append(pltpu.make_async_remote_copy(
                src_ref=s_ref.at[half], dst_ref=land_ref.at[2 * x + y, half], send_sem=send_sems.at[n], recv_sem=recv_sems.at[n],
                device_id=(px, py, c), device_id_type=MESH))
        n = GATHER_COPIES * k + N_CHIPS - 1
        copies.append(pltpu.make_async_remote_copy(
            src_ref=s_ref, dst_ref=land_ref.at[2 * x + y], send_sem=send_sems.at[n], recv_sem=recv_sems.at[n],
            device_id=(x, y, 1 - c), device_id_type=MESH))
    return copies


def _swap_copies(full_refs, land_refs, send_sems, recv_sems):
    x, y, c, _ = _place()
    copies = []
    for k, (g_ref, land_ref) in enumerate(zip(full_refs, land_refs)):
        rh = land_ref.shape[1]
        copies.append(pltpu.make_async_remote_copy(
            src_ref=g_ref.at[:, pl.ds((1 - c) * rh, rh), :], dst_ref=land_ref, send_sem=send_sems.at[k], recv_sem=recv_sems.at[k],
            device_id=(x, y, 1 - c), device_id_type=MESH))
    return copies


def _owner_copies(sum_refs, land_refs, send_sems, recv_sems):
    x, y, c, chips = _place()
    copies = []
    for k, (s_ref, land_ref) in enumerate(zip(sum_refs, land_refs)):
        for j, (px, py) in enumerate(chips):
            n = COPIES_PER_ARRAY * k + j
            copies.append(pltpu.make_async_remote_copy(
                src_ref=s_ref.at[2 * px + py], dst_ref=land_ref.at[j], send_sem=send_sems.at[n], recv_sem=recv_sems.at[n],
                device_id=(px, py, c), device_id_type=MESH))
    return copies


def _copies_start(build, per_array, sources, lands, name, after=None):
    ns, nb = len(sources), len(sources) + len(lands)
    n_copies = per_array * ns
    order = [] if after is None else [after]
    n_in = nb + len(order)

    def kern(*refs):
        for cp in build(refs[:ns], refs[ns:nb], refs[n_in + nb], refs[n_in + nb + 1]):
            cp.start()
        refs[-1][...] = jnp.zeros_like(refs[-1])

    bufs = [*sources, *lands]
    res = pl.pallas_call(
        kern, name=name,
        out_shape=(*[pltpu.HBM(b.shape, b.dtype) for b in bufs], pltpu.SemaphoreType.DMA((n_copies,)), pltpu.SemaphoreType.DMA((n_copies,)),
                   SDS((SUBLANES, 128), F32)),
        in_specs=[HBM] * nb + [ANY] * len(order),
        out_specs=(*[HBM] * nb, SEMAPHORES, SEMAPHORES, pl.BlockSpec(memory_space=pltpu.VMEM)),
        input_output_aliases={i: i for i in range(nb)},
        compiler_params=pltpu.CompilerParams(has_side_effects=pltpu.SideEffectType.DATAFLOW_SIDE_EFFECTING),
    )(*[_in_hbm(b) for b in bufs], *order)
    return res[nb], res[nb + 1], list(res[:nb]), res[-1]


def _copies_wait(build, send_sems, recv_sems, bufs, ns, after, name):
    nb = len(bufs)

    def kern(*refs):
        for cp in build(refs[:ns], refs[ns:nb], refs[nb], refs[nb + 1]):
            cp.wait_send()
            cp.wait_recv()

    res = pl.pallas_call(
        kern, name=name, out_shape=tuple(pltpu.HBM(b.shape, b.dtype) for b in bufs),
        in_specs=[HBM] * nb + [SEMAPHORES, SEMAPHORES, ANY], out_specs=tuple([HBM] * nb),
        input_output_aliases={i: i for i in range(nb)},
        compiler_params=pltpu.CompilerParams(has_side_effects=pltpu.SideEffectType.DATAFLOW_SIDE_EFFECTING),
    )(*bufs, send_sems, recv_sems, after)
    return list(res[:ns]), list(res[ns:])


def _gather_finish(lands, name):
    ns = len(lands)

    def kern(*refs):
        l_refs = refs[:ns]
        send_sems, recv_sems = refs[2 * ns:]
        x, y, c, chips = _place()
        passed = []
        for k, l_ref in enumerate(l_refs):
            rh = l_ref.shape[1] // 2
            for j, (px, py) in enumerate(chips):
                landed = l_ref.at[2 * px + py, pl.ds(c * rh, rh)]
                cp = pltpu.make_async_remote_copy(
                    src_ref=landed, dst_ref=landed, send_sem=send_sems.at[COPIES_PER_ARRAY * k + j],
                    recv_sem=recv_sems.at[COPIES_PER_ARRAY * k + j], device_id=(x, y, 1 - c), device_id_type=MESH)
                cp.start()
                passed.append(cp)
        for k, l_ref in enumerate(l_refs):
            rh = l_ref.shape[1] // 2
            for j, (px, py) in enumerate(chips):
                theirs = l_ref.at[2 * px + py, pl.ds((1 - c) * rh, rh)]
                pltpu.make_async_remote_copy(
                    src_ref=theirs, dst_ref=theirs, send_sem=send_sems.at[COPIES_PER_ARRAY * k + j],
                    recv_sem=recv_sems.at[COPIES_PER_ARRAY * k + j], device_id=(x, y, 1 - c), device_id_type=MESH).wait_recv()
        for cp in passed:
            cp.wait_send()

    return pl.pallas_call(
        kern, out_shape=[SDS(l.shape, l.dtype) for l in lands], in_specs=[ANY] * ns, out_specs=[ANY] * ns,
        input_output_aliases={k: k for k in range(ns)},
        scratch_shapes=[pltpu.SemaphoreType.DMA((COPIES_PER_ARRAY * ns,)), pltpu.SemaphoreType.DMA((COPIES_PER_ARRAY * ns,))],
        name=name)(*lands)


def _add_sibling(full, theirs, out_dtype, name):
    n, r, cols = full.shape
    rh = r // 2
    tr = _tile(rh, ROW_TILE)
    nb = rh // tr

    def kern(c_ref, a_ref, b_ref, o_ref):
        o_ref[...] = (a_ref[...] + b_ref[...]).astype(out_dtype)

    c = lax.axis_index("c").astype(jnp.int32).reshape(1)
    return pl.pallas_call(
        kern, out_shape=SDS((n, rh, cols), out_dtype),
        grid_spec=pltpu.PrefetchScalarGridSpec(
            num_scalar_prefetch=1, grid=(n, nb),
            in_specs=[pl.BlockSpec((1, tr, cols), lambda j, i, c_ref: (j, c_ref[0] * nb + i, 0)),
                      pl.BlockSpec((1, tr, cols), lambda j, i, c_ref: (j, i, 0))],
            out_specs=pl.BlockSpec((1, tr, cols), lambda j, i, c_ref: (j, i, 0))),
        compiler_params=_params("arbitrary", "arbitrary"), name=name)(c, full, theirs)


def _add_owners(partial, received, name, into=None, layer=0, depth=1):
    n, rh, cols = partial.shape
    tr = _tile(rh, ROW_TILE)
    nb = rh // tr
    stacked = [] if into is None else [into]

    def kern(s_ref, a_ref, b_ref, *refs):
        acc = a_ref[0].astype(F32)
        for j in range(N_CHIPS - 1):
            acc = acc + b_ref[j].astype(F32)
        refs[-1][0] = acc

    place = jnp.stack([2 * lax.axis_index("x") + lax.axis_index("y"), lax.axis_index("c")]).astype(jnp.int32)
    return pl.pallas_call(
        kern, out_shape=SDS((depth, 2 * rh, cols), F32),
        grid_spec=pltpu.PrefetchScalarGridSpec(
            num_scalar_prefetch=1, grid=(nb,),
            in_specs=[pl.BlockSpec((1, tr, cols), lambda i, s_ref: (s_ref[0], i, 0)),
                      pl.BlockSpec((N_CHIPS - 1, tr, cols), lambda i, s_ref: (0, i, 0))] + [ANY] * len(stacked),
            out_specs=pl.BlockSpec((1, tr, cols), lambda i, s_ref: (layer, s_ref[1] * nb + i, 0))),
        input_output_aliases={3: 0} if stacked else {},
        compiler_params=_params("arbitrary"), name=name)(place, partial, received, *stacked)


def _pack(parts, lead):
    lead_shape = parts[0].shape[:lead]
    flat = jnp.concatenate([p.reshape(lead_shape + (-1,)) for p in parts], axis=-1)
    n = flat.shape[-1]
    rows = _round_up(-(-n // PACK_W), PACK_ROWS)
    flat = jnp.pad(flat, [(0, 0)] * lead + [(0, rows * PACK_W - n)])
    return flat.reshape(lead_shape + (rows, PACK_W))


def _unpack(buf, shapes, lead):
    lead_shape = buf.shape[:lead]
    flat = buf.reshape(lead_shape + (-1,))
    out, off = [], 0
    for s in shapes:
        n = math.prod(s)
        out.append(flat[..., off:off + n].reshape(lead_shape + tuple(s)))
        off += n
    return out


def _to_slots(full, axis):
    s = full.shape
    split = full.reshape(s[:axis] + (N_CHIPS, s[axis] // N_CHIPS) + s[axis + 1:])
    return jnp.moveaxis(split, axis, 0)


def _from_slots(slots, axis):
    moved = jnp.moveaxis(slots, 0, axis)
    s = moved.shape
    return moved.reshape(s[:axis] + (s[axis] * s[axis + 1],) + s[axis + 2:])


MATMUL_WEIGHTS = (("w_in", 1), ("w_out", 0), ("w_q", 0), ("w_kv", 1), ("w_o", 0), ("w_ff1", 1), ("w_ff2", 0))
CONV_WEIGHTS = (("conv_a_w", 1), ("conv_c_w", 1))
REPLICATED = ("conv_a_b", "ln_a_g", "ln_a_b", "ln_v_g", "ln_v_b", "w_s", "b_s",
              "ln1_g", "ln1_b", "ln2_g", "ln2_b", "ln3_g", "ln3_b")
WEIGHT_ORDER = ("w_in", "conv_a_w", "conv_a_b", "ln_a_g", "ln_a_b", "ln_v_g", "ln_v_b", "w_s", "b_s", "conv_c_w", "w_out",
                "ln1_g", "ln1_b", "w_q", "w_kv", "w_o", "ln2_g", "ln2_b", "w_ff1", "w_ff2", "ln3_g", "ln3_b")


def _row(v):
    return v.reshape(1, -1)


def kernel(x, mem, w_in, conv_a_w, conv_a_b, ln_a_g, ln_a_b, ln_v_g, ln_v_b, w_s, b_s, conv_c_w, w_out, ln1_g, ln1_b, w_q, w_kv, w_o, ln2_g, ln2_b, w_ff1, w_ff2, ln3_g, ln3_b, loss_target, m_w_in, m_conv_a_w, m_conv_a_b, m_ln_a_g, m_ln_a_b, m_ln_v_g, m_ln_v_b, m_w_s, m_b_s, m_conv_c_w, m_w_out, m_ln1_g, m_ln1_b, m_w_q, m_w_kv, m_w_o, m_ln2_g, m_ln2_b, m_w_ff1, m_w_ff2, m_ln3_g, m_ln3_b, v_w_in, v_conv_a_w, v_conv_a_b, v_ln_a_g, v_ln_a_b, v_ln_v_g, v_ln_v_b, v_w_s, v_b_s, v_conv_c_w, v_w_out, v_ln1_g, v_ln1_b, v_w_q, v_w_kv, v_w_o, v_ln2_g, v_ln2_b, v_w_ff1, v_w_ff2, v_ln3_g, v_ln3_b):
    given = dict(locals())
    weights = {n: given[n] for n in WEIGHT_ORDER}
    moment1 = {n: given["m_" + n] for n in WEIGHT_ORDER}
    moment2 = {n: given["v_" + n] for n in WEIGHT_ORDER}

    depth = w_in.shape[0]
    batch, seq, d = x.shape
    t = batch * seq
    hd = d // HEADS
    d_a, d_b, d_c = A_HEADS * hd, B_HEADS * hd, C_HEADS * hd
    widths = (d_a, d_a, d_b, d_b, d_c, d_c, d_c)
    in_offs = [sum(widths[:k]) for k in range(len(widths))]
    alpha = (2.0 * depth) ** 0.25
    layers = range(depth)
    tm = _tile(t, MATMUL_ROW_TILE)
    tm_wide = _tile(t, WIDE_ROW_TILE)

    conv_shards = [weights[n][l] for l in layers for n, _ in CONV_WEIGHTS]
    names = [n for n, _ in MATMUL_WEIGHTS]
    first_used = names[:1]
    parts = {l: [first_used, [n for n in names if n not in first_used]] if l == 0 else [names] for l in layers}
    axis_of = dict(MATMUL_WEIGHTS)
    groups = {}

    def gather_start(l, p, after):
        part = parts[l][p]
        shards = [weights[n][l].astype(BF16) for n in part] + ([_pack(conv_shards, 0)] if (l, p) == (0, 0) else [])
        lands = [lax.empty((N_CHIPS,) + s.shape, s.dtype) for s in shards]
        send_sems, recv_sems, bufs, token = _copies_start(_gather_copies, GATHER_COPIES, shards, lands, f"gather_start_{l}_{p}", after)
        groups[l, p] = (part, send_sems, recv_sems, bufs, len(shards))
        return token

    sequence = [(l, p) for l in layers for p in range(len(parts[l]))]

    def gathered(l, p, after):
        part, send_sems, recv_sems, bufs, ns = groups[l, p]
        _, lands = _copies_wait(_gather_copies, send_sems, recv_sems, bufs, ns, after, f"gather_wait_{l}_{p}")
        slots = _gather_finish(lands, "gather_finish")
        nxt = sequence.index((l, p)) + 1
        started = gather_start(*sequence[nxt], slots[0]) if nxt < len(sequence) else None
        return {n: _from_slots(slots[k], axis_of[n]) for k, n in enumerate(part)}, slots[len(part):], started

    xf = x.reshape(t, d)
    xb = (xf + gather_start(0, 0, None)[0, 0]).astype(BF16)
    memb = mem.reshape(-1, d).astype(BF16)
    saved, full, conv_full = [], [], None
    stream = (xf,)
    for l in layers:
        w, extra, anchor = gathered(l, 0, xb)
        if l == 0:
            conv_full = _unpack(extra[0], [s.shape for s in conv_shards], 1)
        for k, (n, axis) in enumerate(CONV_WEIGHTS):
            w[n] = _from_slots(conv_full[l * len(CONV_WEIGHTS) + k], axis)
        for n in REPLICATED:
            w[n] = weights[n][l]
        w["w_s_t"] = jnp.swapaxes(w["w_s"], 1, 2)
        w["bias_full"] = jnp.repeat(w["b_s"].T, hd, axis=1)
        full.append(w)
        s = {"x0b": xb}
        proj = _in_proj(xb, w["w_in"], widths, "in_proj", after=anchor)
        s["proj"] = proj
        a_val, a_gate, b_u, b_v, c_b, c_c, c_x = proj
        a_out, s["ac"] = _mix_a_fwd(a_val, a_gate, w["conv_a_w"], _row(w["conv_a_b"]), _row(w["ln_a_g"]), _row(w["ln_a_b"]),
                                    batch, "mix_a_fwd")
        b_out = _mix_b_fwd(b_u, b_v, w["w_s"], w["bias_full"], _row(w["ln_v_g"]), _row(w["ln_v_b"]), "mix_b_fwd")
        c_out = _mix_c_fwd(c_b, c_c, c_x, w["conv_c_w"], batch, "mix_c_fwd")
        s["cat"] = (a_out, b_out, c_out)
        anchor = None
        for p in range(1, len(parts[l])):
            more, _, anchor = gathered(l, p, c_out)
            w.update(more)
        s["z1"], xb = _proj_ln([a_out, b_out, c_out], w["w_out"], stream, _row(w["ln1_g"]), _row(w["ln1_b"]), alpha, tm, "out_proj_ln",
                               after=anchor)
        stream = (s["z1"], _row(w["ln1_g"]), _row(w["ln1_b"]))
        s["x1b"] = xb
        s["q"] = _matmul(xb, w["w_q"], "q_proj")
        s["kv"] = _matmul(memb, w["w_kv"], "kv_proj")
        s["o"] = _attention_fwd(s["q"], s["kv"], batch, "attention_fwd")
        s["z2"], xb = _proj_ln([s["o"]], w["w_o"], stream, _row(w["ln2_g"]), _row(w["ln2_b"]), alpha, tm, "o_proj_ln")
        stream = (s["z2"], _row(w["ln2_g"]), _row(w["ln2_b"]))
        s["x2b"] = xb
        s["h"], s["r"] = _ff1(xb, w["w_ff1"], "ff1")
        s["z3"], xb = _proj_ln([s["r"]], w["w_ff2"], stream, _row(w["ln3_g"]), _row(w["ln3_b"]), alpha, tm_wide, "ff2_ln")
        stream = (s["z3"], _row(w["ln3_g"]), _row(w["ln3_b"]))
        saved.append(s)

    target = loss_target.reshape(t, d)
    dz3, dz3b, top_g, top_b, loss_block = _loss_head_ln(saved[-1]["z3"], target, _row(full[-1]["ln3_g"]), _row(full[-1]["ln3_b"]),
                                                        "loss_head_ln3")
    loss = lax.psum(loss_block[0, 0], ("x", "y", "c"))
    grads = [{} for _ in layers]
    grads[-1]["ln3_g"], grads[-1]["ln3_b"] = top_g, top_b
    reductions = []
    swapping = {}
    token, last_started = 0.0, None

    def rows_to_slots(full_):
        return full_.reshape(N_CHIPS, full_.shape[1] // N_CHIPS, full_.shape[2])

    def swap_start(key, named, wire):
        arrays = [a for _, a in named]
        lands = [lax.empty((f.shape[0], f.shape[1] // 2, f.shape[2]), f.dtype) for f in arrays]
        send_sems, recv_sems, bufs, started = _copies_start(_swap_copies, 1, arrays, lands, f"rs_swap_start_{key}")
        swapping[key] = ([n for n, _ in named], send_sems, recv_sems, bufs, wire)
        return started

    def swap_finish(key, after):
        nonlocal token, last_started
        part, send_sems, recv_sems, bufs, wire = swapping.pop(key)
        fulls, theirs = _copies_wait(_swap_copies, send_sems, recv_sems, bufs, len(wire), after, f"rs_swap_wait_{key}")
        sums = [_add_sibling(f, th, dt, "rs_add_sibling") for f, th, dt in zip(fulls, theirs, wire)]
        lands = [lax.empty((N_CHIPS - 1,) + p.shape[1:], p.dtype) for p in sums]
        send_sems, recv_sems, bufs, started = _copies_start(_owner_copies, COPIES_PER_ARRAY, sums, lands, f"rs_send_start_{key}")
        reductions.append((key, part, send_sems, recv_sems, bufs))
        token, last_started = started[0, 0], started

    early = ("w_ff2", "w_ff1", "w_o", "w_kv", "w_q")
    swap_started, pending = None, None
    for l in reversed(layers):
        w, s, g = full[l], saved[l], grads[l]
        g["w_ff2"] = rows_to_slots(_mm_tn(s["r"], dz3b, "d_w_ff2", after=swap_started))
        dh = _bwd_ff2(dz3b, s["h"], w["w_ff2"], "d_ff_hidden", after=swap_started)
        if pending is not None:
            swap_finish(pending, dh)
        g["w_ff1"] = _mm_tn(s["x2b"], dh, "d_w_ff1", slots=N_CHIPS)
        dz2, dz2b, g["ln2_g"], g["ln2_b"] = _bwd_data_ln([dh], w["w_ff1"], [0], dz3, alpha, s["z2"], _row(w["ln2_g"]) + token,
                                                         tm_wide, "d_x2_ln2")
        g["w_o"] = rows_to_slots(_mm_tn(s["o"], dz2b, "d_w_o"))
        do = _bwd_data([dz2b], w["w_o"], [0], None, alpha, BF16, tm, "d_att_out")
        dq, dk, dv = _attention_bwd(s["q"], s["kv"], do, batch, "attention_bwd")
        g["w_kv"] = _mm_tn(memb, jnp.concatenate([dk, dv], axis=1), "d_w_kv", slots=N_CHIPS)
        g["w_q"] = rows_to_slots(_mm_tn(s["x1b"], dq, "d_w_q"))
        dz1, dz1b, g["ln1_g"], g["ln1_b"] = _bwd_data_ln([dq], w["w_q"], [0], dz2, alpha, s["z1"], _row(w["ln1_g"]), tm, "d_x1_ln1")
        early_started = swap_start("0a", [(n, g[n]) for n in early], [BF16] * len(early)) if l == 0 else None
        g["w_out"] = _mm_tn_rows(s["cat"], dz1b, "d_w_out").reshape(N_CHIPS, d // N_CHIPS, d)
        da_, dbo, dco = _bwd_out_proj(dz1b, w["w_out"], (d_a, d_b, d_c), "d_mixer_out", after=early_started)
        a_val, a_gate, b_u, b_v, c_b, c_c, c_x = s["proj"]
        dav, dag, g["conv_a_w"], g["conv_a_b"], g["ln_a_g"], g["ln_a_b"] = _mix_a_bwd(
            a_val, a_gate, s["ac"], da_, w["conv_a_w"], _row(w["ln_a_g"]), _row(w["ln_a_b"]), batch, "mix_a_bwd")
        if l == 0:
            swap_finish("0a", dav)
        dbu, dbv, g["w_s"], dbs, g["ln_v_g"], g["ln_v_b"] = _mix_b_bwd(
            b_u, b_v, dbo, w["w_s"], w["w_s_t"], w["bias_full"], _row(w["ln_v_g"]) + token, _row(w["ln_v_b"]), "mix_b_bwd")
        g["b_s"] = dbs.reshape(B_HEADS, CHUNK)
        dcb, dcc, dcx, g["conv_c_w"] = _mix_c_bwd(c_b, c_c, c_x, dco, w["conv_c_w"], batch, "mix_c_bwd")
        dproj = [dav, dag, dbu, dbv, dcb, dcc, dcx]
        g["w_in"] = _to_slots(_mm_tn_cols(s["x0b"], dproj, "d_w_in"), 1)
        if l > 0:
            below = full[l - 1]
            dz3, dz3b, grads[l - 1]["ln3_g"], grads[l - 1]["ln3_b"] = _bwd_data_ln(
                dproj, w["w_in"], in_offs, dz1, alpha, saved[l - 1]["z3"], _row(below["ln3_g"]), tm, "d_x0_ln3")
            pending = str(l)
            swap_started = swap_start(pending, [(n, g[n]) for n, _ in MATMUL_WEIGHTS], [BF16] * len(MATMUL_WEIGHTS))
        else:
            dx = _bwd_data(dproj, w["w_in"], in_offs, dz1, alpha, F32, tm, "d_x0")
            conv_slots = [_to_slots(grads[k][n], axis) for k in layers for n, axis in CONV_WEIGHTS]
            rep_parts = [jnp.stack([grads[k][n].reshape(weights[n].shape[1:]) for k in layers]) for n in REPLICATED]
            rep_flat = jnp.concatenate([p.reshape(-1) for p in rep_parts])
            n_rep = rep_flat.shape[0]
            per_chip = _round_up(-(-n_rep // N_CHIPS), PACK_W * PACK_ROWS)
            rep_slots = jnp.pad(rep_flat, (0, N_CHIPS * per_chip - n_rep)).reshape(N_CHIPS, per_chip)
            late = [(n, g[n]) for n, _ in MATMUL_WEIGHTS if n not in early] + [("small", _pack(conv_slots + [rep_slots], 1))]
            swap_start("0b", late, [BF16] * (len(late) - 1) + [F32])
            swap_finish("0b", dx)
    grad_x = dx.reshape(batch, seq, d)

    reduced, after = {}, last_started
    for key, part, send_sems, recv_sems, bufs in reductions:
        sums, lands = _copies_wait(_owner_copies, send_sems, recv_sems, bufs, len(part), after, f"rs_send_wait_{key}")
        for n, p, r in zip(part, sums, lands):
            stacked = n != "small"
            reduced[n] = _add_owners(p, r, "rs_add_owners", reduced.get(n), int(key[0]) if stacked else 0, depth if stacked else 1)
            after = reduced[n]
    reduced = dict(zip(reduced, _join_halves(list(reduced.values()), "rs_join")))
    small = _unpack(reduced["small"][0], [c.shape[1:] for c in conv_slots] + [(per_chip,)], 0)
    conv_grad, rep_mine = small[:-1], small[-1]
    rep_all = _all_gather_chips(rep_mine.reshape(-1, PACK_W), "gather_small_grads").reshape(-1)[:n_rep]

    grad = {}
    for n, _ in MATMUL_WEIGHTS:
        grad[n] = reduced[n]
    for k, (n, _) in enumerate(CONV_WEIGHTS):
        grad[n] = jnp.stack([conv_grad[l * len(CONV_WEIGHTS) + k] for l in layers])
    off = 0
    for n in REPLICATED:
        size = math.prod(weights[n].shape)
        grad[n] = rep_all[off:off + size].reshape(weights[n].shape)
        off += size

    delta, new_m, new_v = {}, {}, {}
    for n, _ in MATMUL_WEIGHTS:
        shape = weights[n].shape
        as_rows = lambda a: a.reshape(-1, shape[-1])
        delta[n], new_m[n], new_v[n] = (
            r.reshape(shape) for r in _adamw(as_rows(weights[n]), as_rows(grad[n]), as_rows(moment1[n]), as_rows(moment2[n]), "adamw"))
    small_names = [n for n, _ in CONV_WEIGHTS] + list(REPLICATED)
    small_shapes = [weights[n].shape for n in small_names]
    packed = [_pack([src[n] for n in small_names], 0) for src in (weights, grad, moment1, moment2)]
    for dst, res in zip((delta, new_m, new_v), _adamw(*packed, "adamw_small")):
        for n, a in zip(small_names, _unpack(res, small_shapes, 0)):
            dst[n] = a

    return (loss, grad_x, *[grad[n] for n in WEIGHT_ORDER], *[delta[n] for n in WEIGHT_ORDER],
            *[new_m[n] for n in WEIGHT_ORDER], *[new_v[n] for n in WEIGHT_ORDER])
```

```python
import functools
import math

import jax
import jax.numpy as jnp
from jax import lax
from jax.experimental import pallas as pl
from jax.experimental.pallas import tpu as pltpu

F32 = jnp.float32
BF16 = jnp.bfloat16
SDS = jax.ShapeDtypeStruct

HEADS = 16
A_HEADS, B_HEADS, C_HEADS = 6, 4, 6
X_HEADS = 4
CHUNK = 128
LN_EPS = 1e-5
ADAM_LR, ADAM_B1, ADAM_B2, ADAM_EPS, ADAM_WD, ADAM_STEP = 0.001, 0.9, 0.999, 1e-08, 0.01, 10

N_CHIPS = 4
V7X_VMEM_LIMIT = 56 << 20
SUBLANES = 8
PACK_W = 1024
PACK_ROWS = 32
ROW_TILE = 512
MATMUL_ROW_TILE = 1024
WIDE_ROW_TILE = 512
CONV_ROWS = 512
SUB_ROWS = 64
ATT_ROWS = 2048
TN_TILE = 1024
TN_TOKENS = 2048
MESH = pl.DeviceIdType.MESH
ANY = pl.BlockSpec(memory_space=pl.ANY)


def _tile(n, t):
    for d in range(min(n, t), 0, -1):
        if n % d == 0 and d % (2 * SUBLANES) == 0:
            return d
    return n


def _round_up(n, m):
    return -(-n // m) * m


def _params(*sem):
    return pltpu.CompilerParams(dimension_semantics=sem or None, vmem_limit_bytes=V7X_VMEM_LIMIT)


def _dot(a, b):
    return jnp.dot(a, b, preferred_element_type=F32)


def _dot_nt(a, b):
    return lax.dot_general(a, b, (((1,), (1,)), ((), ())), preferred_element_type=F32)


def _dot_tn(a, b):
    return lax.dot_general(a, b, (((0,), (0,)), ((), ())), preferred_element_type=F32)


def _sigmoid(x):
    return 1.0 / (1.0 + jnp.exp(-x))


def _gelu(x):
    return 0.5 * x * (1.0 + lax.erf(x * (2.0 ** -0.5)))


def _gelu_grad(x):
    return 0.5 * (1.0 + lax.erf(x * (2.0 ** -0.5))) + x * jnp.exp(-0.5 * x * x) * ((2.0 * math.pi) ** -0.5)


def _ln_stats(z):
    mu = jnp.mean(z, axis=-1, keepdims=True)
    zc = z - mu
    rstd = lax.rsqrt(jnp.mean(zc * zc, axis=-1, keepdims=True) + LN_EPS)
    return zc * rstd, rstd


def _ln_bwd(dy, y, rstd, g):
    dyh = dy * g
    return rstd * (dyh - jnp.mean(dyh, axis=-1, keepdims=True) - y * jnp.mean(dyh * y, axis=-1, keepdims=True))


def _colsum(x):
    return jnp.sum(x, axis=0, keepdims=True)


def _rowwise(body, rows, consts, outs, accs=(), *, tm, name, after=None):
    t = rows[0].shape[0]
    steps = t // tm
    n_in = len(rows) + len(consts)
    order = [] if after is None else [after]

    def kern(*refs):
        body(pl.program_id(0), steps, *refs[:n_in], *refs[n_in + len(order):])

    def whole(a, **kw):
        return pl.BlockSpec(a.shape, lambda i, nd=len(a.shape): (0,) * nd, **kw)

    in_specs = ([pl.BlockSpec((tm, r.shape[1]), lambda i: (i, 0)) for r in rows]
                + [whole(c, pipeline_mode=pl.Buffered(1)) for c in consts] + [ANY] * len(order))
    out_shape = [SDS((t, n), dt) for n, dt in outs] + [SDS(s, dt) for s, dt in accs]
    out_specs = [pl.BlockSpec((tm, n), lambda i: (i, 0)) for n, _ in outs] + [whole(SDS(s, dt)) for s, dt in accs]
    return pl.pallas_call(kern, grid=(steps,), in_specs=in_specs, out_specs=out_specs, out_shape=out_shape,
                          compiler_params=_params("arbitrary"), name=name)(*rows, *consts, *order)


def _in_proj(xb, w_in, widths, name, after=None):
    offs = [sum(widths[:k]) for k in range(len(widths))]

    def body(i, steps, x_ref, w_ref, *o_refs):
        x = x_ref[...]
        for o_ref, off, n in zip(o_refs, offs, widths):
            o_ref[...] = _dot(x, w_ref[:, off:off + n])

    return _rowwise(body, [xb], [w_in], [(n, F32) for n in widths], tm=_tile(xb.shape[0], MATMUL_ROW_TILE), name=name, after=after)


def _matmul(ab, w, name):
    def body(i, steps, a_ref, w_ref, o_ref):
        o_ref[...] = _dot(a_ref[...], w_ref[...]).astype(BF16)

    return _rowwise(body, [ab], [w], [(w.shape[1], BF16)], tm=_tile(ab.shape[0], MATMUL_ROW_TILE), name=name)[0]


def _ff1(xb, w, name):
    def body(i, steps, a_ref, w_ref, h_ref, r_ref):
        h = _dot(a_ref[...], w_ref[...])
        h_ref[...] = h.astype(BF16)
        r = jnp.maximum(h, 0.0)
        r_ref[...] = (r * r).astype(BF16)

    n = w.shape[1]
    return _rowwise(body, [xb], [w], [(n, BF16), (n, BF16)], tm=_tile(xb.shape[0], MATMUL_ROW_TILE), name=name)


def _proj_ln(a_list, w, stream, g, b, alpha, tm, name, after=None):
    widths = [a.shape[1] for a in a_list]
    offs = [sum(widths[:k]) for k in range(len(widths))]
    na = len(a_list)
    from_ln = len(stream) == 3

    def body(i, steps, *refs):
        a_refs, rest = refs[:na], refs[na:]
        if from_ln:
            s_ref, w_ref, sg_ref, sb_ref, g_ref, b_ref, z_ref, xb_ref = rest
            x = _ln_stats(s_ref[...])[0] * sg_ref[...] + sb_ref[...]
        else:
            s_ref, w_ref, g_ref, b_ref, z_ref, xb_ref = rest
            x = s_ref[...]
        acc = alpha * x
        for a_ref, off, n in zip(a_refs, offs, widths):
            acc = acc + _dot(a_ref[...], w_ref[off:off + n, :])
        z_ref[...] = acc
        y, _ = _ln_stats(acc)
        xb_ref[...] = (y * g_ref[...] + b_ref[...]).astype(BF16)

    d = w.shape[1]
    return _rowwise(body, [*a_list, stream[0]], [w, *stream[1:], g, b], [(d, F32), (d, BF16)], tm=tm, name=name, after=after)


def _bwd_ff2(dzb, h, w_ff2, name, after=None):
    def body(i, steps, dz_ref, h_ref, w_ref, dh_ref):
        dr = _dot_nt(dz_ref[...], w_ref[...])
        dh_ref[...] = (dr * (2.0 * jnp.maximum(h_ref[...].astype(F32), 0.0))).astype(BF16)

    return _rowwise(body, [dzb, h], [w_ff2], [(h.shape[1], BF16)], tm=_tile(h.shape[0], MATMUL_ROW_TILE), name=name, after=after)[0]


def _bwd_nt(g_list, w, col_offs, res, alpha, tm, name):
    ng = len(g_list)
    widths = [g.shape[1] for g in g_list]

    def body(i, steps, *refs):
        g_refs = refs[:ng]
        if res is None:
            w_ref, o_ref = refs[ng:]
            acc = None
        else:
            r_ref, w_ref, o_ref = refs[ng:]
            acc = alpha * r_ref[...]
        for g_ref, off, n in zip(g_refs, col_offs, widths):
            part = _dot_nt(g_ref[...], w_ref[:, off:off + n])
            acc = part if acc is None else acc + part
        o_ref[...] = acc.astype(o_ref.dtype)

    rows = list(g_list) + ([] if res is None else [res])
    return rows, w, body, tm, name


def _bwd_data(g_list, w, col_offs, res, alpha, out_dtype, tm, name):
    rows, w, body, tm, name = _bwd_nt(g_list, w, col_offs, res, alpha, tm, name)
    return _rowwise(body, rows, [w], [(w.shape[0], out_dtype)], tm=tm, name=name)[0]


def _bwd_data_ln(g_list, w, col_offs, res, alpha, z, gain, tm, name):
    ng = len(g_list)
    widths = [g.shape[1] for g in g_list]

    def body(i, steps, *refs):
        g_refs = refs[:ng]
        r_ref, z_ref, w_ref, gain_ref, dz_ref, dzb_ref, dg_ref, db_ref = refs[ng:]

        @pl.when(i == 0)
        def _():
            dg_ref[...] = jnp.zeros_like(dg_ref)
            db_ref[...] = jnp.zeros_like(db_ref)

        dy = alpha * r_ref[...]
        for g_ref, off, n in zip(g_refs, col_offs, widths):
            dy = dy + _dot_nt(g_ref[...], w_ref[:, off:off + n])
        y, rstd = _ln_stats(z_ref[...])
        dz = _ln_bwd(dy, y, rstd, gain_ref[...])
        dz_ref[...] = dz
        dzb_ref[...] = dz.astype(BF16)
        dg_ref[...] += _colsum(dy * y)
        db_ref[...] += _colsum(dy)

    d = z.shape[1]
    return _rowwise(body, [*g_list, res, z], [w, gain], [(d, F32), (d, BF16)], [((1, d), F32), ((1, d), F32)], tm=tm, name=name)


def _loss_head_ln(z, target, gain, bias, name):
    d = z.shape[1]

    def body(i, steps, z_ref, t_ref, gain_ref, bias_ref, dz_ref, dzb_ref, dg_ref, db_ref, l_ref):
        @pl.when(i == 0)
        def _():
            for r in (dg_ref, db_ref, l_ref):
                r[...] = jnp.zeros_like(r)

        yn, rstd = _ln_stats(z_ref[...])
        err = yn * gain_ref[...] + bias_ref[...] - t_ref[...]
        l_ref[...] += jnp.sum(err * err) * (0.5 / d)
        dy = err * (1.0 / d)
        dz = _ln_bwd(dy, yn, rstd, gain_ref[...])
        dz_ref[...] = dz
        dzb_ref[...] = dz.astype(BF16)
        dg_ref[...] += _colsum(dy * yn)
        db_ref[...] += _colsum(dy)

    return _rowwise(body, [z, target], [gain, bias], [(d, F32), (d, BF16)],
                    [((1, d), F32), ((1, d), F32), ((SUBLANES, 128), F32)], tm=_tile(z.shape[0], ROW_TILE), name=name)


def _bwd_out_proj(dzb, w_out, widths, name, after=None):
    offs = [sum(widths[:k]) for k in range(len(widths))]

    def body(i, steps, dz_ref, w_ref, *o_refs):
        dz = dz_ref[...]
        for o_ref, off, n in zip(o_refs, offs, widths):
            o_ref[...] = _dot_nt(dz, w_ref[off:off + n, :])

    return _rowwise(body, [dzb], [w_out], [(n, F32) for n in widths], tm=_tile(dzb.shape[0], MATMUL_ROW_TILE), name=name,
                    after=after)


def _adamw(w, g, m, v, name):
    def body(i, steps, w_ref, g_ref, m_ref, v_ref, d_ref, nm_ref, nv_ref):
        g_ = g_ref[...]
        nm = ADAM_B1 * m_ref[...] + (1.0 - ADAM_B1) * g_
        nv = ADAM_B2 * v_ref[...] + (1.0 - ADAM_B2) * (g_ * g_)
        m_hat = nm / (1.0 - ADAM_B1 ** ADAM_STEP)
        v_hat = nv / (1.0 - ADAM_B2 ** ADAM_STEP)
        d_ref[...] = -ADAM_LR * (m_hat / (jnp.sqrt(v_hat) + ADAM_EPS) + ADAM_WD * w_ref[...])
        nm_ref[...] = nm
        nv_ref[...] = nv

    c = w.shape[1]
    return _rowwise(body, [w, g, m, v], [], [(c, F32)] * 3, tm=_tile(w.shape[0], ROW_TILE), name=name)


def _mm_tn(a, g, name, slots=1, after=None):
    t, ka = a.shape
    n = g.shape[1]
    ta, tn, tk = _tile(ka, TN_TILE), _tile(n // slots, TN_TILE), _tile(t, TN_TOKENS)
    per = n // slots // tn

    order = [] if after is None else [after]

    def kern(a_ref, g_ref, *refs):
        o_ref = refs[-1]

        @pl.when(pl.program_id(2) == 0)
        def _():
            o_ref[...] = jnp.zeros_like(o_ref)

        o_ref[0] += _dot_tn(a_ref[...], g_ref[...])

    return pl.pallas_call(
        kern, grid=(ka // ta, n // tn, t // tk),
        in_specs=[pl.BlockSpec((tk, ta), lambda i, j, k: (k, i)), pl.BlockSpec((tk, tn), lambda i, j, k: (k, j))] + [ANY] * len(order),
        out_specs=pl.BlockSpec((1, ta, tn), lambda i, j, k: (j // per, i, j % per)), out_shape=SDS((slots, ka, n // slots), F32),
        compiler_params=_params("arbitrary", "arbitrary", "arbitrary"), name=name)(a, g, *order)


def _mm_tn_cols(a, g_list, name):
    t, ka = a.shape
    widths = [g.shape[1] for g in g_list]
    offs = [sum(widths[:k]) for k in range(len(widths))]
    ta, tk = _tile(ka, ROW_TILE), _tile(t, TN_TOKENS)

    def kern(a_ref, *refs):
        g_refs, o_ref = refs[:-1], refs[-1]

        @pl.when(pl.program_id(1) == 0)
        def _():
            o_ref[...] = jnp.zeros_like(o_ref)

        a_ = a_ref[...]
        for g_ref, off, n in zip(g_refs, offs, widths):
            o_ref[:, off:off + n] += _dot_tn(a_, g_ref[...])

    return pl.pallas_call(
        kern, grid=(ka // ta, t // tk),
        in_specs=[pl.BlockSpec((tk, ta), lambda i, k: (k, i))] + [pl.BlockSpec((tk, n), lambda i, k: (k, 0)) for n in widths],
        out_specs=pl.BlockSpec((ta, sum(widths)), lambda i, k: (i, 0)), out_shape=SDS((ka, sum(widths)), F32),
        compiler_params=_params("arbitrary", "arbitrary"), name=name)(a, *g_list)


def _mm_tn_rows(a_list, g, name):
    t, n = g.shape
    widths = [a.shape[1] for a in a_list]
    offs = [sum(widths[:k]) for k in range(len(widths))]
    tn, tk = _tile(n, TN_TILE), _tile(t, TN_TOKENS)

    def kern(*refs):
        a_refs, g_ref, o_ref = refs[:-2], refs[-2], refs[-1]

        @pl.when(pl.program_id(1) == 0)
        def _():
            o_ref[...] = jnp.zeros_like(o_ref)

        g_ = g_ref[...]
        for a_ref, off, ka in zip(a_refs, offs, widths):
            o_ref[off:off + ka, :] += _dot_tn(a_ref[...], g_)

    return pl.pallas_call(
        kern, grid=(n // tn, t // tk),
        in_specs=[pl.BlockSpec((tk, ka), lambda j, k: (k, 0)) for ka in widths] + [pl.BlockSpec((tk, tn), lambda j, k: (k, j))],
        out_specs=pl.BlockSpec((sum(widths), tn), lambda j, k: (0, j)), out_shape=SDS((sum(widths), n), F32),
        compiler_params=_params("arbitrary", "arbitrary"), name=name)(*a_list, g)


def _conv_geometry(seq, taps):
    pad = _round_up(taps - 1, SUBLANES)
    rc = _tile(seq, CONV_ROWS)
    assert rc % pad == 0 and seq % rc == 0
    return pad, rc, seq // rc


def _chunk_spec(rc, n, nch):
    return pl.BlockSpec((rc, n), lambda b, i: (b * nch + i, 0))


def _prev_halo_spec(pad, rc, n, nch):
    per = rc // pad
    return pl.BlockSpec((pad, n), lambda b, i: (jnp.maximum((b * nch + i) * per - 1, 0), 0))


def _next_halo_spec(pad, rc, n, nch, total_rows):
    per = rc // pad
    last = total_rows // pad - 1
    return pl.BlockSpec((pad, n), lambda b, i: (jnp.minimum((b * nch + i + 1) * per, last), 0))


def _whole2(a):
    return pl.BlockSpec(a.shape, lambda b, i, nd=len(a.shape): (0,) * nd)


def _sub_rows(rc):
    return SUB_ROWS if rc % SUB_ROWS == 0 else rc


def _build_shifts(sh_ref, src_ref, offsets):
    rows = src_ref.shape[0]
    for r in sorted({o % SUBLANES for o in offsets} - {0}):
        sh_ref[r, 0:rows - SUBLANES, :] = src_ref[r:r + rows - SUBLANES, :]


def _read_shifted(sh_ref, src_ref, o, s0, sub):
    r = o % SUBLANES
    a = o - r + s0
    return src_ref[a:a + sub, :] if r == 0 else sh_ref[r, a:a + sub, :]


def _tap_sum(sh_ref, src_ref, w_ref, offsets, s0, sub):
    acc = None
    for k, o in enumerate(offsets):
        term = _read_shifted(sh_ref, src_ref, o, s0, sub) * w_ref[k:k + 1, :]
        acc = term if acc is None else acc + term
    return acc


def _row_groups(x):
    acc = x[0:SUBLANES, :]
    for g0 in range(SUBLANES, x.shape[0], SUBLANES):
        acc = acc + x[g0:g0 + SUBLANES, :]
    return acc


def _mix_a_fwd(a_val, a_gate, conv_w, conv_b, ln_g, ln_b, batch, name):
    t, da = a_val.shape
    taps = conv_w.shape[0]
    pad, rc, nch = _conv_geometry(t // batch, taps)
    sub = _sub_rows(rc)
    offs = [pad - (taps - 1) + k for k in range(taps)]

    def kern(av_ref, ag_ref, pav_ref, pag_ref, w_ref, cb_ref, g_ref, b_ref, a_ref, ac_ref, pad_ref, sh_ref):
        first = pl.program_id(1) == 0
        pad_ref[0:pad, :] = jnp.where(first, 0.0, pav_ref[...] * _sigmoid(pag_ref[...]))
        for s0 in range(0, rc, sub):
            pad_ref[pad + s0:pad + s0 + sub, :] = av_ref[s0:s0 + sub, :] * _sigmoid(ag_ref[s0:s0 + sub, :])
        _build_shifts(sh_ref, pad_ref, offs)
        for s0 in range(0, rc, sub):
            ac = _tap_sum(sh_ref, pad_ref, w_ref, offs, s0, sub) + cb_ref[...]
            ac_ref[s0:s0 + sub, :] = ac
            y, _ = _ln_stats(ac)
            aln = y * g_ref[...] + b_ref[...]
            a_ref[s0:s0 + sub, :] = (aln * _sigmoid(aln)).astype(BF16)

    chunk, halo = _chunk_spec(rc, da, nch), _prev_halo_spec(pad, rc, da, nch)
    return pl.pallas_call(
        kern, grid=(batch, nch),
        in_specs=[chunk, chunk, halo, halo, _whole2(conv_w), _whole2(conv_b), _whole2(ln_g), _whole2(ln_b)],
        out_specs=[chunk, chunk], out_shape=[SDS((t, da), BF16), SDS((t, da), F32)],
        scratch_shapes=[pltpu.VMEM((pad + rc, da), F32), pltpu.VMEM((SUBLANES, pad + rc, da), F32)],
        compiler_params=_params("arbitrary", "arbitrary"), name=name)(a_val, a_gate, a_val, a_gate, conv_w, conv_b, ln_g, ln_b)


def _mix_a_bwd(a_val, a_gate, ac, da_, conv_w, ln_g, ln_b, batch, name):
    t, da = a_val.shape
    taps = conv_w.shape[0]
    pad, rc, nch = _conv_geometry(t // batch, taps)
    sub = _sub_rows(rc)
    offs_in = [pad - (taps - 1) + k for k in range(taps)]
    offs_out = [taps - 1 - k for k in range(taps)]

    def kern(av_ref, ag_ref, pav_ref, pag_ref, ac_ref, da_ref, nac_ref, nda_ref, w_ref, g_ref, b_ref,
             dav_ref, dag_ref, dw_ref, dcb_ref, dg_ref, db_ref, gpad_ref, dpad_ref, shg_ref, shd_ref, dwacc_ref):
        i = pl.program_id(1)
        start = (pl.program_id(0) == 0) & (i == 0)
        end = (pl.program_id(0) == batch - 1) & (i == nch - 1)

        @pl.when(start)
        def _():
            for r in (dcb_ref, dg_ref, db_ref, dwacc_ref):
                r[...] = jnp.zeros_like(r)

        gain, bias = g_ref[...], b_ref[...]

        def d_conv_out(ac_, dout):
            y, rstd = _ln_stats(ac_)
            aln = y * gain + bias
            sig = _sigmoid(aln)
            daln = dout * (sig * (1.0 + aln * (1.0 - sig)))
            return _ln_bwd(daln, y, rstd, gain), daln, y

        gpad_ref[0:pad, :] = jnp.where(i == 0, 0.0, pav_ref[...] * _sigmoid(pag_ref[...]))
        dac_next, _, _ = d_conv_out(nac_ref[...], nda_ref[...])
        dpad_ref[rc:rc + pad, :] = jnp.where(i == nch - 1, 0.0, dac_next)
        for s0 in range(0, rc, sub):
            rows = slice(s0, s0 + sub)
            dac, daln, y = d_conv_out(ac_ref[rows, :], da_ref[rows, :])
            dg_ref[...] += _colsum(daln * y)
            db_ref[...] += _colsum(daln)
            dcb_ref[...] += _colsum(dac)
            dpad_ref[rows, :] = dac
            gpad_ref[pad + s0:pad + s0 + sub, :] = av_ref[rows, :] * _sigmoid(ag_ref[rows, :])
        _build_shifts(shg_ref, gpad_ref, offs_in)
        _build_shifts(shd_ref, dpad_ref, offs_out)
        for s0 in range(0, rc, sub):
            rows = slice(s0, s0 + sub)
            dac = dpad_ref[rows, :]
            for k, o in enumerate(offs_in):
                dwacc_ref[k] += _row_groups(dac * _read_shifted(shg_ref, gpad_ref, o, s0, sub))
            dgl = _tap_sum(shd_ref, dpad_ref, w_ref, offs_out, s0, sub)
            av = av_ref[rows, :]
            sig = _sigmoid(ag_ref[rows, :])
            dav_ref[rows, :] = (dgl * sig).astype(BF16)
            dag_ref[rows, :] = (dgl * av * sig * (1.0 - sig)).astype(BF16)

        @pl.when(end)
        def _():
            for k in range(taps):
                dw_ref[k:k + 1, :] = _colsum(dwacc_ref[k])

    chunk, prev = _chunk_spec(rc, da, nch), _prev_halo_spec(pad, rc, da, nch)
    nxt = _next_halo_spec(pad, rc, da, nch, t)
    vec = SDS((1, da), F32)
    return pl.pallas_call(
        kern, grid=(batch, nch),
        in_specs=[chunk, chunk, prev, prev, chunk, chunk, nxt, nxt, _whole2(conv_w), _whole2(ln_g), _whole2(ln_b)],
        out_specs=[chunk, chunk, _whole2(conv_w), _whole2(vec), _whole2(vec), _whole2(vec)],
        out_shape=[SDS((t, da), BF16), SDS((t, da), BF16), SDS(conv_w.shape, F32), vec, vec, vec],
        scratch_shapes=[pltpu.VMEM((pad + rc, da), F32), pltpu.VMEM((rc + pad, da), F32),
                        pltpu.VMEM((SUBLANES, pad + rc, da), F32), pltpu.VMEM((SUBLANES, rc + pad, da), F32),
                        pltpu.VMEM((taps, SUBLANES, da), F32)],
        compiler_params=_params("arbitrary", "arbitrary"), name=name)(
            a_val, a_gate, a_val, a_gate, ac, da_, ac, da_, conv_w, ln_g, ln_b)


def _mix_c_fwd(c_b, c_c, c_x, conv_w, batch, name):
    t, dc = c_b.shape
    taps = conv_w.shape[0]
    pad, rc, nch = _conv_geometry(t // batch, taps)
    sub = _sub_rows(rc)
    offs = [pad - (taps - 1) + k for k in range(taps)]

    def kern(cb_ref, cc_ref, cx_ref, pcc_ref, pcx_ref, w_ref, o_ref, pad_ref, sh_ref):
        pad_ref[0:pad, :] = jnp.where(pl.program_id(1) == 0, 0.0, pcc_ref[...] * pcx_ref[...])
        for s0 in range(0, rc, sub):
            pad_ref[pad + s0:pad + s0 + sub, :] = cc_ref[s0:s0 + sub, :] * cx_ref[s0:s0 + sub, :]
        _build_shifts(sh_ref, pad_ref, offs)
        for s0 in range(0, rc, sub):
            o_ref[s0:s0 + sub, :] = (cb_ref[s0:s0 + sub, :] * _tap_sum(sh_ref, pad_ref, w_ref, offs, s0, sub)).astype(BF16)

    chunk, prev = _chunk_spec(rc, dc, nch), _prev_halo_spec(pad, rc, dc, nch)
    return pl.pallas_call(
        kern, grid=(batch, nch), in_specs=[chunk, chunk, chunk, prev, prev, _whole2(conv_w)],
        out_specs=chunk, out_shape=SDS((t, dc), BF16),
        scratch_shapes=[pltpu.VMEM((pad + rc, dc), F32), pltpu.VMEM((SUBLANES, pad + rc, dc), F32)],
        compiler_params=_params("arbitrary", "arbitrary"), name=name)(c_b, c_c, c_x, c_c, c_x, conv_w)


def _mix_c_bwd(c_b, c_c, c_x, dco, conv_w, batch, name):
    t, dc = c_b.shape
    taps = conv_w.shape[0]
    pad, rc, nch = _conv_geometry(t // batch, taps)
    sub = _sub_rows(rc)
    offs_in = [pad - (taps - 1) + k for k in range(taps)]
    offs_out = [taps - 1 - k for k in range(taps)]

    def kern(cb_ref, cc_ref, cx_ref, do_ref, pcc_ref, pcx_ref, ncb_ref, ndo_ref, w_ref,
             dcb_ref, dcc_ref, dcx_ref, dw_ref, ppad_ref, dpad_ref, shp_ref, shd_ref, dwacc_ref):
        i = pl.program_id(1)

        @pl.when((pl.program_id(0) == 0) & (i == 0))
        def _():
            dwacc_ref[...] = jnp.zeros_like(dwacc_ref)

        ppad_ref[0:pad, :] = jnp.where(i == 0, 0.0, pcc_ref[...] * pcx_ref[...])
        dpad_ref[rc:rc + pad, :] = jnp.where(i == nch - 1, 0.0, ndo_ref[...] * ncb_ref[...])
        for s0 in range(0, rc, sub):
            rows = slice(s0, s0 + sub)
            ppad_ref[pad + s0:pad + s0 + sub, :] = cc_ref[rows, :] * cx_ref[rows, :]
            dpad_ref[rows, :] = do_ref[rows, :] * cb_ref[rows, :]
        _build_shifts(shp_ref, ppad_ref, offs_in)
        _build_shifts(shd_ref, dpad_ref, offs_out)
        for s0 in range(0, rc, sub):
            rows = slice(s0, s0 + sub)
            dcv = dpad_ref[rows, :]
            cv = None
            for k, o in enumerate(offs_in):
                shifted = _read_shifted(shp_ref, ppad_ref, o, s0, sub)
                dwacc_ref[k] += _row_groups(dcv * shifted)
                cv = shifted * w_ref[k:k + 1, :] if cv is None else cv + shifted * w_ref[k:k + 1, :]
            dp = _tap_sum(shd_ref, dpad_ref, w_ref, offs_out, s0, sub)
            dcb_ref[rows, :] = (do_ref[rows, :] * cv).astype(BF16)
            dcc_ref[rows, :] = (dp * cx_ref[rows, :]).astype(BF16)
            dcx_ref[rows, :] = (dp * cc_ref[rows, :]).astype(BF16)

        @pl.when((pl.program_id(0) == batch - 1) & (i == nch - 1))
        def _():
            for k in range(taps):
                dw_ref[k:k + 1, :] = _colsum(dwacc_ref[k])

    chunk, prev = _chunk_spec(rc, dc, nch), _prev_halo_spec(pad, rc, dc, nch)
    nxt = _next_halo_spec(pad, rc, dc, nch, t)
    return pl.pallas_call(
        kern, grid=(batch, nch), in_specs=[chunk, chunk, chunk, chunk, prev, prev, nxt, nxt, _whole2(conv_w)],
        out_specs=[chunk, chunk, chunk, _whole2(conv_w)],
        out_shape=[SDS((t, dc), BF16)] * 3 + [SDS(conv_w.shape, F32)],
        scratch_shapes=[pltpu.VMEM((pad + rc, dc), F32), pltpu.VMEM((rc + pad, dc), F32),
                        pltpu.VMEM((SUBLANES, pad + rc, dc), F32), pltpu.VMEM((SUBLANES, rc + pad, dc), F32),
                        pltpu.VMEM((taps, SUBLANES, dc), F32)],
        compiler_params=_params("arbitrary", "arbitrary"), name=name)(c_b, c_c, c_x, dco, c_c, c_x, c_b, dco, conv_w)


def _head_of_lane(db):
    return lax.broadcasted_iota(jnp.int32, (1, db), 1) // (db // B_HEADS)


def _tril(rows_ge_cols=True):
    r = lax.broadcasted_iota(jnp.int32, (CHUNK, CHUNK), 0)
    c = lax.broadcasted_iota(jnp.int32, (CHUNK, CHUNK), 1)
    return (r >= c) if rows_ge_cols else (r <= c)


def _spatial_mix(wm, vb, bias, head):
    mixed = bias
    for h in range(B_HEADS):
        mixed = mixed + jnp.where(head == h, _dot(wm[h], vb), 0.0)
    return mixed


def _mix_b_fwd(b_u, b_v, w_s, bias_full, ln_g, ln_b, name):
    t, db = b_u.shape
    rb = _tile(t, ROW_TILE)

    def kern(bu_ref, bv_ref, ws_ref, bias_ref, g_ref, b_ref, o_ref):
        head = _head_of_lane(db)
        wm = [jnp.where(_tril(), ws_ref[h], 0.0).astype(BF16) for h in range(B_HEADS)]
        for ch in range(rb // CHUNK):
            rows = slice(ch * CHUNK, (ch + 1) * CHUNK)
            y, _ = _ln_stats(_gelu(bv_ref[rows, :]))
            vb = (y * g_ref[...] + b_ref[...]).astype(BF16)
            mixed = _spatial_mix(wm, vb, bias_ref[...], head)
            o_ref[rows, :] = (_gelu(bu_ref[rows, :]) * mixed).astype(BF16)

    def whole(a):
        return pl.BlockSpec(a.shape, lambda i, nd=len(a.shape): (0,) * nd)

    tile = pl.BlockSpec((rb, db), lambda i: (i, 0))
    return pl.pallas_call(
        kern, grid=(t // rb,), in_specs=[tile, tile, whole(w_s), whole(bias_full), whole(ln_g), whole(ln_b)],
        out_specs=tile, out_shape=SDS((t, db), BF16), compiler_params=_params("arbitrary"), name=name)(
            b_u, b_v, w_s, bias_full, ln_g, ln_b)


def _mix_b_bwd(b_u, b_v, dbo, w_s, w_s_t, bias_full, ln_g, ln_b, name):
    t, db = b_u.shape
    rb = _tile(t, ROW_TILE)
    steps = t // rb

    def kern(bu_ref, bv_ref, do_ref, ws_ref, wst_ref, bias_ref, g_ref, b_ref,
             dbu_ref, dbv_ref, dws_ref, dbs_ref, dg_ref, dbeta_ref, dbias_ref):
        i = pl.program_id(0)

        @pl.when(i == 0)
        def _():
            for r in (dws_ref, dg_ref, dbeta_ref, dbias_ref):
                r[...] = jnp.zeros_like(r)

        head = _head_of_lane(db)
        gain = g_ref[...]
        wm = [jnp.where(_tril(), ws_ref[h], 0.0).astype(BF16) for h in range(B_HEADS)]
        wmt = [jnp.where(_tril(False), wst_ref[h], 0.0).astype(BF16) for h in range(B_HEADS)]
        for ch in range(rb // CHUNK):
            rows = slice(ch * CHUNK, (ch + 1) * CHUNK)
            bu, bv, dout = bu_ref[rows, :], bv_ref[rows, :], do_ref[rows, :]
            y, rstd = _ln_stats(_gelu(bv))
            vb = (y * gain + b_ref[...]).astype(BF16)
            mixed = _spatial_mix(wm, vb, bias_ref[...], head)
            du = dout * mixed
            dmixed = dout * _gelu(bu)
            dbias_ref[...] += dmixed
            dmb = dmixed.astype(BF16)
            dv = None
            for h in range(B_HEADS):
                dws_ref[h] += _dot_nt(jnp.where(head == h, dmb, 0.0).astype(BF16), vb)
                part = jnp.where(head == h, _dot(wmt[h], dmb), 0.0)
                dv = part if dv is None else dv + part
            dg_ref[...] += _colsum(dv * y)
            dbeta_ref[...] += _colsum(dv)
            dbv_ref[rows, :] = (_ln_bwd(dv, y, rstd, gain) * _gelu_grad(bv)).astype(BF16)
            dbu_ref[rows, :] = (du * _gelu_grad(bu)).astype(BF16)

        @pl.when(i == steps - 1)
        def _():
            for h in range(B_HEADS):
                dws_ref[h] = jnp.where(_tril(), dws_ref[h], 0.0)
                dbs_ref[h] = jnp.sum(jnp.where(head == h, dbias_ref[...], 0.0), axis=1, keepdims=True)

    def whole(a):
        return pl.BlockSpec(a.shape, lambda i, nd=len(a.shape): (0,) * nd)

    tile = pl.BlockSpec((rb, db), lambda i: (i, 0))
    vec = SDS((1, db), F32)
    dbs = SDS((B_HEADS, CHUNK, 1), F32)
    return pl.pallas_call(
        kern, grid=(steps,),
        in_specs=[tile, tile, tile, whole(w_s), whole(w_s_t), whole(bias_full), whole(ln_g), whole(ln_b)],
        out_specs=[tile, tile, whole(w_s), whole(dbs), whole(vec), whole(vec)],
        out_shape=[SDS((t, db), BF16), SDS((t, db), BF16), SDS(w_s.shape, F32), dbs, vec, vec],
        scratch_shapes=[pltpu.VMEM((CHUNK, db), F32)],
        compiler_params=_params("arbitrary"), name=name)(b_u, b_v, dbo, w_s, w_s_t, bias_full, ln_g, ln_b)


def _softmax_rows(s):
    e = jnp.exp(s - jnp.max(s, axis=-1, keepdims=True))
    return e / jnp.sum(e, axis=-1, keepdims=True)


def _attention_fwd(q, kv, batch, name):
    t, d = q.shape
    seq, mlen, hd = t // batch, kv.shape[0] // batch, d // X_HEADS
    ar = _tile(seq, ATT_ROWS)
    scale = hd ** -0.5

    def kern(q_ref, k_ref, v_ref, o_ref):
        k, v = k_ref[...], v_ref[...]
        for r0 in range(0, seq, ar):
            p = _softmax_rows(_dot_nt(q_ref[r0:r0 + ar, :], k) * scale)
            o_ref[r0:r0 + ar, :] = _dot(p.astype(BF16), v).astype(BF16)

    qs = pl.BlockSpec((seq, hd), lambda b, h: (b, h))
    return pl.pallas_call(
        kern, grid=(batch, X_HEADS),
        in_specs=[qs, pl.BlockSpec((mlen, hd), lambda b, h: (b, h)), pl.BlockSpec((mlen, hd), lambda b, h: (b, X_HEADS + h))],
        out_specs=qs, out_shape=SDS((t, d), BF16), compiler_params=_params("arbitrary", "arbitrary"), name=name)(q, kv, kv)


def _attention_bwd(q, kv, do, batch, name):
    t, d = q.shape
    seq, mlen, hd = t // batch, kv.shape[0] // batch, d // X_HEADS
    ar = _tile(seq, ATT_ROWS)
    scale = hd ** -0.5

    def kern(q_ref, k_ref, v_ref, do_ref, dq_ref, dk_ref, dv_ref):
        k, v = k_ref[...], v_ref[...]
        dk = jnp.zeros((mlen, hd), F32)
        dv = jnp.zeros((mlen, hd), F32)
        for r0 in range(0, seq, ar):
            qr, dor = q_ref[r0:r0 + ar, :], do_ref[r0:r0 + ar, :]
            p = _softmax_rows(_dot_nt(qr, k) * scale)
            dp = _dot_nt(dor, v)
            ds = (p * (dp - jnp.sum(p * dp, axis=-1, keepdims=True)) * scale).astype(BF16)
            dq_ref[r0:r0 + ar, :] = _dot(ds, k).astype(BF16)
            dk = dk + _dot_tn(ds, qr)
            dv = dv + _dot_tn(p.astype(BF16), dor)
        dk_ref[...] = dk.astype(BF16)
        dv_ref[...] = dv.astype(BF16)

    qs = pl.BlockSpec((seq, hd), lambda b, h: (b, h))
    ks = pl.BlockSpec((mlen, hd), lambda b, h: (b, h))
    dkv = SDS((kv.shape[0], d), BF16)
    return pl.pallas_call(
        kern, grid=(batch, X_HEADS),
        in_specs=[qs, ks, pl.BlockSpec((mlen, hd), lambda b, h: (b, X_HEADS + h)), qs],
        out_specs=[qs, ks, ks], out_shape=[SDS((t, d), BF16), dkv, dkv],
        compiler_params=_params("arbitrary", "arbitrary"), name=name)(q, kv, kv, do)


def _place():
    x, y, c = lax.axis_index("x"), lax.axis_index("y"), lax.axis_index("c")
    other_chips = [(1 - x, y), (x, 1 - y), (1 - x, 1 - y)]
    return x, y, c, other_chips


def _comm_call(kern, out_shape, n_pairs, name, *args):
    return pl.pallas_call(
        kern, out_shape=out_shape, in_specs=[ANY] * len(args), out_specs=jax.tree.map(lambda _: ANY, out_shape),
        scratch_shapes=[pltpu.SemaphoreType.DMA((n_pairs,)), pltpu.SemaphoreType.DMA((n_pairs,)), pltpu.SemaphoreType.DMA((n_pairs,))],
        name=name)(*args)


def _all_gather_chips(shard, name):
    r, cols = shard.shape
    rh = r // 2

    def kern(s_ref, o_ref, send_sems, recv_sems, local_sems):
        x, y, c, chips = _place()
        mine_slot = 2 * x + y
        half = pl.ds(c * rh, rh)
        other_half = pl.ds((1 - c) * rh, rh)

        def copy(k, src, dst, to):
            return pltpu.make_async_remote_copy(src_ref=src, dst_ref=dst, send_sem=send_sems.at[k], recv_sem=recv_sems.at[k],
                                                device_id=to, device_id_type=MESH)

        mine = pltpu.make_async_copy(s_ref, o_ref.at[mine_slot], local_sems.at[0])
        mine.start()
        first = [copy(j, s_ref.at[half], o_ref.at[mine_slot, half], (px, py, c)) for j, (px, py) in enumerate(chips)]
        for cp in first:
            cp.start()
        passed = []
        for j, (px, py) in enumerate(chips):
            landed = o_ref.at[2 * px + py, half]
            copy(j, landed, landed, (px, py, c)).wait_recv()
            cp = copy(3 + j, landed, landed, (x, y, 1 - c))
            cp.start()
            passed.append(cp)
        for j, (px, py) in enumerate(chips):
            theirs = o_ref.at[2 * px + py, other_half]
            copy(3 + j, theirs, theirs, (x, y, 1 - c)).wait_recv()
        for cp in first + passed:
            cp.wait_send()
        mine.wait()

    return _comm_call(kern, SDS((N_CHIPS, r, cols), shard.dtype), 6, name, shard)


def _join_halves(joined, name):
    n = len(joined)

    def kern(*refs):
        j_refs = refs[:n]
        send_sems, recv_sems = refs[2 * n:]
        x, y, c, _ = _place()
        copies = []
        for k, j_ref in enumerate(j_refs):
            rh = j_ref.shape[1] // 2
            rows = j_ref.at[:, pl.ds(c * rh, rh)]
            cp = pltpu.make_async_remote_copy(src_ref=rows, dst_ref=rows, send_sem=send_sems.at[k], recv_sem=recv_sems.at[k],
                                              device_id=(x, y, 1 - c), device_id_type=MESH)
            cp.start()
            copies.append(cp)
        for k, (cp, j_ref) in enumerate(zip(copies, j_refs)):
            rh = j_ref.shape[1] // 2
            theirs = j_ref.at[:, pl.ds((1 - c) * rh, rh)]
            pltpu.make_async_remote_copy(src_ref=theirs, dst_ref=theirs, send_sem=send_sems.at[k], recv_sem=recv_sems.at[k],
                                         device_id=(x, y, 1 - c), device_id_type=MESH).wait_recv()
            cp.wait_send()

    return pl.pallas_call(
        kern, out_shape=[SDS(j.shape, j.dtype) for j in joined], in_specs=[ANY] * n, out_specs=[ANY] * n,
        input_output_aliases={k: k for k in range(n)},
        scratch_shapes=[pltpu.SemaphoreType.DMA((n,)), pltpu.SemaphoreType.DMA((n,))], name=name)(*joined)


HBM = pl.BlockSpec(memory_space=pltpu.HBM)
SEMAPHORES = pl.BlockSpec(memory_space=pltpu.SEMAPHORE)
COPIES_PER_ARRAY = N_CHIPS - 1
GATHER_COPIES = N_CHIPS


def _in_hbm(a):
    return pltpu.with_memory_space_constraint(a, pltpu.HBM)


def _gather_copies(shard_refs, land_refs, send_sems, recv_sems):
    x, y, c, chips = _place()
    copies = []
    for k, (s_ref, land_ref) in enumerate(zip(shard_refs, land_refs)):
        half = pl.ds(c * (s_ref.shape[0] // 2), s_ref.shape[0] // 2)
        for j, (px, py) in enumerate(chips):
            n = GATHER_COPIES * k + j
            copies.append(pltpu.make_async_remote_copy(
                src_ref=s_ref.at[half], dst_ref=land_ref.at[2 * x + y, half], send_sem=send_sems.at[n], recv_sem=recv_sems.at[n],
                device_id=(px, py, c), device_id_type=MESH))
        n = GATHER_COPIES * k + N_CHIPS - 1
        copies.append(pltpu.make_async_remote_copy(
            src_ref=s_ref, dst_ref=land_ref.at[2 * x + y], send_sem=send_sems.at[n], recv_sem=recv_sems.at[n],
            device_id=(x, y, 1 - c), device_id_type=MESH))
    return copies


def _swap_copies(full_refs, land_refs, send_sems, recv_sems):
    x, y, c, _ = _place()
    copies = []
    for k, (g_ref, land_ref) in enumerate(zip(full_refs, land_refs)):
        rh = land_ref.shape[1]
        copies.append(pltpu.make_async_remote_copy(
            src_ref=g_ref.at[:, pl.ds((1 - c) * rh, rh), :], dst_ref=land_ref, send_sem=send_sems.at[k], recv_sem=recv_sems.at[k],
            device_id=(x, y, 1 - c), device_id_type=MESH))
    return copies


def _owner_copies(sum_refs, land_refs, send_sems, recv_sems):
    x, y, c, chips = _place()
    copies = []
    for k, (s_ref, land_ref) in enumerate(zip(sum_refs, land_refs)):
        for j, (px, py) in enumerate(chips):
            n = COPIES_PER_ARRAY * k + j
            copies.append(pltpu.make_async_remote_copy(
                src_ref=s_ref.at[2 * px + py], dst_ref=land_ref.at[j], send_sem=send_sems.at[n], recv_sem=recv_sems.at[n],
                device_id=(px, py, c), device_id_type=MESH))
    return copies


def _copies_start(build, per_array, sources, lands, name, after=None):
    ns, nb = len(sources), len(sources) + len(lands)
    n_copies = per_array * ns
    order = [] if after is None else [after]
    n_in = nb + len(order)

    def kern(*refs):
        for cp in build(refs[:ns], refs[ns:nb], refs[n_in + nb], refs[n_in + nb + 1]):
            cp.start()
        refs[-1][...] = jnp.zeros_like(refs[-1])

    bufs = [*sources, *lands]
    res = pl.pallas_call(
        kern, name=name,
        out_shape=(*[pltpu.HBM(b.shape, b.dtype) for b in bufs], pltpu.SemaphoreType.DMA((n_copies,)), pltpu.SemaphoreType.DMA((n_copies,)),
                   SDS((SUBLANES, 128), F32)),
        in_specs=[HBM] * nb + [ANY] * len(order),
        out_specs=(*[HBM] * nb, SEMAPHORES, SEMAPHORES, pl.BlockSpec(memory_space=pltpu.VMEM)),
        input_output_aliases={i: i for i in range(nb)},
        compiler_params=pltpu.CompilerParams(has_side_effects=pltpu.SideEffectType.DATAFLOW_SIDE_EFFECTING),
    )(*[_in_hbm(b) for b in bufs], *order)
    return res[nb], res[nb + 1], list(res[:nb]), res[-1]


def _copies_wait(build, send_sems, recv_sems, bufs, ns, after, name):
    nb = len(bufs)

    def kern(*refs):
        for cp in build(refs[:ns], refs[ns:nb], refs[nb], refs[nb + 1]):
            cp.wait_send()
            cp.wait_recv()

    res = pl.pallas_call(
        kern, name=name, out_shape=tuple(pltpu.HBM(b.shape, b.dtype) for b in bufs),
        in_specs=[HBM] * nb + [SEMAPHORES, SEMAPHORES, ANY], out_specs=tuple([HBM] * nb),
        input_output_aliases={i: i for i in range(nb)},
        compiler_params=pltpu.CompilerParams(has_side_effects=pltpu.SideEffectType.DATAFLOW_SIDE_EFFECTING),
    )(*bufs, send_sems, recv_sems, after)
    return list(res[:ns]), list(res[ns:])


def _gather_finish(lands, name):
    ns = len(lands)

    def kern(*refs):
        l_refs = refs[:ns]
        send_sems, recv_sems = refs[2 * ns:]
        x, y, c, chips = _place()
        passed = []
        for k, l_ref in enumerate(l_refs):
            rh = l_ref.shape[1] // 2
            for j, (px, py) in enumerate(chips):
                landed = l_ref.at[2 * px + py, pl.ds(c * rh, rh)]
                cp = pltpu.make_async_remote_copy(
                    src_ref=landed, dst_ref=landed, send_sem=send_sems.at[COPIES_PER_ARRAY * k + j],
                    recv_sem=recv_sems.at[COPIES_PER_ARRAY * k + j], device_id=(x, y, 1 - c), device_id_type=MESH)
                cp.start()
                passed.append(cp)
        for k, l_ref in enumerate(l_refs):
            rh = l_ref.shape[1] // 2
            for j, (px, py) in enumerate(chips):
                theirs = l_ref.at[2 * px + py, pl.ds((1 - c) * rh, rh)]
                pltpu.make_async_remote_copy(
                    src_ref=theirs, dst_ref=theirs, send_sem=send_sems.at[COPIES_PER_ARRAY * k + j],
                    recv_sem=recv_sems.at[COPIES_PER_ARRAY * k + j], device_id=(x, y, 1 - c), device_id_type=MESH).wait_recv()
        for cp in passed:
            cp.wait_send()

    return pl.pallas_call(
        kern, out_shape=[SDS(l.shape, l.dtype) for l in lands], in_specs=[ANY] * ns, out_specs=[ANY] * ns,
        input_output_aliases={k: k for k in range(ns)},
        scratch_shapes=[pltpu.SemaphoreType.DMA((COPIES_PER_ARRAY * ns,)), pltpu.SemaphoreType.DMA((COPIES_PER_ARRAY * ns,))],
        name=name)(*lands)


def _add_sibling(full, theirs, out_dtype, name):
    n, r, cols = full.shape
    rh = r // 2
    tr = _tile(rh, ROW_TILE)
    nb = rh // tr

    def kern(c_ref, a_ref, b_ref, o_ref):
        o_ref[...] = (a_ref[...] + b_ref[...]).astype(out_dtype)

    c = lax.axis_index("c").astype(jnp.int32).reshape(1)
    return pl.pallas_call(
        kern, out_shape=SDS((n, rh, cols), out_dtype),
        grid_spec=pltpu.PrefetchScalarGridSpec(
            num_scalar_prefetch=1, grid=(n, nb),
            in_specs=[pl.BlockSpec((1, tr, cols), lambda j, i, c_ref: (j, c_ref[0] * nb + i, 0)),
                      pl.BlockSpec((1, tr, cols), lambda j, i, c_ref: (j, i, 0))],
            out_specs=pl.BlockSpec((1, tr, cols), lambda j, i, c_ref: (j, i, 0))),
        compiler_params=_params("arbitrary", "arbitrary"), name=name)(c, full, theirs)


def _add_owners(partial, received, name, into=None, layer=0, depth=1):
    n, rh, cols = partial.shape
    tr = _tile(rh, ROW_TILE)
    nb = rh // tr
    stacked = [] if into is None else [into]

    def kern(s_ref, a_ref, b_ref, *refs):
        acc = a_ref[0].astype(F32)
        for j in range(N_CHIPS - 1):
            acc = acc + b_ref[j].astype(F32)
        refs[-1][0] = acc

    place = jnp.stack([2 * lax.axis_index("x") + lax.axis_index("y"), lax.axis_index("c")]).astype(jnp.int32)
    return pl.pallas_call(
        kern, out_shape=SDS((depth, 2 * rh, cols), F32),
        grid_spec=pltpu.PrefetchScalarGridSpec(
            num_scalar_prefetch=1, grid=(nb,),
            in_specs=[pl.BlockSpec((1, tr, cols), lambda i, s_ref: (s_ref[0], i, 0)),
                      pl.BlockSpec((N_CHIPS - 1, tr, cols), lambda i, s_ref: (0, i, 0))] + [ANY] * len(stacked),
            out_specs=pl.BlockSpec((1, tr, cols), lambda i, s_ref: (layer, s_ref[1] * nb + i, 0))),
        input_output_aliases={3: 0} if stacked else {},
        compiler_params=_params("arbitrary"), name=name)(place, partial, received, *stacked)


def _pack(parts, lead):
    lead_shape = parts[0].shape[:lead]
    flat = jnp.concatenate([p.reshape(lead_shape + (-1,)) for p in parts], axis=-1)
    n = flat.shape[-1]
    rows = _round_up(-(-n // PACK_W), PACK_ROWS)
    flat = jnp.pad(flat, [(0, 0)] * lead + [(0, rows * PACK_W - n)])
    return flat.reshape(lead_shape + (rows, PACK_W))


def _unpack(buf, shapes, lead):
    lead_shape = buf.shape[:lead]
    flat = buf.reshape(lead_shape + (-1,))
    out, off = [], 0
    for s in shapes:
        n = math.prod(s)
        out.append(flat[..., off:off + n].reshape(lead_shape + tuple(s)))
        off += n
    return out


def _to_slots(full, axis):
    s = full.shape
    split = full.reshape(s[:axis] + (N_CHIPS, s[axis] // N_CHIPS) + s[axis + 1:])
    return jnp.moveaxis(split, axis, 0)


def _from_slots(slots, axis):
    moved = jnp.moveaxis(slots, 0, axis)
    s = moved.shape
    return moved.reshape(s[:axis] + (s[axis] * s[axis + 1],) + s[axis + 2:])


MATMUL_WEIGHTS = (("w_in", 1), ("w_out", 0), ("w_q", 0), ("w_kv", 1), ("w_o", 0), ("w_ff1", 1), ("w_ff2", 0))
CONV_WEIGHTS = (("conv_a_w", 1), ("conv_c_w", 1))
REPLICATED = ("conv_a_b", "ln_a_g", "ln_a_b", "ln_v_g", "ln_v_b", "w_s", "b_s",
              "ln1_g", "ln1_b", "ln2_g", "ln2_b", "ln3_g", "ln3_b")
WEIGHT_ORDER = ("w_in", "conv_a_w", "conv_a_b", "ln_a_g", "ln_a_b", "ln_v_g", "ln_v_b", "w_s", "b_s", "conv_c_w", "w_out",
                "ln1_g", "ln1_b", "w_q", "w_kv", "w_o", "ln2_g", "ln2_b", "w_ff1", "w_ff2", "ln3_g", "ln3_b")


def _row(v):
    return v.reshape(1, -1)


def kernel(x, mem, w_in, conv_a_w, conv_a_b, ln_a_g, ln_a_b, ln_v_g, ln_v_b, w_s, b_s, conv_c_w, w_out, ln1_g, ln1_b, w_q, w_kv, w_o, ln2_g, ln2_b, w_ff1, w_ff2, ln3_g, ln3_b, loss_target, m_w_in, m_conv_a_w, m_conv_a_b, m_ln_a_g, m_ln_a_b, m_ln_v_g, m_ln_v_b, m_w_s, m_b_s, m_conv_c_w, m_w_out, m_ln1_g, m_ln1_b, m_w_q, m_w_kv, m_w_o, m_ln2_g, m_ln2_b, m_w_ff1, m_w_ff2, m_ln3_g, m_ln3_b, v_w_in, v_conv_a_w, v_conv_a_b, v_ln_a_g, v_ln_a_b, v_ln_v_g, v_ln_v_b, v_w_s, v_b_s, v_conv_c_w, v_w_out, v_ln1_g, v_ln1_b, v_w_q, v_w_kv, v_w_o, v_ln2_g, v_ln2_b, v_w_ff1, v_w_ff2, v_ln3_g, v_ln3_b):
    given = dict(locals())
    weights = {n: given[n] for n in WEIGHT_ORDER}
    moment1 = {n: given["m_" + n] for n in WEIGHT_ORDER}
    moment2 = {n: given["v_" + n] for n in WEIGHT_ORDER}

    depth = w_in.shape[0]
    batch, seq, d = x.shape
    t = batch * seq
    hd = d // HEADS
    d_a, d_b, d_c = A_HEADS * hd, B_HEADS * hd, C_HEADS * hd
    widths = (d_a, d_a, d_b, d_b, d_c, d_c, d_c)
    in_offs = [sum(widths[:k]) for k in range(len(widths))]
    alpha = (2.0 * depth) ** 0.25
    layers = range(depth)
    tm = _tile(t, MATMUL_ROW_TILE)
    tm_wide = _tile(t, WIDE_ROW_TILE)

    conv_shards = [weights[n][l] for l in layers for n, _ in CONV_WEIGHTS]
    names = [n for n, _ in MATMUL_WEIGHTS]
    first_used = names[:1]
    parts = {l: [first_used, [n for n in names if n not in first_used]] if l == 0 else [names] for l in layers}
    axis_of = dict(MATMUL_WEIGHTS)
    groups = {}

    def gather_start(l, p, after):
        part = parts[l][p]
        shards = [weights[n][l].astype(BF16) for n in part] + ([_pack(conv_shards, 0)] if (l, p) == (0, 0) else [])
        lands = [lax.empty((N_CHIPS,) + s.shape, s.dtype) for s in shards]
        send_sems, recv_sems, bufs, token = _copies_start(_gather_copies, GATHER_COPIES, shards, lands, f"gather_start_{l}_{p}", after)
        groups[l, p] = (part, send_sems, recv_sems, bufs, len(shards))
        return token

    sequence = [(l, p) for l in layers for p in range(len(parts[l]))]

    def gathered(l, p, after):
        part, send_sems, recv_sems, bufs, ns = groups[l, p]
        _, lands = _copies_wait(_gather_copies, send_sems, recv_sems, bufs, ns, after, f"gather_wait_{l}_{p}")
        slots = _gather_finish(lands, "gather_finish")
        nxt = sequence.index((l, p)) + 1
        started = gather_start(*sequence[nxt], slots[0]) if nxt < len(sequence) else None
        return {n: _from_slots(slots[k], axis_of[n]) for k, n in enumerate(part)}, slots[len(part):], started

    xf = x.reshape(t, d)
    xb = (xf + gather_start(0, 0, None)[0, 0]).astype(BF16)
    memb = mem.reshape(-1, d).astype(BF16)
    saved, full, conv_full = [], [], None
    stream = (xf,)
    for l in layers:
        w, extra, anchor = gathered(l, 0, xb)
        if l == 0:
            conv_full = _unpack(extra[0], [s.shape for s in conv_shards], 1)
        for k, (n, axis) in enumerate(CONV_WEIGHTS):
            w[n] = _from_slots(conv_full[l * len(CONV_WEIGHTS) + k], axis)
        for n in REPLICATED:
            w[n] = weights[n][l]
        w["w_s_t"] = jnp.swapaxes(w["w_s"], 1, 2)
        w["bias_full"] = jnp.repeat(w["b_s"].T, hd, axis=1)
        full.append(w)
        s = {"x0b": xb}
        proj = _in_proj(xb, w["w_in"], widths, "in_proj", after=anchor)
        s["proj"] = proj
        a_val, a_gate, b_u, b_v, c_b, c_c, c_x = proj
        a_out, s["ac"] = _mix_a_fwd(a_val, a_gate, w["conv_a_w"], _row(w["conv_a_b"]), _row(w["ln_a_g"]), _row(w["ln_a_b"]),
                                    batch, "mix_a_fwd")
        b_out = _mix_b_fwd(b_u, b_v, w["w_s"], w["bias_full"], _row(w["ln_v_g"]), _row(w["ln_v_b"]), "mix_b_fwd")
        c_out = _mix_c_fwd(c_b, c_c, c_x, w["conv_c_w"], batch, "mix_c_fwd")
        s["cat"] = (a_out, b_out, c_out)
        anchor = None
        for p in range(1, len(parts[l])):
            more, _, anchor = gathered(l, p, c_out)
            w.update(more)
        s["z1"], xb = _proj_ln([a_out, b_out, c_out], w["w_out"], stream, _row(w["ln1_g"]), _row(w["ln1_b"]), alpha, tm, "out_proj_ln",
                               after=anchor)
        stream = (s["z1"], _row(w["ln1_g"]), _row(w["ln1_b"]))
        s["x1b"] = xb
        s["q"] = _matmul(xb, w["w_q"], "q_proj")
        s["kv"] = _matmul(memb, w["w_kv"], "kv_proj")
        s["o"] = _attention_fwd(s["q"], s["kv"], batch, "attention_fwd")
        s["z2"], xb = _proj_ln([s["o"]], w["w_o"], stream, _row(w["ln2_g"]), _row(w["ln2_b"]), alpha, tm, "o_proj_ln")
        stream = (s["z2"], _row(w["ln2_g"]), _row(w["ln2_b"]))
        s["x2b"] = xb
        s["h"], s["r"] = _ff1(xb, w["w_ff1"], "ff1")
        s["z3"], xb = _proj_ln([s["r"]], w["w_ff2"], stream, _row(w["ln3_g"]), _row(w["ln3_b"]), alpha, tm_wide, "ff2_ln")
        stream = (s["z3"], _row(w["ln3_g"]), _row(w["ln3_b"]))
        saved.append(s)

    target = loss_target.reshape(t, d)
    dz3, dz3b, top_g, top_b, loss_block = _loss_head_ln(saved[-1]["z3"], target, _row(full[-1]["ln3_g"]), _row(full[-1]["ln3_b"]),
                                                        "loss_head_ln3")
    loss = lax.psum(loss_block[0, 0], ("x", "y", "c"))
    grads = [{} for _ in layers]
    grads[-1]["ln3_g"], grads[-1]["ln3_b"] = top_g, top_b
    reductions = []
    swapping = {}
    token, last_started = 0.0, None

    def rows_to_slots(full_):
        return full_.reshape(N_CHIPS, full_.shape[1] // N_CHIPS, full_.shape[2])

    def swap_start(key, named, wire):
        arrays = [a for _, a in named]
        lands = [lax.empty((f.shape[0], f.shape[1] // 2, f.shape[2]), f.dtype) for f in arrays]
        send_sems, recv_sems, bufs, started = _copies_start(_swap_copies, 1, arrays, lands, f"rs_swap_start_{key}")
        swapping[key] = ([n for n, _ in named], send_sems, recv_sems, bufs, wire)
        return started

    def swap_finish(key, after):
        nonlocal token, last_started
        part, send_sems, recv_sems, bufs, wire = swapping.pop(key)
        fulls, theirs = _copies_wait(_swap_copies, send_sems, recv_sems, bufs, len(wire), after, f"rs_swap_wait_{key}")
        sums = [_add_sibling(f, th, dt, "rs_add_sibling") for f, th, dt in zip(fulls, theirs, wire)]
        lands = [lax.empty((N_CHIPS - 1,) + p.shape[1:], p.dtype) for p in sums]
        send_sems, recv_sems, bufs, started = _copies_start(_owner_copies, COPIES_PER_ARRAY, sums, lands, f"rs_send_start_{key}")
        reductions.append((key, part, send_sems, recv_sems, bufs))
        token, last_started = started[0, 0], started

    early = ("w_ff2", "w_ff1", "w_o", "w_kv", "w_q")
    swap_started, pending = None, None
    for l in reversed(layers):
        w, s, g = full[l], saved[l], grads[l]
        g["w_ff2"] = rows_to_slots(_mm_tn(s["r"], dz3b, "d_w_ff2", after=swap_started))
        dh = _bwd_ff2(dz3b, s["h"], w["w_ff2"], "d_ff_hidden", after=swap_started)
        if pending is not None:
            swap_finish(pending, dh)
        g["w_ff1"] = _mm_tn(s["x2b"], dh, "d_w_ff1", slots=N_CHIPS)
        dz2, dz2b, g["ln2_g"], g["ln2_b"] = _bwd_data_ln([dh], w["w_ff1"], [0], dz3, alpha, s["z2"], _row(w["ln2_g"]) + token,
                                                         tm_wide, "d_x2_ln2")
        g["w_o"] = rows_to_slots(_mm_tn(s["o"], dz2b, "d_w_o"))
        do = _bwd_data([dz2b], w["w_o"], [0], None, alpha, BF16, tm, "d_att_out")
        dq, dk, dv = _attention_bwd(s["q"], s["kv"], do, batch, "attention_bwd")
        g["w_kv"] = _mm_tn(memb, jnp.concatenate([dk, dv], axis=1), "d_w_kv", slots=N_CHIPS)
        g["w_q"] = rows_to_slots(_mm_tn(s["x1b"], dq, "d_w_q"))
        dz1, dz1b, g["ln1_g"], g["ln1_b"] = _bwd_data_ln([dq], w["w_q"], [0], dz2, alpha, s["z1"], _row(w["ln1_g"]), tm, "d_x1_ln1")
        early_started = swap_start("0a", [(n, g[n]) for n in early], [BF16] * len(early)) if l == 0 else None
        g["w_out"] = _mm_tn_rows(s["cat"], dz1b, "d_w_out").reshape(N_CHIPS, d // N_CHIPS, d)
        da_, dbo, dco = _bwd_out_proj(dz1b, w["w_out"], (d_a, d_b, d_c), "d_mixer_out", after=early_started)
        a_val, a_gate, b_u, b_v, c_b, c_c, c_x = s["proj"]
        dav, dag, g["conv_a_w"], g["conv_a_b"], g["ln_a_g"], g["ln_a_b"] = _mix_a_bwd(
            a_val, a_gate, s["ac"], da_, w["conv_a_w"], _row(w["ln_a_g"]), _row(w["ln_a_b"]), batch, "mix_a_bwd")
        if l == 0:
            swap_finish("0a", dav)
        dbu, dbv, g["w_s"], dbs, g["ln_v_g"], g["ln_v_b"] = _mix_b_bwd(
            b_u, b_v, dbo, w["w_s"], w["w_s_t"], w["bias_full"], _row(w["ln_v_g"]) + token, _row(w["ln_v_b"]), "mix_b_bwd")
        g["b_s"] = dbs.reshape(B_HEADS, CHUNK)
        dcb, dcc, dcx, g["conv_c_w"] = _mix_c_bwd(c_b, c_c, c_x, dco, w["conv_c_w"], batch, "mix_c_bwd")
        dproj = [dav, dag, dbu, dbv, dcb, dcc, dcx]
        g["w_in"] = _to_slots(_mm_tn_cols(s["x0b"], dproj, "d_w_in"), 1)
        if l > 0:
            below = full[l - 1]
            dz3, dz3b, grads[l - 1]["ln3_g"], grads[l - 1]["ln3_b"] = _bwd_data_ln(
                dproj, w["w_in"], in_offs, dz1, alpha, saved[l - 1]["z3"], _row(below["ln3_g"]), tm, "d_x0_ln3")
            pending = str(l)
            swap_started = swap_start(pending, [(n, g[n]) for n, _ in MATMUL_WEIGHTS], [BF16] * len(MATMUL_WEIGHTS))
        else:
            dx = _bwd_data(dproj, w["w_in"], in_offs, dz1, alpha, F32, tm, "d_x0")
            conv_slots = [_to_slots(grads[k][n], axis) for k in layers for n, axis in CONV_WEIGHTS]
            rep_parts = [jnp.stack([grads[k][n].reshape(weights[n].shape[1:]) for k in layers]) for n in REPLICATED]
            rep_flat = jnp.concatenate([p.reshape(-1) for p in rep_parts])
            n_rep = rep_flat.shape[0]
            per_chip = _round_up(-(-n_rep // N_CHIPS), PACK_W * PACK_ROWS)
            rep_slots = jnp.pad(rep_flat, (0, N_CHIPS * per_chip - n_rep)).reshape(N_CHIPS, per_chip)
            late = [(n, g[n]) for n, _ in MATMUL_WEIGHTS if n not in early] + [("small", _pack(conv_slots + [rep_slots], 1))]
            swap_start("0b", late, [BF16] * (len(late) - 1) + [F32])
            swap_finish("0b", dx)
    grad_x = dx.reshape(batch, seq, d)

    reduced, after = {}, last_started
    for key, part, send_sems, recv_sems, bufs in reductions:
        sums, lands = _copies_wait(_owner_copies, send_sems, recv_sems, bufs, len(part), after, f"rs_send_wait_{key}")
        for n, p, r in zip(part, sums, lands):
            stacked = n != "small"
            reduced[n] = _add_owners(p, r, "rs_add_owners", reduced.get(n), int(key[0]) if stacked else 0, depth if stacked else 1)
            after = reduced[n]
    reduced = dict(zip(reduced, _join_halves(list(reduced.values()), "rs_join")))
    small = _unpack(reduced["small"][0], [c.shape[1:] for c in conv_slots] + [(per_chip,)], 0)
    conv_grad, rep_mine = small[:-1], small[-1]
    rep_all = _all_gather_chips(rep_mine.reshape(-1, PACK_W), "gather_small_grads").reshape(-1)[:n_rep]

    grad = {}
    for n, _ in MATMUL_WEIGHTS:
        grad[n] = reduced[n]
    for k, (n, _) in enumerate(CONV_WEIGHTS):
        grad[n] = jnp.stack([conv_grad[l * len(CONV_WEIGHTS) + k] for l in layers])
    off = 0
    for n in REPLICATED:
        size = math.prod(weights[n].shape)
        grad[n] = rep_all[off:off + size].reshape(weights[n].shape)
        off += size

    delta, new_m, new_v = {}, {}, {}
    for n, _ in MATMUL_WEIGHTS:
        shape = weights[n].shape
        as_rows = lambda a: a.reshape(-1, shape[-1])
        delta[n], new_m[n], new_v[n] = (
            r.reshape(shape) for r in _adamw(as_rows(weights[n]), as_rows(grad[n]), as_rows(moment1[n]), as_rows(moment2[n]), "adamw"))
    small_names = [n for n, _ in CONV_WEIGHTS] + list(REPLICATED)
    small_shapes = [weights[n].shape for n in small_names]
    packed = [_pack([src[n] for n in small_names], 0) for src in (weights, grad, moment1, moment2)]
    for dst, res in zip((delta, new_m, new_v), _adamw(*packed, "adamw_small")):
        for n, a in zip(small_names, _unpack(res, small_shapes, 0)):
            dst[n] = a

    return (loss, grad_x, *[grad[n] for n in WEIGHT_ORDER], *[delta[n] for n in WEIGHT_ORDER],
            *[new_m[n] for n in WEIGHT_ORDER], *[new_v[n] for n in WEIGHT_ORDER])
```

```python
import functools
import math

import jax
import jax.numpy as jnp
from jax import lax
from jax.experimental import pallas as pl
from jax.experimental.pallas import tpu as pltpu

F32 = jnp.float32
BF16 = jnp.bfloat16
SDS = jax.ShapeDtypeStruct

HEADS = 16
A_HEADS, B_HEADS, C_HEADS = 6, 4, 6
X_HEADS = 4
CHUNK = 128
LN_EPS = 1e-5
ADAM_LR, ADAM_B1, ADAM_B2, ADAM_EPS, ADAM_WD, ADAM_STEP = 0.001, 0.9, 0.999, 1e-08, 0.01, 10

N_CHIPS = 4
V7X_VMEM_LIMIT = 56 << 20
SUBLANES = 8
PACK_W = 1024
PACK_ROWS = 32
ROW_TILE = 512
MATMUL_ROW_TILE = 1024
WIDE_ROW_TILE = 512
CONV_ROWS = 1024
SUB_ROWS = 64
ATT_ROWS = 2048
TN_TILE = 1024
TN_TOKENS = 2048
MESH = pl.DeviceIdType.MESH
ANY = pl.BlockSpec(memory_space=pl.ANY)


def _tile(n, t):
    for d in range(min(n, t), 0, -1):
        if n % d == 0 and d % (2 * SUBLANES) == 0:
            return d
    return n


def _round_up(n, m):
    return -(-n // m) * m


def _params(*sem):
    return pltpu.CompilerParams(dimension_semantics=sem or None, vmem_limit_bytes=V7X_VMEM_LIMIT)


def _dot(a, b):
    return jnp.dot(a, b, preferred_element_type=F32)


def _dot_nt(a, b):
    return lax.dot_general(a, b, (((1,), (1,)), ((), ())), preferred_element_type=F32)


def _dot_tn(a, b):
    return lax.dot_general(a, b, (((0,), (0,)), ((), ())), preferred_element_type=F32)


def _sigmoid(x):
    return 1.0 / (1.0 + jnp.exp(-x))


def _gelu(x):
    return 0.5 * x * (1.0 + lax.erf(x * (2.0 ** -0.5)))


def _gelu_grad(x):
    return 0.5 * (1.0 + lax.erf(x * (2.0 ** -0.5))) + x * jnp.exp(-0.5 * x * x) * ((2.0 * math.pi) ** -0.5)


def _ln_stats(z):
    mu = jnp.mean(z, axis=-1, keepdims=True)
    zc = z - mu
    rstd = lax.rsqrt(jnp.mean(zc * zc, axis=-1, keepdims=True) + LN_EPS)
    return zc * rstd, rstd


def _ln_bwd(dy, y, rstd, g):
    dyh = dy * g
    return rstd * (dyh - jnp.mean(dyh, axis=-1, keepdims=True) - y * jnp.mean(dyh * y, axis=-1, keepdims=True))


def _colsum(x):
    return jnp.sum(x, axis=0, keepdims=True)


def _rowwise(body, rows, consts, outs, accs=(), *, tm, name, after=None):
    t = rows[0].shape[0]
    steps = t // tm
    n_in = len(rows) + len(consts)
    order = [] if after is None else [after]

    def kern(*refs):
        body(pl.program_id(0), steps, *refs[:n_in], *refs[n_in + len(order):])

    def whole(a, **kw):
        return pl.BlockSpec(a.shape, lambda i, nd=len(a.shape): (0,) * nd, **kw)

    in_specs = ([pl.BlockSpec((tm, r.shape[1]), lambda i: (i, 0)) for r in rows]
                + [whole(c, pipeline_mode=pl.Buffered(1)) for c in consts] + [ANY] * len(order))
    out_shape = [SDS((t, n), dt) for n, dt in outs] + [SDS(s, dt) for s, dt in accs]
    out_specs = [pl.BlockSpec((tm, n), lambda i: (i, 0)) for n, _ in outs] + [whole(SDS(s, dt)) for s, dt in accs]
    return pl.pallas_call(kern, grid=(steps,), in_specs=in_specs, out_specs=out_specs, out_shape=out_shape,
                          compiler_params=_params("arbitrary"), name=name)(*rows, *consts, *order)


def _in_proj(xb, w_in, widths, name, after=None):
    offs = [sum(widths[:k]) for k in range(len(widths))]

    def body(i, steps, x_ref, w_ref, *o_refs):
        x = x_ref[...]
        for o_ref, off, n in zip(o_refs, offs, widths):
            o_ref[...] = _dot(x, w_ref[:, off:off + n])

    return _rowwise(body, [xb], [w_in], [(n, F32) for n in widths], tm=_tile(xb.shape[0], MATMUL_ROW_TILE), name=name, after=after)


def _matmul(ab, w, name):
    def body(i, steps, a_ref, w_ref, o_ref):
        o_ref[...] = _dot(a_ref[...], w_ref[...]).astype(BF16)

    return _rowwise(body, [ab], [w], [(w.shape[1], BF16)], tm=_tile(ab.shape[0], MATMUL_ROW_TILE), name=name)[0]


def _ff1(xb, w, name):
    def body(i, steps, a_ref, w_ref, h_ref, r_ref):
        h = _dot(a_ref[...], w_ref[...])
        h_ref[...] = h.astype(BF16)
        r = jnp.maximum(h, 0.0)
        r_ref[...] = (r * r).astype(BF16)

    n = w.shape[1]
    return _rowwise(body, [xb], [w], [(n, BF16), (n, BF16)], tm=_tile(xb.shape[0], MATMUL_ROW_TILE), name=name)


def _proj_ln(a_list, w, stream, g, b, alpha, tm, name, after=None):
    widths = [a.shape[1] for a in a_list]
    offs = [sum(widths[:k]) for k in range(len(widths))]
    na = len(a_list)
    from_ln = len(stream) == 3

    def body(i, steps, *refs):
        a_refs, rest = refs[:na], refs[na:]
        if from_ln:
            s_ref, w_ref, sg_ref, sb_ref, g_ref, b_ref, z_ref, xb_ref = rest
            x = _ln_stats(s_ref[...])[0] * sg_ref[...] + sb_ref[...]
        else:
            s_ref, w_ref, g_ref, b_ref, z_ref, xb_ref = rest
            x = s_ref[...]
        acc = alpha * x
        for a_ref, off, n in zip(a_refs, offs, widths):
            acc = acc + _dot(a_ref[...], w_ref[off:off + n, :])
        z_ref[...] = acc
        y, _ = _ln_stats(acc)
        xb_ref[...] = (y * g_ref[...] + b_ref[...]).astype(BF16)

    d = w.shape[1]
    return _rowwise(body, [*a_list, stream[0]], [w, *stream[1:], g, b], [(d, F32), (d, BF16)], tm=tm, name=name, after=after)


def _bwd_ff2(dzb, h, w_ff2, name, after=None):
    def body(i, steps, dz_ref, h_ref, w_ref, dh_ref):
        dr = _dot_nt(dz_ref[...], w_ref[...])
        dh_ref[...] = (dr * (2.0 * jnp.maximum(h_ref[...].astype(F32), 0.0))).astype(BF16)

    return _rowwise(body, [dzb, h], [w_ff2], [(h.shape[1], BF16)], tm=_tile(h.shape[0], MATMUL_ROW_TILE), name=name, after=after)[0]


def _bwd_nt(g_list, w, col_offs, res, alpha, tm, name):
    ng = len(g_list)
    widths = [g.shape[1] for g in g_list]

    def body(i, steps, *refs):
        g_refs = refs[:ng]
        if res is None:
            w_ref, o_ref = refs[ng:]
            acc = None
        else:
            r_ref, w_ref, o_ref = refs[ng:]
            acc = alpha * r_ref[...]
        for g_ref, off, n in zip(g_refs, col_offs, widths):
            part = _dot_nt(g_ref[...], w_ref[:, off:off + n])
            acc = part if acc is None else acc + part
        o_ref[...] = acc.astype(o_ref.dtype)

    rows = list(g_list) + ([] if res is None else [res])
    return rows, w, body, tm, name


def _bwd_data(g_list, w, col_offs, res, alpha, out_dtype, tm, name):
    rows, w, body, tm, name = _bwd_nt(g_list, w, col_offs, res, alpha, tm, name)
    return _rowwise(body, rows, [w], [(w.shape[0], out_dtype)], tm=tm, name=name)[0]


def _bwd_data_ln(g_list, w, col_offs, res, alpha, z, gain, tm, name):
    ng = len(g_list)
    widths = [g.shape[1] for g in g_list]

    def body(i, steps, *refs):
        g_refs = refs[:ng]
        r_ref, z_ref, w_ref, gain_ref, dz_ref, dzb_ref, dg_ref, db_ref = refs[ng:]

        @pl.when(i == 0)
        def _():
            dg_ref[...] = jnp.zeros_like(dg_ref)
            db_ref[...] = jnp.zeros_like(db_ref)

        dy = alpha * r_ref[...]
        for g_ref, off, n in zip(g_refs, col_offs, widths):
            dy = dy + _dot_nt(g_ref[...], w_ref[:, off:off + n])
        y, rstd = _ln_stats(z_ref[...])
        dz = _ln_bwd(dy, y, rstd, gain_ref[...])
        dz_ref[...] = dz
        dzb_ref[...] = dz.astype(BF16)
        dg_ref[...] += _colsum(dy * y)
        db_ref[...] += _colsum(dy)

    d = z.shape[1]
    return _rowwise(body, [*g_list, res, z], [w, gain], [(d, F32), (d, BF16)], [((1, d), F32), ((1, d), F32)], tm=tm, name=name)


def _loss_head_ln(z, target, gain, bias, name):
    d = z.shape[1]

    def body(i, steps, z_ref, t_ref, gain_ref, bias_ref, dz_ref, dzb_ref, dg_ref, db_ref, l_ref):
        @pl.when(i == 0)
        def _():
            for r in (dg_ref, db_ref, l_ref):
                r[...] = jnp.zeros_like(r)

        yn, rstd = _ln_stats(z_ref[...])
        err = yn * gain_ref[...] + bias_ref[...] - t_ref[...]
        l_ref[...] += jnp.sum(err * err) * (0.5 / d)
        dy = err * (1.0 / d)
        dz = _ln_bwd(dy, yn, rstd, gain_ref[...])
        dz_ref[...] = dz
        dzb_ref[...] = dz.astype(BF16)
        dg_ref[...] += _colsum(dy * yn)
        db_ref[...] += _colsum(dy)

    return _rowwise(body, [z, target], [gain, bias], [(d, F32), (d, BF16)],
                    [((1, d), F32), ((1, d), F32), ((SUBLANES, 128), F32)], tm=_tile(z.shape[0], ROW_TILE), name=name)


def _bwd_out_proj(dzb, w_out, widths, name, after=None):
    offs = [sum(widths[:k]) for k in range(len(widths))]

    def body(i, steps, dz_ref, w_ref, *o_refs):
        dz = dz_ref[...]
        for o_ref, off, n in zip(o_refs, offs, widths):
            o_ref[...] = _dot_nt(dz, w_ref[off:off + n, :])

    return _rowwise(body, [dzb], [w_out], [(n, F32) for n in widths], tm=_tile(dzb.shape[0], MATMUL_ROW_TILE), name=name,
                    after=after)


def _adamw(w, g, m, v, name):
    def body(i, steps, w_ref, g_ref, m_ref, v_ref, d_ref, nm_ref, nv_ref):
        g_ = g_ref[...]
        nm = ADAM_B1 * m_ref[...] + (1.0 - ADAM_B1) * g_
        nv = ADAM_B2 * v_ref[...] + (1.0 - ADAM_B2) * (g_ * g_)
        m_hat = nm / (1.0 - ADAM_B1 ** ADAM_STEP)
        v_hat = nv / (1.0 - ADAM_B2 ** ADAM_STEP)
        d_ref[...] = -ADAM_LR * (m_hat / (jnp.sqrt(v_hat) + ADAM_EPS) + ADAM_WD * w_ref[...])
        nm_ref[...] = nm
        nv_ref[...] = nv

    c = w.shape[1]
    return _rowwise(body, [w, g, m, v], [], [(c, F32)] * 3, tm=_tile(w.shape[0], ROW_TILE), name=name)


def _mm_tn(a, g, name, slots=1, after=None):
    t, ka = a.shape
    n = g.shape[1]
    ta, tn, tk = _tile(ka, TN_TILE), _tile(n // slots, TN_TILE), _tile(t, TN_TOKENS)
    per = n // slots // tn

    order = [] if after is None else [after]

    def kern(a_ref, g_ref, *refs):
        o_ref = refs[-1]

        @pl.when(pl.program_id(2) == 0)
        def _():
            o_ref[...] = jnp.zeros_like(o_ref)

        o_ref[0] += _dot_tn(a_ref[...], g_ref[...])

    return pl.pallas_call(
        kern, grid=(ka // ta, n // tn, t // tk),
        in_specs=[pl.BlockSpec((tk, ta), lambda i, j, k: (k, i)), pl.BlockSpec((tk, tn), lambda i, j, k: (k, j))] + [ANY] * len(order),
        out_specs=pl.BlockSpec((1, ta, tn), lambda i, j, k: (j // per, i, j % per)), out_shape=SDS((slots, ka, n // slots), F32),
        compiler_params=_params("arbitrary", "arbitrary", "arbitrary"), name=name)(a, g, *order)


def _mm_tn_cols(a, g_list, name):
    t, ka = a.shape
    widths = [g.shape[1] for g in g_list]
    offs = [sum(widths[:k]) for k in range(len(widths))]
    ta, tk = _tile(ka, ROW_TILE), _tile(t, TN_TOKENS)

    def kern(a_ref, *refs):
        g_refs, o_ref = refs[:-1], refs[-1]

        @pl.when(pl.program_id(1) == 0)
        def _():
            o_ref[...] = jnp.zeros_like(o_ref)

        a_ = a_ref[...]
        for g_ref, off, n in zip(g_refs, offs, widths):
            o_ref[:, off:off + n] += _dot_tn(a_, g_ref[...])

    return pl.pallas_call(
        kern, grid=(ka // ta, t // tk),
        in_specs=[pl.BlockSpec((tk, ta), lambda i, k: (k, i))] + [pl.BlockSpec((tk, n), lambda i, k: (k, 0)) for n in widths],
        out_specs=pl.BlockSpec((ta, sum(widths)), lambda i, k: (i, 0)), out_shape=SDS((ka, sum(widths)), F32),
        compiler_params=_params("arbitrary", "arbitrary"), name=name)(a, *g_list)


def _mm_tn_rows(a_list, g, name):
    t, n = g.shape
    widths = [a.shape[1] for a in a_list]
    offs = [sum(widths[:k]) for k in range(len(widths))]
    tn, tk = _tile(n, TN_TILE), _tile(t, TN_TOKENS)

    def kern(*refs):
        a_refs, g_ref, o_ref = refs[:-2], refs[-2], refs[-1]

        @pl.when(pl.program_id(1) == 0)
        def _():
            o_ref[...] = jnp.zeros_like(o_ref)

        g_ = g_ref[...]
        for a_ref, off, ka in zip(a_refs, offs, widths):
            o_ref[off:off + ka, :] += _dot_tn(a_ref[...], g_)

    return pl.pallas_call(
        kern, grid=(n // tn, t // tk),
        in_specs=[pl.BlockSpec((tk, ka), lambda j, k: (k, 0)) for ka in widths] + [pl.BlockSpec((tk, tn), lambda j, k: (k, j))],
        out_specs=pl.BlockSpec((sum(widths), tn), lambda j, k: (0, j)), out_shape=SDS((sum(widths), n), F32),
        compiler_params=_params("arbitrary", "arbitrary"), name=name)(*a_list, g)


def _conv_geometry(seq, taps):
    pad = _round_up(taps - 1, SUBLANES)
    rc = _tile(seq, CONV_ROWS)
    assert rc % pad == 0 and seq % rc == 0
    return pad, rc, seq // rc


def _chunk_spec(rc, n, nch):
    return pl.BlockSpec((rc, n), lambda b, i: (b * nch + i, 0))


def _prev_halo_spec(pad, rc, n, nch):
    per = rc // pad
    return pl.BlockSpec((pad, n), lambda b, i: (jnp.maximum((b * nch + i) * per - 1, 0), 0))


def _next_halo_spec(pad, rc, n, nch, total_rows):
    per = rc // pad
    last = total_rows // pad - 1
    return pl.BlockSpec((pad, n), lambda b, i: (jnp.minimum((b * nch + i + 1) * per, last), 0))


def _whole2(a):
    return pl.BlockSpec(a.shape, lambda b, i, nd=len(a.shape): (0,) * nd)


def _sub_rows(rc):
    return SUB_ROWS if rc % SUB_ROWS == 0 else rc


def _build_shifts(sh_ref, src_ref, offsets):
    rows = src_ref.shape[0]
    for r in sorted({o % SUBLANES for o in offsets} - {0}):
        sh_ref[r, 0:rows - SUBLANES, :] = src_ref[r:r + rows - SUBLANES, :]


def _read_shifted(sh_ref, src_ref, o, s0, sub):
    r = o % SUBLANES
    a = o - r + s0
    return src_ref[a:a + sub, :] if r == 0 else sh_ref[r, a:a + sub, :]


def _tap_sum(sh_ref, src_ref, w_ref, offsets, s0, sub):
    acc = None
    for k, o in enumerate(offsets):
        term = _read_shifted(sh_ref, src_ref, o, s0, sub) * w_ref[k:k + 1, :]
        acc = term if acc is None else acc + term
    return acc


def _row_groups(x):
    acc = x[0:SUBLANES, :]
    for g0 in range(SUBLANES, x.shape[0], SUBLANES):
        acc = acc + x[g0:g0 + SUBLANES, :]
    return acc


def _mix_a_fwd(a_val, a_gate, conv_w, conv_b, ln_g, ln_b, batch, name):
    t, da = a_val.shape
    taps = conv_w.shape[0]
    pad, rc, nch = _conv_geometry(t // batch, taps)
    sub = _sub_rows(rc)
    offs = [pad - (taps - 1) + k for k in range(taps)]

    def kern(av_ref, ag_ref, pav_ref, pag_ref, w_ref, cb_ref, g_ref, b_ref, a_ref, ac_ref, pad_ref, sh_ref):
        first = pl.program_id(1) == 0
        pad_ref[0:pad, :] = jnp.where(first, 0.0, pav_ref[...] * _sigmoid(pag_ref[...]))
        for s0 in range(0, rc, sub):
            pad_ref[pad + s0:pad + s0 + sub, :] = av_ref[s0:s0 + sub, :] * _sigmoid(ag_ref[s0:s0 + sub, :])
        _build_shifts(sh_ref, pad_ref, offs)
        for s0 in range(0, rc, sub):
            ac = _tap_sum(sh_ref, pad_ref, w_ref, offs, s0, sub) + cb_ref[...]
            ac_ref[s0:s0 + sub, :] = ac
            y, _ = _ln_stats(ac)
            aln = y * g_ref[...] + b_ref[...]
            a_ref[s0:s0 + sub, :] = (aln * _sigmoid(aln)).astype(BF16)

    chunk, halo = _chunk_spec(rc, da, nch), _prev_halo_spec(pad, rc, da, nch)
    return pl.pallas_call(
        kern, grid=(batch, nch),
        in_specs=[chunk, chunk, halo, halo, _whole2(conv_w), _whole2(conv_b), _whole2(ln_g), _whole2(ln_b)],
        out_specs=[chunk, chunk], out_shape=[SDS((t, da), BF16), SDS((t, da), F32)],
        scratch_shapes=[pltpu.VMEM((pad + rc, da), F32), pltpu.VMEM((SUBLANES, pad + rc, da), F32)],
        compiler_params=_params("arbitrary", "arbitrary"), name=name)(a_val, a_gate, a_val, a_gate, conv_w, conv_b, ln_g, ln_b)


def _mix_a_bwd(a_val, a_gate, ac, da_, conv_w, ln_g, ln_b, batch, name):
    t, da = a_val.shape
    taps = conv_w.shape[0]
    pad, rc, nch = _conv_geometry(t // batch, taps)
    sub = _sub_rows(rc)
    offs = [taps - 1 - k for k in range(taps)]

    def kern(av_ref, ag_ref, ac_ref, da_ref, nac_ref, nda_ref, w_ref, g_ref, b_ref,
             dav_ref, dag_ref, dw_ref, dcb_ref, dg_ref, db_ref, dpad_ref, shd_ref, dwacc_ref):
        i = pl.program_id(1)
        start = (pl.program_id(0) == 0) & (i == 0)
        end = (pl.program_id(0) == batch - 1) & (i == nch - 1)

        @pl.when(start)
        def _():
            for r in (dcb_ref, dg_ref, db_ref, dwacc_ref):
                r[...] = jnp.zeros_like(r)

        gain, bias = g_ref[...], b_ref[...]

        def d_conv_out(ac_, dout):
            y, rstd = _ln_stats(ac_)
            aln = y * gain + bias
            sig = _sigmoid(aln)
            daln = dout * (sig * (1.0 + aln * (1.0 - sig)))
            return _ln_bwd(daln, y, rstd, gain), daln, y

        dac_next, _, _ = d_conv_out(nac_ref[...], nda_ref[...])
        dpad_ref[rc:rc + pad, :] = jnp.where(i == nch - 1, 0.0, dac_next)
        for s0 in range(0, rc, sub):
            rows = slice(s0, s0 + sub)
            dac, daln, y = d_conv_out(ac_ref[rows, :], da_ref[rows, :])
            dg_ref[...] += _colsum(daln * y)
            db_ref[...] += _colsum(daln)
            dcb_ref[...] += _colsum(dac)
            dpad_ref[rows, :] = dac
        _build_shifts(shd_ref, dpad_ref, offs)
        for s0 in range(0, rc, sub):
            rows = slice(s0, s0 + sub)
            av = av_ref[rows, :]
            sig = _sigmoid(ag_ref[rows, :])
            glu = av * sig
            dgl = None
            for k, o in enumerate(offs):
                shifted = _read_shifted(shd_ref, dpad_ref, o, s0, sub)
                dwacc_ref[k] += _row_groups(glu * shifted)
                term = shifted * w_ref[k:k + 1, :]
                dgl = term if dgl is None else dgl + term
            dav_ref[rows, :] = (dgl * sig).astype(BF16)
            dag_ref[rows, :] = (dgl * glu * (1.0 - sig)).astype(BF16)

        @pl.when(end)
        def _():
            for k in range(taps):
                dw_ref[k:k + 1, :] = _colsum(dwacc_ref[k])

    chunk = _chunk_spec(rc, da, nch)
    nxt = _next_halo_spec(pad, rc, da, nch, t)
    vec = SDS((1, da), F32)
    return pl.pallas_call(
        kern, grid=(batch, nch),
        in_specs=[chunk, chunk, chunk, chunk, nxt, nxt, _whole2(conv_w), _whole2(ln_g), _whole2(ln_b)],
        out_specs=[chunk, chunk, _whole2(conv_w), _whole2(vec), _whole2(vec), _whole2(vec)],
        out_shape=[SDS((t, da), BF16), SDS((t, da), BF16), SDS(conv_w.shape, F32), vec, vec, vec],
        scratch_shapes=[pltpu.VMEM((rc + pad, da), F32), pltpu.VMEM((SUBLANES, rc + pad, da), F32),
                        pltpu.VMEM((taps, SUBLANES, da), F32)],
        compiler_params=_params("arbitrary", "arbitrary"), name=name)(a_val, a_gate, ac, da_, ac, da_, conv_w, ln_g, ln_b)


def _mix_c_fwd(c_b, c_c, c_x, conv_w, batch, name):
    t, dc = c_b.shape
    taps = conv_w.shape[0]
    pad, rc, nch = _conv_geometry(t // batch, taps)
    sub = _sub_rows(rc)
    offs = [pad - (taps - 1) + k for k in range(taps)]

    def kern(cb_ref, cc_ref, cx_ref, pcc_ref, pcx_ref, w_ref, o_ref, pad_ref, sh_ref):
        pad_ref[0:pad, :] = jnp.where(pl.program_id(1) == 0, 0.0, pcc_ref[...] * pcx_ref[...])
        for s0 in range(0, rc, sub):
            pad_ref[pad + s0:pad + s0 + sub, :] = cc_ref[s0:s0 + sub, :] * cx_ref[s0:s0 + sub, :]
        _build_shifts(sh_ref, pad_ref, offs)
        for s0 in range(0, rc, sub):
            o_ref[s0:s0 + sub, :] = (cb_ref[s0:s0 + sub, :] * _tap_sum(sh_ref, pad_ref, w_ref, offs, s0, sub)).astype(BF16)

    chunk, prev = _chunk_spec(rc, dc, nch), _prev_halo_spec(pad, rc, dc, nch)
    return pl.pallas_call(
        kern, grid=(batch, nch), in_specs=[chunk, chunk, chunk, prev, prev, _whole2(conv_w)],
        out_specs=chunk, out_shape=SDS((t, dc), BF16),
        scratch_shapes=[pltpu.VMEM((pad + rc, dc), F32), pltpu.VMEM((SUBLANES, pad + rc, dc), F32)],
        compiler_params=_params("arbitrary", "arbitrary"), name=name)(c_b, c_c, c_x, c_c, c_x, conv_w)


def _mix_c_bwd(c_b, c_c, c_x, dco, conv_w, batch, name):
    t, dc = c_b.shape
    taps = conv_w.shape[0]
    pad, rc, nch = _conv_geometry(t // batch, taps)
    sub = _sub_rows(rc)
    offs_in = [pad - (taps - 1) + k for k in range(taps)]
    offs_out = [taps - 1 - k for k in range(taps)]

    def kern(cb_ref, cc_ref, cx_ref, do_ref, pcc_ref, pcx_ref, ncb_ref, ndo_ref, w_ref,
             dcb_ref, dcc_ref, dcx_ref, dw_ref, ppad_ref, dpad_ref, shp_ref, shd_ref, dwacc_ref):
        i = pl.program_id(1)

        @pl.when((pl.program_id(0) == 0) & (i == 0))
        def _():
            dwacc_ref[...] = jnp.zeros_like(dwacc_ref)

        ppad_ref[0:pad, :] = jnp.where(i == 0, 0.0, pcc_ref[...] * pcx_ref[...])
        dpad_ref[rc:rc + pad, :] = jnp.where(i == nch - 1, 0.0, ndo_ref[...] * ncb_ref[...])
        for s0 in range(0, rc, sub):
            rows = slice(s0, s0 + sub)
            ppad_ref[pad + s0:pad + s0 + sub, :] = cc_ref[rows, :] * cx_ref[rows, :]
            dpad_ref[rows, :] = do_ref[rows, :] * cb_ref[rows, :]
        _build_shifts(shp_ref, ppad_ref, offs_in)
        _build_shifts(shd_ref, dpad_ref, offs_out)
        for s0 in range(0, rc, sub):
            rows = slice(s0, s0 + sub)
            dcv = dpad_ref[rows, :]
            cv = None
            for k, o in enumerate(offs_in):
                shifted = _read_shifted(shp_ref, ppad_ref, o, s0, sub)
                dwacc_ref[k] += _row_groups(dcv * shifted)
                cv = shifted * w_ref[k:k + 1, :] if cv is None else cv + shifted * w_ref[k:k + 1, :]
            dp = _tap_sum(shd_ref, dpad_ref, w_ref, offs_out, s0, sub)
            dcb_ref[rows, :] = (do_ref[rows, :] * cv).astype(BF16)
            dcc_ref[rows, :] = (dp * cx_ref[rows, :]).astype(BF16)
            dcx_ref[rows, :] = (dp * cc_ref[rows, :]).astype(BF16)

        @pl.when((pl.program_id(0) == batch - 1) & (i == nch - 1))
        def _():
            for k in range(taps):
                dw_ref[k:k + 1, :] = _colsum(dwacc_ref[k])

    chunk, prev = _chunk_spec(rc, dc, nch), _prev_halo_spec(pad, rc, dc, nch)
    nxt = _next_halo_spec(pad, rc, dc, nch, t)
    return pl.pallas_call(
        kern, grid=(batch, nch), in_specs=[chunk, chunk, chunk, chunk, prev, prev, nxt, nxt, _whole2(conv_w)],
        out_specs=[chunk, chunk, chunk, _whole2(conv_w)],
        out_shape=[SDS((t, dc), BF16)] * 3 + [SDS(conv_w.shape, F32)],
        scratch_shapes=[pltpu.VMEM((pad + rc, dc), F32), pltpu.VMEM((rc + pad, dc), F32),
                        pltpu.VMEM((SUBLANES, pad + rc, dc), F32), pltpu.VMEM((SUBLANES, rc + pad, dc), F32),
                        pltpu.VMEM((taps, SUBLANES, dc), F32)],
        compiler_params=_params("arbitrary", "arbitrary"), name=name)(c_b, c_c, c_x, dco, c_c, c_x, c_b, dco, conv_w)


def _head_of_lane(db):
    return lax.broadcasted_iota(jnp.int32, (1, db), 1) // (db // B_HEADS)


def _tril(rows_ge_cols=True):
    r = lax.broadcasted_iota(jnp.int32, (CHUNK, CHUNK), 0)
    c = lax.broadcasted_iota(jnp.int32, (CHUNK, CHUNK), 1)
    return (r >= c) if rows_ge_cols else (r <= c)


def _spatial_mix(wm, vb, bias, head):
    mixed = bias
    for h in range(B_HEADS):
        mixed = mixed + jnp.where(head == h, _dot(wm[h], vb), 0.0)
    return mixed


def _mix_b_fwd(b_u, b_v, w_s, bias_full, ln_g, ln_b, name):
    t, db = b_u.shape
    rb = _tile(t, ROW_TILE)

    def kern(bu_ref, bv_ref, ws_ref, bias_ref, g_ref, b_ref, o_ref):
        head = _head_of_lane(db)
        wm = [jnp.where(_tril(), ws_ref[h], 0.0).astype(BF16) for h in range(B_HEADS)]
        for ch in range(rb // CHUNK):
            rows = slice(ch * CHUNK, (ch + 1) * CHUNK)
            y, _ = _ln_stats(_gelu(bv_ref[rows, :]))
            vb = (y * g_ref[...] + b_ref[...]).astype(BF16)
            mixed = _spatial_mix(wm, vb, bias_ref[...], head)
            o_ref[rows, :] = (_gelu(bu_ref[rows, :]) * mixed).astype(BF16)

    def whole(a):
        return pl.BlockSpec(a.shape, lambda i, nd=len(a.shape): (0,) * nd)

    tile = pl.BlockSpec((rb, db), lambda i: (i, 0))
    return pl.pallas_call(
        kern, grid=(t // rb,), in_specs=[tile, tile, whole(w_s), whole(bias_full), whole(ln_g), whole(ln_b)],
        out_specs=tile, out_shape=SDS((t, db), BF16), compiler_params=_params("arbitrary"), name=name)(
            b_u, b_v, w_s, bias_full, ln_g, ln_b)


def _mix_b_bwd(b_u, b_v, dbo, w_s, w_s_t, bias_full, ln_g, ln_b, name):
    t, db = b_u.shape
    rb = _tile(t, ROW_TILE)
    steps = t // rb

    def kern(bu_ref, bv_ref, do_ref, ws_ref, wst_ref, bias_ref, g_ref, b_ref,
             dbu_ref, dbv_ref, dws_ref, dbs_ref, dg_ref, dbeta_ref, dbias_ref):
        i = pl.program_id(0)

        @pl.when(i == 0)
        def _():
            for r in (dws_ref, dg_ref, dbeta_ref, dbias_ref):
                r[...] = jnp.zeros_like(r)

        head = _head_of_lane(db)
        gain = g_ref[...]
        wm = [jnp.where(_tril(), ws_ref[h], 0.0).astype(BF16) for h in range(B_HEADS)]
        wmt = [jnp.where(_tril(False), wst_ref[h], 0.0).astype(BF16) for h in range(B_HEADS)]
        for ch in range(rb // CHUNK):
            rows = slice(ch * CHUNK, (ch + 1) * CHUNK)
            bu, bv, dout = bu_ref[rows, :], bv_ref[rows, :], do_ref[rows, :]
            y, rstd = _ln_stats(_gelu(bv))
            vb = (y * gain + b_ref[...]).astype(BF16)
            mixed = _spatial_mix(wm, vb, bias_ref[...], head)
            du = dout * mixed
            dmixed = dout * _gelu(bu)
            dbias_ref[...] += dmixed
            dmb = dmixed.astype(BF16)
            dv = None
            for h in range(B_HEADS):
                dws_ref[h] += _dot_nt(jnp.where(head == h, dmb, 0.0).astype(BF16), vb)
                part = jnp.where(head == h, _dot(wmt[h], dmb), 0.0)
                dv = part if dv is None else dv + part
            dg_ref[...] += _colsum(dv * y)
            dbeta_ref[...] += _colsum(dv)
            dbv_ref[rows, :] = (_ln_bwd(dv, y, rstd, gain) * _gelu_grad(bv)).astype(BF16)
            dbu_ref[rows, :] = (du * _gelu_grad(bu)).astype(BF16)

        @pl.when(i == steps - 1)
        def _():
            for h in range(B_HEADS):
                dws_ref[h] = jnp.where(_tril(), dws_ref[h], 0.0)
                dbs_ref[h] = jnp.sum(jnp.where(head == h, dbias_ref[...], 0.0), axis=1, keepdims=True)

    def whole(a):
        return pl.BlockSpec(a.shape, lambda i, nd=len(a.shape): (0,) * nd)

    tile = pl.BlockSpec((rb, db), lambda i: (i, 0))
    vec = SDS((1, db), F32)
    dbs = SDS((B_HEADS, CHUNK, 1), F32)
    return pl.pallas_call(
        kern, grid=(steps,),
        in_specs=[tile, tile, tile, whole(w_s), whole(w_s_t), whole(bias_full), whole(ln_g), whole(ln_b)],
        out_specs=[tile, tile, whole(w_s), whole(dbs), whole(vec), whole(vec)],
        out_shape=[SDS((t, db), BF16), SDS((t, db), BF16), SDS(w_s.shape, F32), dbs, vec, vec],
        scratch_shapes=[pltpu.VMEM((CHUNK, db), F32)],
        compiler_params=_params("arbitrary"), name=name)(b_u, b_v, dbo, w_s, w_s_t, bias_full, ln_g, ln_b)


def _softmax_rows(s):
    e = jnp.exp(s - jnp.max(s, axis=-1, keepdims=True))
    return e / jnp.sum(e, axis=-1, keepdims=True)


def _attention_fwd(q, kv, batch, name):
    t, d = q.shape
    seq, mlen, hd = t // batch, kv.shape[0] // batch, d // X_HEADS
    ar = _tile(seq, ATT_ROWS)
    scale = hd ** -0.5

    def kern(q_ref, k_ref, v_ref, o_ref):
        k, v = k_ref[...], v_ref[...]
        for r0 in range(0, seq, ar):
            p = _softmax_rows(_dot_nt(q_ref[r0:r0 + ar, :], k) * scale)
            o_ref[r0:r0 + ar, :] = _dot(p.astype(BF16), v).astype(BF16)

    qs = pl.BlockSpec((seq, hd), lambda b, h: (b, h))
    return pl.pallas_call(
        kern, grid=(batch, X_HEADS),
        in_specs=[qs, pl.BlockSpec((mlen, hd), lambda b, h: (b, h)), pl.BlockSpec((mlen, hd), lambda b, h: (b, X_HEADS + h))],
        out_specs=qs, out_shape=SDS((t, d), BF16), compiler_params=_params("arbitrary", "arbitrary"), name=name)(q, kv, kv)


def _attention_bwd(q, kv, do, batch, name):
    t, d = q.shape
    seq, mlen, hd = t // batch, kv.shape[0] // batch, d // X_HEADS
    ar = _tile(seq, ATT_ROWS)
    scale = hd ** -0.5

    def kern(q_ref, k_ref, v_ref, do_ref, dq_ref, dk_ref, dv_ref):
        k, v = k_ref[...], v_ref[...]
        dk = jnp.zeros((mlen, hd), F32)
        dv = jnp.zeros((mlen, hd), F32)
        for r0 in range(0, seq, ar):
            qr, dor = q_ref[r0:r0 + ar, :], do_ref[r0:r0 + ar, :]
            p = _softmax_rows(_dot_nt(qr, k) * scale)
            dp = _dot_nt(dor, v)
            ds = (p * (dp - jnp.sum(p * dp, axis=-1, keepdims=True)) * scale).astype(BF16)
            dq_ref[r0:r0 + ar, :] = _dot(ds, k).astype(BF16)
            dk = dk + _dot_tn(ds, qr)
            dv = dv + _dot_tn(p.astype(BF16), dor)
        dk_ref[...] = dk.astype(BF16)
        dv_ref[...] = dv.astype(BF16)

    qs = pl.BlockSpec((seq, hd), lambda b, h: (b, h))
    ks = pl.BlockSpec((mlen, hd), lambda b, h: (b, h))
    dkv = SDS((kv.shape[0], d), BF16)
    return pl.pallas_call(
        kern, grid=(batch, X_HEADS),
        in_specs=[qs, ks, pl.BlockSpec((mlen, hd), lambda b, h: (b, X_HEADS + h)), qs],
        out_specs=[qs, ks, ks], out_shape=[SDS((t, d), BF16), dkv, dkv],
        compiler_params=_params("arbitrary", "arbitrary"), name=name)(q, kv, kv, do)


def _place():
    x, y, c = lax.axis_index("x"), lax.axis_index("y"), lax.axis_index("c")
    other_chips = [(1 - x, y), (x, 1 - y), (1 - x, 1 - y)]
    return x, y, c, other_chips


def _comm_call(kern, out_shape, n_pairs, name, *args):
    return pl.pallas_call(
        kern, out_shape=out_shape, in_specs=[ANY] * len(args), out_specs=jax.tree.map(lambda _: ANY, out_shape),
        scratch_shapes=[pltpu.SemaphoreType.DMA((n_pairs,)), pltpu.SemaphoreType.DMA((n_pairs,)), pltpu.SemaphoreType.DMA((n_pairs,))],
        name=name)(*args)


def _all_gather_chips(shard, name):
    r, cols = shard.shape
    rh = r // 2

    def kern(s_ref, o_ref, send_sems, recv_sems, local_sems):
        x, y, c, chips = _place()
        mine_slot = 2 * x + y
        half = pl.ds(c * rh, rh)
        other_half = pl.ds((1 - c) * rh, rh)

        def copy(k, src, dst, to):
            return pltpu.make_async_remote_copy(src_ref=src, dst_ref=dst, send_sem=send_sems.at[k], recv_sem=recv_sems.at[k],
                                                device_id=to, device_id_type=MESH)

        mine = pltpu.make_async_copy(s_ref, o_ref.at[mine_slot], local_sems.at[0])
        mine.start()
        first = [copy(j, s_ref.at[half], o_ref.at[mine_slot, half], (px, py, c)) for j, (px, py) in enumerate(chips)]
        for cp in first:
            cp.start()
        passed = []
        for j, (px, py) in enumerate(chips):
            landed = o_ref.at[2 * px + py, half]
            copy(j, landed, landed, (px, py, c)).wait_recv()
            cp = copy(3 + j, landed, landed, (x, y, 1 - c))
            cp.start()
            passed.append(cp)
        for j, (px, py) in enumerate(chips):
            theirs = o_ref.at[2 * px + py, other_half]
            copy(3 + j, theirs, theirs, (x, y, 1 - c)).wait_recv()
        for cp in first + passed:
            cp.wait_send()
        mine.wait()

    return _comm_call(kern, SDS((N_CHIPS, r, cols), shard.dtype), 6, name, shard)


def _join_halves(joined, name):
    n = len(joined)

    def kern(*refs):
        j_refs = refs[:n]
        send_sems, recv_sems = refs[2 * n:]
        x, y, c, _ = _place()
        copies = []
        for k, j_ref in enumerate(j_refs):
            rh = j_ref.shape[1] // 2
            rows = j_ref.at[:, pl.ds(c * rh, rh)]
            cp = pltpu.make_async_remote_copy(src_ref=rows, dst_ref=rows, send_sem=send_sems.at[k], recv_sem=recv_sems.at[k],
                                              device_id=(x, y, 1 - c), device_id_type=MESH)
            cp.start()
            copies.append(cp)
        for k, (cp, j_ref) in enumerate(zip(copies, j_refs)):
            rh = j_ref.shape[1] // 2
            theirs = j_ref.at[:, pl.ds((1 - c) * rh, rh)]
            pltpu.make_async_remote_copy(src_ref=theirs, dst_ref=theirs, send_sem=send_sems.at[k], recv_sem=recv_sems.at[k],
                                         device_id=(x, y, 1 - c), device_id_type=MESH).wait_recv()
            cp.wait_send()

    return pl.pallas_call(
        kern, out_shape=[SDS(j.shape, j.dtype) for j in joined], in_specs=[ANY] * n, out_specs=[ANY] * n,
        input_output_aliases={k: k for k in range(n)},
        scratch_shapes=[pltpu.SemaphoreType.DMA((n,)), pltpu.SemaphoreType.DMA((n,))], name=name)(*joined)


HBM = pl.BlockSpec(memory_space=pltpu.HBM)
SEMAPHORES = pl.BlockSpec(memory_space=pltpu.SEMAPHORE)
COPIES_PER_ARRAY = N_CHIPS - 1
GATHER_COPIES = N_CHIPS


def _in_hbm(a):
    return pltpu.with_memory_space_constraint(a, pltpu.HBM)


def _gather_copies(shard_refs, land_refs, send_sems, recv_sems):
    x, y, c, chips = _place()
    copies = []
    for k, (s_ref, land_ref) in enumerate(zip(shard_refs, land_refs)):
        half = pl.ds(c * (s_ref.shape[0] // 2), s_ref.shape[0] // 2)
        for j, (px, py) in enumerate(chips):
            n = GATHER_COPIES * k + j
            copies.append(pltpu.make_async_remote_copy(
                src_ref=s_ref.at[half], dst_ref=land_ref.at[2 * x + y, half], send_sem=send_sems.at[n], recv_sem=recv_sems.at[n],
                device_id=(px, py, c), device_id_type=MESH))
        n = GATHER_COPIES * k + N_CHIPS - 1
        copies.append(pltpu.make_async_remote_copy(
            src_ref=s_ref, dst_ref=land_ref.at[2 * x + y], send_sem=send_sems.at[n], recv_sem=recv_sems.at[n],
            device_id=(x, y, 1 - c), device_id_type=MESH))
    return copies


def _swap_copies(full_refs, land_refs, send_sems, recv_sems):
    x, y, c, _ = _place()
    copies = []
    for k, (g_ref, land_ref) in enumerate(zip(full_refs, land_refs)):
        rh = land_ref.shape[1]
        copies.append(pltpu.make_async_remote_copy(
            src_ref=g_ref.at[:, pl.ds((1 - c) * rh, rh), :], dst_ref=land_ref, send_sem=send_sems.at[k], recv_sem=recv_sems.at[k],
            device_id=(x, y, 1 - c), device_id_type=MESH))
    return copies


def _owner_copies(sum_refs, land_refs, send_sems, recv_sems):
    x, y, c, chips = _place()
    copies = []
    for k, (s_ref, land_ref) in enumerate(zip(sum_refs, land_refs)):
        for j, (px, py) in enumerate(chips):
            n = COPIES_PER_ARRAY * k + j
            copies.append(pltpu.make_async_remote_copy(
                src_ref=s_ref.at[2 * px + py], dst_ref=land_ref.at[j], send_sem=send_sems.at[n], recv_sem=recv_sems.at[n],
                device_id=(px, py, c), device_id_type=MESH))
    return copies


def _copies_start(build, per_array, sources, lands, name, after=None):
    ns, nb = len(sources), len(sources) + len(lands)
    n_copies = per_array * ns
    order = [] if after is None else [after]
    n_in = nb + len(order)

    def kern(*refs):
        for cp in build(refs[:ns], refs[ns:nb], refs[n_in + nb], refs[n_in + nb + 1]):
            cp.start()
        refs[-1][...] = jnp.zeros_like(refs[-1])

    bufs = [*sources, *lands]
    res = pl.pallas_call(
        kern, name=name,
        out_shape=(*[pltpu.HBM(b.shape, b.dtype) for b in bufs], pltpu.SemaphoreType.DMA((n_copies,)), pltpu.SemaphoreType.DMA((n_copies,)),
                   SDS((SUBLANES, 128), F32)),
        in_specs=[HBM] * nb + [ANY] * len(order),
        out_specs=(*[HBM] * nb, SEMAPHORES, SEMAPHORES, pl.BlockSpec(memory_space=pltpu.VMEM)),
        input_output_aliases={i: i for i in range(nb)},
        compiler_params=pltpu.CompilerParams(has_side_effects=pltpu.SideEffectType.DATAFLOW_SIDE_EFFECTING),
    )(*[_in_hbm(b) for b in bufs], *order)
    return res[nb], res[nb + 1], list(res[:nb]), res[-1]


def _copies_wait(build, send_sems, recv_sems, bufs, ns, after, name):
    nb = len(bufs)

    def kern(*refs):
        for cp in build(refs[:ns], refs[ns:nb], refs[nb], refs[nb + 1]):
            cp.wait_send()
            cp.wait_recv()

    res = pl.pallas_call(
        kern, name=name, out_shape=tuple(pltpu.HBM(b.shape, b.dtype) for b in bufs),
        in_specs=[HBM] * nb + [SEMAPHORES, SEMAPHORES, ANY], out_specs=tuple([HBM] * nb),
        input_output_aliases={i: i for i in range(nb)},
        compiler_params=pltpu.CompilerParams(has_side_effects=pltpu.SideEffectType.DATAFLOW_SIDE_EFFECTING),
    )(*bufs, send_sems, recv_sems, after)
    return list(res[:ns]), list(res[ns:])


def _gather_finish(lands, name):
    ns = len(lands)

    def kern(*refs):
        l_refs = refs[:ns]
        send_sems, recv_sems = refs[2 * ns:]
        x, y, c, chips = _place()
        passed = []
        for k, l_ref in enumerate(l_refs):
            rh = l_ref.shape[1] // 2
            for j, (px, py) in enumerate(chips):
                landed = l_ref.at[2 * px + py, pl.ds(c * rh, rh)]
                cp = pltpu.make_async_remote_copy(
                    src_ref=landed, dst_ref=landed, send_sem=send_sems.at[COPIES_PER_ARRAY * k + j],
                    recv_sem=recv_sems.at[COPIES_PER_ARRAY * k + j], device_id=(x, y, 1 - c), device_id_type=MESH)
                cp.start()
                passed.append(cp)
        for k, l_ref in enumerate(l_refs):
            rh = l_ref.shape[1] // 2
            for j, (px, py) in enumerate(chips):
                theirs = l_ref.at[2 * px + py, pl.ds((1 - c) * rh, rh)]
                pltpu.make_async_remote_copy(
                    src_ref=theirs, dst_ref=theirs, send_sem=send_sems.at[COPIES_PER_ARRAY * k + j],
                    recv_sem=recv_sems.at[COPIES_PER_ARRAY * k + j], device_id=(x, y, 1 - c), device_id_type=MESH).wait_recv()
        for cp in passed:
            cp.wait_send()

    return pl.pallas_call(
        kern, out_shape=[SDS(l.shape, l.dtype) for l in lands], in_specs=[ANY] * ns, out_specs=[ANY] * ns,
        input_output_aliases={k: k for k in range(ns)},
        scratch_shapes=[pltpu.SemaphoreType.DMA((COPIES_PER_ARRAY * ns,)), pltpu.SemaphoreType.DMA((COPIES_PER_ARRAY * ns,))],
        name=name)(*lands)


def _add_sibling(full, theirs, out_dtype, name):
    n, r, cols = full.shape
    rh = r // 2
    tr = _tile(rh, ROW_TILE)
    nb = rh // tr

    def kern(c_ref, a_ref, b_ref, o_ref):
        o_ref[...] = (a_ref[...] + b_ref[...]).astype(out_dtype)

    c = lax.axis_index("c").astype(jnp.int32).reshape(1)
    return pl.pallas_call(
        kern, out_shape=SDS((n, rh, cols), out_dtype),
        grid_spec=pltpu.PrefetchScalarGridSpec(
            num_scalar_prefetch=1, grid=(n, nb),
            in_specs=[pl.BlockSpec((1, tr, cols), lambda j, i, c_ref: (j, c_ref[0] * nb + i, 0)),
                      pl.BlockSpec((1, tr, cols), lambda j, i, c_ref: (j, i, 0))],
            out_specs=pl.BlockSpec((1, tr, cols), lambda j, i, c_ref: (j, i, 0))),
        compiler_params=_params("arbitrary", "arbitrary"), name=name)(c, full, theirs)


def _add_owners(partial, received, name, into=None, layer=0, depth=1):
    n, rh, cols = partial.shape
    tr = _tile(rh, ROW_TILE)
    nb = rh // tr
    stacked = [] if into is None else [into]

    def kern(s_ref, a_ref, b_ref, *refs):
        acc = a_ref[0].astype(F32)
        for j in range(N_CHIPS - 1):
            acc = acc + b_ref[j].astype(F32)
        refs[-1][0] = acc

    place = jnp.stack([2 * lax.axis_index("x") + lax.axis_index("y"), lax.axis_index("c")]).astype(jnp.int32)
    return pl.pallas_call(
        kern, out_shape=SDS((depth, 2 * rh, cols), F32),
        grid_spec=pltpu.PrefetchScalarGridSpec(
            num_scalar_prefetch=1, grid=(nb,),
            in_specs=[pl.BlockSpec((1, tr, cols), lambda i, s_ref: (s_ref[0], i, 0)),
                      pl.BlockSpec((N_CHIPS - 1, tr, cols), lambda i, s_ref: (0, i, 0))] + [ANY] * len(stacked),
            out_specs=pl.BlockSpec((1, tr, cols), lambda i, s_ref: (layer, s_ref[1] * nb + i, 0))),
        input_output_aliases={3: 0} if stacked else {},
        compiler_params=_params("arbitrary"), name=name)(place, partial, received, *stacked)


def _pack(parts, lead):
    lead_shape = parts[0].shape[:lead]
    flat = jnp.concatenate([p.reshape(lead_shape + (-1,)) for p in parts], axis=-1)
    n = flat.shape[-1]
    rows = _round_up(-(-n // PACK_W), PACK_ROWS)
    flat = jnp.pad(flat, [(0, 0)] * lead + [(0, rows * PACK_W - n)])
    return flat.reshape(lead_shape + (rows, PACK_W))


def _unpack(buf, shapes, lead):
    lead_shape = buf.shape[:lead]
    flat = buf.reshape(lead_shape + (-1,))
    out, off = [], 0
    for s in shapes:
        n = math.prod(s)
        out.append(flat[..., off:off + n].reshape(lead_shape + tuple(s)))
        off += n
    return out


def _to_slots(full, axis):
    s = full.shape
    split = full.reshape(s[:axis] + (N_CHIPS, s[axis] // N_CHIPS) + s[axis + 1:])
    return jnp.moveaxis(split, axis, 0)


def _from_slots(slots, axis):
    moved = jnp.moveaxis(slots, 0, axis)
    s = moved.shape
    return moved.reshape(s[:axis] + (s[axis] * s[axis + 1],) + s[axis + 2:])


MATMUL_WEIGHTS = (("w_in", 1), ("w_out", 0), ("w_q", 0), ("w_kv", 1), ("w_o", 0), ("w_ff1", 1), ("w_ff2", 0))
CONV_WEIGHTS = (("conv_a_w", 1), ("conv_c_w", 1))
REPLICATED = ("conv_a_b", "ln_a_g", "ln_a_b", "ln_v_g", "ln_v_b", "w_s", "b_s",
              "ln1_g", "ln1_b", "ln2_g", "ln2_b", "ln3_g", "ln3_b")
WEIGHT_ORDER = ("w_in", "conv_a_w", "conv_a_b", "ln_a_g", "ln_a_b", "ln_v_g", "ln_v_b", "w_s", "b_s", "conv_c_w", "w_out",
                "ln1_g", "ln1_b", "w_q", "w_kv", "w_o", "ln2_g", "ln2_b", "w_ff1", "w_ff2", "ln3_g", "ln3_b")


def _row(v):
    return v.reshape(1, -1)


def kernel(x, mem, w_in, conv_a_w, conv_a_b, ln_a_g, ln_a_b, ln_v_g, ln_v_b, w_s, b_s, conv_c_w, w_out, ln1_g, ln1_b, w_q, w_kv, w_o, ln2_g, ln2_b, w_ff1, w_ff2, ln3_g, ln3_b, loss_target, m_w_in, m_conv_a_w, m_conv_a_b, m_ln_a_g, m_ln_a_b, m_ln_v_g, m_ln_v_b, m_w_s, m_b_s, m_conv_c_w, m_w_out, m_ln1_g, m_ln1_b, m_w_q, m_w_kv, m_w_o, m_ln2_g, m_ln2_b, m_w_ff1, m_w_ff2, m_ln3_g, m_ln3_b, v_w_in, v_conv_a_w, v_conv_a_b, v_ln_a_g, v_ln_a_b, v_ln_v_g, v_ln_v_b, v_w_s, v_b_s, v_conv_c_w, v_w_out, v_ln1_g, v_ln1_b, v_w_q, v_w_kv, v_w_o, v_ln2_g, v_ln2_b, v_w_ff1, v_w_ff2, v_ln3_g, v_ln3_b):
    given = dict(locals())
    weights = {n: given[n] for n in WEIGHT_ORDER}
    moment1 = {n: given["m_" + n] for n in WEIGHT_ORDER}
    moment2 = {n: given["v_" + n] for n in WEIGHT_ORDER}

    depth = w_in.shape[0]
    batch, seq, d = x.shape
    t = batch * seq
    hd = d // HEADS
    d_a, d_b, d_c = A_HEADS * hd, B_HEADS * hd, C_HEADS * hd
    widths = (d_a, d_a, d_b, d_b, d_c, d_c, d_c)
    in_offs = [sum(widths[:k]) for k in range(len(widths))]
    alpha = (2.0 * depth) ** 0.25
    layers = range(depth)
    tm = _tile(t, MATMUL_ROW_TILE)
    tm_wide = _tile(t, WIDE_ROW_TILE)

    conv_shards = [weights[n][l] for l in layers for n, _ in CONV_WEIGHTS]
    names = [n for n, _ in MATMUL_WEIGHTS]
    first_used = names[:1]
    parts = {l: [first_used, [n for n in names if n not in first_used]] if l == 0 else [names] for l in layers}
    axis_of = dict(MATMUL_WEIGHTS)
    groups = {}

    def gather_start(l, p, after):
        part = parts[l][p]
        shards = [weights[n][l].astype(BF16) for n in part] + ([_pack(conv_shards, 0)] if (l, p) == (0, 0) else [])
        lands = [lax.empty((N_CHIPS,) + s.shape, s.dtype) for s in shards]
        send_sems, recv_sems, bufs, token = _copies_start(_gather_copies, GATHER_COPIES, shards, lands, f"gather_start_{l}_{p}", after)
        groups[l, p] = (part, send_sems, recv_sems, bufs, len(shards))
        return token

    sequence = [(l, p) for l in layers for p in range(len(parts[l]))]

    def gathered(l, p, after):
        part, send_sems, recv_sems, bufs, ns = groups[l, p]
        _, lands = _copies_wait(_gather_copies, send_sems, recv_sems, bufs, ns, after, f"gather_wait_{l}_{p}")
        slots = _gather_finish(lands, "gather_finish")
        nxt = sequence.index((l, p)) + 1
        started = gather_start(*sequence[nxt], slots[0]) if nxt < len(sequence) else None
        return {n: _from_slots(slots[k], axis_of[n]) for k, n in enumerate(part)}, slots[len(part):], started

    xf = x.reshape(t, d)
    xb = (xf + gather_start(0, 0, None)[0, 0]).astype(BF16)
    memb = mem.reshape(-1, d).astype(BF16)
    saved, full, conv_full = [], [], None
    stream = (xf,)
    for l in layers:
        w, extra, anchor = gathered(l, 0, xb)
        if l == 0:
            conv_full = _unpack(extra[0], [s.shape for s in conv_shards], 1)
        for k, (n, axis) in enumerate(CONV_WEIGHTS):
            w[n] = _from_slots(conv_full[l * len(CONV_WEIGHTS) + k], axis)
        for n in REPLICATED:
            w[n] = weights[n][l]
        w["w_s_t"] = jnp.swapaxes(w["w_s"], 1, 2)
        w["bias_full"] = jnp.repeat(w["b_s"].T, hd, axis=1)
        full.append(w)
        s = {"x0b": xb}
        proj = _in_proj(xb, w["w_in"], widths, "in_proj", after=anchor)
        s["proj"] = proj
        a_val, a_gate, b_u, b_v, c_b, c_c, c_x = proj
        a_out, s["ac"] = _mix_a_fwd(a_val, a_gate, w["conv_a_w"], _row(w["conv_a_b"]), _row(w["ln_a_g"]), _row(w["ln_a_b"]),
                                    batch, "mix_a_fwd")
        b_out = _mix_b_fwd(b_u, b_v, w["w_s"], w["bias_full"], _row(w["ln_v_g"]), _row(w["ln_v_b"]), "mix_b_fwd")
        c_out = _mix_c_fwd(c_b, c_c, c_x, w["conv_c_w"], batch, "mix_c_fwd")
        s["cat"] = (a_out, b_out, c_out)
        anchor = None
        for p in range(1, len(parts[l])):
            more, _, anchor = gathered(l, p, c_out)
            w.update(more)
        s["z1"], xb = _proj_ln([a_out, b_out, c_out], w["w_out"], stream, _row(w["ln1_g"]), _row(w["ln1_b"]), alpha, tm, "out_proj_ln",
                               after=anchor)
        stream = (s["z1"], _row(w["ln1_g"]), _row(w["ln1_b"]))
        s["x1b"] = xb
        s["q"] = _matmul(xb, w["w_q"], "q_proj")
        s["kv"] = _matmul(memb, w["w_kv"], "kv_proj")
        s["o"] = _attention_fwd(s["q"], s["kv"], batch, "attention_fwd")
        s["z2"], xb = _proj_ln([s["o"]], w["w_o"], stream, _row(w["ln2_g"]), _row(w["ln2_b"]), alpha, tm, "o_proj_ln")
        stream = (s["z2"], _row(w["ln2_g"]), _row(w["ln2_b"]))
        s["x2b"] = xb
        s["h"], s["r"] = _ff1(xb, w["w_ff1"], "ff1")
        s["z3"], xb = _proj_ln([s["r"]], w["w_ff2"], stream, _row(w["ln3_g"]), _row(w["ln3_b"]), alpha, tm_wide, "ff2_ln")
        stream = (s["z3"], _row(w["ln3_g"]), _row(w["ln3_b"]))
        saved.append(s)

    target = loss_target.reshape(t, d)
    dz3, dz3b, top_g, top_b, loss_block = _loss_head_ln(saved[-1]["z3"], target, _row(full[-1]["ln3_g"]), _row(full[-1]["ln3_b"]),
                                                        "loss_head_ln3")
    loss = lax.psum(loss_block[0, 0], ("x", "y", "c"))
    grads = [{} for _ in layers]
    grads[-1]["ln3_g"], grads[-1]["ln3_b"] = top_g, top_b
    reductions = []
    swapping = {}
    token, last_started = 0.0, None

    def rows_to_slots(full_):
        return full_.reshape(N_CHIPS, full_.shape[1] // N_CHIPS, full_.shape[2])

    def swap_start(key, named, wire):
        arrays = [a for _, a in named]
        lands = [lax.empty((f.shape[0], f.shape[1] // 2, f.shape[2]), f.dtype) for f in arrays]
        send_sems, recv_sems, bufs, started = _copies_start(_swap_copies, 1, arrays, lands, f"rs_swap_start_{key}")
        swapping[key] = ([n for n, _ in named], send_sems, recv_sems, bufs, wire)
        return started

    def swap_finish(key, after):
        nonlocal token, last_started
        part, send_sems, recv_sems, bufs, wire = swapping.pop(key)
        fulls, theirs = _copies_wait(_swap_copies, send_sems, recv_sems, bufs, len(wire), after, f"rs_swap_wait_{key}")
        sums = [_add_sibling(f, th, dt, "rs_add_sibling") for f, th, dt in zip(fulls, theirs, wire)]
        lands = [lax.empty((N_CHIPS - 1,) + p.shape[1:], p.dtype) for p in sums]
        send_sems, recv_sems, bufs, started = _copies_start(_owner_copies, COPIES_PER_ARRAY, sums, lands, f"rs_send_start_{key}")
        reductions.append((key, part, send_sems, recv_sems, bufs))
        token, last_started = started[0, 0], started

    early = ("w_ff2", "w_ff1", "w_o", "w_kv", "w_q")
    swap_started, pending = None, None
    for l in reversed(layers):
        w, s, g = full[l], saved[l], grads[l]
        g["w_ff2"] = rows_to_slots(_mm_tn(s["r"], dz3b, "d_w_ff2", after=swap_started))
        dh = _bwd_ff2(dz3b, s["h"], w["w_ff2"], "d_ff_hidden", after=swap_started)
        if pending is not None:
            swap_finish(pending, dh)
        g["w_ff1"] = _mm_tn(s["x2b"], dh, "d_w_ff1", slots=N_CHIPS)
        dz2, dz2b, g["ln2_g"], g["ln2_b"] = _bwd_data_ln([dh], w["w_ff1"], [0], dz3, alpha, s["z2"], _row(w["ln2_g"]) + token,
                                                         tm_wide, "d_x2_ln2")
        g["w_o"] = rows_to_slots(_mm_tn(s["o"], dz2b, "d_w_o"))
        do = _bwd_data([dz2b], w["w_o"], [0], None, alpha, BF16, tm, "d_att_out")
        dq, dk, dv = _attention_bwd(s["q"], s["kv"], do, batch, "attention_bwd")
        g["w_kv"] = _mm_tn(memb, jnp.concatenate([dk, dv], axis=1), "d_w_kv", slots=N_CHIPS)
        g["w_q"] = rows_to_slots(_mm_tn(s["x1b"], dq, "d_w_q"))
        dz1, dz1b, g["ln1_g"], g["ln1_b"] = _bwd_data_ln([dq], w["w_q"], [0], dz2, alpha, s["z1"], _row(w["ln1_g"]), tm, "d_x1_ln1")
        early_started = swap_start("0a", [(n, g[n]) for n in early], [BF16] * len(early)) if l == 0 else None
        g["w_out"] = _mm_tn_rows(s["cat"], dz1b, "d_w_out").reshape(N_CHIPS, d // N_CHIPS, d)
        da_, dbo, dco = _bwd_out_proj(dz1b, w["w_out"], (d_a, d_b, d_c), "d_mixer_out", after=early_started)
        a_val, a_gate, b_u, b_v, c_b, c_c, c_x = s["proj"]
        dav, dag, g["conv_a_w"], g["conv_a_b"], g["ln_a_g"], g["ln_a_b"] = _mix_a_bwd(
            a_val, a_gate, s["ac"], da_, w["conv_a_w"], _row(w["ln_a_g"]), _row(w["ln_a_b"]), batch, "mix_a_bwd")
        if l == 0:
            swap_finish("0a", dav)
        dbu, dbv, g["w_s"], dbs, g["ln_v_g"], g["ln_v_b"] = _mix_b_bwd(
            b_u, b_v, dbo, w["w_s"], w["w_s_t"], w["bias_full"], _row(w["ln_v_g"]) + token, _row(w["ln_v_b"]), "mix_b_bwd")
        g["b_s"] = dbs.reshape(B_HEADS, CHUNK)
        dcb, dcc, dcx, g["conv_c_w"] = _mix_c_bwd(c_b, c_c, c_x, dco, w["conv_c_w"], batch, "mix_c_bwd")
        dproj = [dav, dag, dbu, dbv, dcb, dcc, dcx]
        g["w_in"] = _to_slots(_mm_tn_cols(s["x0b"], dproj, "d_w_in"), 1)
        if l > 0:
            below = full[l - 1]
            dz3, dz3b, grads[l - 1]["ln3_g"], grads[l - 1]["ln3_b"] = _bwd_data_ln(
                dproj, w["w_in"], in_offs, dz1, alpha, saved[l - 1]["z3"], _row(below["ln3_g"]), tm, "d_x0_ln3")
            pending = str(l)
            swap_started = swap_start(pending, [(n, g[n]) for n, _ in MATMUL_WEIGHTS], [BF16] * len(MATMUL_WEIGHTS))
        else:
            dx = _bwd_data(dproj, w["w_in"], in_offs, dz1, alpha, F32, tm, "d_x0")
            conv_slots = [_to_slots(grads[k][n], axis) for k in layers for n, axis in CONV_WEIGHTS]
            rep_parts = [jnp.stack([grads[k][n].reshape(weights[n].shape[1:]) for k in layers]) for n in REPLICATED]
            rep_flat = jnp.concatenate([p.reshape(-1) for p in rep_parts])
            n_rep = rep_flat.shape[0]
            per_chip = _round_up(-(-n_rep // N_CHIPS), PACK_W * PACK_ROWS)
            rep_slots = jnp.pad(rep_flat, (0, N_CHIPS * per_chip - n_rep)).reshape(N_CHIPS, per_chip)
            late = [(n, g[n]) for n, _ in MATMUL_WEIGHTS if n not in early] + [("small", _pack(conv_slots + [rep_slots], 1))]
            swap_start("0b", late, [BF16] * (len(late) - 1) + [F32])
            swap_finish("0b", dx)
    grad_x = dx.reshape(batch, seq, d)

    reduced, after = {}, last_started
    for key, part, send_sems, recv_sems, bufs in reductions:
        sums, lands = _copies_wait(_owner_copies, send_sems, recv_sems, bufs, len(part), after, f"rs_send_wait_{key}")
        for n, p, r in zip(part, sums, lands):
            stacked = n != "small"
            reduced[n] = _add_owners(p, r, "rs_add_owners", reduced.get(n), int(key[0]) if stacked else 0, depth if stacked else 1)
            after = reduced[n]
    reduced = dict(zip(reduced, _join_halves(list(reduced.values()), "rs_join")))
    small = _unpack(reduced["small"][0], [c.shape[1:] for c in conv_slots] + [(per_chip,)], 0)
    conv_grad, rep_mine = small[:-1], small[-1]
    rep_all = _all_gather_chips(rep_mine.reshape(-1, PACK_W), "gather_small_grads").reshape(-1)[:n_rep]

    grad = {}
    for n, _ in MATMUL_WEIGHTS:
        grad[n] = reduced[n]
    for k, (n, _) in enumerate(CONV_WEIGHTS):
        grad[n] = jnp.stack([conv_grad[l * len(CONV_WEIGHTS) + k] for l in layers])
    off = 0
    for n in REPLICATED:
        size = math.prod(weights[n].shape)
        grad[n] = rep_all[off:off + size].reshape(weights[n].shape)
        off += size

    delta, new_m, new_v = {}, {}, {}
    for n, _ in MATMUL_WEIGHTS:
        shape = weights[n].shape
        as_rows = lambda a: a.reshape(-1, shape[-1])
        delta[n], new_m[n], new_v[n] = (
            r.reshape(shape) for r in _adamw(as_rows(weights[n]), as_rows(grad[n]), as_rows(moment1[n]), as_rows(moment2[n]), "adamw"))
    small_names = [n for n, _ in CONV_WEIGHTS] + list(REPLICATED)
    small_shapes = [weights[n].shape for n in small_names]
    packed = [_pack([src[n] for n in small_names], 0) for src in (weights, grad, moment1, moment2)]
    for dst, res in zip((delta, new_m, new_v), _adamw(*packed, "adamw_small")):
        for n, a in zip(small_names, _unpack(res, small_shapes, 0)):
            dst[n] = a

    return (loss, grad_x, *[grad[n] for n in WEIGHT_ORDER], *[delta[n] for n in WEIGHT_ORDER],
            *[new_m[n] for n in WEIGHT_ORDER], *[new_v[n] for n in WEIGHT_ORDER])
```

```python
import functools
import math

import jax
import jax.numpy as jnp
from jax import lax
from jax.experimental import pallas as pl
from jax.experimental.pallas import tpu as pltpu

F32 = jnp.float32
BF16 = jnp.bfloat16
SDS = jax.ShapeDtypeStruct

HEADS = 16
A_HEADS, B_HEADS, C_HEADS = 6, 4, 6
X_HEADS = 4
CHUNK = 128
LN_EPS = 1e-5
ADAM_LR, ADAM_B1, ADAM_B2, ADAM_EPS, ADAM_WD, ADAM_STEP = 0.001, 0.9, 0.999, 1e-08, 0.01, 10

N_CHIPS = 4
V7X_VMEM_LIMIT = 56 << 20
SUBLANES = 8
PACK_W = 1024
PACK_ROWS = 32
ROW_TILE = 512
MATMUL_ROW_TILE = 1024
WIDE_ROW_TILE = 512
CONV_ROWS = 1024
SUB_ROWS = 64
ATT_ROWS = 2048
TN_TILE = 1024
TN_TOKENS = 2048
MESH = pl.DeviceIdType.MESH
ANY = pl.BlockSpec(memory_space=pl.ANY)


def _tile(n, t):
    for d in range(min(n, t), 0, -1):
        if n % d == 0 and d % (2 * SUBLANES) == 0:
            return d
    return n


def _round_up(n, m):
    return -(-n // m) * m


def _params(*sem):
    return pltpu.CompilerParams(dimension_semantics=sem or None, vmem_limit_bytes=V7X_VMEM_LIMIT)


def _dot(a, b):
    return jnp.dot(a, b, preferred_element_type=F32)


def _dot_nt(a, b):
    return lax.dot_general(a, b, (((1,), (1,)), ((), ())), preferred_element_type=F32)


def _dot_tn(a, b):
    return lax.dot_general(a, b, (((0,), (0,)), ((), ())), preferred_element_type=F32)


def _sigmoid(x):
    return 1.0 / (1.0 + jnp.exp(-x))


def _gelu(x):
    return 0.5 * x * (1.0 + lax.erf(x * (2.0 ** -0.5)))


def _gelu_grad(x):
    return 0.5 * (1.0 + lax.erf(x * (2.0 ** -0.5))) + x * jnp.exp(-0.5 * x * x) * ((2.0 * math.pi) ** -0.5)


def _ln_stats(z):
    mu = jnp.mean(z, axis=-1, keepdims=True)
    zc = z - mu
    rstd = lax.rsqrt(jnp.mean(zc * zc, axis=-1, keepdims=True) + LN_EPS)
    return zc * rstd, rstd


def _ln_bwd(dy, y, rstd, g):
    dyh = dy * g
    return rstd * (dyh - jnp.mean(dyh, axis=-1, keepdims=True) - y * jnp.mean(dyh * y, axis=-1, keepdims=True))


def _colsum(x):
    return jnp.sum(x, axis=0, keepdims=True)


def _rowwise(body, rows, consts, outs, accs=(), *, tm, name, after=None):
    t = rows[0].shape[0]
    steps = t // tm
    n_in = len(rows) + len(consts)
    order = [] if after is None else [after]

    def kern(*refs):
        body(pl.program_id(0), steps, *refs[:n_in], *refs[n_in + len(order):])

    def whole(a, **kw):
        return pl.BlockSpec(a.shape, lambda i, nd=len(a.shape): (0,) * nd, **kw)

    in_specs = ([pl.BlockSpec((tm, r.shape[1]), lambda i: (i, 0)) for r in rows]
                + [whole(c, pipeline_mode=pl.Buffered(1)) for c in consts] + [ANY] * len(order))
    out_shape = [SDS((t, n), dt) for n, dt in outs] + [SDS(s, dt) for s, dt in accs]
    out_specs = [pl.BlockSpec((tm, n), lambda i: (i, 0)) for n, _ in outs] + [whole(SDS(s, dt)) for s, dt in accs]
    return pl.pallas_call(kern, grid=(steps,), in_specs=in_specs, out_specs=out_specs, out_shape=out_shape,
                          compiler_params=_params("arbitrary"), name=name)(*rows, *consts, *order)


def _in_proj(xb, w_in, widths, name, after=None):
    offs = [sum(widths[:k]) for k in range(len(widths))]

    def body(i, steps, x_ref, w_ref, *o_refs):
        x = x_ref[...]
        for o_ref, off, n in zip(o_refs, offs, widths):
            o_ref[...] = _dot(x, w_ref[:, off:off + n])

    return _rowwise(body, [xb], [w_in], [(n, F32) for n in widths], tm=_tile(xb.shape[0], MATMUL_ROW_TILE), name=name, after=after)


def _matmul(ab, w, name):
    def body(i, steps, a_ref, w_ref, o_ref):
        o_ref[...] = _dot(a_ref[...], w_ref[...]).astype(BF16)

    return _rowwise(body, [ab], [w], [(w.shape[1], BF16)], tm=_tile(ab.shape[0], MATMUL_ROW_TILE), name=name)[0]


def _ff1(xb, w, name):
    def body(i, steps, a_ref, w_ref, h_ref, r_ref):
        h = _dot(a_ref[...], w_ref[...])
        h_ref[...] = h.astype(BF16)
        r = jnp.maximum(h, 0.0)
        r_ref[...] = (r * r).astype(BF16)

    n = w.shape[1]
    return _rowwise(body, [xb], [w], [(n, BF16), (n, BF16)], tm=_tile(xb.shape[0], MATMUL_ROW_TILE), name=name)


def _proj_ln(a_list, w, stream, g, b, alpha, tm, name, after=None):
    widths = [a.shape[1] for a in a_list]
    offs = [sum(widths[:k]) for k in range(len(widths))]
    na = len(a_list)
    from_ln = len(stream) == 3

    def body(i, steps, *refs):
        a_refs, rest = refs[:na], refs[na:]
        if from_ln:
            s_ref, w_ref, sg_ref, sb_ref, g_ref, b_ref, z_ref, xb_ref = rest
            x = _ln_stats(s_ref[...])[0] * sg_ref[...] + sb_ref[...]
        else:
            s_ref, w_ref, g_ref, b_ref, z_ref, xb_ref = rest
            x = s_ref[...]
        acc = alpha * x
        for a_ref, off, n in zip(a_refs, offs, widths):
            acc = acc + _dot(a_ref[...], w_ref[off:off + n, :])
        z_ref[...] = acc
        y, _ = _ln_stats(acc)
        xb_ref[...] = (y * g_ref[...] + b_ref[...]).astype(BF16)

    d = w.shape[1]
    return _rowwise(body, [*a_list, stream[0]], [w, *stream[1:], g, b], [(d, F32), (d, BF16)], tm=tm, name=name, after=after)


def _bwd_ff2(dzb, h, w_ff2, name, after=None):
    def body(i, steps, dz_ref, h_ref, w_ref, dh_ref):
        dr = _dot_nt(dz_ref[...], w_ref[...])
        dh_ref[...] = (dr * (2.0 * jnp.maximum(h_ref[...].astype(F32), 0.0))).astype(BF16)

    return _rowwise(body, [dzb, h], [w_ff2], [(h.shape[1], BF16)], tm=_tile(h.shape[0], MATMUL_ROW_TILE), name=name, after=after)[0]


def _bwd_nt(g_list, w, col_offs, res, alpha, tm, name):
    ng = len(g_list)
    widths = [g.shape[1] for g in g_list]

    def body(i, steps, *refs):
        g_refs = refs[:ng]
        if res is None:
            w_ref, o_ref = refs[ng:]
            acc = None
        else:
            r_ref, w_ref, o_ref = refs[ng:]
            acc = alpha * r_ref[...]
        for g_ref, off, n in zip(g_refs, col_offs, widths):
            part = _dot_nt(g_ref[...], w_ref[:, off:off + n])
            acc = part if acc is None else acc + part
        o_ref[...] = acc.astype(o_ref.dtype)

    rows = list(g_list) + ([] if res is None else [res])
    return rows, w, body, tm, name


def _bwd_data(g_list, w, col_offs, res, alpha, out_dtype, tm, name):
    rows, w, body, tm, name = _bwd_nt(g_list, w, col_offs, res, alpha, tm, name)
    return _rowwise(body, rows, [w], [(w.shape[0], out_dtype)], tm=tm, name=name)[0]


def _bwd_data_ln(g_list, w, col_offs, res, alpha, z, gain, tm, name):
    ng = len(g_list)
    widths = [g.shape[1] for g in g_list]

    def body(i, steps, *refs):
        g_refs = refs[:ng]
        r_ref, z_ref, w_ref, gain_ref, dz_ref, dzb_ref, dg_ref, db_ref = refs[ng:]

        @pl.when(i == 0)
        def _():
            dg_ref[...] = jnp.zeros_like(dg_ref)
            db_ref[...] = jnp.zeros_like(db_ref)

        dy = alpha * r_ref[...]
        for g_ref, off, n in zip(g_refs, col_offs, widths):
            dy = dy + _dot_nt(g_ref[...], w_ref[:, off:off + n])
        y, rstd = _ln_stats(z_ref[...])
        dz = _ln_bwd(dy, y, rstd, gain_ref[...])
        dz_ref[...] = dz
        dzb_ref[...] = dz.astype(BF16)
        dg_ref[...] += _colsum(dy * y)
        db_ref[...] += _colsum(dy)

    d = z.shape[1]
    return _rowwise(body, [*g_list, res, z], [w, gain], [(d, F32), (d, BF16)], [((1, d), F32), ((1, d), F32)], tm=tm, name=name)


def _loss_head_ln(z, target, gain, bias, name):
    d = z.shape[1]

    def body(i, steps, z_ref, t_ref, gain_ref, bias_ref, dz_ref, dzb_ref, dg_ref, db_ref, l_ref):
        @pl.when(i == 0)
        def _():
            for r in (dg_ref, db_ref, l_ref):
                r[...] = jnp.zeros_like(r)

        yn, rstd = _ln_stats(z_ref[...])
        err = yn * gain_ref[...] + bias_ref[...] - t_ref[...]
        l_ref[...] += jnp.sum(err * err) * (0.5 / d)
        dy = err * (1.0 / d)
        dz = _ln_bwd(dy, yn, rstd, gain_ref[...])
        dz_ref[...] = dz
        dzb_ref[...] = dz.astype(BF16)
        dg_ref[...] += _colsum(dy * yn)
        db_ref[...] += _colsum(dy)

    return _rowwise(body, [z, target], [gain, bias], [(d, F32), (d, BF16)],
                    [((1, d), F32), ((1, d), F32), ((SUBLANES, 128), F32)], tm=_tile(z.shape[0], ROW_TILE), name=name)


def _bwd_out_proj(dzb, w_out, widths, name, after=None):
    offs = [sum(widths[:k]) for k in range(len(widths))]

    def body(i, steps, dz_ref, w_ref, *o_refs):
        dz = dz_ref[...]
        for o_ref, off, n in zip(o_refs, offs, widths):
            o_ref[...] = _dot_nt(dz, w_ref[off:off + n, :])

    return _rowwise(body, [dzb], [w_out], [(n, F32) for n in widths], tm=_tile(dzb.shape[0], MATMUL_ROW_TILE), name=name,
                    after=after)


def _adamw(w, g, m, v, name):
    def body(i, steps, w_ref, g_ref, m_ref, v_ref, d_ref, nm_ref, nv_ref):
        g_ = g_ref[...]
        nm = ADAM_B1 * m_ref[...] + (1.0 - ADAM_B1) * g_
        nv = ADAM_B2 * v_ref[...] + (1.0 - ADAM_B2) * (g_ * g_)
        m_hat = nm / (1.0 - ADAM_B1 ** ADAM_STEP)
        v_hat = nv / (1.0 - ADAM_B2 ** ADAM_STEP)
        d_ref[...] = -ADAM_LR * (m_hat / (jnp.sqrt(v_hat) + ADAM_EPS) + ADAM_WD * w_ref[...])
        nm_ref[...] = nm
        nv_ref[...] = nv

    c = w.shape[1]
    return _rowwise(body, [w, g, m, v], [], [(c, F32)] * 3, tm=_tile(w.shape[0], ROW_TILE), name=name)


def _mm_tn(a, g, name, slots=1, after=None):
    t, ka = a.shape
    n = g.shape[1]
    ta, tn, tk = _tile(ka, TN_TILE), _tile(n // slots, TN_TILE), _tile(t, TN_TOKENS)
    per = n // slots // tn

    order = [] if after is None else [after]

    def kern(a_ref, g_ref, *refs):
        o_ref = refs[-1]

        @pl.when(pl.program_id(2) == 0)
        def _():
            o_ref[...] = jnp.zeros_like(o_ref)

        o_ref[0] += _dot_tn(a_ref[...], g_ref[...])

    return pl.pallas_call(
        kern, grid=(ka // ta, n // tn, t // tk),
        in_specs=[pl.BlockSpec((tk, ta), lambda i, j, k: (k, i)), pl.BlockSpec((tk, tn), lambda i, j, k: (k, j))] + [ANY] * len(order),
        out_specs=pl.BlockSpec((1, ta, tn), lambda i, j, k: (j // per, i, j % per)), out_shape=SDS((slots, ka, n // slots), F32),
        compiler_params=_params("arbitrary", "arbitrary", "arbitrary"), name=name)(a, g, *order)


def _mm_tn_cols(a, g_list, name):
    t, ka = a.shape
    widths = [g.shape[1] for g in g_list]
    offs = [sum(widths[:k]) for k in range(len(widths))]
    ta, tk = _tile(ka, ROW_TILE), _tile(t, TN_TOKENS)

    def kern(a_ref, *refs):
        g_refs, o_ref = refs[:-1], refs[-1]

        @pl.when(pl.program_id(1) == 0)
        def _():
            o_ref[...] = jnp.zeros_like(o_ref)

        a_ = a_ref[...]
        for g_ref, off, n in zip(g_refs, offs, widths):
            o_ref[:, off:off + n] += _dot_tn(a_, g_ref[...])

    return pl.pallas_call(
        kern, grid=(ka // ta, t // tk),
        in_specs=[pl.BlockSpec((tk, ta), lambda i, k: (k, i))] + [pl.BlockSpec((tk, n), lambda i, k: (k, 0)) for n in widths],
        out_specs=pl.BlockSpec((ta, sum(widths)), lambda i, k: (i, 0)), out_shape=SDS((ka, sum(widths)), F32),
        compiler_params=_params("arbitrary", "arbitrary"), name=name)(a, *g_list)


def _mm_tn_rows(a_list, g, name):
    t, n = g.shape
    widths = [a.shape[1] for a in a_list]
    offs = [sum(widths[:k]) for k in range(len(widths))]
    tn, tk = _tile(n, TN_TILE), _tile(t, TN_TOKENS)

    def kern(*refs):
        a_refs, g_ref, o_ref = refs[:-2], refs[-2], refs[-1]

        @pl.when(pl.program_id(1) == 0)
        def _():
            o_ref[...] = jnp.zeros_like(o_ref)

        g_ = g_ref[...]
        for a_ref, off, ka in zip(a_refs, offs, widths):
            o_ref[off:off + ka, :] += _dot_tn(a_ref[...], g_)

    return pl.pallas_call(
        kern, grid=(n // tn, t // tk),
        in_specs=[pl.BlockSpec((tk, ka), lambda j, k: (k, 0)) for ka in widths] + [pl.BlockSpec((tk, tn), lambda j, k: (k, j))],
        out_specs=pl.BlockSpec((sum(widths), tn), lambda j, k: (0, j)), out_shape=SDS((sum(widths), n), F32),
        compiler_params=_params("arbitrary", "arbitrary"), name=name)(*a_list, g)


def _conv_geometry(seq, taps):
    pad = _round_up(taps - 1, SUBLANES)
    rc = _tile(seq, CONV_ROWS)
    assert rc % pad == 0 and seq % rc == 0
    return pad, rc, seq // rc


def _chunk_spec(rc, n, nch):
    return pl.BlockSpec((rc, n), lambda b, i: (b * nch + i, 0))


def _prev_halo_spec(pad, rc, n, nch):
    per = rc // pad
    return pl.BlockSpec((pad, n), lambda b, i: (jnp.maximum((b * nch + i) * per - 1, 0), 0))


def _next_halo_spec(pad, rc, n, nch, total_rows):
    per = rc // pad
    last = total_rows // pad - 1
    return pl.BlockSpec((pad, n), lambda b, i: (jnp.minimum((b * nch + i + 1) * per, last), 0))


def _whole2(a):
    return pl.BlockSpec(a.shape, lambda b, i, nd=len(a.shape): (0,) * nd)


def _sub_rows(rc):
    return SUB_ROWS if rc % SUB_ROWS == 0 else rc


def _build_shifts(sh_ref, src_ref, offsets):
    rows = src_ref.shape[0]
    for r in sorted({o % SUBLANES for o in offsets} - {0}):
        sh_ref[r, 0:rows - SUBLANES, :] = src_ref[r:r + rows - SUBLANES, :]


def _read_shifted(sh_ref, src_ref, o, s0, sub):
    r = o % SUBLANES
    a = o - r + s0
    return src_ref[a:a + sub, :] if r == 0 else sh_ref[r, a:a + sub, :]


def _tap_sum(sh_ref, src_ref, w_ref, offsets, s0, sub):
    acc = None
    for k, o in enumerate(offsets):
        term = _read_shifted(sh_ref, src_ref, o, s0, sub) * w_ref[k:k + 1, :]
        acc = term if acc is None else acc + term
    return acc


def _row_groups(x):
    acc = x[0:SUBLANES, :]
    for g0 in range(SUBLANES, x.shape[0], SUBLANES):
        acc = acc + x[g0:g0 + SUBLANES, :]
    return acc


def _mix_a_fwd(a_val, a_gate, conv_w, conv_b, ln_g, ln_b, batch, name):
    t, da = a_val.shape
    taps = conv_w.shape[0]
    pad, rc, nch = _conv_geometry(t // batch, taps)
    sub = _sub_rows(rc)
    offs = [pad - (taps - 1) + k for k in range(taps)]

    def kern(av_ref, ag_ref, pav_ref, pag_ref, w_ref, cb_ref, g_ref, b_ref, a_ref, ac_ref, pad_ref, sh_ref):
        first = pl.program_id(1) == 0
        pad_ref[0:pad, :] = jnp.where(first, 0.0, pav_ref[...] * _sigmoid(pag_ref[...]))
        for s0 in range(0, rc, sub):
            pad_ref[pad + s0:pad + s0 + sub, :] = av_ref[s0:s0 + sub, :] * _sigmoid(ag_ref[s0:s0 + sub, :])
        _build_shifts(sh_ref, pad_ref, offs)
        for s0 in range(0, rc, sub):
            ac = _tap_sum(sh_ref, pad_ref, w_ref, offs, s0, sub) + cb_ref[...]
            ac_ref[s0:s0 + sub, :] = ac
            y, _ = _ln_stats(ac)
            aln = y * g_ref[...] + b_ref[...]
            a_ref[s0:s0 + sub, :] = (aln * _sigmoid(aln)).astype(BF16)

    chunk, halo = _chunk_spec(rc, da, nch), _prev_halo_spec(pad, rc, da, nch)
    return pl.pallas_call(
        kern, grid=(batch, nch),
        in_specs=[chunk, chunk, halo, halo, _whole2(conv_w), _whole2(conv_b), _whole2(ln_g), _whole2(ln_b)],
        out_specs=[chunk, chunk], out_shape=[SDS((t, da), BF16), SDS((t, da), F32)],
        scratch_shapes=[pltpu.VMEM((pad + rc, da), F32), pltpu.VMEM((SUBLANES, pad + rc, da), F32)],
        compiler_params=_params("arbitrary", "arbitrary"), name=name)(a_val, a_gate, a_val, a_gate, conv_w, conv_b, ln_g, ln_b)


def _mix_a_bwd(a_val, a_gate, ac, da_, conv_w, ln_g, ln_b, batch, name):
    t, da = a_val.shape
    taps = conv_w.shape[0]
    pad, rc, nch = _conv_geometry(t // batch, taps)
    sub = _sub_rows(rc)
    offs = [taps - 1 - k for k in range(taps)]

    def kern(av_ref, ag_ref, ac_ref, da_ref, nac_ref, nda_ref, w_ref, g_ref, b_ref,
             dav_ref, dag_ref, dw_ref, dcb_ref, dg_ref, db_ref, dpad_ref, shd_ref, dwacc_ref):
        i = pl.program_id(1)
        start = (pl.program_id(0) == 0) & (i == 0)
        end = (pl.program_id(0) == batch - 1) & (i == nch - 1)

        @pl.when(start)
        def _():
            for r in (dcb_ref, dg_ref, db_ref, dwacc_ref):
                r[...] = jnp.zeros_like(r)

        gain, bias = g_ref[...], b_ref[...]

        def d_conv_out(ac_, dout):
            y, rstd = _ln_stats(ac_)
            aln = y * gain + bias
            sig = _sigmoid(aln)
            daln = dout * (sig * (1.0 + aln * (1.0 - sig)))
            return _ln_bwd(daln, y, rstd, gain), daln, y

        dac_next, _, _ = d_conv_out(nac_ref[...], nda_ref[...])
        dpad_ref[rc:rc + pad, :] = jnp.where(i == nch - 1, 0.0, dac_next)
        for s0 in range(0, rc, sub):
            rows = slice(s0, s0 + sub)
            dac, daln, y = d_conv_out(ac_ref[rows, :], da_ref[rows, :])
            dg_ref[...] += _colsum(daln * y)
            db_ref[...] += _colsum(daln)
            dcb_ref[...] += _colsum(dac)
            dpad_ref[rows, :] = dac
        _build_shifts(shd_ref, dpad_ref, offs)
        for s0 in range(0, rc, sub):
            rows = slice(s0, s0 + sub)
            av = av_ref[rows, :]
            sig = _sigmoid(ag_ref[rows, :])
            glu = av * sig
            dgl = None
            for k, o in enumerate(offs):
                shifted = _read_shifted(shd_ref, dpad_ref, o, s0, sub)
                dwacc_ref[k] += _row_groups(glu * shifted)
                term = shifted * w_ref[k:k + 1, :]
                dgl = term if dgl is None else dgl + term
            dav_ref[rows, :] = (dgl * sig).astype(BF16)
            dag_ref[rows, :] = (dgl * glu * (1.0 - sig)).astype(BF16)

        @pl.when(end)
        def _():
            for k in range(taps):
                dw_ref[k:k + 1, :] = _colsum(dwacc_ref[k])

    chunk = _chunk_spec(rc, da, nch)
    nxt = _next_halo_spec(pad, rc, da, nch, t)
    vec = SDS((1, da), F32)
    return pl.pallas_call(
        kern, grid=(batch, nch),
        in_specs=[chunk, chunk, chunk, chunk, nxt, nxt, _whole2(conv_w), _whole2(ln_g), _whole2(ln_b)],
        out_specs=[chunk, chunk, _whole2(conv_w), _whole2(vec), _whole2(vec), _whole2(vec)],
        out_shape=[SDS((t, da), BF16), SDS((t, da), BF16), SDS(conv_w.shape, F32), vec, vec, vec],
        scratch_shapes=[pltpu.VMEM((rc + pad, da), F32), pltpu.VMEM((SUBLANES, rc + pad, da), F32),
                        pltpu.VMEM((taps, SUBLANES, da), F32)],
        compiler_params=_params("arbitrary", "arbitrary"), name=name)(a_val, a_gate, ac, da_, ac, da_, conv_w, ln_g, ln_b)


def _mix_c_fwd(c_b, c_c, c_x, conv_w, batch, name):
    t, dc = c_b.shape
    taps = conv_w.shape[0]
    pad, rc, nch = _conv_geometry(t // batch, taps)
    sub = _sub_rows(rc)
    offs = [pad - (taps - 1) + k for k in range(taps)]

    def kern(cb_ref, cc_ref, cx_ref, pcc_ref, pcx_ref, w_ref, o_ref, pad_ref, sh_ref):
        pad_ref[0:pad, :] = jnp.where(pl.program_id(1) == 0, 0.0, pcc_ref[...] * pcx_ref[...])
        for s0 in range(0, rc, sub):
            pad_ref[pad + s0:pad + s0 + sub, :] = cc_ref[s0:s0 + sub, :] * cx_ref[s0:s0 + sub, :]
        _build_shifts(sh_ref, pad_ref, offs)
        for s0 in range(0, rc, sub):
            o_ref[s0:s0 + sub, :] = (cb_ref[s0:s0 + sub, :] * _tap_sum(sh_ref, pad_ref, w_ref, offs, s0, sub)).astype(BF16)

    chunk, prev = _chunk_spec(rc, dc, nch), _prev_halo_spec(pad, rc, dc, nch)
    return pl.pallas_call(
        kern, grid=(batch, nch), in_specs=[chunk, chunk, chunk, prev, prev, _whole2(conv_w)],
        out_specs=chunk, out_shape=SDS((t, dc), BF16),
        scratch_shapes=[pltpu.VMEM((pad + rc, dc), F32), pltpu.VMEM((SUBLANES, pad + rc, dc), F32)],
        compiler_params=_params("arbitrary", "arbitrary"), name=name)(c_b, c_c, c_x, c_c, c_x, conv_w)


def _mix_c_bwd(c_b, c_c, c_x, dco, conv_w, batch, name):
    t, dc = c_b.shape
    taps = conv_w.shape[0]
    pad, rc, nch = _conv_geometry(t // batch, taps)
    sub = _sub_rows(rc)
    offs_in = [pad - (taps - 1) + k for k in range(taps)]
    offs_out = [taps - 1 - k for k in range(taps)]

    def kern(cb_ref, cc_ref, cx_ref, do_ref, pcc_ref, pcx_ref, ncb_ref, ndo_ref, w_ref,
             dcb_ref, dcc_ref, dcx_ref, dw_ref, ppad_ref, dpad_ref, shp_ref, shd_ref, dwacc_ref):
        i = pl.program_id(1)

        @pl.when((pl.program_id(0) == 0) & (i == 0))
        def _():
            dwacc_ref[...] = jnp.zeros_like(dwacc_ref)

        ppad_ref[0:pad, :] = jnp.where(i == 0, 0.0, pcc_ref[...] * pcx_ref[...])
        dpad_ref[rc:rc + pad, :] = jnp.where(i == nch - 1, 0.0, ndo_ref[...] * ncb_ref[...])
        for s0 in range(0, rc, sub):
            rows = slice(s0, s0 + sub)
            ppad_ref[pad + s0:pad + s0 + sub, :] = cc_ref[rows, :] * cx_ref[rows, :]
            dpad_ref[rows, :] = do_ref[rows, :] * cb_ref[rows, :]
        _build_shifts(shp_ref, ppad_ref, offs_in)
        _build_shifts(shd_ref, dpad_ref, offs_out)
        for s0 in range(0, rc, sub):
            rows = slice(s0, s0 + sub)
            dcv = dpad_ref[rows, :]
            cv = None
            for k, o in enumerate(offs_in):
                shifted = _read_shifted(shp_ref, ppad_ref, o, s0, sub)
                dwacc_ref[k] += _row_groups(dcv * shifted)
                cv = shifted * w_ref[k:k + 1, :] if cv is None else cv + shifted * w_ref[k:k + 1, :]
            dp = _tap_sum(shd_ref, dpad_ref, w_ref, offs_out, s0, sub)
            dcb_ref[rows, :] = (do_ref[rows, :] * cv).astype(BF16)
            dcc_ref[rows, :] = (dp * cx_ref[rows, :]).astype(BF16)
            dcx_ref[rows, :] = (dp * cc_ref[rows, :]).astype(BF16)

        @pl.when((pl.program_id(0) == batch - 1) & (i == nch - 1))
        def _():
            for k in range(taps):
                dw_ref[k:k + 1, :] = _colsum(dwacc_ref[k])

    chunk, prev = _chunk_spec(rc, dc, nch), _prev_halo_spec(pad, rc, dc, nch)
    nxt = _next_halo_spec(pad, rc, dc, nch, t)
    return pl.pallas_call(
        kern, grid=(batch, nch), in_specs=[chunk, chunk, chunk, chunk, prev, prev, nxt, nxt, _whole2(conv_w)],
        out_specs=[chunk, chunk, chunk, _whole2(conv_w)],
        out_shape=[SDS((t, dc), BF16)] * 3 + [SDS(conv_w.shape, F32)],
        scratch_shapes=[pltpu.VMEM((pad + rc, dc), F32), pltpu.VMEM((rc + pad, dc), F32),
                        pltpu.VMEM((SUBLANES, pad + rc, dc), F32), pltpu.VMEM((SUBLANES, rc + pad, dc), F32),
                        pltpu.VMEM((taps, SUBLANES, dc), F32)],
        compiler_params=_params("arbitrary", "arbitrary"), name=name)(c_b, c_c, c_x, dco, c_c, c_x, c_b, dco, conv_w)


def _head_of_lane(db):
    return lax.broadcasted_iota(jnp.int32, (1, db), 1) // (db // B_HEADS)


def _tril(rows_ge_cols=True):
    r = lax.broadcasted_iota(jnp.int32, (CHUNK, CHUNK), 0)
    c = lax.broadcasted_iota(jnp.int32, (CHUNK, CHUNK), 1)
    return (r >= c) if rows_ge_cols else (r <= c)


def _spatial_mix(wm, vb, bias, head):
    mixed = bias
    for h in range(B_HEADS):
        mixed = mixed + jnp.where(head == h, _dot(wm[h], vb), 0.0)
    return mixed


def _mix_b_fwd(b_u, b_v, w_s, bias_full, ln_g, ln_b, name):
    t, db = b_u.shape
    rb = _tile(t, MATMUL_ROW_TILE)

    def kern(bu_ref, bv_ref, ws_ref, bias_ref, g_ref, b_ref, o_ref):
        head = _head_of_lane(db)
        wm = [jnp.where(_tril(), ws_ref[h], 0.0).astype(BF16) for h in range(B_HEADS)]
        for ch in range(rb // CHUNK):
            rows = slice(ch * CHUNK, (ch + 1) * CHUNK)
            y, _ = _ln_stats(_gelu(bv_ref[rows, :]))
            vb = (y * g_ref[...] + b_ref[...]).astype(BF16)
            mixed = _spatial_mix(wm, vb, bias_ref[...], head)
            o_ref[rows, :] = (_gelu(bu_ref[rows, :]) * mixed).astype(BF16)

    def whole(a):
        return pl.BlockSpec(a.shape, lambda i, nd=len(a.shape): (0,) * nd)

    tile = pl.BlockSpec((rb, db), lambda i: (i, 0))
    return pl.pallas_call(
        kern, grid=(t // rb,), in_specs=[tile, tile, whole(w_s), whole(bias_full), whole(ln_g), whole(ln_b)],
        out_specs=tile, out_shape=SDS((t, db), BF16), compiler_params=_params("arbitrary"), name=name)(
            b_u, b_v, w_s, bias_full, ln_g, ln_b)


def _mix_b_bwd(b_u, b_v, dbo, w_s, w_s_t, bias_full, ln_g, ln_b, name):
    t, db = b_u.shape
    rb = _tile(t, MATMUL_ROW_TILE)
    steps = t // rb

    def kern(bu_ref, bv_ref, do_ref, ws_ref, wst_ref, bias_ref, g_ref, b_ref,
             dbu_ref, dbv_ref, dws_ref, dbs_ref, dg_ref, dbeta_ref, dbias_ref):
        i = pl.program_id(0)

        @pl.when(i == 0)
        def _():
            for r in (dws_ref, dg_ref, dbeta_ref, dbias_ref):
                r[...] = jnp.zeros_like(r)

        head = _head_of_lane(db)
        gain = g_ref[...]
        wm = [jnp.where(_tril(), ws_ref[h], 0.0).astype(BF16) for h in range(B_HEADS)]
        wmt = [jnp.where(_tril(False), wst_ref[h], 0.0).astype(BF16) for h in range(B_HEADS)]
        for ch in range(rb // CHUNK):
            rows = slice(ch * CHUNK, (ch + 1) * CHUNK)
            bu, bv, dout = bu_ref[rows, :], bv_ref[rows, :], do_ref[rows, :]
            y, rstd = _ln_stats(_gelu(bv))
            vb = (y * gain + b_ref[...]).astype(BF16)
            mixed = _spatial_mix(wm, vb, bias_ref[...], head)
            du = dout * mixed
            dmixed = dout * _gelu(bu)
            dbias_ref[...] += dmixed
            dmb = dmixed.astype(BF16)
            dv = None
            for h in range(B_HEADS):
                dws_ref[h] += _dot_nt(jnp.where(head == h, dmb, 0.0).astype(BF16), vb)
                part = jnp.where(head == h, _dot(wmt[h], dmb), 0.0)
                dv = part if dv is None else dv + part
            dg_ref[...] += _colsum(dv * y)
            dbeta_ref[...] += _colsum(dv)
            dbv_ref[rows, :] = (_ln_bwd(dv, y, rstd, gain) * _gelu_grad(bv)).astype(BF16)
            dbu_ref[rows, :] = (du * _gelu_grad(bu)).astype(BF16)

        @pl.when(i == steps - 1)
        def _():
            for h in range(B_HEADS):
                dws_ref[h] = jnp.where(_tril(), dws_ref[h], 0.0)
                dbs_ref[h] = jnp.sum(jnp.where(head == h, dbias_ref[...], 0.0), axis=1, keepdims=True)

    def whole(a):
        return pl.BlockSpec(a.shape, lambda i, nd=len(a.shape): (0,) * nd)

    tile = pl.BlockSpec((rb, db), lambda i: (i, 0))
    vec = SDS((1, db), F32)
    dbs = SDS((B_HEADS, CHUNK, 1), F32)
    return pl.pallas_call(
        kern, grid=(steps,),
        in_specs=[tile, tile, tile, whole(w_s), whole(w_s_t), whole(bias_full), whole(ln_g), whole(ln_b)],
        out_specs=[tile, tile, whole(w_s), whole(dbs), whole(vec), whole(vec)],
        out_shape=[SDS((t, db), BF16), SDS((t, db), BF16), SDS(w_s.shape, F32), dbs, vec, vec],
        scratch_shapes=[pltpu.VMEM((CHUNK, db), F32)],
        compiler_params=_params("arbitrary"), name=name)(b_u, b_v, dbo, w_s, w_s_t, bias_full, ln_g, ln_b)


def _softmax_rows(s):
    e = jnp.exp(s - jnp.max(s, axis=-1, keepdims=True))
    return e / jnp.sum(e, axis=-1, keepdims=True)


def _attention_fwd(q, kv, batch, name):
    t, d = q.shape
    seq, mlen, hd = t // batch, kv.shape[0] // batch, d // X_HEADS
    ar = _tile(seq, ATT_ROWS)
    scale = hd ** -0.5

    def kern(q_ref, k_ref, v_ref, o_ref):
        k, v = k_ref[...], v_ref[...]
        for r0 in range(0, seq, ar):
            p = _softmax_rows(_dot_nt(q_ref[r0:r0 + ar, :], k) * scale)
            o_ref[r0:r0 + ar, :] = _dot(p.astype(BF16), v).astype(BF16)

    qs = pl.BlockSpec((seq, hd), lambda b, h: (b, h))
    return pl.pallas_call(
        kern, grid=(batch, X_HEADS),
        in_specs=[qs, pl.BlockSpec((mlen, hd), lambda b, h: (b, h)), pl.BlockSpec((mlen, hd), lambda b, h: (b, X_HEADS + h))],
        out_specs=qs, out_shape=SDS((t, d), BF16), compiler_params=_params("arbitrary", "arbitrary"), name=name)(q, kv, kv)


def _attention_bwd(q, kv, do, batch, name):
    t, d = q.shape
    seq, mlen, hd = t // batch, kv.shape[0] // batch, d // X_HEADS
    ar = _tile(seq, ATT_ROWS)
    scale = hd ** -0.5

    def kern(q_ref, k_ref, v_ref, do_ref, dq_ref, dk_ref, dv_ref):
        k, v = k_ref[...], v_ref[...]
        dk = jnp.zeros((mlen, hd), F32)
        dv = jnp.zeros((mlen, hd), F32)
        for r0 in range(0, seq, ar):
            qr, dor = q_ref[r0:r0 + ar, :], do_ref[r0:r0 + ar, :]
            p = _softmax_rows(_dot_nt(qr, k) * scale)
            dp = _dot_nt(dor, v)
            ds = (p * (dp - jnp.sum(p * dp, axis=-1, keepdims=True)) * scale).astype(BF16)
            dq_ref[r0:r0 + ar, :] = _dot(ds, k).astype(BF16)
            dk = dk + _dot_tn(ds, qr)
            dv = dv + _dot_tn(p.astype(BF16), dor)
        dk_ref[...] = dk.astype(BF16)
        dv_ref[...] = dv.astype(BF16)

    qs = pl.BlockSpec((seq, hd), lambda b, h: (b, h))
    ks = pl.BlockSpec((mlen, hd), lambda b, h: (b, h))
    dkv = SDS((kv.shape[0], d), BF16)
    return pl.pallas_call(
        kern, grid=(batch, X_HEADS),
        in_specs=[qs, ks, pl.BlockSpec((mlen, hd), lambda b, h: (b, X_HEADS + h)), qs],
        out_specs=[qs, ks, ks], out_shape=[SDS((t, d), BF16), dkv, dkv],
        compiler_params=_params("arbitrary", "arbitrary"), name=name)(q, kv, kv, do)


def _place():
    x, y, c = lax.axis_index("x"), lax.axis_index("y"), lax.axis_index("c")
    other_chips = [(1 - x, y), (x, 1 - y), (1 - x, 1 - y)]
    return x, y, c, other_chips


def _comm_call(kern, out_shape, n_pairs, name, *args):
    return pl.pallas_call(
        kern, out_shape=out_shape, in_specs=[ANY] * len(args), out_specs=jax.tree.map(lambda _: ANY, out_shape),
        scratch_shapes=[pltpu.SemaphoreType.DMA((n_pairs,)), pltpu.SemaphoreType.DMA((n_pairs,)), pltpu.SemaphoreType.DMA((n_pairs,))],
        name=name)(*args)


def _all_gather_chips(shard, name):
    r, cols = shard.shape
    rh = r // 2

    def kern(s_ref, o_ref, send_sems, recv_sems, local_sems):
        x, y, c, chips = _place()
        mine_slot = 2 * x + y
        half = pl.ds(c * rh, rh)
        other_half = pl.ds((1 - c) * rh, rh)

        def copy(k, src, dst, to):
            return pltpu.make_async_remote_copy(src_ref=src, dst_ref=dst, send_sem=send_sems.at[k], recv_sem=recv_sems.at[k],
                                                device_id=to, device_id_type=MESH)

        mine = pltpu.make_async_copy(s_ref, o_ref.at[mine_slot], local_sems.at[0])
        mine.start()
        first = [copy(j, s_ref.at[half], o_ref.at[mine_slot, half], (px, py, c)) for j, (px, py) in enumerate(chips)]
        for cp in first:
            cp.start()
        passed = []
        for j, (px, py) in enumerate(chips):
            landed = o_ref.at[2 * px + py, half]
            copy(j, landed, landed, (px, py, c)).wait_recv()
            cp = copy(3 + j, landed, landed, (x, y, 1 - c))
            cp.start()
            passed.append(cp)
        for j, (px, py) in enumerate(chips):
            theirs = o_ref.at[2 * px + py, other_half]
            copy(3 + j, theirs, theirs, (x, y, 1 - c)).wait_recv()
        for cp in first + passed:
            cp.wait_send()
        mine.wait()

    return _comm_call(kern, SDS((N_CHIPS, r, cols), shard.dtype), 6, name, shard)


def _join_halves(joined, name):
    n = len(joined)

    def kern(*refs):
        j_refs = refs[:n]
        send_sems, recv_sems = refs[2 * n:]
        x, y, c, _ = _place()
        copies = []
        for k, j_ref in enumerate(j_refs):
            rh = j_ref.shape[1] // 2
            rows = j_ref.at[:, pl.ds(c * rh, rh)]
            cp = pltpu.make_async_remote_copy(src_ref=rows, dst_ref=rows, send_sem=send_sems.at[k], recv_sem=recv_sems.at[k],
                                              device_id=(x, y, 1 - c), device_id_type=MESH)
            cp.start()
            copies.append(cp)
        for k, (cp, j_ref) in enumerate(zip(copies, j_refs)):
            rh = j_ref.shape[1] // 2
            theirs = j_ref.at[:, pl.ds((1 - c) * rh, rh)]
            pltpu.make_async_remote_copy(src_ref=theirs, dst_ref=theirs, send_sem=send_sems.at[k], recv_sem=recv_sems.at[k],
                                         device_id=(x, y, 1 - c), device_id_type=MESH).wait_recv()
            cp.wait_send()

    return pl.pallas_call(
        kern, out_shape=[SDS(j.shape, j.dtype) for j in joined], in_specs=[ANY] * n, out_specs=[ANY] * n,
        input_output_aliases={k: k for k in range(n)},
        scratch_shapes=[pltpu.SemaphoreType.DMA((n,)), pltpu.SemaphoreType.DMA((n,))], name=name)(*joined)


HBM = pl.BlockSpec(memory_space=pltpu.HBM)
SEMAPHORES = pl.BlockSpec(memory_space=pltpu.SEMAPHORE)
COPIES_PER_ARRAY = N_CHIPS - 1
GATHER_COPIES = N_CHIPS


def _in_hbm(a):
    return pltpu.with_memory_space_constraint(a, pltpu.HBM)


def _gather_copies(shard_refs, land_refs, send_sems, recv_sems):
    x, y, c, chips = _place()
    copies = []
    for k, (s_ref, land_ref) in enumerate(zip(shard_refs, land_refs)):
        half = pl.ds(c * (s_ref.shape[0] // 2), s_ref.shape[0] // 2)
        for j, (px, py) in enumerate(chips):
            n = GATHER_COPIES * k + j
            copies.append(pltpu.make_async_remote_copy(
                src_ref=s_ref.at[half], dst_ref=land_ref.at[2 * x + y, half], send_sem=send_sems.at[n], recv_sem=recv_sems.at[n],
                device_id=(px, py, c), device_id_type=MESH))
        n = GATHER_COPIES * k + N_CHIPS - 1
        copies.append(pltpu.make_async_remote_copy(
            src_ref=s_ref, dst_ref=land_ref.at[2 * x + y], send_sem=send_sems.at[n], recv_sem=recv_sems.at[n],
            device_id=(x, y, 1 - c), device_id_type=MESH))
    return copies


def _swap_copies(full_refs, land_refs, send_sems, recv_sems):
    x, y, c, _ = _place()
    copies = []
    for k, (g_ref, land_ref) in enumerate(zip(full_refs, land_refs)):
        rh = land_ref.shape[1]
        copies.append(pltpu.make_async_remote_copy(
            src_ref=g_ref.at[:, pl.ds((1 - c) * rh, rh), :], dst_ref=land_ref, send_sem=send_sems.at[k], recv_sem=recv_sems.at[k],
            device_id=(x, y, 1 - c), device_id_type=MESH))
    return copies


def _owner_copies(sum_refs, land_refs, send_sems, recv_sems):
    x, y, c, chips = _place()
    copies = []
    for k, (s_ref, land_ref) in enumerate(zip(sum_refs, land_refs)):
        for j, (px, py) in enumerate(chips):
            n = COPIES_PER_ARRAY * k + j
            copies.append(pltpu.make_async_remote_copy(
                src_ref=s_ref.at[2 * px + py], dst_ref=land_ref.at[j], send_sem=send_sems.at[n], recv_sem=recv_sems.at[n],
                device_id=(px, py, c), device_id_type=MESH))
    return copies


def _copies_start(build, per_array, sources, lands, name, after=None):
    ns, nb = len(sources), len(sources) + len(lands)
    n_copies = per_array * ns
    order = [] if after is None else [after]
    n_in = nb + len(order)

    def kern(*refs):
        for cp in build(refs[:ns], refs[ns:nb], refs[n_in + nb], refs[n_in + nb + 1]):
            cp.start()
        refs[-1][...] = jnp.zeros_like(refs[-1])

    bufs = [*sources, *lands]
    res = pl.pallas_call(
        kern, name=name,
        out_shape=(*[pltpu.HBM(b.shape, b.dtype) for b in bufs], pltpu.SemaphoreType.DMA((n_copies,)), pltpu.SemaphoreType.DMA((n_copies,)),
                   SDS((SUBLANES, 128), F32)),
        in_specs=[HBM] * nb + [ANY] * len(order),
        out_specs=(*[HBM] * nb, SEMAPHORES, SEMAPHORES, pl.BlockSpec(memory_space=pltpu.VMEM)),
        input_output_aliases={i: i for i in range(nb)},
        compiler_params=pltpu.CompilerParams(has_side_effects=pltpu.SideEffectType.DATAFLOW_SIDE_EFFECTING),
    )(*[_in_hbm(b) for b in bufs], *order)
    return res[nb], res[nb + 1], list(res[:nb]), res[-1]


def _copies_wait(build, send_sems, recv_sems, bufs, ns, after, name):
    nb = len(bufs)

    def kern(*refs):
        for cp in build(refs[:ns], refs[ns:nb], refs[nb], refs[nb + 1]):
            cp.wait_send()
            cp.wait_recv()

    res = pl.pallas_call(
        kern, name=name, out_shape=tuple(pltpu.HBM(b.shape, b.dtype) for b in bufs),
        in_specs=[HBM] * nb + [SEMAPHORES, SEMAPHORES, ANY], out_specs=tuple([HBM] * nb),
        input_output_aliases={i: i for i in range(nb)},
        compiler_params=pltpu.CompilerParams(has_side_effects=pltpu.SideEffectType.DATAFLOW_SIDE_EFFECTING),
    )(*bufs, send_sems, recv_sems, after)
    return list(res[:ns]), list(res[ns:])


def _gather_finish(lands, name):
    ns = len(lands)

    def kern(*refs):
        l_refs = refs[:ns]
        send_sems, recv_sems = refs[2 * ns:]
        x, y, c, chips = _place()
        passed = []
        for k, l_ref in enumerate(l_refs):
            rh = l_ref.shape[1] // 2
            for j, (px, py) in enumerate(chips):
                landed = l_ref.at[2 * px + py, pl.ds(c * rh, rh)]
                cp = pltpu.make_async_remote_copy(
                    src_ref=landed, dst_ref=landed, send_sem=send_sems.at[COPIES_PER_ARRAY * k + j],
                    recv_sem=recv_sems.at[COPIES_PER_ARRAY * k + j], device_id=(x, y, 1 - c), device_id_type=MESH)
                cp.start()
                passed.append(cp)
        for k, l_ref in enumerate(l_refs):
            rh = l_ref.shape[1] // 2
            for j, (px, py) in enumerate(chips):
                theirs = l_ref.at[2 * px + py, pl.ds((1 - c) * rh, rh)]
                pltpu.make_async_remote_copy(
                    src_ref=theirs, dst_ref=theirs, send_sem=send_sems.at[COPIES_PER_ARRAY * k + j],
                    recv_sem=recv_sems.at[COPIES_PER_ARRAY * k + j], device_id=(x, y, 1 - c), device_id_type=MESH).wait_recv()
        for cp in passed:
            cp.wait_send()

    return pl.pallas_call(
        kern, out_shape=[SDS(l.shape, l.dtype) for l in lands], in_specs=[ANY] * ns, out_specs=[ANY] * ns,
        input_output_aliases={k: k for k in range(ns)},
        scratch_shapes=[pltpu.SemaphoreType.DMA((COPIES_PER_ARRAY * ns,)), pltpu.SemaphoreType.DMA((COPIES_PER_ARRAY * ns,))],
        name=name)(*lands)


def _add_sibling(full, theirs, out_dtype, name):
    n, r, cols = full.shape
    rh = r // 2
    tr = _tile(rh, ROW_TILE)
    nb = rh // tr

    def kern(c_ref, a_ref, b_ref, o_ref):
        o_ref[...] = (a_ref[...] + b_ref[...]).astype(out_dtype)

    c = lax.axis_index("c").astype(jnp.int32).reshape(1)
    return pl.pallas_call(
        kern, out_shape=SDS((n, rh, cols), out_dtype),
        grid_spec=pltpu.PrefetchScalarGridSpec(
            num_scalar_prefetch=1, grid=(n, nb),
            in_specs=[pl.BlockSpec((1, tr, cols), lambda j, i, c_ref: (j, c_ref[0] * nb + i, 0)),
                      pl.BlockSpec((1, tr, cols), lambda j, i, c_ref: (j, i, 0))],
            out_specs=pl.BlockSpec((1, tr, cols), lambda j, i, c_ref: (j, i, 0))),
        compiler_params=_params("arbitrary", "arbitrary"), name=name)(c, full, theirs)


def _add_owners(partial, received, name, into=None, layer=0, depth=1):
    n, rh, cols = partial.shape
    tr = _tile(rh, ROW_TILE)
    nb = rh // tr
    stacked = [] if into is None else [into]

    def kern(s_ref, a_ref, b_ref, *refs):
        acc = a_ref[0].astype(F32)
        for j in range(N_CHIPS - 1):
            acc = acc + b_ref[j].astype(F32)
        refs[-1][0] = acc

    place = jnp.stack([2 * lax.axis_index("x") + lax.axis_index("y"), lax.axis_index("c")]).astype(jnp.int32)
    return pl.pallas_call(
        kern, out_shape=SDS((depth, 2 * rh, cols), F32),
        grid_spec=pltpu.PrefetchScalarGridSpec(
            num_scalar_prefetch=1, grid=(nb,),
            in_specs=[pl.BlockSpec((1, tr, cols), lambda i, s_ref: (s_ref[0], i, 0)),
                      pl.BlockSpec((N_CHIPS - 1, tr, cols), lambda i, s_ref: (0, i, 0))] + [ANY] * len(stacked),
            out_specs=pl.BlockSpec((1, tr, cols), lambda i, s_ref: (layer, s_ref[1] * nb + i, 0))),
        input_output_aliases={3: 0} if stacked else {},
        compiler_params=_params("arbitrary"), name=name)(place, partial, received, *stacked)


def _pack(parts, lead):
    lead_shape = parts[0].shape[:lead]
    flat = jnp.concatenate([p.reshape(lead_shape + (-1,)) for p in parts], axis=-1)
    n = flat.shape[-1]
    rows = _round_up(-(-n // PACK_W), PACK_ROWS)
    flat = jnp.pad(flat, [(0, 0)] * lead + [(0, rows * PACK_W - n)])
    return flat.reshape(lead_shape + (rows, PACK_W))


def _unpack(buf, shapes, lead):
    lead_shape = buf.shape[:lead]
    flat = buf.reshape(lead_shape + (-1,))
    out, off = [], 0
    for s in shapes:
        n = math.prod(s)
        out.append(flat[..., off:off + n].reshape(lead_shape + tuple(s)))
        off += n
    return out


def _to_slots(full, axis):
    s = full.shape
    split = full.reshape(s[:axis] + (N_CHIPS, s[axis] // N_CHIPS) + s[axis + 1:])
    return jnp.moveaxis(split, axis, 0)


def _from_slots(slots, axis):
    moved = jnp.moveaxis(slots, 0, axis)
    s = moved.shape
    return moved.reshape(s[:axis] + (s[axis] * s[axis + 1],) + s[axis + 2:])


MATMUL_WEIGHTS = (("w_in", 1), ("w_out", 0), ("w_q", 0), ("w_kv", 1), ("w_o", 0), ("w_ff1", 1), ("w_ff2", 0))
CONV_WEIGHTS = (("conv_a_w", 1), ("conv_c_w", 1))
REPLICATED = ("conv_a_b", "ln_a_g", "ln_a_b", "ln_v_g", "ln_v_b", "w_s", "b_s",
              "ln1_g", "ln1_b", "ln2_g", "ln2_b", "ln3_g", "ln3_b")
WEIGHT_ORDER = ("w_in", "conv_a_w", "conv_a_b", "ln_a_g", "ln_a_b", "ln_v_g", "ln_v_b", "w_s", "b_s", "conv_c_w", "w_out",
                "ln1_g", "ln1_b", "w_q", "w_kv", "w_o", "ln2_g", "ln2_b", "w_ff1", "w_ff2", "ln3_g", "ln3_b")


def _row(v):
    return v.reshape(1, -1)


def kernel(x, mem, w_in, conv_a_w, conv_a_b, ln_a_g, ln_a_b, ln_v_g, ln_v_b, w_s, b_s, conv_c_w, w_out, ln1_g, ln1_b, w_q, w_kv, w_o, ln2_g, ln2_b, w_ff1, w_ff2, ln3_g, ln3_b, loss_target, m_w_in, m_conv_a_w, m_conv_a_b, m_ln_a_g, m_ln_a_b, m_ln_v_g, m_ln_v_b, m_w_s, m_b_s, m_conv_c_w, m_w_out, m_ln1_g, m_ln1_b, m_w_q, m_w_kv, m_w_o, m_ln2_g, m_ln2_b, m_w_ff1, m_w_ff2, m_ln3_g, m_ln3_b, v_w_in, v_conv_a_w, v_conv_a_b, v_ln_a_g, v_ln_a_b, v_ln_v_g, v_ln_v_b, v_w_s, v_b_s, v_conv_c_w, v_w_out, v_ln1_g, v_ln1_b, v_w_q, v_w_kv, v_w_o, v_ln2_g, v_ln2_b, v_w_ff1, v_w_ff2, v_ln3_g, v_ln3_b):
    given = dict(locals())
    weights = {n: given[n] for n in WEIGHT_ORDER}
    moment1 = {n: given["m_" + n] for n in WEIGHT_ORDER}
    moment2 = {n: given["v_" + n] for n in WEIGHT_ORDER}

    depth = w_in.shape[0]
    batch, seq, d = x.shape
    t = batch * seq
    hd = d // HEADS
    d_a, d_b, d_c = A_HEADS * hd, B_HEADS * hd, C_HEADS * hd
    widths = (d_a, d_a, d_b, d_b, d_c, d_c, d_c)
    in_offs = [sum(widths[:k]) for k in range(len(widths))]
    alpha = (2.0 * depth) ** 0.25
    layers = range(depth)
    tm = _tile(t, MATMUL_ROW_TILE)
    tm_wide = _tile(t, WIDE_ROW_TILE)

    conv_shards = [weights[n][l] for l in layers for n, _ in CONV_WEIGHTS]
    names = [n for n, _ in MATMUL_WEIGHTS]
    first_used = names[:1]
    parts = {l: [first_used, [n for n in names if n not in first_used]] if l == 0 else [names] for l in layers}
    axis_of = dict(MATMUL_WEIGHTS)
    groups = {}

    def gather_start(l, p, after):
        part = parts[l][p]
        shards = [weights[n][l].astype(BF16) for n in part] + ([_pack(conv_shards, 0)] if (l, p) == (0, 0) else [])
        lands = [lax.empty((N_CHIPS,) + s.shape, s.dtype) for s in shards]
        send_sems, recv_sems, bufs, token = _copies_start(_gather_copies, GATHER_COPIES, shards, lands, f"gather_start_{l}_{p}", after)
        groups[l, p] = (part, send_sems, recv_sems, bufs, len(shards))
        return token

    sequence = [(l, p) for l in layers for p in range(len(parts[l]))]

    def gathered(l, p, after):
        part, send_sems, recv_sems, bufs, ns = groups[l, p]
        _, lands = _copies_wait(_gather_copies, send_sems, recv_sems, bufs, ns, after, f"gather_wait_{l}_{p}")
        slots = _gather_finish(lands, "gather_finish")
        nxt = sequence.index((l, p)) + 1
        started = gather_start(*sequence[nxt], slots[0]) if nxt < len(sequence) else None
        return {n: _from_slots(slots[k], axis_of[n]) for k, n in enumerate(part)}, slots[len(part):], started

    xf = x.reshape(t, d)
    xb = (xf + gather_start(0, 0, None)[0, 0]).astype(BF16)
    memb = mem.reshape(-1, d).astype(BF16)
    saved, full, conv_full = [], [], None
    stream = (xf,)
    for l in layers:
        w, extra, anchor = gathered(l, 0, xb)
        if l == 0:
            conv_full = _unpack(extra[0], [s.shape for s in conv_shards], 1)
        for k, (n, axis) in enumerate(CONV_WEIGHTS):
            w[n] = _from_slots(conv_full[l * len(CONV_WEIGHTS) + k], axis)
        for n in REPLICATED:
            w[n] = weights[n][l]
        w["w_s_t"] = jnp.swapaxes(w["w_s"], 1, 2)
        w["bias_full"] = jnp.repeat(w["b_s"].T, hd, axis=1)
        full.append(w)
        s = {"x0b": xb}
        proj = _in_proj(xb, w["w_in"], widths, "in_proj", after=anchor)
        s["proj"] = proj
        a_val, a_gate, b_u, b_v, c_b, c_c, c_x = proj
        a_out, s["ac"] = _mix_a_fwd(a_val, a_gate, w["conv_a_w"], _row(w["conv_a_b"]), _row(w["ln_a_g"]), _row(w["ln_a_b"]),
                                    batch, "mix_a_fwd")
        b_out = _mix_b_fwd(b_u, b_v, w["w_s"], w["bias_full"], _row(w["ln_v_g"]), _row(w["ln_v_b"]), "mix_b_fwd")
        c_out = _mix_c_fwd(c_b, c_c, c_x, w["conv_c_w"], batch, "mix_c_fwd")
        s["cat"] = (a_out, b_out, c_out)
        anchor = None
        for p in range(1, len(parts[l])):
            more, _, anchor = gathered(l, p, c_out)
            w.update(more)
        s["z1"], xb = _proj_ln([a_out, b_out, c_out], w["w_out"], stream, _row(w["ln1_g"]), _row(w["ln1_b"]), alpha, tm, "out_proj_ln",
                               after=anchor)
        stream = (s["z1"], _row(w["ln1_g"]), _row(w["ln1_b"]))
        s["x1b"] = xb
        s["q"] = _matmul(xb, w["w_q"], "q_proj")
        s["kv"] = _matmul(memb, w["w_kv"], "kv_proj")
        s["o"] = _attention_fwd(s["q"], s["kv"], batch, "attention_fwd")
        s["z2"], xb = _proj_ln([s["o"]], w["w_o"], stream, _row(w["ln2_g"]), _row(w["ln2_b"]), alpha, tm, "o_proj_ln")
        stream = (s["z2"], _row(w["ln2_g"]), _row(w["ln2_b"]))
        s["x2b"] = xb
        s["h"], s["r"] = _ff1(xb, w["w_ff1"], "ff1")
        s["z3"], xb = _proj_ln([s["r"]], w["w_ff2"], stream, _row(w["ln3_g"]), _row(w["ln3_b"]), alpha, tm_wide, "ff2_ln")
        stream = (s["z3"], _row(w["ln3_g"]), _row(w["ln3_b"]))
        saved.append(s)

    target = loss_target.reshape(t, d)
    dz3, dz3b, top_g, top_b, loss_block = _loss_head_ln(saved[-1]["z3"], target, _row(full[-1]["ln3_g"]), _row(full[-1]["ln3_b"]),
                                                        "loss_head_ln3")
    loss = lax.psum(loss_block[0, 0], ("x", "y", "c"))
    grads = [{} for _ in layers]
    grads[-1]["ln3_g"], grads[-1]["ln3_b"] = top_g, top_b
    reductions = []
    swapping = {}
    token, last_started = 0.0, None

    def rows_to_slots(full_):
        return full_.reshape(N_CHIPS, full_.shape[1] // N_CHIPS, full_.shape[2])

    def swap_start(key, named, wire):
        arrays = [a for _, a in named]
        lands = [lax.empty((f.shape[0], f.shape[1] // 2, f.shape[2]), f.dtype) for f in arrays]
        send_sems, recv_sems, bufs, started = _copies_start(_swap_copies, 1, arrays, lands, f"rs_swap_start_{key}")
        swapping[key] = ([n for n, _ in named], send_sems, recv_sems, bufs, wire)
        return started

    def swap_finish(key, after):
        nonlocal token, last_started
        part, send_sems, recv_sems, bufs, wire = swapping.pop(key)
        fulls, theirs = _copies_wait(_swap_copies, send_sems, recv_sems, bufs, len(wire), after, f"rs_swap_wait_{key}")
        sums = [_add_sibling(f, th, dt, "rs_add_sibling") for f, th, dt in zip(fulls, theirs, wire)]
        lands = [lax.empty((N_CHIPS - 1,) + p.shape[1:], p.dtype) for p in sums]
        send_sems, recv_sems, bufs, started = _copies_start(_owner_copies, COPIES_PER_ARRAY, sums, lands, f"rs_send_start_{key}")
        reductions.append((key, part, send_sems, recv_sems, bufs))
        token, last_started = started[0, 0], started

    early = ("w_ff2", "w_ff1", "w_o", "w_kv", "w_q")
    swap_started, pending = None, None
    for l in reversed(layers):
        w, s, g = full[l], saved[l], grads[l]
        g["w_ff2"] = rows_to_slots(_mm_tn(s["r"], dz3b, "d_w_ff2", after=swap_started))
        dh = _bwd_ff2(dz3b, s["h"], w["w_ff2"], "d_ff_hidden", after=swap_started)
        if pending is not None:
            swap_finish(pending, dh)
        g["w_ff1"] = _mm_tn(s["x2b"], dh, "d_w_ff1", slots=N_CHIPS)
        dz2, dz2b, g["ln2_g"], g["ln2_b"] = _bwd_data_ln([dh], w["w_ff1"], [0], dz3, alpha, s["z2"], _row(w["ln2_g"]) + token,
                                                         tm_wide, "d_x2_ln2")
        g["w_o"] = rows_to_slots(_mm_tn(s["o"], dz2b, "d_w_o"))
        do = _bwd_data([dz2b], w["w_o"], [0], None, alpha, BF16, tm, "d_att_out")
        dq, dk, dv = _attention_bwd(s["q"], s["kv"], do, batch, "attention_bwd")
        g["w_kv"] = _mm_tn(memb, jnp.concatenate([dk, dv], axis=1), "d_w_kv", slots=N_CHIPS)
        g["w_q"] = rows_to_slots(_mm_tn(s["x1b"], dq, "d_w_q"))
        dz1, dz1b, g["ln1_g"], g["ln1_b"] = _bwd_data_ln([dq], w["w_q"], [0], dz2, alpha, s["z1"], _row(w["ln1_g"]), tm, "d_x1_ln1")
        early_started = swap_start("0a", [(n, g[n]) for n in early], [BF16] * len(early)) if l == 0 else None
        g["w_out"] = _mm_tn_rows(s["cat"], dz1b, "d_w_out").reshape(N_CHIPS, d // N_CHIPS, d)
        da_, dbo, dco = _bwd_out_proj(dz1b, w["w_out"], (d_a, d_b, d_c), "d_mixer_out", after=early_started)
        a_val, a_gate, b_u, b_v, c_b, c_c, c_x = s["proj"]
        dav, dag, g["conv_a_w"], g["conv_a_b"], g["ln_a_g"], g["ln_a_b"] = _mix_a_bwd(
            a_val, a_gate, s["ac"], da_, w["conv_a_w"], _row(w["ln_a_g"]), _row(w["ln_a_b"]), batch, "mix_a_bwd")
        if l == 0:
            swap_finish("0a", dav)
        dbu, dbv, g["w_s"], dbs, g["ln_v_g"], g["ln_v_b"] = _mix_b_bwd(
            b_u, b_v, dbo, w["w_s"], w["w_s_t"], w["bias_full"], _row(w["ln_v_g"]) + token, _row(w["ln_v_b"]), "mix_b_bwd")
        g["b_s"] = dbs.reshape(B_HEADS, CHUNK)
        dcb, dcc, dcx, g["conv_c_w"] = _mix_c_bwd(c_b, c_c, c_x, dco, w["conv_c_w"], batch, "mix_c_bwd")
        dproj = [dav, dag, dbu, dbv, dcb, dcc, dcx]
        g["w_in"] = _to_slots(_mm_tn_cols(s["x0b"], dproj, "d_w_in"), 1)
        if l > 0:
            below = full[l - 1]
            dz3, dz3b, grads[l - 1]["ln3_g"], grads[l - 1]["ln3_b"] = _bwd_data_ln(
                dproj, w["w_in"], in_offs, dz1, alpha, saved[l - 1]["z3"], _row(below["ln3_g"]), tm, "d_x0_ln3")
            pending = str(l)
            swap_started = swap_start(pending, [(n, g[n]) for n, _ in MATMUL_WEIGHTS], [BF16] * len(MATMUL_WEIGHTS))
        else:
            dx = _bwd_data(dproj, w["w_in"], in_offs, dz1, alpha, F32, tm, "d_x0")
            conv_slots = [_to_slots(grads[k][n], axis) for k in layers for n, axis in CONV_WEIGHTS]
            rep_parts = [jnp.stack([grads[k][n].reshape(weights[n].shape[1:]) for k in layers]) for n in REPLICATED]
            rep_flat = jnp.concatenate([p.reshape(-1) for p in rep_parts])
            n_rep = rep_flat.shape[0]
            per_chip = _round_up(-(-n_rep // N_CHIPS), PACK_W * PACK_ROWS)
            rep_slots = jnp.pad(rep_flat, (0, N_CHIPS * per_chip - n_rep)).reshape(N_CHIPS, per_chip)
            late = [(n, g[n]) for n, _ in MATMUL_WEIGHTS if n not in early] + [("small", _pack(conv_slots + [rep_slots], 1))]
            swap_start("0b", late, [BF16] * (len(late) - 1) + [F32])
            swap_finish("0b", dx)
    grad_x = dx.reshape(batch, seq, d)

    reduced, after = {}, last_started
    for key, part, send_sems, recv_sems, bufs in reductions:
        sums, lands = _copies_wait(_owner_copies, send_sems, recv_sems, bufs, len(part), after, f"rs_send_wait_{key}")
        for n, p, r in zip(part, sums, lands):
            stacked = n != "small"
            reduced[n] = _add_owners(p, r, "rs_add_owners", reduced.get(n), int(key[0]) if stacked else 0, depth if stacked else 1)
            after = reduced[n]
    reduced = dict(zip(reduced, _join_halves(list(reduced.values()), "rs_join")))
    small = _unpack(reduced["small"][0], [c.shape[1:] for c in conv_slots] + [(per_chip,)], 0)
    conv_grad, rep_mine = small[:-1], small[-1]
    rep_all = _all_gather_chips(rep_mine.reshape(-1, PACK_W), "gather_small_grads").reshape(-1)[:n_rep]

    grad = {}
    for n, _ in MATMUL_WEIGHTS:
        grad[n] = reduced[n]
    for k, (n, _) in enumerate(CONV_WEIGHTS):
        grad[n] = jnp.stack([conv_grad[l * len(CONV_WEIGHTS) + k] for l in layers])
    off = 0
    for n in REPLICATED:
        size = math.prod(weights[n].shape)
        grad[n] = rep_all[off:off + size].reshape(weights[n].shape)
        off += size

    delta, new_m, new_v = {}, {}, {}
    for n, _ in MATMUL_WEIGHTS:
        shape = weights[n].shape
        as_rows = lambda a: a.reshape(-1, shape[-1])
        delta[n], new_m[n], new_v[n] = (
            r.reshape(shape) for r in _adamw(as_rows(weights[n]), as_rows(grad[n]), as_rows(moment1[n]), as_rows(moment2[n]), "adamw"))
    small_names = [n for n, _ in CONV_WEIGHTS] + list(REPLICATED)
    small_shapes = [weights[n].shape for n in small_names]
    packed = [_pack([src[n] for n in small_names], 0) for src in (weights, grad, moment1, moment2)]
    for dst, res in zip((delta, new_m, new_v), _adamw(*packed, "adamw_small")):
        for n, a in zip(small_names, _unpack(res, small_shapes, 0)):
            dst[n] = a

    return (loss, grad_x, *[grad[n] for n in WEIGHT_ORDER], *[delta[n] for n in WEIGHT_ORDER],
            *[new_m[n] for n in WEIGHT_ORDER], *[new_v[n] for n in WEIGHT_ORDER])
```
